```python
import math
import jax, jax.numpy as jnp
from jax import lax
import numpy as np

D_MODEL = 2048
BATCH = 1
SEQ = 8192
DEPTH = 1

SSM_WIDTH = D_MODEL // 2
SSM_GROUP = 16
SSM_GROUPS = SSM_WIDTH // SSM_GROUP
SSM_STATE = 64
ATTN_WIDTH = D_MODEL // 2
ATTN_HEADS = 8
ATTN_HEAD_DIM = ATTN_WIDTH // ATTN_HEADS // 2
ROPE_THETA = 10000.0
Q_BLOCK = 128
IN_WIDTH = SSM_WIDTH + 3 * ATTN_WIDTH + 2 * D_MODEL
N_GROUPS = 4
EXPERTS_PER_GROUP = 8
N_EXPERTS = N_GROUPS * EXPERTS_PER_GROUP
TOP_K = 2
D_FF_EXPERT = D_MODEL // 4
MOE_BLOCK = 128
NORM_EPS = 1e-6
SUBLN_EPS = 1e-5

kernel_name = "hybrid_s5_diffattn_hmoe_block"


def rms_norm(x, w, eps=NORM_EPS):
    xf = x.astype(jnp.float32)
    y = xf * lax.rsqrt(jnp.mean(xf * xf, axis=-1, keepdims=True) + eps)
    return (y * w.astype(jnp.float32)).astype(x.dtype)


def lambda_init_fn(layer_idx):
    return 0.8 - 0.6 * math.exp(-0.3 * layer_idx)


def s5_branch(u, lam_re, lam_im, log_dt, b_re, b_im, c_re, c_im, d_skip, glu_w, glu_b):
    bsz, seq = u.shape[0], u.shape[1]
    f32 = jnp.float32
    uf = u.astype(f32).reshape(bsz, seq, SSM_GROUPS, SSM_GROUP)
    lr = lam_re.astype(f32)
    li = lam_im.astype(f32)
    dt = jnp.exp(log_dt.astype(f32))[:, None]
    mag = jnp.exp(lr * dt)
    abar_re = mag * jnp.cos(li * dt)
    abar_im = mag * jnp.sin(li * dt)
    nr = abar_re - 1.0
    ni = abar_im
    den = lr * lr + li * li
    f_re = (nr * lr + ni * li) / den
    f_im = (ni * lr - nr * li) / den
    br = b_re.astype(f32)
    bi = b_im.astype(f32)
    bbar_re = f_re[..., None] * br - f_im[..., None] * bi
    bbar_im = f_re[..., None] * bi + f_im[..., None] * br
    bu_re = jnp.einsum('bsgm,gpm->bsgp', uf, bbar_re)
    bu_im = jnp.einsum('bsgm,gpm->bsgp', uf, bbar_im)
    a_re = jnp.broadcast_to(abar_re, bu_re.shape)
    a_im = jnp.broadcast_to(abar_im, bu_im.shape)

    def combine(e1, e2):
        a1r, a1i, b1r, b1i = e1
        a2r, a2i, b2r, b2i = e2
        return (a2r * a1r - a2i * a1i,
                a2r * a1i + a2i * a1r,
                a2r * b1r - a2i * b1i + b2r,
                a2r * b1i + a2i * b1r + b2i)

    _, _, x_re, x_im = lax.associative_scan(combine, (a_re, a_im, bu_re, bu_im), axis=1)
    y = (jnp.einsum('bsgp,gmp->bsgm', x_re, c_re.astype(f32))
         - jnp.einsum('bsgp,gmp->bsgm', x_im, c_im.astype(f32)))
    y = y.reshape(bsz, seq, SSM_WIDTH) + d_skip.astype(f32) * uf.reshape(bsz, seq, SSM_WIDTH)
    y = jax.nn.gelu(y)
    y = y * jax.nn.sigmoid(y @ glu_w.astype(f32) + glu_b.astype(f32))
    return y.astype(u.dtype)


def rope_cos_sin(positions, dim):
    inv_freq = 1.0 / (ROPE_THETA ** (jnp.arange(0, dim, 2, dtype=jnp.float32) / dim))
    ang = positions.astype(jnp.float32)[..., None] * inv_freq
    return jnp.cos(ang)[:, :, None, None, :], jnp.sin(ang)[:, :, None, None, :]


def apply_rope(t, cos, sin):
    half = t.shape[-1] // 2
    t1, t2 = t[..., :half], t[..., half:]
    return jnp.concatenate([t1 * cos - t2 * sin, t2 * cos + t1 * sin], axis=-1)


def diff_attention(q, k, v, positions, lam_q1, lam_k1, lam_q2, lam_k2, subln_w, lambda_init):
    bsz, seq = q.shape[0], q.shape[1]
    f32 = jnp.float32
    hd = ATTN_HEAD_DIM
    cos, sin = rope_cos_sin(positions, hd)
    qf = apply_rope(q.astype(f32).reshape(bsz, seq, ATTN_HEADS, 2, hd), cos, sin)
    kf = apply_rope(k.astype(f32).reshape(bsz, seq, ATTN_HEADS, 2, hd), cos, sin)
    vf = v.astype(f32).reshape(bsz, seq, ATTN_HEADS, 2 * hd)
    lam = (jnp.exp(jnp.sum(lam_q1.astype(f32) * lam_k1.astype(f32)))
           - jnp.exp(jnp.sum(lam_q2.astype(f32) * lam_k2.astype(f32))) + lambda_init)
    n_blk = seq // Q_BLOCK
    q_blocks = jnp.moveaxis(qf.reshape(bsz, n_blk, Q_BLOCK, ATTN_HEADS, 2, hd), 1, 0)
    key_pos = jnp.arange(seq)
    scale = hd ** -0.5

    def block(args):
        qb, bi = args
        s = jnp.einsum('bqhcd,bkhcd->bhcqk', qb, kf) * scale
        q_pos = bi * Q_BLOCK + jnp.arange(Q_BLOCK)
        mask = key_pos[None, :] <= q_pos[:, None]
        s = jnp.where(mask, s, -jnp.inf)
        p = jax.nn.softmax(s, axis=-1)
        a = p[:, :, 0] - lam * p[:, :, 1]
        return jnp.einsum('bhqk,bkhe->bqhe', a, vf)

    o = lax.map(block, (q_blocks, jnp.arange(n_blk)))
    o = jnp.moveaxis(o, 0, 1).reshape(bsz, seq, ATTN_HEADS, 2 * hd)
    o = o * lax.rsqrt(jnp.mean(o * o, axis=-1, keepdims=True) + SUBLN_EPS)
    o = o * subln_w.astype(f32) * (1.0 - lambda_init)
    return o.reshape(bsz, seq, ATTN_WIDTH).astype(q.dtype)


def hierarchical_moe(h, router_group_w, router_expert_w, w_gate, w_up, w_down):
    bsz, seq, dm = h.shape
    n_tok = bsz * seq
    f32 = jnp.float32
    hf = h.reshape(n_tok, dm)
    hf32 = hf.astype(f32)
    g_logits = hf32 @ router_group_w.astype(f32)
    g_prob = jax.nn.softmax(g_logits, axis=-1)
    g_val, g_idx = lax.top_k(g_prob, 1)
    e_logits = (hf32 @ router_expert_w.astype(f32)).reshape(n_tok, N_GROUPS, EXPERTS_PER_GROUP)
    e_sel = jnp.take_along_axis(e_logits, g_idx[:, :, None], axis=1)[:, 0]
    e_val, e_idx = lax.top_k(e_sel, TOP_K)
    gate = jax.nn.softmax(e_val, axis=-1) * g_val
    expert_id = g_idx * EXPERTS_PER_GROUP + e_idx

    n_slots = n_tok * TOP_K
    e_flat = expert_id.reshape(n_slots)
    tok_flat = jnp.repeat(jnp.arange(n_tok, dtype=jnp.int32), TOP_K)
    w_flat = gate.reshape(n_slots)
    order = jnp.argsort(e_flat)
    sorted_e = e_flat[order]
    counts = jnp.bincount(e_flat, length=N_EXPERTS)
    starts = jnp.cumsum(counts) - counts
    padded = ((counts + MOE_BLOCK - 1) // MOE_BLOCK) * MOE_BLOCK
    padded_ends = jnp.cumsum(padded)
    padded_starts = padded_ends - padded
    dest = padded_starts[sorted_e] + (jnp.arange(n_slots) - starts[sorted_e])
    buf_len = n_slots + N_EXPERTS * MOE_BLOCK
    n_blocks = buf_len // MOE_BLOCK
    buf_tok = jnp.zeros((buf_len,), jnp.int32).at[dest].set(tok_flat[order])
    buf_w = jnp.zeros((buf_len,), f32).at[dest].set(w_flat[order])
    block_start = jnp.arange(n_blocks) * MOE_BLOCK
    block_e = jnp.clip(jnp.searchsorted(padded_ends, block_start, side='right'), 0, N_EXPERTS - 1)

    def expert_block(args):
        tok, e = args
        xb = hf[tok]
        a = jax.nn.silu(xb @ w_gate[e]) * (xb @ w_up[e])
        return a @ w_down[e]

    ys = lax.map(expert_block, (buf_tok.reshape(n_blocks, MOE_BLOCK), block_e))
    ys = ys.reshape(buf_len, dm)
    out = jnp.zeros((n_tok, dm), hf.dtype).at[buf_tok].add(ys * buf_w[:, None].astype(ys.dtype))
    return out.reshape(bsz, seq, dm)


def setup_inputs(seed: int = 0) -> dict:
    key = jax.random.key(seed)
    ks = jax.random.split(key, 32)
    f32 = jnp.float32
    L, D, G, P, M = DEPTH, D_MODEL, SSM_GROUPS, SSM_STATE, SSM_GROUP
    nrm = lambda k, shape, s: jax.random.normal(k, shape, f32) * s
    x = jax.random.normal(ks[0], (BATCH, SEQ, D), f32)
    positions = jnp.broadcast_to(jnp.arange(SEQ, dtype=jnp.int32), (BATCH, SEQ))
    n_idx = jnp.arange(P, dtype=f32)[None, None, :]
    return {
        "x": x,
        "positions": positions,
        "norm1_w": 1.0 + nrm(ks[1], (L, D), 0.02),
        "w_in": nrm(ks[2], (L, D, IN_WIDTH), D ** -0.5),
        "ssm_lambda_re": -0.5 + nrm(ks[3], (L, G, P), 0.01),
        "ssm_lambda_im": jnp.pi * n_idx + nrm(ks[4], (L, G, P), 0.01),
        "ssm_log_dt": jax.random.uniform(ks[5], (L, G), f32, math.log(1e-3), math.log(1e-1)),
        "ssm_b_re": nrm(ks[6], (L, G, P, M), (2 * M) ** -0.5),
        "ssm_b_im": nrm(ks[7], (L, G, P, M), (2 * M) ** -0.5),
        "ssm_c_re": nrm(ks[8], (L, G, M, P), P ** -0.5),
        "ssm_c_im": nrm(ks[9], (L, G, M, P), P ** -0.5),
        "ssm_d": nrm(ks[10], (L, SSM_WIDTH), 0.5),
        "ssm_glu_w": nrm(ks[11], (L, SSM_WIDTH, SSM_WIDTH), SSM_WIDTH ** -0.5),
        "ssm_glu_b": nrm(ks[12], (L, SSM_WIDTH), 0.01),
        "attn_lambda_q1": nrm(ks[13], (L, ATTN_HEAD_DIM), 0.1),
        "attn_lambda_k1": nrm(ks[14], (L, ATTN_HEAD_DIM), 0.1),
        "attn_lambda_q2": nrm(ks[15], (L, ATTN_HEAD_DIM), 0.1),
        "attn_lambda_k2": nrm(ks[16], (L, ATTN_HEAD_DIM), 0.1),
        "attn_subln_w": 1.0 + nrm(ks[17], (L, 2 * ATTN_HEAD_DIM), 0.02),
        "w_proj_ssm": nrm(ks[18], (L, SSM_WIDTH, D), SSM_WIDTH ** -0.5),
        "w_proj_attn": nrm(ks[19], (L, ATTN_WIDTH, D), ATTN_WIDTH ** -0.5),
        "w_out": nrm(ks[20], (L, D, D), D ** -0.5),
        "norm2_w": 1.0 + nrm(ks[21], (L, D), 0.02),
        "router_group_w": nrm(ks[22], (L, D, N_GROUPS), D ** -0.5),
        "router_expert_w": nrm(ks[23], (L, D, N_EXPERTS), D ** -0.5),
        "expert_w_gate": nrm(ks[24], (L, N_EXPERTS, D, D_FF_EXPERT), D ** -0.5),
        "expert_w_up": nrm(ks[25], (L, N_EXPERTS, D, D_FF_EXPERT), D ** -0.5),
        "expert_w_down": nrm(ks[26], (L, N_EXPERTS, D_FF_EXPERT, D), D_FF_EXPERT ** -0.5),
        "final_norm_w": 1.0 + nrm(ks[27], (D,), 0.02),
    }


def reference(x, positions, norm1_w, w_in, ssm_lambda_re, ssm_lambda_im, ssm_log_dt,
              ssm_b_re, ssm_b_im, ssm_c_re, ssm_c_im, ssm_d, ssm_glu_w, ssm_glu_b,
              attn_lambda_q1, attn_lambda_k1, attn_lambda_q2, attn_lambda_k2, attn_subln_w,
              w_proj_ssm, w_proj_attn, w_out, norm2_w, router_group_w, router_expert_w,
              expert_w_gate, expert_w_up, expert_w_down, final_norm_w):
    o_q = SSM_WIDTH
    o_k = o_q + ATTN_WIDTH
    o_v = o_k + ATTN_WIDTH
    o_gs = o_v + ATTN_WIDTH
    o_ga = o_gs + D_MODEL
    for l in range(DEPTH):
        h = rms_norm(x, norm1_w[l])
        z = h @ w_in[l]
        u = z[..., :o_q]
        q = z[..., o_q:o_k]
        k = z[..., o_k:o_v]
        v = z[..., o_v:o_gs]
        g_ssm = jax.nn.sigmoid(z[..., o_gs:o_ga])
        g_attn = jax.nn.sigmoid(z[..., o_ga:])
        y_ssm = s5_branch(u, ssm_lambda_re[l], ssm_lambda_im[l], ssm_log_dt[l],
                          ssm_b_re[l], ssm_b_im[l], ssm_c_re[l], ssm_c_im[l],
                          ssm_d[l], ssm_glu_w[l], ssm_glu_b[l])
        y_attn = diff_attention(q, k, v, positions, attn_lambda_q1[l], attn_lambda_k1[l],
                                attn_lambda_q2[l], attn_lambda_k2[l], attn_subln_w[l],
                                lambda_init_fn(l))
        merged = g_ssm * (y_ssm @ w_proj_ssm[l]) + g_attn * (y_attn @ w_proj_attn[l])
        x = x + merged @ w_out[l]
        h2 = rms_norm(x, norm2_w[l])
        x = x + hierarchical_moe(h2, router_group_w[l], router_expert_w[l],
                                 expert_w_gate[l], expert_w_up[l], expert_w_down[l])
    return rms_norm(x, final_norm_w)
```

```python
import functools
import math

import jax
import jax.numpy as jnp
from jax import lax
from jax.experimental import pallas as pl
from jax.experimental.pallas import tpu as pltpu

F32 = jnp.float32
BF16 = jnp.bfloat16

D_MODEL = 2048
SSM_WIDTH = 1024
SSM_GROUP = 16
SSM_GROUPS = 64
SSM_STATE = 64
ATTN_WIDTH = 1024
ATTN_HEADS = 8
HEAD_DIM = 64
ROPE_THETA = 10000.0
N_GROUPS = 4
EXPERTS_PER_GROUP = 8
N_EXPERTS = 32
TOP_K = 2
D_FF = 512
NORM_EPS = 1e-6
SUBLN_EPS = 1e-5
LAMBDA_INIT = 0.8 - 0.6 * math.exp(-0.3 * 0)

LANES = 128
VMEM_LIMIT = 48 * 1024 * 1024

SSM_CHUNK = 16
SSM_GB = 8
MOE_BLK = 128


def _cparams(sem):
    return pltpu.CompilerParams(dimension_semantics=sem, vmem_limit_bytes=VMEM_LIMIT)


def _rmsnorm_body(x_ref, w_ref, o_ref):
    x = x_ref[...]
    y = x * lax.rsqrt(jnp.mean(x * x, axis=-1, keepdims=True) + NORM_EPS)
    o_ref[...] = (y * w_ref[...]).astype(o_ref.dtype)


def _rmsnorm(x, w, out_dtype, tm=512):
    s, d = x.shape
    return pl.pallas_call(
        _rmsnorm_body,
        grid=(s // tm,),
        in_specs=[pl.BlockSpec((tm, d), lambda i: (i, 0)),
                  pl.BlockSpec((1, d), lambda i: (0, 0))],
        out_specs=pl.BlockSpec((tm, d), lambda i: (i, 0)),
        out_shape=jax.ShapeDtypeStruct((s, d), out_dtype),
        compiler_params=_cparams(("parallel",)),
        name="rmsnorm",
    )(x, w.reshape(1, d))


def _mm_body(h_ref, w_ref, o_ref, *, act):
    acc = jnp.dot(h_ref[...], w_ref[...], preferred_element_type=F32)
    if act == "sigmoid":
        acc = jax.nn.sigmoid(acc)
    o_ref[...] = acc.astype(o_ref.dtype)


def _mm(h, w, col0, ncols, act, out_dtype, tm=512, tn=1024):
    s, k = h.shape
    jb = col0 // tn
    return pl.pallas_call(
        functools.partial(_mm_body, act=act),
        grid=(ncols // tn, s // tm),
        in_specs=[pl.BlockSpec((tm, k), lambda j, i: (i, 0)),
                  pl.BlockSpec((k, tn), lambda j, i: (0, j + jb))],
        out_specs=pl.BlockSpec((tm, tn), lambda j, i: (i, j)),
        out_shape=jax.ShapeDtypeStruct((s, ncols), out_dtype),
        compiler_params=_cparams(("parallel", "parallel")),
        name="mm_" + act,
    )(h, w)


def _mm_rope_body(h_ref, w_ref, pos_ref, invf_ref, o_ref, *, tn):
    j = pl.program_id(0)
    acc = jnp.dot(h_ref[...], w_ref[...], preferred_element_type=F32)
    scale = jnp.where(j == 0, HEAD_DIM ** -0.5, 1.0).astype(F32)
    ang = pos_ref[...].astype(F32) * invf_ref[...]
    lane = lax.broadcasted_iota(jnp.int32, ang.shape, 1)
    low = (lane % HEAD_DIM) < (HEAD_DIM // 2)
    cos = jnp.cos(ang) * scale
    sin = jnp.where(low, -jnp.sin(ang), jnp.sin(ang)) * scale
    for c in range(tn // LANES):
        t = acc[:, c * LANES:(c + 1) * LANES]
        partner = jnp.where(low, pltpu.roll(t, LANES - HEAD_DIM // 2, 1),
                            pltpu.roll(t, HEAD_DIM // 2, 1))
        o_ref[:, c * LANES:(c + 1) * LANES] = (t * cos + partner * sin).astype(o_ref.dtype)


def _mm_rope(h, w, positions, col0, tm=512, tn=1024):
    s, k = h.shape
    jb = col0 // tn
    inv_freq = 1.0 / (ROPE_THETA ** (jnp.arange(0, HEAD_DIM, 2, dtype=F32) / HEAD_DIM))
    invf = jnp.tile(inv_freq, LANES // (HEAD_DIM // 2)).reshape(1, LANES)
    return pl.pallas_call(
        functools.partial(_mm_rope_body, tn=tn),
        grid=(2, s // tm),
        in_specs=[pl.BlockSpec((tm, k), lambda j, i: (i, 0)),
                  pl.BlockSpec((k, tn), lambda j, i: (0, j + jb)),
                  pl.BlockSpec((tm, 1), lambda j, i: (i, 0)),
                  pl.BlockSpec((1, LANES), lambda j, i: (0, 0))],
        out_specs=pl.BlockSpec((tm, tn), lambda j, i: (i, j)),
        out_shape=jax.ShapeDtypeStruct((s, 2 * tn), BF16),
        compiler_params=_cparams(("parallel", "parallel")),
        name="mm_rope",
    )(h, w, positions.reshape(s, 1), invf)


def _s5_tables(lam_re, lam_im, log_dt, b_re, b_im, c_re, c_im):
    L, G, P, M = SSM_CHUNK, SSM_GROUPS, SSM_STATE, SSM_GROUP
    hi = lax.Precision.HIGHEST
    lr, li = lam_re.astype(F32), lam_im.astype(F32)
    dt = jnp.exp(log_dt.astype(F32))[:, None]
    n = jnp.arange(L + 1, dtype=F32)[:, None, None]
    mag = jnp.exp(lr * dt * n)
    pr = mag * jnp.cos(li * dt * n)
    pi = mag * jnp.sin(li * dt * n)
    nr, ni = pr[1] - 1.0, pi[1]
    den = lr * lr + li * li
    f_re = (nr * lr + ni * li) / den
    f_im = (ni * lr - nr * li) / den
    br, bi = b_re.astype(F32), b_im.astype(F32)
    bbr = f_re[..., None] * br - f_im[..., None] * bi
    bbi = f_re[..., None] * bi + f_im[..., None] * br
    cr, ci = c_re.astype(F32), c_im.astype(F32)

    abr = pr[..., None] * bbr - pi[..., None] * bbi
    abi = pr[..., None] * bbi + pi[..., None] * bbr
    kk = (jnp.einsum('gop,ngpi->ngoi', cr, abr[:L], precision=hi)
          - jnp.einsum('gop,ngpi->ngoi', ci, abi[:L], precision=hi))
    ii = jnp.arange(L)[:, None]
    jj = jnp.arange(L)[None, :]
    tau = jnp.clip(jj - ii, 0, L - 1)
    t5 = kk[tau] * (jj >= ii)[:, :, None, None, None].astype(F32)
    t_tab = jnp.transpose(t5, (2, 0, 4, 1, 3)).reshape(G, L * M, L * M)

    wr = jnp.transpose(abr[:L][::-1], (1, 0, 3, 2)).reshape(G, L * M, P)
    wi = jnp.transpose(abi[:L][::-1], (1, 0, 3, 2)).reshape(G, L * M, P)
    z = jnp.zeros_like(wr[0::2])
    w_top = jnp.concatenate([wr[0::2], z, wi[0::2], z], axis=-1)
    w_bot = jnp.concatenate([z, wr[1::2], z, wi[1::2]], axis=-1)
    w_tab = jnp.concatenate([w_top, w_bot], axis=1)

    ar1, ai1 = pr[1:], pi[1:]
    vr = cr[None] * ar1[:, :, None, :] - ci[None] * ai1[:, :, None, :]
    vi = -(cr[None] * ai1[:, :, None, :] + ci[None] * ar1[:, :, None, :])
    vr = jnp.transpose(vr, (1, 3, 0, 2)).reshape(G, P, L * M)
    vi = jnp.transpose(vi, (1, 3, 0, 2)).reshape(G, P, L * M)
    zv = jnp.zeros_like(vr[0::2])
    v_tab = jnp.concatenate([
        jnp.concatenate([vr[0::2], zv], axis=-1),
        jnp.concatenate([zv, vr[1::2]], axis=-1),
        jnp.concatenate([vi[0::2], zv], axis=-1),
        jnp.concatenate([zv, vi[1::2]], axis=-1)], axis=1)

    al = jnp.stack([pr[L].reshape(G * P), pi[L].reshape(G * P)])
    return t_tab.astype(BF16), w_tab.astype(BF16), v_tab.astype(BF16), al


def _s5_body(u_ref, t_ref, w_ref, v_ref, al_ref, d_ref, y_ref, er_ref, ei_ref, xr_ref, xi_ref):
    nc = u_ref.shape[0]
    lm = SSM_CHUNK * SSM_GROUP
    pw = 2 * SSM_STATE
    npair = SSM_GB // 2
    for pp in range(npair):
        up = u_ref[:, pp * 2 * lm:(pp + 1) * 2 * lm].astype(BF16)
        e = jnp.dot(up, w_ref[pp], preferred_element_type=F32)
        er_ref[:, pp * pw:(pp + 1) * pw] = e[:, :pw]
        ei_ref[:, pp * pw:(pp + 1) * pw] = e[:, pw:]

    ar = al_ref[0:1, :]
    ai = al_ref[1:2, :]

    def step(c, carry):
        xr, xi = carry
        xr_ref[pl.ds(c, 1), :] = xr
        xi_ref[pl.ds(c, 1), :] = xi
        er = er_ref[pl.ds(c, 1), :]
        ei = ei_ref[pl.ds(c, 1), :]
        return ar * xr - ai * xi + er, ar * xi + ai * xr + ei

    zero = jnp.zeros((1, npair * pw), F32)
    lax.fori_loop(0, nc, step, (zero, zero))

    for pp in range(npair):
        xin = jnp.concatenate([xr_ref[:, pp * pw:(pp + 1) * pw],
                               xi_ref[:, pp * pw:(pp + 1) * pw]], axis=1).astype(BF16)
        yc = jnp.dot(xin, v_ref[pp], preferred_element_type=F32)
        for q in range(2):
            g = pp * 2 + q
            uf = u_ref[:, g * lm:(g + 1) * lm]
            yi = jnp.dot(uf.astype(BF16), t_ref[g], preferred_element_type=F32)
            y = yi + yc[:, q * lm:(q + 1) * lm] + d_ref[:, g * lm:(g + 1) * lm] * uf
            y_ref[:, g * lm:(g + 1) * lm] = jax.nn.gelu(y)


def _s5_core(u, tables, d_skip):
    s = u.shape[0]
    L, G, M, P = SSM_CHUNK, SSM_GROUPS, SSM_GROUP, SSM_STATE
    nc = s // L
    lm = L * M
    t_tab, w_tab, v_tab, al = tables
    u2 = u.reshape(nc, L, G, M).transpose(0, 2, 1, 3).reshape(nc, G * lm)
    d2 = jnp.broadcast_to(d_skip.astype(F32).reshape(1, G, 1, M), (1, G, L, M)).reshape(1, G * lm)
    gb = SSM_GB
    y2 = pl.pallas_call(
        _s5_body,
        grid=(G // gb,),
        in_specs=[pl.BlockSpec((nc, gb * lm), lambda b: (0, b)),
                  pl.BlockSpec((gb, lm, lm), lambda b: (b, 0, 0)),
                  pl.BlockSpec((gb // 2, 2 * lm, 4 * P), lambda b: (b, 0, 0)),
                  pl.BlockSpec((gb // 2, 4 * P, 2 * lm), lambda b: (b, 0, 0)),
                  pl.BlockSpec((2, gb * P), lambda b: (0, b)),
                  pl.BlockSpec((1, gb * lm), lambda b: (0, b))],
        out_specs=pl.BlockSpec((nc, gb * lm), lambda b: (0, b)),
        out_shape=jax.ShapeDtypeStruct((nc, G * lm), F32),
        scratch_shapes=[pltpu.VMEM((nc, gb * P), F32) for _ in range(4)],
        compiler_params=_cparams(("parallel",)),
        name="s5_scan",
    )(u2, t_tab, w_tab, v_tab, al, d2)
    return y2.reshape(nc, G, L, M).transpose(0, 2, 1, 3).reshape(s, G * M)


def _glu_body(y_ref, w_ref, b_ref, o_ref):
    y = y_ref[...]
    z = jnp.dot(y.astype(BF16), w_ref[...], preferred_element_type=F32) + b_ref[...]
    o_ref[...] = (y * jax.nn.sigmoid(z)).astype(o_ref.dtype)


def _glu(y, w, b, tm=512):
    s, n = y.shape
    return pl.pallas_call(
        _glu_body,
        grid=(s // tm,),
        in_specs=[pl.BlockSpec((tm, n), lambda i: (i, 0)),
                  pl.BlockSpec((n, n), lambda i: (0, 0)),
                  pl.BlockSpec((1, n), lambda i: (0, 0))],
        out_specs=pl.BlockSpec((tm, n), lambda i: (i, 0)),
        out_shape=jax.ShapeDtypeStruct((s, n), BF16),
        compiler_params=_cparams(("parallel",)),
        name="s5_glu",
    )(y, w, b.reshape(1, n).astype(F32))


def _attn_body(q_ref, k_ref, v_ref, lam_ref, sw_ref, o_ref,
               q1_ref, q2_ref, m_ref, l_ref, acc1_ref, acc2_ref, *, tq, tk):
    qi = pl.program_id(1)
    ki = pl.program_id(2)

    @pl.when(ki == 0)
    def _init():
        q = q_ref[...]
        lane = lax.broadcasted_iota(jnp.int32, q.shape, 1)
        zero = jnp.zeros_like(q)
        q1_ref[...] = jnp.where(lane < HEAD_DIM, q, zero)
        q2_ref[...] = jnp.where(lane >= HEAD_DIM, q, zero)
        m_ref[...] = jnp.full(m_ref.shape, -jnp.inf, F32)
        l_ref[...] = jnp.zeros(l_ref.shape, F32)
        acc1_ref[...] = jnp.zeros(acc1_ref.shape, F32)
        acc2_ref[...] = jnp.zeros(acc2_ref.shape, F32)

    def update(masked):
        k = k_ref[...]
        v = v_ref[...]
        if masked:
            row = lax.broadcasted_iota(jnp.int32, (tq, tk), 0)
            col = lax.broadcasted_iota(jnp.int32, (tq, tk), 1)
            keep = col <= row
        for c, (qc_ref, acc_ref) in enumerate(((q1_ref, acc1_ref), (q2_ref, acc2_ref))):
            s = lax.dot_general(qc_ref[...], k, (((1,), (1,)), ((), ())),
                                preferred_element_type=F32)
            if masked:
                s = jnp.where(keep, s, -jnp.inf)
            m_old = m_ref[:, c:c + 1]
            m_new = jnp.maximum(m_old, jnp.max(s, axis=-1, keepdims=True))
            alpha = jnp.exp(m_old - m_new)
            p = jnp.exp(s - m_new)
            l_ref[:, c:c + 1] = alpha * l_ref[:, c:c + 1] + jnp.sum(p, axis=-1, keepdims=True)
            acc_ref[...] = alpha * acc_ref[...] + jnp.dot(p.astype(BF16), v,
                                                          preferred_element_type=F32)
            m_ref[:, c:c + 1] = m_new

    @pl.when(ki < qi)
    def _full():
        update(False)

    @pl.when(ki == qi)
    def _diag():
        update(True)
        lam = (jnp.exp(jnp.sum(lam_ref[0:1, :] * lam_ref[1:2, :]))
               - jnp.exp(jnp.sum(lam_ref[2:3, :] * lam_ref[3:4, :])) + LAMBDA_INIT)
        o = acc1_ref[...] / l_ref[:, 0:1] - lam * (acc2_ref[...] / l_ref[:, 1:2])
        o = o * lax.rsqrt(jnp.mean(o * o, axis=-1, keepdims=True) + SUBLN_EPS)
        o_ref[...] = (o * sw_ref[...] * (1.0 - LAMBDA_INIT)).astype(o_ref.dtype)


def _diff_attention(qk, v, lam_params, subln_w, tq=512):
    s = v.shape[0]
    tk = tq
    nq = s // tq
    hw = 2 * HEAD_DIM
    kcol0 = ATTN_WIDTH // hw
    return pl.pallas_call(
        functools.partial(_attn_body, tq=tq, tk=tk),
        grid=(ATTN_HEADS, nq, nq),
        in_specs=[pl.BlockSpec((tq, hw), lambda h, i, j: (i, h)),
                  pl.BlockSpec((tk, hw), lambda h, i, j: (jnp.minimum(j, i), kcol0 + h)),
                  pl.BlockSpec((tk, hw), lambda h, i, j: (jnp.minimum(j, i), h)),
                  pl.BlockSpec((4, HEAD_DIM), lambda h, i, j: (0, 0)),
                  pl.BlockSpec((1, hw), lambda h, i, j: (0, 0))],
        out_specs=pl.BlockSpec((tq, hw), lambda h, i, j: (i, h)),
        out_shape=jax.ShapeDtypeStruct((s, ATTN_WIDTH), BF16),
        scratch_shapes=[pltpu.VMEM((tq, hw), BF16), pltpu.VMEM((tq, hw), BF16),
                        pltpu.VMEM((tq, LANES), F32), pltpu.VMEM((tq, LANES), F32),
                        pltpu.VMEM((tq, hw), F32), pltpu.VMEM((tq, hw), F32)],
        compiler_params=_cparams(("parallel", "parallel", "arbitrary")),
        name="diff_attn",
    )(qk, qk, v, lam_params, subln_w.reshape(1, hw).astype(F32))


def _merge_body(x_ref, ys_ref, ya_ref, gs_ref, ga_ref, wps_ref, wpa_ref, wo_ref, n2_ref, wr_ref,
                x1_ref, h2_ref, eid_ref, gate_ref):
    ps = jnp.dot(ys_ref[...], wps_ref[...], preferred_element_type=F32)
    pa = jnp.dot(ya_ref[...], wpa_ref[...], preferred_element_type=F32)
    merged = gs_ref[...].astype(F32) * ps + ga_ref[...].astype(F32) * pa
    x1 = x_ref[...] + jnp.dot(merged.astype(BF16), wo_ref[...], preferred_element_type=F32)
    x1_ref[...] = x1
    h2 = x1 * lax.rsqrt(jnp.mean(x1 * x1, axis=-1, keepdims=True) + NORM_EPS) * n2_ref[...]
    h2_ref[...] = h2
    wr = wr_ref[...]
    h_hi = h2.astype(BF16)
    h_lo = (h2 - h_hi.astype(F32)).astype(BF16)
    w_hi = wr.astype(BF16)
    w_lo = (wr - w_hi.astype(F32)).astype(BF16)
    logits = (jnp.dot(h_hi, w_hi, preferred_element_type=F32)
              + jnp.dot(h_lo, w_hi, preferred_element_type=F32)
              + jnp.dot(h_hi, w_lo, preferred_element_type=F32))
    lane = lax.broadcasted_iota(jnp.int32, logits.shape, 1)
    big = jnp.int32(1 << 20)
    ninf = jnp.float32(-jnp.inf)
    is_g = lane < N_GROUPS
    gl = jnp.where(is_g, logits, ninf)
    gm = jnp.max(gl, axis=-1, keepdims=True)
    g_idx = jnp.min(jnp.where(gl == gm, lane, big), axis=-1, keepdims=True)
    g_val = 1.0 / jnp.sum(jnp.where(is_g, jnp.exp(gl - gm), 0.0), axis=-1, keepdims=True)
    lo = N_GROUPS + g_idx * EXPERTS_PER_GROUP
    in_grp = (lane >= lo) & (lane < lo + EXPERTS_PER_GROUP)
    el = jnp.where(in_grp, logits, ninf)
    e1 = jnp.max(el, axis=-1, keepdims=True)
    i1 = jnp.min(jnp.where(el == e1, lane, big), axis=-1, keepdims=True)
    el2 = jnp.where(lane == i1, ninf, el)
    e2 = jnp.max(el2, axis=-1, keepdims=True)
    i2 = jnp.min(jnp.where(el2 == e2, lane, big), axis=-1, keepdims=True)
    t = jnp.exp(e2 - e1)
    p1 = 1.0 / (1.0 + t)
    p2 = t / (1.0 + t)
    eid_ref[...] = jnp.where(lane == 0, i1 - N_GROUPS, jnp.where(lane == 1, i2 - N_GROUPS, 0))
    gate_ref[...] = jnp.where(lane == 0, p1 * g_val, jnp.where(lane == 1, p2 * g_val, 0.0))


def _merge_route(x, y_ssm, y_attn, gates, wps, wpa, wo, n2w, wr, tm=256):
    s, d = x.shape
    nw = y_ssm.shape[1]
    const = lambda i: (0, 0)
    return pl.pallas_call(
        _merge_body,
        grid=(s // tm,),
        in_specs=[pl.BlockSpec((tm, d), lambda i: (i, 0)),
                  pl.BlockSpec((tm, nw), lambda i: (i, 0)),
                  pl.BlockSpec((tm, nw), lambda i: (i, 0)),
                  pl.BlockSpec((tm, d), lambda i: (i, 0)),
                  pl.BlockSpec((tm, d), lambda i: (i, 1)),
                  pl.BlockSpec((nw, d), const),
                  pl.BlockSpec((nw, d), const),
                  pl.BlockSpec((d, d), const),
                  pl.BlockSpec((1, d), const),
                  pl.BlockSpec((d, LANES), const)],
        out_specs=[pl.BlockSpec((tm, d), lambda i: (i, 0)),
                   pl.BlockSpec((tm, d), lambda i: (i, 0)),
                   pl.BlockSpec((tm, LANES), lambda i: (i, 0)),
                   pl.BlockSpec((tm, LANES), lambda i: (i, 0))],
        out_shape=[jax.ShapeDtypeStruct((s, d), F32),
                   jax.ShapeDtypeStruct((s, d), F32),
                   jax.ShapeDtypeStruct((s, LANES), jnp.int32),
                   jax.ShapeDtypeStruct((s, LANES), F32)],
        compiler_params=_cparams(("parallel",)),
        name="merge_route",
    )(x, y_ssm, y_attn, gates, gates, wps, wpa, wo, n2w.reshape(1, d).astype(F32), wr)


def _row_copy(src_hbm, row, dst_ref, r, sem):
    return pltpu.make_async_copy(src_hbm.at[pl.ds(row, 1), :], dst_ref.at[pl.ds(r, 1), :], sem)


def _expert_body(be_ref, nb_ref, tok_ref, h2_hbm, wg_ref, wu_ref, wd_ref, gw_ref, o_ref,
                 xb_ref, sem):
    b = pl.program_id(0)

    @pl.when(b < nb_ref[0])
    def _run():
        base = b * MOE_BLK

        def start(r, c):
            _row_copy(h2_hbm, tok_ref[base + r], xb_ref, r, sem).start()
            return c

        lax.fori_loop(0, MOE_BLK, start, 0)

        def wait(r, c):
            _row_copy(h2_hbm, 0, xb_ref, r, sem).wait()
            return c

        lax.fori_loop(0, MOE_BLK, wait, 0)
        xb = xb_ref[...].astype(BF16)
        g = jnp.dot(xb, wg_ref[0], preferred_element_type=F32)
        u = jnp.dot(xb, wu_ref[0], preferred_element_type=F32)
        a = (jax.nn.silu(g) * u).astype(BF16)
        y = jnp.dot(a, wd_ref[0], preferred_element_type=F32)
        o_ref[...] = y * gw_ref[...]

    @pl.when(b >= nb_ref[0])
    def _unused():
        o_ref[...] = jnp.zeros(o_ref.shape, o_ref.dtype)


def _experts(h2, block_e, n_used, buf_tok, buf_w, wg, wu, wd):
    s, d = h2.shape
    buf_len = buf_tok.shape[0]
    n_blocks = buf_len // MOE_BLK
    wmap = lambda b, be, nb, tok: (be[b], 0, 0)
    rmap = lambda b, be, nb, tok: (b, 0)
    return pl.pallas_call(
        _expert_body,
        grid_spec=pltpu.PrefetchScalarGridSpec(
            num_scalar_prefetch=3,
            grid=(n_blocks,),
            in_specs=[pl.BlockSpec(memory_space=pl.ANY),
                      pl.BlockSpec((1, d, D_FF), wmap),
                      pl.BlockSpec((1, d, D_FF), wmap),
                      pl.BlockSpec((1, D_FF, d), wmap),
                      pl.BlockSpec((MOE_BLK, 1), rmap)],
            out_specs=pl.BlockSpec((MOE_BLK, d), rmap),
            scratch_shapes=[pltpu.VMEM((MOE_BLK, d), F32), pltpu.SemaphoreType.DMA(())]),
        out_shape=jax.ShapeDtypeStruct((buf_len, d), F32),
        compiler_params=_cparams(("arbitrary",)),
        name="experts",
    )(block_e, n_used, buf_tok, h2, wg, wu, wd, buf_w.reshape(buf_len, 1))


def _combine_body(pos_ref, x1_ref, fw_ref, ys_hbm, o_ref, g0_ref, g1_ref, sem, *, tm):
    i = pl.program_id(0)
    base = i * tm

    def start(r, c):
        _row_copy(ys_hbm, pos_ref[2 * (base + r)], g0_ref, r, sem).start()
        _row_copy(ys_hbm, pos_ref[2 * (base + r) + 1], g1_ref, r, sem).start()
        return c

    lax.fori_loop(0, tm, start, 0)

    def wait(r, c):
        _row_copy(ys_hbm, 0, g0_ref, r, sem).wait()
        _row_copy(ys_hbm, 0, g1_ref, r, sem).wait()
        return c

    lax.fori_loop(0, tm, wait, 0)
    x = x1_ref[...] + (g0_ref[...] + g1_ref[...])
    y = x * lax.rsqrt(jnp.mean(x * x, axis=-1, keepdims=True) + NORM_EPS)
    o_ref[...] = y * fw_ref[...]


def _combine(x1, ys, pos, fw, tm=256):
    s, d = x1.shape
    return pl.pallas_call(
        functools.partial(_combine_body, tm=tm),
        grid_spec=pltpu.PrefetchScalarGridSpec(
            num_scalar_prefetch=1,
            grid=(s // tm,),
            in_specs=[pl.BlockSpec((tm, d), lambda i, p: (i, 0)),
                      pl.BlockSpec((1, d), lambda i, p: (0, 0)),
                      pl.BlockSpec(memory_space=pl.ANY)],
            out_specs=pl.BlockSpec((tm, d), lambda i, p: (i, 0)),
            scratch_shapes=[pltpu.VMEM((tm, d), F32), pltpu.VMEM((tm, d), F32),
                            pltpu.SemaphoreType.DMA(())]),
        out_shape=jax.ShapeDtypeStruct((s, d), F32),
        compiler_params=_cparams(("arbitrary",)),
        name="combine_norm",
    )(pos, x1, fw.reshape(1, d).astype(F32), ys)


def _dispatch_tables(expert_id, gate):
    n_tok = expert_id.shape[0]
    n_slots = n_tok * TOP_K
    e_flat = expert_id.reshape(n_slots)
    w_flat = gate.reshape(n_slots)
    tok_flat = jnp.repeat(jnp.arange(n_tok, dtype=jnp.int32), TOP_K)
    order = jnp.argsort(e_flat)
    sorted_e = e_flat[order]
    counts = jnp.bincount(e_flat, length=N_EXPERTS)
    starts = jnp.cumsum(counts) - counts
    padded = ((counts + MOE_BLK - 1) // MOE_BLK) * MOE_BLK
    padded_ends = jnp.cumsum(padded)
    padded_starts = padded_ends - padded
    dest = (padded_starts[sorted_e] + (jnp.arange(n_slots) - starts[sorted_e])).astype(jnp.int32)
    buf_len = n_slots + N_EXPERTS * MOE_BLK
    n_blocks = buf_len // MOE_BLK
    buf_tok = jnp.zeros((buf_len,), jnp.int32).at[dest].set(tok_flat[order])
    buf_w = jnp.zeros((buf_len,), F32).at[dest].set(w_flat[order])
    pos = jnp.zeros((n_slots,), jnp.int32).at[order].set(dest)
    block_start = jnp.arange(n_blocks) * MOE_BLK
    block_e = jnp.clip(jnp.searchsorted(padded_ends, block_start, side='right'),
                       0, N_EXPERTS - 1).astype(jnp.int32)
    n_used = (padded_ends[-1] // MOE_BLK).astype(jnp.int32).reshape(1)
    block_e = jnp.where(jnp.arange(n_blocks) < n_used[0], block_e,
                        block_e[jnp.maximum(n_used[0] - 1, 0)])
    return block_e, n_used, buf_tok, buf_w, pos


def kernel(x, positions, norm1_w, w_in, ssm_lambda_re, ssm_lambda_im, ssm_log_dt, ssm_b_re, ssm_b_im, ssm_c_re, ssm_c_im, ssm_d, ssm_glu_w, ssm_glu_b, attn_lambda_q1, attn_lambda_k1, attn_lambda_q2, attn_lambda_k2, attn_subln_w, w_proj_ssm, w_proj_attn, w_out, norm2_w, router_group_w, router_expert_w, expert_w_gate, expert_w_up, expert_w_down, final_norm_w):
    bsz, seq, d = x.shape
    depth = norm1_w.shape[0]
    xs = x.reshape(bsz * seq, d)
    pos = positions.reshape(bsz * seq)
    o_q = SSM_WIDTH
    o_v = o_q + 2 * ATTN_WIDTH
    o_g = o_v + ATTN_WIDTH
    for l in range(depth):
        h = _rmsnorm(xs, norm1_w[l].astype(F32), BF16)
        w_in_b = w_in[l].astype(BF16)
        u = _mm(h, w_in_b, 0, SSM_WIDTH, "none", F32)
        qk = _mm_rope(h, w_in_b, pos, o_q)
        v = _mm(h, w_in_b, o_v, ATTN_WIDTH, "none", BF16)
        gates = _mm(h, w_in_b, o_g, 2 * D_MODEL, "sigmoid", F32)

        tables = _s5_tables(ssm_lambda_re[l], ssm_lambda_im[l], ssm_log_dt[l],
                            ssm_b_re[l], ssm_b_im[l], ssm_c_re[l], ssm_c_im[l])
        y = _s5_core(u, tables, ssm_d[l])
        y_ssm = _glu(y, ssm_glu_w[l].astype(BF16), ssm_glu_b[l])

        lam_params = jnp.stack([attn_lambda_q1[l], attn_lambda_k1[l],
                                attn_lambda_q2[l], attn_lambda_k2[l]]).astype(F32)
        y_attn = _diff_attention(qk, v, lam_params, attn_subln_w[l])

        wr = jnp.concatenate([router_group_w[l], router_expert_w[l]], axis=1).astype(F32)
        wr = jnp.pad(wr, ((0, 0), (0, LANES - wr.shape[1])))
        x1, h2, eid, gate = _merge_route(
            xs, y_ssm, y_attn, gates, w_proj_ssm[l].astype(BF16), w_proj_attn[l].astype(BF16),
            w_out[l].astype(BF16), norm2_w[l], wr)

        block_e, n_used, buf_tok, buf_w, slot_pos = _dispatch_tables(eid[:, :TOP_K],
                                                                     gate[:, :TOP_K])
        ys = _experts(h2, block_e, n_used, buf_tok, buf_w, expert_w_gate[l].astype(BF16),
                      expert_w_up[l].astype(BF16), expert_w_down[l].astype(BF16))
        last = l == depth - 1
        assert last, "DEPTH > 1 needs an un-normalised combine"
        xs = _combine(x1, ys, slot_pos, final_norm_w)
    return xs.reshape(bsz, seq, d)
```

```python
import functools
import math

import jax
import jax.numpy as jnp
from jax import lax
from jax.experimental import pallas as pl
from jax.experimental.pallas import tpu as pltpu

F32 = jnp.float32
BF16 = jnp.bfloat16

D_MODEL = 2048
SSM_WIDTH = 1024
SSM_GROUP = 16
SSM_GROUPS = 64
SSM_STATE = 64
ATTN_WIDTH = 1024
ATTN_HEADS = 8
HEAD_DIM = 64
ROPE_THETA = 10000.0
N_GROUPS = 4
EXPERTS_PER_GROUP = 8
N_EXPERTS = 32
TOP_K = 2
D_FF = 512
NORM_EPS = 1e-6
SUBLN_EPS = 1e-5
LAMBDA_INIT = 0.8 - 0.6 * math.exp(-0.3 * 0)

LANES = 128
VMEM_LIMIT = 48 * 1024 * 1024

SSM_CHUNK = 16
SSM_GB = 8
MOE_BLK = 128


def _cparams(sem):
    return pltpu.CompilerParams(dimension_semantics=sem, vmem_limit_bytes=VMEM_LIMIT)


def _rmsnorm_body(x_ref, w_ref, o_ref):
    x = x_ref[...]
    y = x * lax.rsqrt(jnp.mean(x * x, axis=-1, keepdims=True) + NORM_EPS)
    o_ref[...] = (y * w_ref[...]).astype(o_ref.dtype)


def _rmsnorm(x, w, out_dtype, tm=512):
    s, d = x.shape
    return pl.pallas_call(
        _rmsnorm_body,
        grid=(s // tm,),
        in_specs=[pl.BlockSpec((tm, d), lambda i: (i, 0)),
                  pl.BlockSpec((1, d), lambda i: (0, 0))],
        out_specs=pl.BlockSpec((tm, d), lambda i: (i, 0)),
        out_shape=jax.ShapeDtypeStruct((s, d), out_dtype),
        compiler_params=_cparams(("parallel",)),
        name="rmsnorm",
    )(x, w.reshape(1, d))


def _mm_body(h_ref, w_ref, o_ref, *, act):
    acc = jnp.dot(h_ref[...], w_ref[...], preferred_element_type=F32)
    if act == "sigmoid":
        acc = jax.nn.sigmoid(acc)
    o_ref[...] = acc.astype(o_ref.dtype)


def _mm(h, w, col0, ncols, act, out_dtype, tm=512, tn=1024):
    s, k = h.shape
    jb = col0 // tn
    return pl.pallas_call(
        functools.partial(_mm_body, act=act),
        grid=(ncols // tn, s // tm),
        in_specs=[pl.BlockSpec((tm, k), lambda j, i: (i, 0)),
                  pl.BlockSpec((k, tn), lambda j, i: (0, j + jb))],
        out_specs=pl.BlockSpec((tm, tn), lambda j, i: (i, j)),
        out_shape=jax.ShapeDtypeStruct((s, ncols), out_dtype),
        compiler_params=_cparams(("parallel", "parallel")),
        name="mm_" + act,
    )(h, w)


def _mm_vones_body(h_ref, w_ref, o_ref):
    acc = jnp.dot(h_ref[...], w_ref[...], preferred_element_type=F32)
    hw = 2 * HEAD_DIM
    ones = jnp.ones((acc.shape[0], hw), o_ref.dtype)
    for hd in range(acc.shape[1] // hw):
        o_ref[:, 2 * hd * hw:(2 * hd + 1) * hw] = acc[:, hd * hw:(hd + 1) * hw].astype(o_ref.dtype)
        o_ref[:, (2 * hd + 1) * hw:(2 * hd + 2) * hw] = ones


def _mm_vones(h, w, col0, tm=512):
    s, k = h.shape
    tn = ATTN_WIDTH
    jb = col0 // tn
    return pl.pallas_call(
        _mm_vones_body,
        grid=(s // tm,),
        in_specs=[pl.BlockSpec((tm, k), lambda i: (i, 0)),
                  pl.BlockSpec((k, tn), lambda i: (0, jb))],
        out_specs=pl.BlockSpec((tm, 2 * tn), lambda i: (i, 0)),
        out_shape=jax.ShapeDtypeStruct((s, 2 * tn), BF16),
        compiler_params=_cparams(("parallel",)),
        name="mm_vones",
    )(h, w)


def _mm_rope_body(h_ref, w_ref, pos_ref, invf_ref, o_ref, *, tn):
    j = pl.program_id(0)
    acc = jnp.dot(h_ref[...], w_ref[...], preferred_element_type=F32)
    scale = jnp.where(j == 0, HEAD_DIM ** -0.5, 1.0).astype(F32)
    ang = pos_ref[...].astype(F32) * invf_ref[...]
    lane = lax.broadcasted_iota(jnp.int32, ang.shape, 1)
    low = (lane % HEAD_DIM) < (HEAD_DIM // 2)
    cos = jnp.cos(ang) * scale
    sin = jnp.where(low, -jnp.sin(ang), jnp.sin(ang)) * scale
    for c in range(tn // LANES):
        t = acc[:, c * LANES:(c + 1) * LANES]
        partner = jnp.where(low, pltpu.roll(t, LANES - HEAD_DIM // 2, 1),
                            pltpu.roll(t, HEAD_DIM // 2, 1))
        o_ref[:, c * LANES:(c + 1) * LANES] = (t * cos + partner * sin).astype(o_ref.dtype)


def _mm_rope(h, w, positions, col0, tm=512, tn=1024):
    s, k = h.shape
    jb = col0 // tn
    inv_freq = 1.0 / (ROPE_THETA ** (jnp.arange(0, HEAD_DIM, 2, dtype=F32) / HEAD_DIM))
    invf = jnp.tile(inv_freq, LANES // (HEAD_DIM // 2)).reshape(1, LANES)
    return pl.pallas_call(
        functools.partial(_mm_rope_body, tn=tn),
        grid=(2, s // tm),
        in_specs=[pl.BlockSpec((tm, k), lambda j, i: (i, 0)),
                  pl.BlockSpec((k, tn), lambda j, i: (0, j + jb)),
                  pl.BlockSpec((tm, 1), lambda j, i: (i, 0)),
                  pl.BlockSpec((1, LANES), lambda j, i: (0, 0))],
        out_specs=pl.BlockSpec((tm, tn), lambda j, i: (i, j)),
        out_shape=jax.ShapeDtypeStruct((s, 2 * tn), BF16),
        compiler_params=_cparams(("parallel", "parallel")),
        name="mm_rope",
    )(h, w, positions.reshape(s, 1), invf)


def _s5_tables(lam_re, lam_im, log_dt, b_re, b_im, c_re, c_im):
    L, G, P, M = SSM_CHUNK, SSM_GROUPS, SSM_STATE, SSM_GROUP
    hi = lax.Precision.HIGHEST
    lr, li = lam_re.astype(F32), lam_im.astype(F32)
    dt = jnp.exp(log_dt.astype(F32))[:, None]
    n = jnp.arange(L + 1, dtype=F32)[:, None, None]
    mag = jnp.exp(lr * dt * n)
    pr = mag * jnp.cos(li * dt * n)
    pi = mag * jnp.sin(li * dt * n)
    nr, ni = pr[1] - 1.0, pi[1]
    den = lr * lr + li * li
    f_re = (nr * lr + ni * li) / den
    f_im = (ni * lr - nr * li) / den
    br, bi = b_re.astype(F32), b_im.astype(F32)
    bbr = f_re[..., None] * br - f_im[..., None] * bi
    bbi = f_re[..., None] * bi + f_im[..., None] * br
    cr, ci = c_re.astype(F32), c_im.astype(F32)

    abr = pr[..., None] * bbr - pi[..., None] * bbi
    abi = pr[..., None] * bbi + pi[..., None] * bbr
    kk = (jnp.einsum('gop,ngpi->ngoi', cr, abr[:L], precision=hi)
          - jnp.einsum('gop,ngpi->ngoi', ci, abi[:L], precision=hi))
    ii = jnp.arange(L)[:, None]
    jj = jnp.arange(L)[None, :]
    tau = jnp.clip(jj - ii, 0, L - 1)
    t5 = kk[tau] * (jj >= ii)[:, :, None, None, None].astype(F32)
    t_tab = jnp.transpose(t5, (2, 0, 4, 1, 3)).reshape(G, L * M, L * M)

    wr = jnp.transpose(abr[:L][::-1], (1, 0, 3, 2)).reshape(G, L * M, P)
    wi = jnp.transpose(abi[:L][::-1], (1, 0, 3, 2)).reshape(G, L * M, P)
    z = jnp.zeros_like(wr[0::2])
    w_top = jnp.concatenate([wr[0::2], z, wi[0::2], z], axis=-1)
    w_bot = jnp.concatenate([z, wr[1::2], z, wi[1::2]], axis=-1)
    w_tab = jnp.concatenate([w_top, w_bot], axis=1)

    ar1, ai1 = pr[1:], pi[1:]
    vr = cr[None] * ar1[:, :, None, :] - ci[None] * ai1[:, :, None, :]
    vi = -(cr[None] * ai1[:, :, None, :] + ci[None] * ar1[:, :, None, :])
    vr = jnp.transpose(vr, (1, 3, 0, 2)).reshape(G, P, L * M)
    vi = jnp.transpose(vi, (1, 3, 0, 2)).reshape(G, P, L * M)
    zv = jnp.zeros_like(vr[0::2])
    v_tab = jnp.concatenate([
        jnp.concatenate([vr[0::2], zv], axis=-1),
        jnp.concatenate([zv, vr[1::2]], axis=-1),
        jnp.concatenate([vi[0::2], zv], axis=-1),
        jnp.concatenate([zv, vi[1::2]], axis=-1)], axis=1)

    al = jnp.stack([pr[L].reshape(G * P), pi[L].reshape(G * P)])
    return t_tab.astype(BF16), w_tab.astype(BF16), v_tab.astype(BF16), al


def _s5_body(u_ref, t_ref, w_ref, v_ref, al_ref, d_ref, y_ref, er_ref, ei_ref, xr_ref, xi_ref):
    nc = u_ref.shape[0]
    lm = SSM_CHUNK * SSM_GROUP
    pw = 2 * SSM_STATE
    npair = SSM_GB // 2
    for pp in range(npair):
        up = u_ref[:, pp * 2 * lm:(pp + 1) * 2 * lm].astype(BF16)
        e = jnp.dot(up, w_ref[pp], preferred_element_type=F32)
        er_ref[:, pp * pw:(pp + 1) * pw] = e[:, :pw]
        ei_ref[:, pp * pw:(pp + 1) * pw] = e[:, pw:]

    ar = al_ref[0:1, :]
    ai = al_ref[1:2, :]

    def step(c, carry):
        xr, xi = carry
        xr_ref[pl.ds(c, 1), :] = xr
        xi_ref[pl.ds(c, 1), :] = xi
        er = er_ref[pl.ds(c, 1), :]
        ei = ei_ref[pl.ds(c, 1), :]
        return ar * xr - ai * xi + er, ar * xi + ai * xr + ei

    zero = jnp.zeros((1, npair * pw), F32)
    lax.fori_loop(0, nc, step, (zero, zero))

    for pp in range(npair):
        xin = jnp.concatenate([xr_ref[:, pp * pw:(pp + 1) * pw],
                               xi_ref[:, pp * pw:(pp + 1) * pw]], axis=1).astype(BF16)
        yc = jnp.dot(xin, v_ref[pp], preferred_element_type=F32)
        for q in range(2):
            g = pp * 2 + q
            uf = u_ref[:, g * lm:(g + 1) * lm]
            yi = jnp.dot(uf.astype(BF16), t_ref[g], preferred_element_type=F32)
            y = yi + yc[:, q * lm:(q + 1) * lm] + d_ref[:, g * lm:(g + 1) * lm] * uf
            y_ref[:, g * lm:(g + 1) * lm] = jax.nn.gelu(y)


def _s5_core(u, tables, d_skip):
    s = u.shape[0]
    L, G, M, P = SSM_CHUNK, SSM_GROUPS, SSM_GROUP, SSM_STATE
    nc = s // L
    lm = L * M
    t_tab, w_tab, v_tab, al = tables
    u2 = u.reshape(nc, L, G, M).transpose(0, 2, 1, 3).reshape(nc, G * lm)
    d2 = jnp.broadcast_to(d_skip.astype(F32).reshape(1, G, 1, M), (1, G, L, M)).reshape(1, G * lm)
    gb = SSM_GB
    y2 = pl.pallas_call(
        _s5_body,
        grid=(G // gb,),
        in_specs=[pl.BlockSpec((nc, gb * lm), lambda b: (0, b)),
                  pl.BlockSpec((gb, lm, lm), lambda b: (b, 0, 0)),
                  pl.BlockSpec((gb // 2, 2 * lm, 4 * P), lambda b: (b, 0, 0)),
                  pl.BlockSpec((gb // 2, 4 * P, 2 * lm), lambda b: (b, 0, 0)),
                  pl.BlockSpec((2, gb * P), lambda b: (0, b)),
                  pl.BlockSpec((1, gb * lm), lambda b: (0, b))],
        out_specs=pl.BlockSpec((nc, gb * lm), lambda b: (0, b)),
        out_shape=jax.ShapeDtypeStruct((nc, G * lm), F32),
        scratch_shapes=[pltpu.VMEM((nc, gb * P), F32) for _ in range(4)],
        compiler_params=_cparams(("parallel",)),
        name="s5_scan",
    )(u2, t_tab, w_tab, v_tab, al, d2)
    return y2.reshape(nc, G, L, M).transpose(0, 2, 1, 3).reshape(s, G * M)


def _glu_body(y_ref, w_ref, b_ref, o_ref):
    y = y_ref[...]
    z = jnp.dot(y.astype(BF16), w_ref[...], preferred_element_type=F32) + b_ref[...]
    o_ref[...] = (y * jax.nn.sigmoid(z)).astype(o_ref.dtype)


def _glu(y, w, b, tm=512):
    s, n = y.shape
    return pl.pallas_call(
        _glu_body,
        grid=(s // tm,),
        in_specs=[pl.BlockSpec((tm, n), lambda i: (i, 0)),
                  pl.BlockSpec((n, n), lambda i: (0, 0)),
                  pl.BlockSpec((1, n), lambda i: (0, 0))],
        out_specs=pl.BlockSpec((tm, n), lambda i: (i, 0)),
        out_shape=jax.ShapeDtypeStruct((s, n), BF16),
        compiler_params=_cparams(("parallel",)),
        name="s5_glu",
    )(y, w, b.reshape(1, n).astype(F32))


def _attn_body(q_ref, k_ref, v_ref, lam_ref, sw_ref, o_ref, qs_ref, m_ref, acc_ref, *, tq):
    qi = pl.program_id(1)
    hw = 2 * HEAD_DIM
    q = q_ref[...]
    lane = lax.broadcasted_iota(jnp.int32, q.shape, 1)
    zero = jnp.zeros_like(q)
    qs_ref[0] = jnp.where(lane < HEAD_DIM, q, zero)
    qs_ref[1] = jnp.where(lane >= HEAD_DIM, q, zero)
    m_ref[...] = jnp.full(m_ref.shape, -jnp.inf, F32)
    acc_ref[...] = jnp.zeros(acc_ref.shape, F32)

    def chunk(j, masked):
        off = pl.multiple_of(j * tq, tq)
        k = k_ref[pl.ds(off, tq), :]
        v = v_ref[pl.ds(off, tq), :]
        if masked:
            row = lax.broadcasted_iota(jnp.int32, (tq, tq), 0)
            col = lax.broadcasted_iota(jnp.int32, (tq, tq), 1)
            keep = col <= row
        for c in range(2):
            s = lax.dot_general(qs_ref[c], k, (((1,), (1,)), ((), ())),
                                preferred_element_type=F32)
            if masked:
                s = jnp.where(keep, s, -jnp.inf)
            m_old = m_ref[c]
            m_new = jnp.maximum(m_old, jnp.max(s, axis=-1, keepdims=True))
            alpha = jnp.exp(m_old - m_new)
            p = jnp.exp(s - m_new).astype(BF16)
            acc_ref[c] = alpha * acc_ref[c] + jnp.dot(p, v, preferred_element_type=F32)
            m_ref[c] = m_new

    def full_chunk(j, carry):
        chunk(j, False)
        return carry

    lax.fori_loop(0, qi, full_chunk, 0)
    chunk(qi, True)

    lam = (jnp.exp(jnp.sum(lam_ref[0:1, :] * lam_ref[1:2, :]))
           - jnp.exp(jnp.sum(lam_ref[2:3, :] * lam_ref[3:4, :])) + LAMBDA_INIT)
    a1 = acc_ref[0]
    a2 = acc_ref[1]
    o = a1[:, :hw] / a1[:, hw:] - lam * (a2[:, :hw] / a2[:, hw:])
    o = o * lax.rsqrt(jnp.mean(o * o, axis=-1, keepdims=True) + SUBLN_EPS)
    o_ref[...] = (o * sw_ref[...] * (1.0 - LAMBDA_INIT)).astype(o_ref.dtype)


def _diff_attention(qk, v1, lam_params, subln_w, tq=512):
    s = qk.shape[0]
    nq = s // tq
    hw = 2 * HEAD_DIM
    kcol0 = ATTN_WIDTH // hw
    return pl.pallas_call(
        functools.partial(_attn_body, tq=tq),
        grid=(ATTN_HEADS, nq),
        in_specs=[pl.BlockSpec((tq, hw), lambda h, i: (i, h)),
                  pl.BlockSpec((s, hw), lambda h, i: (0, kcol0 + h)),
                  pl.BlockSpec((s, 2 * hw), lambda h, i: (0, h)),
                  pl.BlockSpec((4, HEAD_DIM), lambda h, i: (0, 0)),
                  pl.BlockSpec((1, hw), lambda h, i: (0, 0))],
        out_specs=pl.BlockSpec((tq, hw), lambda h, i: (i, h)),
        out_shape=jax.ShapeDtypeStruct((s, ATTN_WIDTH), BF16),
        scratch_shapes=[pltpu.VMEM((2, tq, hw), BF16),
                        pltpu.VMEM((2, tq, 1), F32),
                        pltpu.VMEM((2, tq, 2 * hw), F32)],
        compiler_params=_cparams(("parallel", "arbitrary")),
        name="diff_attn",
    )(qk, qk, v1, lam_params, subln_w.reshape(1, hw).astype(F32))


def _merge_body(x_ref, ys_ref, ya_ref, gs_ref, ga_ref, wps_ref, wpa_ref, wo_ref, n2_ref, wr_ref,
                x1_ref, h2_ref, eid_ref, gate_ref):
    ps = jnp.dot(ys_ref[...], wps_ref[...], preferred_element_type=F32)
    pa = jnp.dot(ya_ref[...], wpa_ref[...], preferred_element_type=F32)
    merged = gs_ref[...].astype(F32) * ps + ga_ref[...].astype(F32) * pa
    x1 = x_ref[...] + jnp.dot(merged.astype(BF16), wo_ref[...], preferred_element_type=F32)
    x1_ref[...] = x1
    h2 = x1 * lax.rsqrt(jnp.mean(x1 * x1, axis=-1, keepdims=True) + NORM_EPS) * n2_ref[...]
    h2_ref[...] = h2
    wr = wr_ref[...]
    h_hi = h2.astype(BF16)
    h_lo = (h2 - h_hi.astype(F32)).astype(BF16)
    w_hi = wr.astype(BF16)
    w_lo = (wr - w_hi.astype(F32)).astype(BF16)
    logits = (jnp.dot(h_hi, w_hi, preferred_element_type=F32)
              + jnp.dot(h_lo, w_hi, preferred_element_type=F32)
              + jnp.dot(h_hi, w_lo, preferred_element_type=F32))
    lane = lax.broadcasted_iota(jnp.int32, logits.shape, 1)
    big = jnp.int32(1 << 20)
    ninf = jnp.float32(-jnp.inf)
    is_g = lane < N_GROUPS
    gl = jnp.where(is_g, logits, ninf)
    gm = jnp.max(gl, axis=-1, keepdims=True)
    g_idx = jnp.min(jnp.where(gl == gm, lane, big), axis=-1, keepdims=True)
    g_val = 1.0 / jnp.sum(jnp.where(is_g, jnp.exp(gl - gm), 0.0), axis=-1, keepdims=True)
    lo = N_GROUPS + g_idx * EXPERTS_PER_GROUP
    in_grp = (lane >= lo) & (lane < lo + EXPERTS_PER_GROUP)
    el = jnp.where(in_grp, logits, ninf)
    e1 = jnp.max(el, axis=-1, keepdims=True)
    i1 = jnp.min(jnp.where(el == e1, lane, big), axis=-1, keepdims=True)
    el2 = jnp.where(lane == i1, ninf, el)
    e2 = jnp.max(el2, axis=-1, keepdims=True)
    i2 = jnp.min(jnp.where(el2 == e2, lane, big), axis=-1, keepdims=True)
    t = jnp.exp(e2 - e1)
    p1 = 1.0 / (1.0 + t)
    p2 = t / (1.0 + t)
    eid_ref[...] = jnp.where(lane == 0, i1 - N_GROUPS, jnp.where(lane == 1, i2 - N_GROUPS, 0))
    gate_ref[...] = jnp.where(lane == 0, p1 * g_val, jnp.where(lane == 1, p2 * g_val, 0.0))


def _merge_route(x, y_ssm, y_attn, gates, wps, wpa, wo, n2w, wr, tm=256):
    s, d = x.shape
    nw = y_ssm.shape[1]
    const = lambda i: (0, 0)
    return pl.pallas_call(
        _merge_body,
        grid=(s // tm,),
        in_specs=[pl.BlockSpec((tm, d), lambda i: (i, 0)),
                  pl.BlockSpec((tm, nw), lambda i: (i, 0)),
                  pl.BlockSpec((tm, nw), lambda i: (i, 0)),
                  pl.BlockSpec((tm, d), lambda i: (i, 0)),
                  pl.BlockSpec((tm, d), lambda i: (i, 1)),
                  pl.BlockSpec((nw, d), const),
                  pl.BlockSpec((nw, d), const),
                  pl.BlockSpec((d, d), const),
                  pl.BlockSpec((1, d), const),
                  pl.BlockSpec((d, LANES), const)],
        out_specs=[pl.BlockSpec((tm, d), lambda i: (i, 0)),
                   pl.BlockSpec((tm, d), lambda i: (i, 0)),
                   pl.BlockSpec((tm, LANES), lambda i: (i, 0)),
                   pl.BlockSpec((tm, LANES), lambda i: (i, 0))],
        out_shape=[jax.ShapeDtypeStruct((s, d), F32),
                   jax.ShapeDtypeStruct((s, d), F32),
                   jax.ShapeDtypeStruct((s, LANES), jnp.int32),
                   jax.ShapeDtypeStruct((s, LANES), F32)],
        compiler_params=_cparams(("parallel",)),
        name="merge_route",
    )(x, y_ssm, y_attn, gates, gates, wps, wpa, wo, n2w.reshape(1, d).astype(F32), wr)


def _row_copy(src_hbm, row, dst_ref, r, sem):
    return pltpu.make_async_copy(src_hbm.at[pl.ds(row, 1), :], dst_ref.at[pl.ds(r, 1), :], sem)


def _expert_body(be_ref, nb_ref, tok_ref, h2_hbm, wg_ref, wu_ref, wd_ref, gw_ref, o_ref,
                 xb_ref, sem):
    b = pl.program_id(0)

    @pl.when(b < nb_ref[0])
    def _run():
        base = b * MOE_BLK

        def start(r, c):
            _row_copy(h2_hbm, tok_ref[base + r], xb_ref, r, sem).start()
            return c

        lax.fori_loop(0, MOE_BLK, start, 0)

        def wait(r, c):
            _row_copy(h2_hbm, 0, xb_ref, r, sem).wait()
            return c

        lax.fori_loop(0, MOE_BLK, wait, 0)
        xb = xb_ref[...].astype(BF16)
        g = jnp.dot(xb, wg_ref[0], preferred_element_type=F32)
        u = jnp.dot(xb, wu_ref[0], preferred_element_type=F32)
        a = (jax.nn.silu(g) * u).astype(BF16)
        y = jnp.dot(a, wd_ref[0], preferred_element_type=F32)
        o_ref[...] = y * gw_ref[...]

    @pl.when(b >= nb_ref[0])
    def _unused():
        o_ref[...] = jnp.zeros(o_ref.shape, o_ref.dtype)


def _experts(h2, block_e, n_used, buf_tok, buf_w, wg, wu, wd):
    s, d = h2.shape
    buf_len = buf_tok.shape[0]
    n_blocks = buf_len // MOE_BLK
    wmap = lambda b, be, nb, tok: (be[b], 0, 0)
    rmap = lambda b, be, nb, tok: (b, 0)
    return pl.pallas_call(
        _expert_body,
        grid_spec=pltpu.PrefetchScalarGridSpec(
            num_scalar_prefetch=3,
            grid=(n_blocks,),
            in_specs=[pl.BlockSpec(memory_space=pl.ANY),
                      pl.BlockSpec((1, d, D_FF), wmap),
                      pl.BlockSpec((1, d, D_FF), wmap),
                      pl.BlockSpec((1, D_FF, d), wmap),
                      pl.BlockSpec((MOE_BLK, 1), rmap)],
            out_specs=pl.BlockSpec((MOE_BLK, d), rmap),
            scratch_shapes=[pltpu.VMEM((MOE_BLK, d), F32), pltpu.SemaphoreType.DMA(())]),
        out_shape=jax.ShapeDtypeStruct((buf_len, d), F32),
        compiler_params=_cparams(("arbitrary",)),
        name="experts",
    )(block_e, n_used, buf_tok, h2, wg, wu, wd, buf_w.reshape(buf_len, 1))


def _combine_body(pos_ref, x1_ref, fw_ref, ys_hbm, o_ref, g0_ref, g1_ref, sem, *, tm):
    i = pl.program_id(0)
    base = i * tm

    def start(r, c):
        _row_copy(ys_hbm, pos_ref[2 * (base + r)], g0_ref, r, sem).start()
        _row_copy(ys_hbm, pos_ref[2 * (base + r) + 1], g1_ref, r, sem).start()
        return c

    lax.fori_loop(0, tm, start, 0)

    def wait(r, c):
        _row_copy(ys_hbm, 0, g0_ref, r, sem).wait()
        _row_copy(ys_hbm, 0, g1_ref, r, sem).wait()
        return c

    lax.fori_loop(0, tm, wait, 0)
    x = x1_ref[...] + (g0_ref[...] + g1_ref[...])
    y = x * lax.rsqrt(jnp.mean(x * x, axis=-1, keepdims=True) + NORM_EPS)
    o_ref[...] = y * fw_ref[...]


def _combine(x1, ys, pos, fw, tm=256):
    s, d = x1.shape
    return pl.pallas_call(
        functools.partial(_combine_body, tm=tm),
        grid_spec=pltpu.PrefetchScalarGridSpec(
            num_scalar_prefetch=1,
            grid=(s // tm,),
            in_specs=[pl.BlockSpec((tm, d), lambda i, p: (i, 0)),
                      pl.BlockSpec((1, d), lambda i, p: (0, 0)),
                      pl.BlockSpec(memory_space=pl.ANY)],
            out_specs=pl.BlockSpec((tm, d), lambda i, p: (i, 0)),
            scratch_shapes=[pltpu.VMEM((tm, d), F32), pltpu.VMEM((tm, d), F32),
                            pltpu.SemaphoreType.DMA(())]),
        out_shape=jax.ShapeDtypeStruct((s, d), F32),
        compiler_params=_cparams(("arbitrary",)),
        name="combine_norm",
    )(pos, x1, fw.reshape(1, d).astype(F32), ys)


def _dispatch_tables(expert_id, gate):
    n_tok = expert_id.shape[0]
    n_slots = n_tok * TOP_K
    e_flat = expert_id.reshape(n_slots)
    w_flat = gate.reshape(n_slots)
    tok_flat = jnp.repeat(jnp.arange(n_tok, dtype=jnp.int32), TOP_K)
    order = jnp.argsort(e_flat)
    sorted_e = e_flat[order]
    counts = jnp.bincount(e_flat, length=N_EXPERTS)
    starts = jnp.cumsum(counts) - counts
    padded = ((counts + MOE_BLK - 1) // MOE_BLK) * MOE_BLK
    padded_ends = jnp.cumsum(padded)
    padded_starts = padded_ends - padded
    dest = (padded_starts[sorted_e] + (jnp.arange(n_slots) - starts[sorted_e])).astype(jnp.int32)
    buf_len = n_slots + N_EXPERTS * MOE_BLK
    n_blocks = buf_len // MOE_BLK
    buf_tok = jnp.zeros((buf_len,), jnp.int32).at[dest].set(tok_flat[order])
    buf_w = jnp.zeros((buf_len,), F32).at[dest].set(w_flat[order])
    pos = jnp.zeros((n_slots,), jnp.int32).at[order].set(dest)
    block_start = jnp.arange(n_blocks) * MOE_BLK
    block_e = jnp.clip(jnp.searchsorted(padded_ends, block_start, side='right'),
                       0, N_EXPERTS - 1).astype(jnp.int32)
    n_used = (padded_ends[-1] // MOE_BLK).astype(jnp.int32).reshape(1)
    block_e = jnp.where(jnp.arange(n_blocks) < n_used[0], block_e,
                        block_e[jnp.maximum(n_used[0] - 1, 0)])
    return block_e, n_used, buf_tok, buf_w, pos


def kernel(x, positions, norm1_w, w_in, ssm_lambda_re, ssm_lambda_im, ssm_log_dt, ssm_b_re, ssm_b_im, ssm_c_re, ssm_c_im, ssm_d, ssm_glu_w, ssm_glu_b, attn_lambda_q1, attn_lambda_k1, attn_lambda_q2, attn_lambda_k2, attn_subln_w, w_proj_ssm, w_proj_attn, w_out, norm2_w, router_group_w, router_expert_w, expert_w_gate, expert_w_up, expert_w_down, final_norm_w):
    bsz, seq, d = x.shape
    depth = norm1_w.shape[0]
    xs = x.reshape(bsz * seq, d)
    pos = positions.reshape(bsz * seq)
    o_q = SSM_WIDTH
    o_v = o_q + 2 * ATTN_WIDTH
    o_g = o_v + ATTN_WIDTH
    for l in range(depth):
        h = _rmsnorm(xs, norm1_w[l].astype(F32), BF16)
        w_in_b = w_in[l].astype(BF16)
        u = _mm(h, w_in_b, 0, SSM_WIDTH, "none", F32)
        qk = _mm_rope(h, w_in_b, pos, o_q)
        v = _mm_vones(h, w_in_b, o_v)
        gates = _mm(h, w_in_b, o_g, 2 * D_MODEL, "sigmoid", F32)

        tables = _s5_tables(ssm_lambda_re[l], ssm_lambda_im[l], ssm_log_dt[l],
                            ssm_b_re[l], ssm_b_im[l], ssm_c_re[l], ssm_c_im[l])
        y = _s5_core(u, tables, ssm_d[l])
        y_ssm = _glu(y, ssm_glu_w[l].astype(BF16), ssm_glu_b[l])

        lam_params = jnp.stack([attn_lambda_q1[l], attn_lambda_k1[l],
                                attn_lambda_q2[l], attn_lambda_k2[l]]).astype(F32)
        y_attn = _diff_attention(qk, v, lam_params, attn_subln_w[l])

        wr = jnp.concatenate([router_group_w[l], router_expert_w[l]], axis=1).astype(F32)
        wr = jnp.pad(wr, ((0, 0), (0, LANES - wr.shape[1])))
        x1, h2, eid, gate = _merge_route(
            xs, y_ssm, y_attn, gates, w_proj_ssm[l].astype(BF16), w_proj_attn[l].astype(BF16),
            w_out[l].astype(BF16), norm2_w[l], wr)

        block_e, n_used, buf_tok, buf_w, slot_pos = _dispatch_tables(eid[:, :TOP_K],
                                                                     gate[:, :TOP_K])
        ys = _experts(h2, block_e, n_used, buf_tok, buf_w, expert_w_gate[l].astype(BF16),
                      expert_w_up[l].astype(BF16), expert_w_down[l].astype(BF16))
        last = l == depth - 1
        assert last, "DEPTH > 1 needs an un-normalised combine"
        xs = _combine(x1, ys, slot_pos, final_norm_w)
    return xs.reshape(bsz, seq, d)
```

```python
import functools
import math

import jax
import jax.numpy as jnp
from jax import lax
from jax.experimental import pallas as pl
from jax.experimental.pallas import tpu as pltpu

F32 = jnp.float32
BF16 = jnp.bfloat16

D_MODEL = 2048
SSM_WIDTH = 1024
SSM_GROUP = 16
SSM_GROUPS = 64
SSM_STATE = 64
ATTN_WIDTH = 1024
ATTN_HEADS = 8
HEAD_DIM = 64
ROPE_THETA = 10000.0
N_GROUPS = 4
EXPERTS_PER_GROUP = 8
N_EXPERTS = 32
TOP_K = 2
D_FF = 512
NORM_EPS = 1e-6
SUBLN_EPS = 1e-5
LAMBDA_INIT = 0.8 - 0.6 * math.exp(-0.3 * 0)

LANES = 128
VMEM_LIMIT = 48 * 1024 * 1024

SSM_CHUNK = 16
SSM_GB = 8
MOE_BLK = 128


def _cparams(sem):
    return pltpu.CompilerParams(dimension_semantics=sem, vmem_limit_bytes=VMEM_LIMIT)


def _rmsnorm_body(x_ref, w_ref, o_ref):
    x = x_ref[...]
    y = x * lax.rsqrt(jnp.mean(x * x, axis=-1, keepdims=True) + NORM_EPS)
    o_ref[...] = (y * w_ref[...]).astype(o_ref.dtype)


def _rmsnorm(x, w, out_dtype, tm=512):
    s, d = x.shape
    return pl.pallas_call(
        _rmsnorm_body,
        grid=(s // tm,),
        in_specs=[pl.BlockSpec((tm, d), lambda i: (i, 0)),
                  pl.BlockSpec((1, d), lambda i: (0, 0))],
        out_specs=pl.BlockSpec((tm, d), lambda i: (i, 0)),
        out_shape=jax.ShapeDtypeStruct((s, d), out_dtype),
        compiler_params=_cparams(("parallel",)),
        name="rmsnorm",
    )(x, w.reshape(1, d))


def _proj(h_ref, w_ref, wb_ref, row_axis):
    @pl.when(pl.program_id(row_axis) == 0)
    def _cast():
        wb_ref[...] = w_ref[...].astype(BF16)

    return jnp.dot(h_ref[...], wb_ref[...], preferred_element_type=F32)


def _mm_body(h_ref, w_ref, o_ref, wb_ref, *, act):
    acc = _proj(h_ref, w_ref, wb_ref, 1)
    if act == "sigmoid":
        acc = jax.nn.sigmoid(acc)
    o_ref[...] = acc.astype(o_ref.dtype)


def _mm(h, w, col0, ncols, act, out_dtype, tm=512, tn=1024):
    s, k = h.shape
    jb = col0 // tn
    return pl.pallas_call(
        functools.partial(_mm_body, act=act),
        grid=(ncols // tn, s // tm),
        in_specs=[pl.BlockSpec((tm, k), lambda j, i: (i, 0)),
                  pl.BlockSpec((k, tn), lambda j, i: (0, j + jb))],
        out_specs=pl.BlockSpec((tm, tn), lambda j, i: (i, j)),
        out_shape=jax.ShapeDtypeStruct((s, ncols), out_dtype),
        scratch_shapes=[pltpu.VMEM((k, tn), BF16)],
        compiler_params=_cparams(("arbitrary", "arbitrary")),
        name="mm_" + act,
    )(h, w)


def _mm_vones_body(h_ref, w_ref, o_ref, wb_ref):
    acc = _proj(h_ref, w_ref, wb_ref, 0)
    hw = 2 * HEAD_DIM
    ones = jnp.ones((acc.shape[0], hw), o_ref.dtype)
    for hd in range(acc.shape[1] // hw):
        o_ref[:, 2 * hd * hw:(2 * hd + 1) * hw] = acc[:, hd * hw:(hd + 1) * hw].astype(o_ref.dtype)
        o_ref[:, (2 * hd + 1) * hw:(2 * hd + 2) * hw] = ones


def _mm_vones(h, w, col0, tm=512):
    s, k = h.shape
    tn = ATTN_WIDTH
    jb = col0 // tn
    return pl.pallas_call(
        _mm_vones_body,
        grid=(s // tm,),
        in_specs=[pl.BlockSpec((tm, k), lambda i: (i, 0)),
                  pl.BlockSpec((k, tn), lambda i: (0, jb))],
        out_specs=pl.BlockSpec((tm, 2 * tn), lambda i: (i, 0)),
        out_shape=jax.ShapeDtypeStruct((s, 2 * tn), BF16),
        scratch_shapes=[pltpu.VMEM((k, tn), BF16)],
        compiler_params=_cparams(("arbitrary",)),
        name="mm_vones",
    )(h, w)


def _mm_rope_body(h_ref, w_ref, pos_ref, invf_ref, o_ref, wb_ref, *, tn):
    j = pl.program_id(0)
    acc = _proj(h_ref, w_ref, wb_ref, 1)
    scale = jnp.where(j == 0, HEAD_DIM ** -0.5, 1.0).astype(F32)
    ang = pos_ref[...].astype(F32) * invf_ref[...]
    lane = lax.broadcasted_iota(jnp.int32, ang.shape, 1)
    low = (lane % HEAD_DIM) < (HEAD_DIM // 2)
    cos = jnp.cos(ang) * scale
    sin = jnp.where(low, -jnp.sin(ang), jnp.sin(ang)) * scale
    for c in range(tn // LANES):
        t = acc[:, c * LANES:(c + 1) * LANES]
        partner = jnp.where(low, pltpu.roll(t, LANES - HEAD_DIM // 2, 1),
                            pltpu.roll(t, HEAD_DIM // 2, 1))
        o_ref[:, c * LANES:(c + 1) * LANES] = (t * cos + partner * sin).astype(o_ref.dtype)


def _mm_rope(h, w, positions, col0, tm=512, tn=1024):
    s, k = h.shape
    jb = col0 // tn
    inv_freq = 1.0 / (ROPE_THETA ** (jnp.arange(0, HEAD_DIM, 2, dtype=F32) / HEAD_DIM))
    invf = jnp.tile(inv_freq, LANES // (HEAD_DIM // 2)).reshape(1, LANES)
    return pl.pallas_call(
        functools.partial(_mm_rope_body, tn=tn),
        grid=(2, s // tm),
        in_specs=[pl.BlockSpec((tm, k), lambda j, i: (i, 0)),
                  pl.BlockSpec((k, tn), lambda j, i: (0, j + jb)),
                  pl.BlockSpec((tm, 1), lambda j, i: (i, 0)),
                  pl.BlockSpec((1, LANES), lambda j, i: (0, 0))],
        out_specs=pl.BlockSpec((tm, tn), lambda j, i: (i, j)),
        out_shape=jax.ShapeDtypeStruct((s, 2 * tn), BF16),
        scratch_shapes=[pltpu.VMEM((k, tn), BF16)],
        compiler_params=_cparams(("arbitrary", "arbitrary")),
        name="mm_rope",
    )(h, w, positions.reshape(s, 1), invf)


def _s5_tables(lam_re, lam_im, log_dt, b_re, b_im, c_re, c_im):
    L, G, P, M = SSM_CHUNK, SSM_GROUPS, SSM_STATE, SSM_GROUP
    hi = lax.Precision.HIGHEST
    lr, li = lam_re.astype(F32), lam_im.astype(F32)
    dt = jnp.exp(log_dt.astype(F32))[:, None]
    n = jnp.arange(L + 1, dtype=F32)[:, None, None]
    mag = jnp.exp(lr * dt * n)
    pr = mag * jnp.cos(li * dt * n)
    pi = mag * jnp.sin(li * dt * n)
    nr, ni = pr[1] - 1.0, pi[1]
    den = lr * lr + li * li
    f_re = (nr * lr + ni * li) / den
    f_im = (ni * lr - nr * li) / den
    br, bi = b_re.astype(F32), b_im.astype(F32)
    bbr = f_re[..., None] * br - f_im[..., None] * bi
    bbi = f_re[..., None] * bi + f_im[..., None] * br
    cr, ci = c_re.astype(F32), c_im.astype(F32)

    abr = pr[..., None] * bbr - pi[..., None] * bbi
    abi = pr[..., None] * bbi + pi[..., None] * bbr
    kk = (jnp.einsum('gop,ngpi->ngoi', cr, abr[:L], precision=hi)
          - jnp.einsum('gop,ngpi->ngoi', ci, abi[:L], precision=hi))
    ii = jnp.arange(L)[:, None]
    jj = jnp.arange(L)[None, :]
    tau = jnp.clip(jj - ii, 0, L - 1)
    t5 = kk[tau] * (jj >= ii)[:, :, None, None, None].astype(F32)
    t_tab = jnp.transpose(t5, (2, 0, 4, 1, 3)).reshape(G, L * M, L * M)

    wr = jnp.transpose(abr[:L][::-1], (1, 0, 3, 2)).reshape(G, L * M, P)
    wi = jnp.transpose(abi[:L][::-1], (1, 0, 3, 2)).reshape(G, L * M, P)
    z = jnp.zeros_like(wr[0::2])
    w_top = jnp.concatenate([wr[0::2], z, wi[0::2], z], axis=-1)
    w_bot = jnp.concatenate([z, wr[1::2], z, wi[1::2]], axis=-1)
    w_tab = jnp.concatenate([w_top, w_bot], axis=1)

    ar1, ai1 = pr[1:], pi[1:]
    vr = cr[None] * ar1[:, :, None, :] - ci[None] * ai1[:, :, None, :]
    vi = -(cr[None] * ai1[:, :, None, :] + ci[None] * ar1[:, :, None, :])
    vr = jnp.transpose(vr, (1, 3, 0, 2)).reshape(G, P, L * M)
    vi = jnp.transpose(vi, (1, 3, 0, 2)).reshape(G, P, L * M)
    zv = jnp.zeros_like(vr[0::2])
    v_tab = jnp.concatenate([
        jnp.concatenate([vr[0::2], zv], axis=-1),
        jnp.concatenate([zv, vr[1::2]], axis=-1),
        jnp.concatenate([vi[0::2], zv], axis=-1),
        jnp.concatenate([zv, vi[1::2]], axis=-1)], axis=1)

    al = jnp.stack([pr[L].reshape(G * P), pi[L].reshape(G * P)])
    return t_tab.astype(BF16), w_tab.astype(BF16), v_tab.astype(BF16), al


def _s5_body(u_ref, t_ref, w_ref, v_ref, al_ref, d_ref, y_ref, er_ref, ei_ref, xr_ref, xi_ref):
    nc = u_ref.shape[0]
    lm = SSM_CHUNK * SSM_GROUP
    pw = 2 * SSM_STATE
    npair = SSM_GB // 2
    for pp in range(npair):
        up = u_ref[:, pp * 2 * lm:(pp + 1) * 2 * lm].astype(BF16)
        e = jnp.dot(up, w_ref[pp], preferred_element_type=F32)
        er_ref[:, pp * pw:(pp + 1) * pw] = e[:, :pw]
        ei_ref[:, pp * pw:(pp + 1) * pw] = e[:, pw:]

    ar = al_ref[0:1, :]
    ai = al_ref[1:2, :]

    def step(c, carry):
        xr, xi = carry
        xr_ref[pl.ds(c, 1), :] = xr
        xi_ref[pl.ds(c, 1), :] = xi
        er = er_ref[pl.ds(c, 1), :]
        ei = ei_ref[pl.ds(c, 1), :]
        return ar * xr - ai * xi + er, ar * xi + ai * xr + ei

    zero = jnp.zeros((1, npair * pw), F32)
    lax.fori_loop(0, nc, step, (zero, zero))

    for pp in range(npair):
        xin = jnp.concatenate([xr_ref[:, pp * pw:(pp + 1) * pw],
                               xi_ref[:, pp * pw:(pp + 1) * pw]], axis=1).astype(BF16)
        yc = jnp.dot(xin, v_ref[pp], preferred_element_type=F32)
        for q in range(2):
            g = pp * 2 + q
            uf = u_ref[:, g * lm:(g + 1) * lm]
            yi = jnp.dot(uf.astype(BF16), t_ref[g], preferred_element_type=F32)
            y = yi + yc[:, q * lm:(q + 1) * lm] + d_ref[:, g * lm:(g + 1) * lm] * uf
            y_ref[:, g * lm:(g + 1) * lm] = jax.nn.gelu(y)


def _s5_core(u, tables, d_skip):
    s = u.shape[0]
    L, G, M, P = SSM_CHUNK, SSM_GROUPS, SSM_GROUP, SSM_STATE
    nc = s // L
    lm = L * M
    t_tab, w_tab, v_tab, al = tables
    u2 = u.reshape(nc, L, G, M).transpose(0, 2, 1, 3).reshape(nc, G * lm)
    d2 = jnp.broadcast_to(d_skip.astype(F32).reshape(1, G, 1, M), (1, G, L, M)).reshape(1, G * lm)
    gb = SSM_GB
    y2 = pl.pallas_call(
        _s5_body,
        grid=(G // gb,),
        in_specs=[pl.BlockSpec((nc, gb * lm), lambda b: (0, b)),
                  pl.BlockSpec((gb, lm, lm), lambda b: (b, 0, 0)),
                  pl.BlockSpec((gb // 2, 2 * lm, 4 * P), lambda b: (b, 0, 0)),
                  pl.BlockSpec((gb // 2, 4 * P, 2 * lm), lambda b: (b, 0, 0)),
                  pl.BlockSpec((2, gb * P), lambda b: (0, b)),
                  pl.BlockSpec((1, gb * lm), lambda b: (0, b))],
        out_specs=pl.BlockSpec((nc, gb * lm), lambda b: (0, b)),
        out_shape=jax.ShapeDtypeStruct((nc, G * lm), F32),
        scratch_shapes=[pltpu.VMEM((nc, gb * P), F32) for _ in range(4)],
        compiler_params=_cparams(("parallel",)),
        name="s5_scan",
    )(u2, t_tab, w_tab, v_tab, al, d2)
    return y2.reshape(nc, G, L, M).transpose(0, 2, 1, 3).reshape(s, G * M)


def _glu_body(y_ref, w_ref, b_ref, o_ref):
    y = y_ref[...]
    z = jnp.dot(y.astype(BF16), w_ref[...], preferred_element_type=F32) + b_ref[...]
    o_ref[...] = (y * jax.nn.sigmoid(z)).astype(o_ref.dtype)


def _glu(y, w, b, tm=512):
    s, n = y.shape
    return pl.pallas_call(
        _glu_body,
        grid=(s // tm,),
        in_specs=[pl.BlockSpec((tm, n), lambda i: (i, 0)),
                  pl.BlockSpec((n, n), lambda i: (0, 0)),
                  pl.BlockSpec((1, n), lambda i: (0, 0))],
        out_specs=pl.BlockSpec((tm, n), lambda i: (i, 0)),
        out_shape=jax.ShapeDtypeStruct((s, n), BF16),
        compiler_params=_cparams(("parallel",)),
        name="s5_glu",
    )(y, w, b.reshape(1, n).astype(F32))


def _attn_body(q_ref, k_ref, v_ref, lam_ref, sw_ref, o_ref, qs_ref, m_ref, acc_ref, *, tq):
    qi = pl.program_id(1)
    hw = 2 * HEAD_DIM
    q = q_ref[...]
    lane = lax.broadcasted_iota(jnp.int32, q.shape, 1)
    zero = jnp.zeros_like(q)
    qs_ref[0] = jnp.where(lane < HEAD_DIM, q, zero)
    qs_ref[1] = jnp.where(lane >= HEAD_DIM, q, zero)
    m_ref[...] = jnp.full(m_ref.shape, -jnp.inf, F32)
    acc_ref[...] = jnp.zeros(acc_ref.shape, F32)

    def chunk(j, masked):
        off = pl.multiple_of(j * tq, tq)
        k = k_ref[pl.ds(off, tq), :]
        v = v_ref[pl.ds(off, tq), :]
        if masked:
            row = lax.broadcasted_iota(jnp.int32, (tq, tq), 0)
            col = lax.broadcasted_iota(jnp.int32, (tq, tq), 1)
            keep = col <= row
        for c in range(2):
            s = lax.dot_general(qs_ref[c], k, (((1,), (1,)), ((), ())),
                                preferred_element_type=F32)
            if masked:
                s = jnp.where(keep, s, -jnp.inf)
            m_old = m_ref[c]
            m_new = jnp.maximum(m_old, jnp.max(s, axis=-1, keepdims=True))
            alpha = jnp.exp(m_old - m_new)
            p = jnp.exp(s - m_new).astype(BF16)
            acc_ref[c] = alpha * acc_ref[c] + jnp.dot(p, v, preferred_element_type=F32)
            m_ref[c] = m_new

    def full_chunk(j, carry):
        chunk(j, False)
        return carry

    lax.fori_loop(0, qi, full_chunk, 0)
    chunk(qi, True)

    lam = (jnp.exp(jnp.sum(lam_ref[0:1, :] * lam_ref[1:2, :]))
           - jnp.exp(jnp.sum(lam_ref[2:3, :] * lam_ref[3:4, :])) + LAMBDA_INIT)
    a1 = acc_ref[0]
    a2 = acc_ref[1]
    o = a1[:, :hw] / a1[:, hw:] - lam * (a2[:, :hw] / a2[:, hw:])
    o = o * lax.rsqrt(jnp.mean(o * o, axis=-1, keepdims=True) + SUBLN_EPS)
    o_ref[...] = (o * sw_ref[...] * (1.0 - LAMBDA_INIT)).astype(o_ref.dtype)


def _diff_attention(qk, v1, lam_params, subln_w, tq=512):
    s = qk.shape[0]
    nq = s // tq
    hw = 2 * HEAD_DIM
    kcol0 = ATTN_WIDTH // hw
    return pl.pallas_call(
        functools.partial(_attn_body, tq=tq),
        grid=(ATTN_HEADS, nq),
        in_specs=[pl.BlockSpec((tq, hw), lambda h, i: (i, h)),
                  pl.BlockSpec((s, hw), lambda h, i: (0, kcol0 + h)),
                  pl.BlockSpec((s, 2 * hw), lambda h, i: (0, h)),
                  pl.BlockSpec((4, HEAD_DIM), lambda h, i: (0, 0)),
                  pl.BlockSpec((1, hw), lambda h, i: (0, 0))],
        out_specs=pl.BlockSpec((tq, hw), lambda h, i: (i, h)),
        out_shape=jax.ShapeDtypeStruct((s, ATTN_WIDTH), BF16),
        scratch_shapes=[pltpu.VMEM((2, tq, hw), BF16),
                        pltpu.VMEM((2, tq, 1), F32),
                        pltpu.VMEM((2, tq, 2 * hw), F32)],
        compiler_params=_cparams(("parallel", "arbitrary")),
        name="diff_attn",
    )(qk, qk, v1, lam_params, subln_w.reshape(1, hw).astype(F32))


def _merge_body(x_ref, ys_ref, ya_ref, gs_ref, ga_ref, wps_ref, wpa_ref, wo_ref, n2_ref, wr_ref,
                x1_ref, h2_ref, eid_ref, gate_ref, cnt_ref):
    ps = jnp.dot(ys_ref[...], wps_ref[...], preferred_element_type=F32)
    pa = jnp.dot(ya_ref[...], wpa_ref[...], preferred_element_type=F32)
    merged = gs_ref[...].astype(F32) * ps + ga_ref[...].astype(F32) * pa
    x1 = x_ref[...] + jnp.dot(merged.astype(BF16), wo_ref[...], preferred_element_type=F32)
    x1_ref[...] = x1
    h2 = x1 * lax.rsqrt(jnp.mean(x1 * x1, axis=-1, keepdims=True) + NORM_EPS) * n2_ref[...]
    h2_ref[...] = h2
    wr = wr_ref[...]
    h_hi = h2.astype(BF16)
    h_lo = (h2 - h_hi.astype(F32)).astype(BF16)
    w_hi = wr.astype(BF16)
    w_lo = (wr - w_hi.astype(F32)).astype(BF16)
    logits = (jnp.dot(h_hi, w_hi, preferred_element_type=F32)
              + jnp.dot(h_lo, w_hi, preferred_element_type=F32)
              + jnp.dot(h_hi, w_lo, preferred_element_type=F32))
    lane = lax.broadcasted_iota(jnp.int32, logits.shape, 1)
    big = jnp.int32(1 << 20)
    ninf = jnp.float32(-jnp.inf)
    is_g = lane < N_GROUPS
    gl = jnp.where(is_g, logits, ninf)
    gm = jnp.max(gl, axis=-1, keepdims=True)
    g_idx = jnp.min(jnp.where(gl == gm, lane, big), axis=-1, keepdims=True)
    g_val = 1.0 / jnp.sum(jnp.where(is_g, jnp.exp(gl - gm), 0.0), axis=-1, keepdims=True)
    lo = N_GROUPS + g_idx * EXPERTS_PER_GROUP
    in_grp = (lane >= lo) & (lane < lo + EXPERTS_PER_GROUP)
    el = jnp.where(in_grp, logits, ninf)
    e1 = jnp.max(el, axis=-1, keepdims=True)
    i1 = jnp.min(jnp.where(el == e1, lane, big), axis=-1, keepdims=True)
    el2 = jnp.where(lane == i1, ninf, el)
    e2 = jnp.max(el2, axis=-1, keepdims=True)
    i2 = jnp.min(jnp.where(el2 == e2, lane, big), axis=-1, keepdims=True)
    t = jnp.exp(e2 - e1)
    p1 = 1.0 / (1.0 + t)
    p2 = t / (1.0 + t)
    gate_ref[...] = jnp.where(lane == 0, p1 * g_val, jnp.where(lane == 1, p2 * g_val, 0.0))

    @pl.when(pl.program_id(0) == 0)
    def _zero_counts():
        cnt_ref[...] = jnp.zeros(cnt_ref.shape, F32)

    ex1 = i1 - N_GROUPS
    ex2 = i2 - N_GROUPS
    oh1 = lane == ex1
    oh2 = lane == ex2
    onehot = jnp.where(oh1 | oh2, 1.0, 0.0)
    tm = logits.shape[0]
    r_i = lax.broadcasted_iota(jnp.int32, (tm, tm), 0)
    c_i = lax.broadcasted_iota(jnp.int32, (tm, tm), 1)
    tri = jnp.where(c_i < r_i, 1.0, 0.0).astype(BF16)
    before = jnp.dot(tri, onehot.astype(BF16), preferred_element_type=F32) + cnt_ref[...]
    rank1 = jnp.sum(jnp.where(oh1, before, 0.0), axis=-1, keepdims=True).astype(jnp.int32)
    rank2 = jnp.sum(jnp.where(oh2, before, 0.0), axis=-1, keepdims=True).astype(jnp.int32)
    cnt_ref[...] = cnt_ref[...] + jnp.sum(onehot, axis=0, keepdims=True)
    eid_ref[...] = jnp.where(lane == 0, ex1, jnp.where(lane == 1, ex2,
                             jnp.where(lane == 2, rank1, jnp.where(lane == 3, rank2, 0))))


def _merge_route(x, y_ssm, y_attn, gates, wps, wpa, wo, n2w, wr, tm=256):
    s, d = x.shape
    nw = y_ssm.shape[1]
    const = lambda i: (0, 0)
    return pl.pallas_call(
        _merge_body,
        grid=(s // tm,),
        in_specs=[pl.BlockSpec((tm, d), lambda i: (i, 0)),
                  pl.BlockSpec((tm, nw), lambda i: (i, 0)),
                  pl.BlockSpec((tm, nw), lambda i: (i, 0)),
                  pl.BlockSpec((tm, d), lambda i: (i, 0)),
                  pl.BlockSpec((tm, d), lambda i: (i, 1)),
                  pl.BlockSpec((nw, d), const),
                  pl.BlockSpec((nw, d), const),
                  pl.BlockSpec((d, d), const),
                  pl.BlockSpec((1, d), const),
                  pl.BlockSpec((d, LANES), const)],
        out_specs=[pl.BlockSpec((tm, d), lambda i: (i, 0)),
                   pl.BlockSpec((tm, d), lambda i: (i, 0)),
                   pl.BlockSpec((tm, LANES), lambda i: (i, 0)),
                   pl.BlockSpec((tm, LANES), lambda i: (i, 0)),
                   pl.BlockSpec((1, LANES), const)],
        out_shape=[jax.ShapeDtypeStruct((s, d), F32),
                   jax.ShapeDtypeStruct((s, d), F32),
                   jax.ShapeDtypeStruct((s, LANES), jnp.int32),
                   jax.ShapeDtypeStruct((s, LANES), F32),
                   jax.ShapeDtypeStruct((1, LANES), F32)],
        compiler_params=_cparams(("arbitrary",)),
        name="merge_route",
    )(x, y_ssm, y_attn, gates, gates, wps, wpa, wo, n2w.reshape(1, d).astype(F32), wr)


def _row_copy(src_hbm, row, dst_ref, r, sem):
    return pltpu.make_async_copy(src_hbm.at[pl.ds(row, 1), :], dst_ref.at[pl.ds(r, 1), :], sem)


def _gather_rows(src_hbm, idx_ref, base, stride, dst_ref, sem, n):
    def start(r, c):
        _row_copy(src_hbm, idx_ref[base + stride * r], dst_ref, r, sem).start()
        return c

    lax.fori_loop(0, n, start, 0, unroll=8)


def _wait_rows(src_hbm, dst_ref, sem, n):
    pltpu.make_async_copy(src_hbm.at[pl.ds(0, n), :], dst_ref, sem).wait()


def _scatter_tok_body(dest_ref, tok_ref):
    def zero(i, c):
        tok_ref[i] = 0
        return c

    lax.fori_loop(0, tok_ref.shape[0], zero, 0, unroll=8)

    def put(i, c):
        tok_ref[dest_ref[i]] = i // TOP_K
        return c

    lax.fori_loop(0, dest_ref.shape[0], put, 0, unroll=8)


def _scatter_tok(dest, buf_len):
    return pl.pallas_call(
        _scatter_tok_body,
        in_specs=[pl.BlockSpec(memory_space=pltpu.SMEM)],
        out_specs=pl.BlockSpec(memory_space=pltpu.SMEM),
        out_shape=jax.ShapeDtypeStruct((buf_len,), jnp.int32),
        name="scatter_tok",
    )(dest)


def _expert_body(be_ref, nb_ref, tok_ref, h2_hbm, wg_ref, wu_ref, wd_ref, o_ref,
                 xb_ref, wgb_ref, wub_ref, wdb_ref, sem):
    b = pl.program_id(0)
    nb = nb_ref[0]

    @pl.when(b == 0)
    def _first():
        _gather_rows(h2_hbm, tok_ref, 0, 1, xb_ref.at[0], sem.at[0], MOE_BLK)

    @pl.when(b + 1 < nb)
    def _prefetch():
        nxt = (b + 1) % 2
        _gather_rows(h2_hbm, tok_ref, (b + 1) * MOE_BLK, 1, xb_ref.at[nxt], sem.at[nxt], MOE_BLK)

    @pl.when((b == 0) | (be_ref[b] != be_ref[jnp.maximum(b - 1, 0)]))
    def _new_expert():
        wgb_ref[...] = wg_ref[0].astype(BF16)
        wub_ref[...] = wu_ref[0].astype(BF16)
        wdb_ref[...] = wd_ref[0].astype(BF16)

    @pl.when(b < nb)
    def _run():
        cur = b % 2
        _wait_rows(h2_hbm, xb_ref.at[cur], sem.at[cur], MOE_BLK)
        xb = xb_ref[cur].astype(BF16)
        g = jnp.dot(xb, wgb_ref[...], preferred_element_type=F32)
        u = jnp.dot(xb, wub_ref[...], preferred_element_type=F32)
        a = (jax.nn.silu(g) * u).astype(BF16)
        o_ref[...] = jnp.dot(a, wdb_ref[...], preferred_element_type=F32)

    @pl.when(b >= nb)
    def _unused():
        o_ref[...] = jnp.zeros(o_ref.shape, o_ref.dtype)


def _experts(h2, block_e, n_used, buf_tok, wg, wu, wd):
    s, d = h2.shape
    buf_len = buf_tok.shape[0]
    n_blocks = buf_len // MOE_BLK
    wmap = lambda b, be, nb, tok: (be[b], 0, 0)
    return pl.pallas_call(
        _expert_body,
        grid_spec=pltpu.PrefetchScalarGridSpec(
            num_scalar_prefetch=3,
            grid=(n_blocks,),
            in_specs=[pl.BlockSpec(memory_space=pl.ANY),
                      pl.BlockSpec((1, d, D_FF), wmap),
                      pl.BlockSpec((1, d, D_FF), wmap),
                      pl.BlockSpec((1, D_FF, d), wmap)],
            out_specs=pl.BlockSpec((MOE_BLK, d), lambda b, be, nb, tok: (b, 0)),
            scratch_shapes=[pltpu.VMEM((2, MOE_BLK, d), F32),
                            pltpu.VMEM((d, D_FF), BF16), pltpu.VMEM((d, D_FF), BF16),
                            pltpu.VMEM((D_FF, d), BF16),
                            pltpu.SemaphoreType.DMA((2,))]),
        out_shape=jax.ShapeDtypeStruct((buf_len, d), F32),
        compiler_params=_cparams(("arbitrary",)),
        name="experts",
    )(block_e, n_used, buf_tok, h2, wg, wu, wd)


def _combine_body(pos_ref, x1_ref, gate_ref, fw_ref, ys_hbm, o_ref, g_ref, sem, *, tm):
    i = pl.program_id(0)

    def gather(tile, slot):
        for k in range(TOP_K):
            _gather_rows(ys_hbm, pos_ref, TOP_K * tile * tm + k, TOP_K, g_ref.at[slot, k],
                         sem.at[slot], tm)

    @pl.when(i == 0)
    def _first():
        gather(0, 0)

    @pl.when(i + 1 < pl.num_programs(0))
    def _prefetch():
        gather(i + 1, (i + 1) % 2)

    cur = i % 2
    for k in range(TOP_K):
        _wait_rows(ys_hbm, g_ref.at[cur, k], sem.at[cur], tm)
    gate = gate_ref[...]
    x = x1_ref[...] + (gate[:, 0:1] * g_ref[cur, 0] + gate[:, 1:2] * g_ref[cur, 1])
    y = x * lax.rsqrt(jnp.mean(x * x, axis=-1, keepdims=True) + NORM_EPS)
    o_ref[...] = y * fw_ref[...]


def _combine(x1, ys, pos, gate, fw, tm=256):
    s, d = x1.shape
    return pl.pallas_call(
        functools.partial(_combine_body, tm=tm),
        grid_spec=pltpu.PrefetchScalarGridSpec(
            num_scalar_prefetch=1,
            grid=(s // tm,),
            in_specs=[pl.BlockSpec((tm, d), lambda i, p: (i, 0)),
                      pl.BlockSpec((tm, LANES), lambda i, p: (i, 0)),
                      pl.BlockSpec((1, d), lambda i, p: (0, 0)),
                      pl.BlockSpec(memory_space=pl.ANY)],
            out_specs=pl.BlockSpec((tm, d), lambda i, p: (i, 0)),
            scratch_shapes=[pltpu.VMEM((2, TOP_K, tm, d), F32),
                            pltpu.SemaphoreType.DMA((2,))]),
        out_shape=jax.ShapeDtypeStruct((s, d), F32),
        compiler_params=_cparams(("arbitrary",)),
        name="combine_norm",
    )(pos, x1, gate, fw.reshape(1, d).astype(F32), ys)


def _dispatch_tables(eid, counts):
    n_tok = eid.shape[0]
    n_slots = n_tok * TOP_K
    buf_len = n_slots + N_EXPERTS * MOE_BLK
    n_blocks = buf_len // MOE_BLK
    counts = counts.astype(jnp.int32)
    padded = ((counts + MOE_BLK - 1) // MOE_BLK) * MOE_BLK
    padded_ends = jnp.cumsum(padded)
    padded_starts = padded_ends - padded
    experts = eid[:, :TOP_K]
    onehot = experts[:, :, None] == jnp.arange(N_EXPERTS, dtype=jnp.int32)
    dest = jnp.sum(jnp.where(onehot, padded_starts, 0), axis=-1) + eid[:, TOP_K:2 * TOP_K]
    block_start = jnp.arange(n_blocks, dtype=jnp.int32) * MOE_BLK
    n_used = (padded_ends[-1] // MOE_BLK).astype(jnp.int32).reshape(1)
    last_start = (n_used[0] - 1) * MOE_BLK
    block_e = jnp.sum(padded_ends[None, :] <= jnp.minimum(block_start, last_start)[:, None],
                      axis=-1).astype(jnp.int32)
    return block_e, n_used, dest.reshape(n_slots).astype(jnp.int32), buf_len


def kernel(x, positions, norm1_w, w_in, ssm_lambda_re, ssm_lambda_im, ssm_log_dt, ssm_b_re, ssm_b_im, ssm_c_re, ssm_c_im, ssm_d, ssm_glu_w, ssm_glu_b, attn_lambda_q1, attn_lambda_k1, attn_lambda_q2, attn_lambda_k2, attn_subln_w, w_proj_ssm, w_proj_attn, w_out, norm2_w, router_group_w, router_expert_w, expert_w_gate, expert_w_up, expert_w_down, final_norm_w):
    bsz, seq, d = x.shape
    depth = norm1_w.shape[0]
    xs = x.reshape(bsz * seq, d)
    pos = positions.reshape(bsz * seq)
    o_q = SSM_WIDTH
    o_v = o_q + 2 * ATTN_WIDTH
    o_g = o_v + ATTN_WIDTH
    for l in range(depth):
        h = _rmsnorm(xs, norm1_w[l].astype(F32), BF16)
        w_in_b = w_in[l].astype(F32)
        u = _mm(h, w_in_b, 0, SSM_WIDTH, "none", F32)
        qk = _mm_rope(h, w_in_b, pos, o_q)
        v = _mm_vones(h, w_in_b, o_v)
        gates = _mm(h, w_in_b, o_g, 2 * D_MODEL, "sigmoid", F32)

        tables = _s5_tables(ssm_lambda_re[l], ssm_lambda_im[l], ssm_log_dt[l],
                            ssm_b_re[l], ssm_b_im[l], ssm_c_re[l], ssm_c_im[l])
        y = _s5_core(u, tables, ssm_d[l])
        y_ssm = _glu(y, ssm_glu_w[l].astype(BF16), ssm_glu_b[l])

        lam_params = jnp.stack([attn_lambda_q1[l], attn_lambda_k1[l],
                                attn_lambda_q2[l], attn_lambda_k2[l]]).astype(F32)
        y_attn = _diff_attention(qk, v, lam_params, attn_subln_w[l])

        wr = jnp.concatenate([router_group_w[l], router_expert_w[l]], axis=1).astype(F32)
        wr = jnp.pad(wr, ((0, 0), (0, LANES - wr.shape[1])))
        x1, h2, eid, gate, counts = _merge_route(
            xs, y_ssm, y_attn, gates, w_proj_ssm[l].astype(BF16), w_proj_attn[l].astype(BF16),
            w_out[l].astype(BF16), norm2_w[l], wr)

        block_e, n_used, dest, buf_len = _dispatch_tables(eid, counts[0, :N_EXPERTS])
        buf_tok = _scatter_tok(dest, buf_len)
        ys = _experts(h2, block_e, n_used, buf_tok, expert_w_gate[l].astype(F32),
                      expert_w_up[l].astype(F32), expert_w_down[l].astype(F32))
        last = l == depth - 1
        assert last, "DEPTH > 1 needs an un-normalised combine"
        xs = _combine(x1, ys, dest, gate, final_norm_w)
    return xs.reshape(bsz, seq, d)
```

```python
import functools
import math

import jax
import jax.numpy as jnp
from jax import lax
from jax.experimental import pallas as pl
from jax.experimental.pallas import tpu as pltpu

F32 = jnp.float32
BF16 = jnp.bfloat16

D_MODEL = 2048
SSM_WIDTH = 1024
SSM_GROUP = 16
SSM_GROUPS = 64
SSM_STATE = 64
ATTN_WIDTH = 1024
ATTN_HEADS = 8
HEAD_DIM = 64
ROPE_THETA = 10000.0
N_GROUPS = 4
EXPERTS_PER_GROUP = 8
N_EXPERTS = 32
TOP_K = 2
TOP_K_LOG2 = 1
D_FF = 512
NORM_EPS = 1e-6
SUBLN_EPS = 1e-5
LAMBDA_INIT = 0.8 - 0.6 * math.exp(-0.3 * 0)

LANES = 128
VMEM_LIMIT = 48 * 1024 * 1024

SSM_CHUNK = 16
SSM_GB = 8
MOE_BLK = 128


def _cparams(sem):
    return pltpu.CompilerParams(dimension_semantics=sem, vmem_limit_bytes=VMEM_LIMIT)


def _rmsnorm_body(x_ref, w_ref, o_ref):
    x = x_ref[...]
    y = x * lax.rsqrt(jnp.mean(x * x, axis=-1, keepdims=True) + NORM_EPS)
    o_ref[...] = (y * w_ref[...]).astype(o_ref.dtype)


def _rmsnorm(x, w, out_dtype, tm=512):
    s, d = x.shape
    return pl.pallas_call(
        _rmsnorm_body,
        grid=(s // tm,),
        in_specs=[pl.BlockSpec((tm, d), lambda i: (i, 0)),
                  pl.BlockSpec((1, d), lambda i: (0, 0))],
        out_specs=pl.BlockSpec((tm, d), lambda i: (i, 0)),
        out_shape=jax.ShapeDtypeStruct((s, d), out_dtype),
        compiler_params=_cparams(("parallel",)),
        name="rmsnorm",
    )(x, w.reshape(1, d))


def _proj(h_ref, w_ref, wb_ref, row_axis):
    @pl.when(pl.program_id(row_axis) == 0)
    def _cast():
        wb_ref[...] = w_ref[...].astype(BF16)

    return jnp.dot(h_ref[...], wb_ref[...], preferred_element_type=F32)


def _mm_body(h_ref, w_ref, o_ref, wb_ref, *, act):
    acc = _proj(h_ref, w_ref, wb_ref, 1)
    if act == "sigmoid":
        acc = jax.nn.sigmoid(acc)
    o_ref[...] = acc.astype(o_ref.dtype)


def _mm(h, w, col0, ncols, act, out_dtype, tm=512, tn=1024):
    s, k = h.shape
    jb = col0 // tn
    return pl.pallas_call(
        functools.partial(_mm_body, act=act),
        grid=(ncols // tn, s // tm),
        in_specs=[pl.BlockSpec((tm, k), lambda j, i: (i, 0)),
                  pl.BlockSpec((k, tn), lambda j, i: (0, j + jb))],
        out_specs=pl.BlockSpec((tm, tn), lambda j, i: (i, j)),
        out_shape=jax.ShapeDtypeStruct((s, ncols), out_dtype),
        scratch_shapes=[pltpu.VMEM((k, tn), BF16)],
        compiler_params=_cparams(("arbitrary", "arbitrary")),
        name="mm_" + act,
    )(h, w)


def _mm_vones_body(h_ref, w_ref, o_ref, wb_ref):
    acc = _proj(h_ref, w_ref, wb_ref, 0)
    hw = 2 * HEAD_DIM
    ones = jnp.ones((acc.shape[0], hw), o_ref.dtype)
    for hd in range(acc.shape[1] // hw):
        o_ref[:, 2 * hd * hw:(2 * hd + 1) * hw] = acc[:, hd * hw:(hd + 1) * hw].astype(o_ref.dtype)
        o_ref[:, (2 * hd + 1) * hw:(2 * hd + 2) * hw] = ones


def _mm_vones(h, w, col0, tm=512):
    s, k = h.shape
    tn = ATTN_WIDTH
    jb = col0 // tn
    return pl.pallas_call(
        _mm_vones_body,
        grid=(s // tm,),
        in_specs=[pl.BlockSpec((tm, k), lambda i: (i, 0)),
                  pl.BlockSpec((k, tn), lambda i: (0, jb))],
        out_specs=pl.BlockSpec((tm, 2 * tn), lambda i: (i, 0)),
        out_shape=jax.ShapeDtypeStruct((s, 2 * tn), BF16),
        scratch_shapes=[pltpu.VMEM((k, tn), BF16)],
        compiler_params=_cparams(("arbitrary",)),
        name="mm_vones",
    )(h, w)


def _mm_rope_body(h_ref, w_ref, pos_ref, invf_ref, o_ref, wb_ref, *, tn):
    j = pl.program_id(0)
    acc = _proj(h_ref, w_ref, wb_ref, 1)
    scale = jnp.where(j == 0, HEAD_DIM ** -0.5, 1.0).astype(F32)
    ang = pos_ref[...].astype(F32) * invf_ref[...]
    lane = lax.broadcasted_iota(jnp.int32, ang.shape, 1)
    low = (lane % HEAD_DIM) < (HEAD_DIM // 2)
    cos = jnp.cos(ang) * scale
    sin = jnp.where(low, -jnp.sin(ang), jnp.sin(ang)) * scale
    for c in range(tn // LANES):
        t = acc[:, c * LANES:(c + 1) * LANES]
        partner = jnp.where(low, pltpu.roll(t, LANES - HEAD_DIM // 2, 1),
                            pltpu.roll(t, HEAD_DIM // 2, 1))
        o_ref[:, c * LANES:(c + 1) * LANES] = (t * cos + partner * sin).astype(o_ref.dtype)


def _mm_rope(h, w, positions, col0, tm=512, tn=1024):
    s, k = h.shape
    jb = col0 // tn
    inv_freq = 1.0 / (ROPE_THETA ** (jnp.arange(0, HEAD_DIM, 2, dtype=F32) / HEAD_DIM))
    invf = jnp.tile(inv_freq, LANES // (HEAD_DIM // 2)).reshape(1, LANES)
    return pl.pallas_call(
        functools.partial(_mm_rope_body, tn=tn),
        grid=(2, s // tm),
        in_specs=[pl.BlockSpec((tm, k), lambda j, i: (i, 0)),
                  pl.BlockSpec((k, tn), lambda j, i: (0, j + jb)),
                  pl.BlockSpec((tm, 1), lambda j, i: (i, 0)),
                  pl.BlockSpec((1, LANES), lambda j, i: (0, 0))],
        out_specs=pl.BlockSpec((tm, tn), lambda j, i: (i, j)),
        out_shape=jax.ShapeDtypeStruct((s, 2 * tn), BF16),
        scratch_shapes=[pltpu.VMEM((k, tn), BF16)],
        compiler_params=_cparams(("arbitrary", "arbitrary")),
        name="mm_rope",
    )(h, w, positions.reshape(s, 1), invf)


def _s5_tables(lam_re, lam_im, log_dt, b_re, b_im, c_re, c_im):
    L, G, P, M = SSM_CHUNK, SSM_GROUPS, SSM_STATE, SSM_GROUP
    hi = lax.Precision.HIGHEST
    lr, li = lam_re.astype(F32), lam_im.astype(F32)
    dt = jnp.exp(log_dt.astype(F32))[:, None]
    n = jnp.arange(L + 1, dtype=F32)[:, None, None]
    mag = jnp.exp(lr * dt * n)
    pr = mag * jnp.cos(li * dt * n)
    pi = mag * jnp.sin(li * dt * n)
    nr, ni = pr[1] - 1.0, pi[1]
    den = lr * lr + li * li
    f_re = (nr * lr + ni * li) / den
    f_im = (ni * lr - nr * li) / den
    br, bi = b_re.astype(F32), b_im.astype(F32)
    bbr = f_re[..., None] * br - f_im[..., None] * bi
    bbi = f_re[..., None] * bi + f_im[..., None] * br
    cr, ci = c_re.astype(F32), c_im.astype(F32)

    eye2 = jnp.eye(2, dtype=F32)
    pr_g = jnp.transpose(pr, (1, 0, 2))
    pi_g = jnp.transpose(pi, (1, 0, 2))
    bbr_t = jnp.transpose(bbr, (0, 2, 1))
    bbi_t = jnp.transpose(bbi, (0, 2, 1))

    abr = pr_g[:, :, None, :] * bbr_t[:, None] - pi_g[:, :, None, :] * bbi_t[:, None]
    abi = pr_g[:, :, None, :] * bbi_t[:, None] + pi_g[:, :, None, :] * bbr_t[:, None]

    kk = (jnp.einsum('gnip,gop->gino', abr[:, :L], cr, precision=hi)
          - jnp.einsum('gnip,gop->gino', abi[:, :L], ci, precision=hi)).reshape(G, M, L * M)
    kpad = jnp.pad(kk, ((0, 0), (0, 0), ((L - 1) * M, 0)))
    t_tab = jnp.stack([kpad[:, :, (L - 1 - i) * M:(L - 1 - i) * M + L * M] for i in range(L)],
                      axis=1).reshape(G, L * M, L * M)

    wc = jnp.stack([abr[:, L - 1::-1], abi[:, L - 1::-1]], axis=3)
    wc = wc.reshape(G // 2, 2, L * M, 2, 1, P) * eye2[None, :, None, None, :, None]
    w_tab = wc.reshape(G // 2, 2 * L * M, 4 * P)

    cr_t = jnp.transpose(cr, (0, 2, 1))
    ci_t = jnp.transpose(ci, (0, 2, 1))
    ar1 = jnp.transpose(pr_g[:, 1:], (0, 2, 1))[..., None]
    ai1 = jnp.transpose(pi_g[:, 1:], (0, 2, 1))[..., None]
    vr = cr_t[:, :, None, :] * ar1 - ci_t[:, :, None, :] * ai1
    vi = -(cr_t[:, :, None, :] * ai1 + ci_t[:, :, None, :] * ar1)
    vc = jnp.stack([vr, vi], axis=1).reshape(G // 2, 2, 2, P, L * M)
    vc = jnp.transpose(vc, (0, 2, 3, 1, 4))[:, :, None] * eye2[None, None, :, None, :, None]
    v_tab = vc.reshape(G // 2, 4 * P, 2 * L * M)

    al = jnp.stack([pr[L].reshape(G * P), pi[L].reshape(G * P)])
    return t_tab.astype(BF16), w_tab.astype(BF16), v_tab.astype(BF16), al


def _s5_body(u_ref, t_ref, w_ref, v_ref, al_ref, d_ref, y_ref, er_ref, ei_ref, xr_ref, xi_ref):
    nc = u_ref.shape[0]
    lm = SSM_CHUNK * SSM_GROUP
    pw = 2 * SSM_STATE
    npair = SSM_GB // 2
    for pp in range(npair):
        up = u_ref[:, pp * 2 * lm:(pp + 1) * 2 * lm].astype(BF16)
        e = jnp.dot(up, w_ref[pp], preferred_element_type=F32)
        er_ref[:, pp * pw:(pp + 1) * pw] = e[:, :pw]
        ei_ref[:, pp * pw:(pp + 1) * pw] = e[:, pw:]

    ar = al_ref[0:1, :]
    ai = al_ref[1:2, :]

    def step(c, carry):
        xr, xi = carry
        xr_ref[pl.ds(c, 1), :] = xr
        xi_ref[pl.ds(c, 1), :] = xi
        er = er_ref[pl.ds(c, 1), :]
        ei = ei_ref[pl.ds(c, 1), :]
        return ar * xr - ai * xi + er, ar * xi + ai * xr + ei

    zero = jnp.zeros((1, npair * pw), F32)
    lax.fori_loop(0, nc, step, (zero, zero))

    for pp in range(npair):
        xin = jnp.concatenate([xr_ref[:, pp * pw:(pp + 1) * pw],
                               xi_ref[:, pp * pw:(pp + 1) * pw]], axis=1).astype(BF16)
        yc = jnp.dot(xin, v_ref[pp], preferred_element_type=F32)
        for q in range(2):
            g = pp * 2 + q
            uf = u_ref[:, g * lm:(g + 1) * lm]
            yi = jnp.dot(uf.astype(BF16), t_ref[g], preferred_element_type=F32)
            y = yi + yc[:, q * lm:(q + 1) * lm] + d_ref[:, g * lm:(g + 1) * lm] * uf
            y_ref[:, g * lm:(g + 1) * lm] = jax.nn.gelu(y)


def _s5_core(u, tables, d_skip):
    s = u.shape[0]
    L, G, M, P = SSM_CHUNK, SSM_GROUPS, SSM_GROUP, SSM_STATE
    nc = s // L
    lm = L * M
    t_tab, w_tab, v_tab, al = tables
    u2 = u.reshape(nc, L, G, M).transpose(0, 2, 1, 3).reshape(nc, G * lm)
    d2 = jnp.broadcast_to(d_skip.astype(F32).reshape(1, G, 1, M), (1, G, L, M)).reshape(1, G * lm)
    gb = SSM_GB
    y2 = pl.pallas_call(
        _s5_body,
        grid=(G // gb,),
        in_specs=[pl.BlockSpec((nc, gb * lm), lambda b: (0, b)),
                  pl.BlockSpec((gb, lm, lm), lambda b: (b, 0, 0)),
                  pl.BlockSpec((gb // 2, 2 * lm, 4 * P), lambda b: (b, 0, 0)),
                  pl.BlockSpec((gb // 2, 4 * P, 2 * lm), lambda b: (b, 0, 0)),
                  pl.BlockSpec((2, gb * P), lambda b: (0, b)),
                  pl.BlockSpec((1, gb * lm), lambda b: (0, b))],
        out_specs=pl.BlockSpec((nc, gb * lm), lambda b: (0, b)),
        out_shape=jax.ShapeDtypeStruct((nc, G * lm), F32),
        scratch_shapes=[pltpu.VMEM((nc, gb * P), F32) for _ in range(4)],
        compiler_params=_cparams(("parallel",)),
        name="s5_scan",
    )(u2, t_tab, w_tab, v_tab, al, d2)
    return y2.reshape(nc, G, L, M).transpose(0, 2, 1, 3).reshape(s, G * M)


def _glu_body(y_ref, w_ref, b_ref, o_ref):
    y = y_ref[...]
    z = jnp.dot(y.astype(BF16), w_ref[...], preferred_element_type=F32) + b_ref[...]
    o_ref[...] = (y * jax.nn.sigmoid(z)).astype(o_ref.dtype)


def _glu(y, w, b, tm=512):
    s, n = y.shape
    return pl.pallas_call(
        _glu_body,
        grid=(s // tm,),
        in_specs=[pl.BlockSpec((tm, n), lambda i: (i, 0)),
                  pl.BlockSpec((n, n), lambda i: (0, 0)),
                  pl.BlockSpec((1, n), lambda i: (0, 0))],
        out_specs=pl.BlockSpec((tm, n), lambda i: (i, 0)),
        out_shape=jax.ShapeDtypeStruct((s, n), BF16),
        compiler_params=_cparams(("parallel",)),
        name="s5_glu",
    )(y, w, b.reshape(1, n).astype(F32))


def _attn_body(q_ref, k_ref, v_ref, lam_ref, sw_ref, o_ref, qs_ref, m_ref, acc_ref, *, tq):
    qi = pl.program_id(1)
    hw = 2 * HEAD_DIM
    q = q_ref[...]
    lane = lax.broadcasted_iota(jnp.int32, q.shape, 1)
    zero = jnp.zeros_like(q)
    qs_ref[0] = jnp.where(lane < HEAD_DIM, q, zero)
    qs_ref[1] = jnp.where(lane >= HEAD_DIM, q, zero)
    m_ref[...] = jnp.full(m_ref.shape, -jnp.inf, F32)
    acc_ref[...] = jnp.zeros(acc_ref.shape, F32)

    def chunk(j, masked):
        off = pl.multiple_of(j * tq, tq)
        k = k_ref[pl.ds(off, tq), :]
        v = v_ref[pl.ds(off, tq), :]
        if masked:
            row = lax.broadcasted_iota(jnp.int32, (tq, tq), 0)
            col = lax.broadcasted_iota(jnp.int32, (tq, tq), 1)
            keep = col <= row
        for c in range(2):
            s = lax.dot_general(qs_ref[c], k, (((1,), (1,)), ((), ())),
                                preferred_element_type=F32)
            if masked:
                s = jnp.where(keep, s, -jnp.inf)
            m_old = m_ref[c]
            m_new = jnp.maximum(m_old, jnp.max(s, axis=-1, keepdims=True))
            alpha = jnp.exp(m_old - m_new)
            p = jnp.exp(s - m_new).astype(BF16)
            acc_ref[c] = alpha * acc_ref[c] + jnp.dot(p, v, preferred_element_type=F32)
            m_ref[c] = m_new

    def full_chunk(j, carry):
        chunk(j, False)
        return carry

    lax.fori_loop(0, qi, full_chunk, 0)
    chunk(qi, True)

    lam = (jnp.exp(jnp.sum(lam_ref[0:1, :] * lam_ref[1:2, :]))
           - jnp.exp(jnp.sum(lam_ref[2:3, :] * lam_ref[3:4, :])) + LAMBDA_INIT)
    a1 = acc_ref[0]
    a2 = acc_ref[1]
    o = a1[:, :hw] / a1[:, hw:] - lam * (a2[:, :hw] / a2[:, hw:])
    o = o * lax.rsqrt(jnp.mean(o * o, axis=-1, keepdims=True) + SUBLN_EPS)
    o_ref[...] = (o * sw_ref[...] * (1.0 - LAMBDA_INIT)).astype(o_ref.dtype)


def _diff_attention(qk, v1, lam_params, subln_w, tq=512):
    s = qk.shape[0]
    nq = s // tq
    hw = 2 * HEAD_DIM
    kcol0 = ATTN_WIDTH // hw
    return pl.pallas_call(
        functools.partial(_attn_body, tq=tq),
        grid=(ATTN_HEADS, nq),
        in_specs=[pl.BlockSpec((tq, hw), lambda h, i: (i, h)),
                  pl.BlockSpec((s, hw), lambda h, i: (0, kcol0 + h)),
                  pl.BlockSpec((s, 2 * hw), lambda h, i: (0, h)),
                  pl.BlockSpec((4, HEAD_DIM), lambda h, i: (0, 0)),
                  pl.BlockSpec((1, hw), lambda h, i: (0, 0))],
        out_specs=pl.BlockSpec((tq, hw), lambda h, i: (i, h)),
        out_shape=jax.ShapeDtypeStruct((s, ATTN_WIDTH), BF16),
        scratch_shapes=[pltpu.VMEM((2, tq, hw), BF16),
                        pltpu.VMEM((2, tq, 1), F32),
                        pltpu.VMEM((2, tq, 2 * hw), F32)],
        compiler_params=_cparams(("parallel", "arbitrary")),
        name="diff_attn",
    )(qk, qk, v1, lam_params, subln_w.reshape(1, hw).astype(F32))


def _merge_body(x_ref, ys_ref, ya_ref, gs_ref, ga_ref, wps_ref, wpa_ref, wo_ref, n2_ref, wr_ref,
                x1_ref, h2_ref, eid_ref, gate_ref, cnt_ref):
    ps = jnp.dot(ys_ref[...], wps_ref[...], preferred_element_type=F32)
    pa = jnp.dot(ya_ref[...], wpa_ref[...], preferred_element_type=F32)
    merged = gs_ref[...].astype(F32) * ps + ga_ref[...].astype(F32) * pa
    x1 = x_ref[...] + jnp.dot(merged.astype(BF16), wo_ref[...], preferred_element_type=F32)
    x1_ref[...] = x1
    h2 = x1 * lax.rsqrt(jnp.mean(x1 * x1, axis=-1, keepdims=True) + NORM_EPS) * n2_ref[...]
    h2_ref[...] = h2
    wr = wr_ref[...]
    h_hi = h2.astype(BF16)
    h_lo = (h2 - h_hi.astype(F32)).astype(BF16)
    w_hi = wr.astype(BF16)
    w_lo = (wr - w_hi.astype(F32)).astype(BF16)
    logits = (jnp.dot(h_hi, w_hi, preferred_element_type=F32)
              + jnp.dot(h_lo, w_hi, preferred_element_type=F32)
              + jnp.dot(h_hi, w_lo, preferred_element_type=F32))
    lane = lax.broadcasted_iota(jnp.int32, logits.shape, 1)
    big = jnp.int32(1 << 20)
    ninf = jnp.float32(-jnp.inf)
    is_g = lane < N_GROUPS
    gl = jnp.where(is_g, logits, ninf)
    gm = jnp.max(gl, axis=-1, keepdims=True)
    g_idx = jnp.min(jnp.where(gl == gm, lane, big), axis=-1, keepdims=True)
    g_val = 1.0 / jnp.sum(jnp.where(is_g, jnp.exp(gl - gm), 0.0), axis=-1, keepdims=True)
    lo = N_GROUPS + g_idx * EXPERTS_PER_GROUP
    in_grp = (lane >= lo) & (lane < lo + EXPERTS_PER_GROUP)
    el = jnp.where(in_grp, logits, ninf)
    e1 = jnp.max(el, axis=-1, keepdims=True)
    i1 = jnp.min(jnp.where(el == e1, lane, big), axis=-1, keepdims=True)
    el2 = jnp.where(lane == i1, ninf, el)
    e2 = jnp.max(el2, axis=-1, keepdims=True)
    i2 = jnp.min(jnp.where(el2 == e2, lane, big), axis=-1, keepdims=True)
    t = jnp.exp(e2 - e1)
    p1 = 1.0 / (1.0 + t)
    p2 = t / (1.0 + t)
    gate_ref[...] = jnp.where(lane == 0, p1 * g_val, jnp.where(lane == 1, p2 * g_val, 0.0))

    @pl.when(pl.program_id(0) == 0)
    def _zero_counts():
        cnt_ref[...] = jnp.zeros(cnt_ref.shape, F32)

    ex1 = i1 - N_GROUPS
    ex2 = i2 - N_GROUPS
    oh1 = lane == ex1
    oh2 = lane == ex2
    onehot = jnp.where(oh1 | oh2, 1.0, 0.0)
    tm = logits.shape[0]
    r_i = lax.broadcasted_iota(jnp.int32, (tm, tm), 0)
    c_i = lax.broadcasted_iota(jnp.int32, (tm, tm), 1)
    tri = jnp.where(c_i < r_i, 1.0, 0.0).astype(BF16)
    before = jnp.dot(tri, onehot.astype(BF16), preferred_element_type=F32) + cnt_ref[...]
    rank1 = jnp.sum(jnp.where(oh1, before, 0.0), axis=-1, keepdims=True).astype(jnp.int32)
    rank2 = jnp.sum(jnp.where(oh2, before, 0.0), axis=-1, keepdims=True).astype(jnp.int32)
    cnt_ref[...] = cnt_ref[...] + jnp.sum(onehot, axis=0, keepdims=True)
    eid_ref[...] = jnp.where(lane == 0, ex1, jnp.where(lane == 1, ex2,
                             jnp.where(lane == 2, rank1, jnp.where(lane == 3, rank2, 0))))


def _merge_route(x, y_ssm, y_attn, gates, wps, wpa, wo, n2w, wr, tm=256):
    s, d = x.shape
    nw = y_ssm.shape[1]
    const = lambda i: (0, 0)
    return pl.pallas_call(
        _merge_body,
        grid=(s // tm,),
        in_specs=[pl.BlockSpec((tm, d), lambda i: (i, 0)),
                  pl.BlockSpec((tm, nw), lambda i: (i, 0)),
                  pl.BlockSpec((tm, nw), lambda i: (i, 0)),
                  pl.BlockSpec((tm, d), lambda i: (i, 0)),
                  pl.BlockSpec((tm, d), lambda i: (i, 1)),
                  pl.BlockSpec((nw, d), const),
                  pl.BlockSpec((nw, d), const),
                  pl.BlockSpec((d, d), const),
                  pl.BlockSpec((1, d), const),
                  pl.BlockSpec((d, LANES), const)],
        out_specs=[pl.BlockSpec((tm, d), lambda i: (i, 0)),
                   pl.BlockSpec((tm, d), lambda i: (i, 0)),
                   pl.BlockSpec((tm, LANES), lambda i: (i, 0)),
                   pl.BlockSpec((tm, LANES), lambda i: (i, 0)),
                   pl.BlockSpec((1, LANES), const)],
        out_shape=[jax.ShapeDtypeStruct((s, d), F32),
                   jax.ShapeDtypeStruct((s, d), F32),
                   jax.ShapeDtypeStruct((s, LANES), jnp.int32),
                   jax.ShapeDtypeStruct((s, LANES), F32),
                   jax.ShapeDtypeStruct((1, LANES), F32)],
        compiler_params=_cparams(("arbitrary",)),
        name="merge_route",
    )(x, y_ssm, y_attn, gates, gates, wps, wpa, wo, n2w.reshape(1, d).astype(F32), wr)


def _row_copy(src_hbm, row, dst_ref, r, sem):
    return pltpu.make_async_copy(src_hbm.at[pl.ds(row, 1), :], dst_ref.at[pl.ds(r, 1), :], sem)


def _gather_rows(src_hbm, idx_ref, base, stride, dst_ref, sem, n):
    def start(r, c):
        _row_copy(src_hbm, idx_ref[base + stride * r], dst_ref, r, sem).start()
        return c

    lax.fori_loop(0, n, start, 0, unroll=8)


def _wait_rows(src_hbm, dst_ref, sem, n):
    pltpu.make_async_copy(src_hbm.at[pl.ds(0, n), :], dst_ref, sem).wait()


def _scatter_tok_body(dest_ref, tok_ref):
    def zero(i, c):
        tok_ref[i] = 0
        return c

    lax.fori_loop(0, tok_ref.shape[0], zero, 0, unroll=8)

    def put(i, c):
        tok_ref[dest_ref[i]] = lax.shift_right_logical(i, TOP_K_LOG2)
        return c

    lax.fori_loop(0, dest_ref.shape[0], put, 0, unroll=8)


def _scatter_tok(dest, buf_len):
    return pl.pallas_call(
        _scatter_tok_body,
        in_specs=[pl.BlockSpec(memory_space=pltpu.SMEM)],
        out_specs=pl.BlockSpec(memory_space=pltpu.SMEM),
        out_shape=jax.ShapeDtypeStruct((buf_len,), jnp.int32),
        name="scatter_tok",
    )(dest)


def _expert_body(be_ref, nb_ref, tok_ref, h2_hbm, wg_ref, wu_ref, wd_ref, o_ref,
                 xb_ref, wgb_ref, wub_ref, wdb_ref, sem):
    b = pl.program_id(0)
    nb = nb_ref[0]

    @pl.when(b == 0)
    def _first():
        _gather_rows(h2_hbm, tok_ref, 0, 1, xb_ref.at[0], sem.at[0], MOE_BLK)

    @pl.when((b == 0) | (be_ref[b] != be_ref[jnp.maximum(b - 1, 0)]))
    def _new_expert():
        wgb_ref[...] = wg_ref[0].astype(BF16)
        wub_ref[...] = wu_ref[0].astype(BF16)
        wdb_ref[...] = wd_ref[0].astype(BF16)

    @pl.when(b < nb)
    def _run():
        cur = b % 2
        nxt = 1 - cur
        _wait_rows(h2_hbm, xb_ref.at[cur], sem.at[cur], MOE_BLK)
        base = (b + 1) * MOE_BLK
        for r in range(MOE_BLK):
            _row_copy(h2_hbm, tok_ref[base + r], xb_ref.at[nxt], r, sem.at[nxt]).start()
        xb = xb_ref[cur].astype(BF16)
        g = jnp.dot(xb, wgb_ref[...], preferred_element_type=F32)
        u = jnp.dot(xb, wub_ref[...], preferred_element_type=F32)
        a = (jax.nn.silu(g) * u).astype(BF16)
        o_ref[...] = jnp.dot(a, wdb_ref[...], preferred_element_type=F32)

        @pl.when(b == nb - 1)
        def _drain():
            _wait_rows(h2_hbm, xb_ref.at[nxt], sem.at[nxt], MOE_BLK)

    @pl.when(b >= nb)
    def _unused():
        o_ref[...] = jnp.zeros(o_ref.shape, o_ref.dtype)


def _experts(h2, block_e, n_used, buf_tok, wg, wu, wd):
    s, d = h2.shape
    buf_len = buf_tok.shape[0]
    n_blocks = buf_len // MOE_BLK
    wmap = lambda b, be, nb, tok: (be[b], 0, 0)
    return pl.pallas_call(
        _expert_body,
        grid_spec=pltpu.PrefetchScalarGridSpec(
            num_scalar_prefetch=3,
            grid=(n_blocks,),
            in_specs=[pl.BlockSpec(memory_space=pl.ANY),
                      pl.BlockSpec((1, d, D_FF), wmap),
                      pl.BlockSpec((1, d, D_FF), wmap),
                      pl.BlockSpec((1, D_FF, d), wmap)],
            out_specs=pl.BlockSpec((MOE_BLK, d), lambda b, be, nb, tok: (b, 0)),
            scratch_shapes=[pltpu.VMEM((2, MOE_BLK, d), F32),
                            pltpu.VMEM((d, D_FF), BF16), pltpu.VMEM((d, D_FF), BF16),
                            pltpu.VMEM((D_FF, d), BF16),
                            pltpu.SemaphoreType.DMA((2,))]),
        out_shape=jax.ShapeDtypeStruct((buf_len, d), F32),
        compiler_params=_cparams(("arbitrary",)),
        name="experts",
    )(block_e, n_used, buf_tok, h2, wg, wu, wd)


def _combine_body(pos_ref, x1_ref, gate_ref, fw_ref, ys_hbm, o_ref, g_ref, sem, *, tm):
    i = pl.program_id(0)

    def gather(tile, slot):
        for k in range(TOP_K):
            _gather_rows(ys_hbm, pos_ref, TOP_K * tile * tm + k, TOP_K, g_ref.at[slot, k],
                         sem.at[slot], tm)

    @pl.when(i == 0)
    def _first():
        gather(0, 0)

    @pl.when(i + 1 < pl.num_programs(0))
    def _prefetch():
        gather(i + 1, (i + 1) % 2)

    cur = i % 2
    for k in range(TOP_K):
        _wait_rows(ys_hbm, g_ref.at[cur, k], sem.at[cur], tm)
    gate = gate_ref[...]
    x = x1_ref[...] + (gate[:, 0:1] * g_ref[cur, 0] + gate[:, 1:2] * g_ref[cur, 1])
    y = x * lax.rsqrt(jnp.mean(x * x, axis=-1, keepdims=True) + NORM_EPS)
    o_ref[...] = y * fw_ref[...]


def _combine(x1, ys, pos, gate, fw, tm=256):
    s, d = x1.shape
    return pl.pallas_call(
        functools.partial(_combine_body, tm=tm),
        grid_spec=pltpu.PrefetchScalarGridSpec(
            num_scalar_prefetch=1,
            grid=(s // tm,),
            in_specs=[pl.BlockSpec((tm, d), lambda i, p: (i, 0)),
                      pl.BlockSpec((tm, LANES), lambda i, p: (i, 0)),
                      pl.BlockSpec((1, d), lambda i, p: (0, 0)),
                      pl.BlockSpec(memory_space=pl.ANY)],
            out_specs=pl.BlockSpec((tm, d), lambda i, p: (i, 0)),
            scratch_shapes=[pltpu.VMEM((2, TOP_K, tm, d), F32),
                            pltpu.SemaphoreType.DMA((2,))]),
        out_shape=jax.ShapeDtypeStruct((s, d), F32),
        compiler_params=_cparams(("arbitrary",)),
        name="combine_norm",
    )(pos, x1, gate, fw.reshape(1, d).astype(F32), ys)


def _dispatch_tables(eid, counts):
    n_tok = eid.shape[0]
    n_slots = n_tok * TOP_K
    buf_len = n_slots + N_EXPERTS * MOE_BLK
    n_blocks = buf_len // MOE_BLK
    counts = counts.astype(jnp.int32)
    padded = ((counts + MOE_BLK - 1) // MOE_BLK) * MOE_BLK
    padded_ends = jnp.cumsum(padded)
    padded_starts = padded_ends - padded
    experts = eid[:, :TOP_K]
    onehot = experts[:, :, None] == jnp.arange(N_EXPERTS, dtype=jnp.int32)
    dest = jnp.sum(jnp.where(onehot, padded_starts, 0), axis=-1) + eid[:, TOP_K:2 * TOP_K]
    block_start = jnp.arange(n_blocks, dtype=jnp.int32) * MOE_BLK
    n_used = (padded_ends[-1] // MOE_BLK).astype(jnp.int32).reshape(1)
    last_start = (n_used[0] - 1) * MOE_BLK
    block_e = jnp.sum(padded_ends[None, :] <= jnp.minimum(block_start, last_start)[:, None],
                      axis=-1).astype(jnp.int32)
    return block_e, n_used, dest.reshape(n_slots).astype(jnp.int32), buf_len


def kernel(x, positions, norm1_w, w_in, ssm_lambda_re, ssm_lambda_im, ssm_log_dt, ssm_b_re, ssm_b_im, ssm_c_re, ssm_c_im, ssm_d, ssm_glu_w, ssm_glu_b, attn_lambda_q1, attn_lambda_k1, attn_lambda_q2, attn_lambda_k2, attn_subln_w, w_proj_ssm, w_proj_attn, w_out, norm2_w, router_group_w, router_expert_w, expert_w_gate, expert_w_up, expert_w_down, final_norm_w):
    bsz, seq, d = x.shape
    depth = norm1_w.shape[0]
    xs = x.reshape(bsz * seq, d)
    pos = positions.reshape(bsz * seq)
    o_q = SSM_WIDTH
    o_v = o_q + 2 * ATTN_WIDTH
    o_g = o_v + ATTN_WIDTH
    for l in range(depth):
        h = _rmsnorm(xs, norm1_w[l].astype(F32), BF16)
        w_in_b = w_in[l].astype(F32)
        u = _mm(h, w_in_b, 0, SSM_WIDTH, "none", F32)
        qk = _mm_rope(h, w_in_b, pos, o_q)
        v = _mm_vones(h, w_in_b, o_v)
        gates = _mm(h, w_in_b, o_g, 2 * D_MODEL, "sigmoid", F32)

        tables = _s5_tables(ssm_lambda_re[l], ssm_lambda_im[l], ssm_log_dt[l],
                            ssm_b_re[l], ssm_b_im[l], ssm_c_re[l], ssm_c_im[l])
        y = _s5_core(u, tables, ssm_d[l])
        y_ssm = _glu(y, ssm_glu_w[l].astype(BF16), ssm_glu_b[l])

        lam_params = jnp.stack([attn_lambda_q1[l], attn_lambda_k1[l],
                                attn_lambda_q2[l], attn_lambda_k2[l]]).astype(F32)
        y_attn = _diff_attention(qk, v, lam_params, attn_subln_w[l])

        wr = jnp.concatenate([router_group_w[l], router_expert_w[l]], axis=1).astype(F32)
        wr = jnp.pad(wr, ((0, 0), (0, LANES - wr.shape[1])))
        x1, h2, eid, gate, counts = _merge_route(
            xs, y_ssm, y_attn, gates, w_proj_ssm[l].astype(BF16), w_proj_attn[l].astype(BF16),
            w_out[l].astype(BF16), norm2_w[l], wr)

        block_e, n_used, dest, buf_len = _dispatch_tables(eid, counts[0, :N_EXPERTS])
        buf_tok = _scatter_tok(dest, buf_len)
        ys = _experts(h2, block_e, n_used, buf_tok, expert_w_gate[l].astype(F32),
                      expert_w_up[l].astype(F32), expert_w_down[l].astype(F32))
        last = l == depth - 1
        assert last, "DEPTH > 1 needs an un-normalised combine"
        xs = _combine(x1, ys, dest, gate, final_norm_w)
    return xs.reshape(bsz, seq, d)
```

```python
import functools
import math

import jax
import jax.numpy as jnp
from jax import lax
from jax.experimental import pallas as pl
from jax.experimental.pallas import tpu as pltpu

F32 = jnp.float32
BF16 = jnp.bfloat16

D_MODEL = 2048
SSM_WIDTH = 1024
SSM_GROUP = 16
SSM_GROUPS = 64
SSM_STATE = 64
ATTN_WIDTH = 1024
ATTN_HEADS = 8
HEAD_DIM = 64
ROPE_THETA = 10000.0
N_GROUPS = 4
EXPERTS_PER_GROUP = 8
N_EXPERTS = 32
TOP_K = 2
TOP_K_LOG2 = 1
D_FF = 512
NORM_EPS = 1e-6
SUBLN_EPS = 1e-5
LAMBDA_INIT = 0.8 - 0.6 * math.exp(-0.3 * 0)

LANES = 128
VMEM_LIMIT = 48 * 1024 * 1024

SSM_CHUNK = 16
SSM_GB = 8
MOE_BLK = 128


def _cparams(sem):
    return pltpu.CompilerParams(dimension_semantics=sem, vmem_limit_bytes=VMEM_LIMIT)


def _rmsnorm_body(x_ref, w_ref, o_ref):
    x = x_ref[...]
    y = x * lax.rsqrt(jnp.mean(x * x, axis=-1, keepdims=True) + NORM_EPS)
    o_ref[...] = (y * w_ref[...]).astype(o_ref.dtype)


def _rmsnorm(x, w, out_dtype, tm=512):
    s, d = x.shape
    return pl.pallas_call(
        _rmsnorm_body,
        grid=(s // tm,),
        in_specs=[pl.BlockSpec((tm, d), lambda i: (i, 0)),
                  pl.BlockSpec((1, d), lambda i: (0, 0))],
        out_specs=pl.BlockSpec((tm, d), lambda i: (i, 0)),
        out_shape=jax.ShapeDtypeStruct((s, d), out_dtype),
        compiler_params=_cparams(("parallel",)),
        name="rmsnorm",
    )(x, w.reshape(1, d))


def _proj(h_ref, w_ref, wb_ref, row_axis):
    @pl.when(pl.program_id(row_axis) == 0)
    def _cast():
        wb_ref[...] = w_ref[...].astype(BF16)

    return jnp.dot(h_ref[...], wb_ref[...], preferred_element_type=F32)


def _mm_body(h_ref, w_ref, o_ref, wb_ref, *, act):
    acc = _proj(h_ref, w_ref, wb_ref, 1)
    if act == "sigmoid":
        acc = jax.nn.sigmoid(acc)
    o_ref[...] = acc.astype(o_ref.dtype)


def _mm(h, w, col0, ncols, act, out_dtype, tm=512, tn=1024):
    s, k = h.shape
    jb = col0 // tn
    return pl.pallas_call(
        functools.partial(_mm_body, act=act),
        grid=(ncols // tn, s // tm),
        in_specs=[pl.BlockSpec((tm, k), lambda j, i: (i, 0)),
                  pl.BlockSpec((k, tn), lambda j, i: (0, j + jb))],
        out_specs=pl.BlockSpec((tm, tn), lambda j, i: (i, j)),
        out_shape=jax.ShapeDtypeStruct((s, ncols), out_dtype),
        scratch_shapes=[pltpu.VMEM((k, tn), BF16)],
        compiler_params=_cparams(("arbitrary", "arbitrary")),
        name="mm_" + act,
    )(h, w)


def _mm_vones_body(h_ref, w_ref, o_ref, wb_ref):
    acc = _proj(h_ref, w_ref, wb_ref, 0)
    hw = 2 * HEAD_DIM
    ones = jnp.ones((acc.shape[0], hw), o_ref.dtype)
    for hd in range(acc.shape[1] // hw):
        o_ref[:, 2 * hd * hw:(2 * hd + 1) * hw] = acc[:, hd * hw:(hd + 1) * hw].astype(o_ref.dtype)
        o_ref[:, (2 * hd + 1) * hw:(2 * hd + 2) * hw] = ones


def _mm_vones(h, w, col0, tm=512):
    s, k = h.shape
    tn = ATTN_WIDTH
    jb = col0 // tn
    return pl.pallas_call(
        _mm_vones_body,
        grid=(s // tm,),
        in_specs=[pl.BlockSpec((tm, k), lambda i: (i, 0)),
                  pl.BlockSpec((k, tn), lambda i: (0, jb))],
        out_specs=pl.BlockSpec((tm, 2 * tn), lambda i: (i, 0)),
        out_shape=jax.ShapeDtypeStruct((s, 2 * tn), BF16),
        scratch_shapes=[pltpu.VMEM((k, tn), BF16)],
        compiler_params=_cparams(("arbitrary",)),
        name="mm_vones",
    )(h, w)


def _mm_rope_body(h_ref, w_ref, pos_ref, invf_ref, o_ref, wb_ref, *, tn):
    j = pl.program_id(0)
    acc = _proj(h_ref, w_ref, wb_ref, 1)
    scale = jnp.where(j == 0, HEAD_DIM ** -0.5, 1.0).astype(F32)
    ang = pos_ref[...].astype(F32) * invf_ref[...]
    lane = lax.broadcasted_iota(jnp.int32, ang.shape, 1)
    low = (lane % HEAD_DIM) < (HEAD_DIM // 2)
    cos = jnp.cos(ang) * scale
    sin = jnp.where(low, -jnp.sin(ang), jnp.sin(ang)) * scale
    for c in range(tn // LANES):
        t = acc[:, c * LANES:(c + 1) * LANES]
        partner = jnp.where(low, pltpu.roll(t, LANES - HEAD_DIM // 2, 1),
                            pltpu.roll(t, HEAD_DIM // 2, 1))
        o_ref[:, c * LANES:(c + 1) * LANES] = (t * cos + partner * sin).astype(o_ref.dtype)


def _mm_rope(h, w, positions, col0, tm=512, tn=1024):
    s, k = h.shape
    jb = col0 // tn
    inv_freq = 1.0 / (ROPE_THETA ** (jnp.arange(0, HEAD_DIM, 2, dtype=F32) / HEAD_DIM))
    invf = jnp.tile(inv_freq, LANES // (HEAD_DIM // 2)).reshape(1, LANES)
    return pl.pallas_call(
        functools.partial(_mm_rope_body, tn=tn),
        grid=(2, s // tm),
        in_specs=[pl.BlockSpec((tm, k), lambda j, i: (i, 0)),
                  pl.BlockSpec((k, tn), lambda j, i: (0, j + jb)),
                  pl.BlockSpec((tm, 1), lambda j, i: (i, 0)),
                  pl.BlockSpec((1, LANES), lambda j, i: (0, 0))],
        out_specs=pl.BlockSpec((tm, tn), lambda j, i: (i, j)),
        out_shape=jax.ShapeDtypeStruct((s, 2 * tn), BF16),
        scratch_shapes=[pltpu.VMEM((k, tn), BF16)],
        compiler_params=_cparams(("arbitrary", "arbitrary")),
        name="mm_rope",
    )(h, w, positions.reshape(s, 1), invf)


def _s5_tables(lam_re, lam_im, log_dt, b_re, b_im, c_re, c_im):
    L, G, P, M = SSM_CHUNK, SSM_GROUPS, SSM_STATE, SSM_GROUP
    hi = lax.Precision.HIGHEST
    lr, li = lam_re.astype(F32), lam_im.astype(F32)
    dt = jnp.exp(log_dt.astype(F32))[:, None]
    n = jnp.arange(L + 1, dtype=F32)[:, None, None]
    mag = jnp.exp(lr * dt * n)
    pr = mag * jnp.cos(li * dt * n)
    pi = mag * jnp.sin(li * dt * n)
    nr, ni = pr[1] - 1.0, pi[1]
    den = lr * lr + li * li
    f_re = (nr * lr + ni * li) / den
    f_im = (ni * lr - nr * li) / den
    br, bi = b_re.astype(F32), b_im.astype(F32)
    bbr = f_re[..., None] * br - f_im[..., None] * bi
    bbi = f_re[..., None] * bi + f_im[..., None] * br
    cr, ci = c_re.astype(F32), c_im.astype(F32)

    eye2 = jnp.eye(2, dtype=F32)
    pr_g = jnp.transpose(pr, (1, 0, 2))
    pi_g = jnp.transpose(pi, (1, 0, 2))
    bbr_t = jnp.transpose(bbr, (0, 2, 1))
    bbi_t = jnp.transpose(bbi, (0, 2, 1))

    abr = pr_g[:, :, None, :] * bbr_t[:, None] - pi_g[:, :, None, :] * bbi_t[:, None]
    abi = pr_g[:, :, None, :] * bbi_t[:, None] + pi_g[:, :, None, :] * bbr_t[:, None]

    kk = (jnp.einsum('gnip,gop->gino', abr[:, :L], cr, precision=hi)
          - jnp.einsum('gnip,gop->gino', abi[:, :L], ci, precision=hi)).reshape(G, M, L * M)
    kpad = jnp.pad(kk, ((0, 0), (0, 0), ((L - 1) * M, 0)))
    t_tab = jnp.stack([kpad[:, :, (L - 1 - i) * M:(L - 1 - i) * M + L * M] for i in range(L)],
                      axis=1).reshape(G, L * M, L * M)

    wc = jnp.stack([abr[:, L - 1::-1], abi[:, L - 1::-1]], axis=3)
    wc = wc.reshape(G // 2, 2, L * M, 2, 1, P) * eye2[None, :, None, None, :, None]
    w_tab = wc.reshape(G // 2, 2 * L * M, 4 * P)

    cr_t = jnp.transpose(cr, (0, 2, 1))
    ci_t = jnp.transpose(ci, (0, 2, 1))
    ar1 = jnp.transpose(pr_g[:, 1:], (0, 2, 1))[..., None]
    ai1 = jnp.transpose(pi_g[:, 1:], (0, 2, 1))[..., None]
    vr = cr_t[:, :, None, :] * ar1 - ci_t[:, :, None, :] * ai1
    vi = -(cr_t[:, :, None, :] * ai1 + ci_t[:, :, None, :] * ar1)
    vc = jnp.stack([vr, vi], axis=1).reshape(G // 2, 2, 2, P, L * M)
    vc = jnp.transpose(vc, (0, 2, 3, 1, 4))[:, :, None] * eye2[None, None, :, None, :, None]
    v_tab = vc.reshape(G // 2, 4 * P, 2 * L * M)

    al = jnp.stack([pr[L].reshape(G * P), pi[L].reshape(G * P)])
    return t_tab.astype(BF16), w_tab.astype(BF16), v_tab.astype(BF16), al


def _s5_body(u_ref, t_ref, w_ref, v_ref, al_ref, d_ref, y_ref, er_ref, ei_ref, xr_ref, xi_ref):
    nc = u_ref.shape[0]
    lm = SSM_CHUNK * SSM_GROUP
    pw = 2 * SSM_STATE
    npair = SSM_GB // 2
    for pp in range(npair):
        up = u_ref[:, pp * 2 * lm:(pp + 1) * 2 * lm].astype(BF16)
        e = jnp.dot(up, w_ref[pp], preferred_element_type=F32)
        er_ref[:, pp * pw:(pp + 1) * pw] = e[:, :pw]
        ei_ref[:, pp * pw:(pp + 1) * pw] = e[:, pw:]

    ar = al_ref[0:1, :]
    ai = al_ref[1:2, :]

    def step(c, carry):
        xr, xi = carry
        xr_ref[pl.ds(c, 1), :] = xr
        xi_ref[pl.ds(c, 1), :] = xi
        er = er_ref[pl.ds(c, 1), :]
        ei = ei_ref[pl.ds(c, 1), :]
        return ar * xr - ai * xi + er, ar * xi + ai * xr + ei

    zero = jnp.zeros((1, npair * pw), F32)
    lax.fori_loop(0, nc, step, (zero, zero))

    for pp in range(npair):
        xin = jnp.concatenate([xr_ref[:, pp * pw:(pp + 1) * pw],
                               xi_ref[:, pp * pw:(pp + 1) * pw]], axis=1).astype(BF16)
        yc = jnp.dot(xin, v_ref[pp], preferred_element_type=F32)
        for q in range(2):
            g = pp * 2 + q
            uf = u_ref[:, g * lm:(g + 1) * lm]
            yi = jnp.dot(uf.astype(BF16), t_ref[g], preferred_element_type=F32)
            y = yi + yc[:, q * lm:(q + 1) * lm] + d_ref[:, g * lm:(g + 1) * lm] * uf
            y_ref[:, g * lm:(g + 1) * lm] = jax.nn.gelu(y)


def _s5_core(u, tables, d_skip):
    s = u.shape[0]
    L, G, M, P = SSM_CHUNK, SSM_GROUPS, SSM_GROUP, SSM_STATE
    nc = s // L
    lm = L * M
    t_tab, w_tab, v_tab, al = tables
    u2 = u.reshape(nc, L, G, M).transpose(0, 2, 1, 3).reshape(nc, G * lm)
    d2 = jnp.broadcast_to(d_skip.astype(F32).reshape(1, G, 1, M), (1, G, L, M)).reshape(1, G * lm)
    gb = SSM_GB
    y2 = pl.pallas_call(
        _s5_body,
        grid=(G // gb,),
        in_specs=[pl.BlockSpec((nc, gb * lm), lambda b: (0, b)),
                  pl.BlockSpec((gb, lm, lm), lambda b: (b, 0, 0)),
                  pl.BlockSpec((gb // 2, 2 * lm, 4 * P), lambda b: (b, 0, 0)),
                  pl.BlockSpec((gb // 2, 4 * P, 2 * lm), lambda b: (b, 0, 0)),
                  pl.BlockSpec((2, gb * P), lambda b: (0, b)),
                  pl.BlockSpec((1, gb * lm), lambda b: (0, b))],
        out_specs=pl.BlockSpec((nc, gb * lm), lambda b: (0, b)),
        out_shape=jax.ShapeDtypeStruct((nc, G * lm), F32),
        scratch_shapes=[pltpu.VMEM((nc, gb * P), F32) for _ in range(4)],
        compiler_params=_cparams(("parallel",)),
        name="s5_scan",
    )(u2, t_tab, w_tab, v_tab, al, d2)
    return y2.reshape(nc, G, L, M).transpose(0, 2, 1, 3).reshape(s, G * M)


def _glu_body(y_ref, w_ref, b_ref, o_ref):
    y = y_ref[...]
    z = jnp.dot(y.astype(BF16), w_ref[...], preferred_element_type=F32) + b_ref[...]
    o_ref[...] = (y * jax.nn.sigmoid(z)).astype(o_ref.dtype)


def _glu(y, w, b, tm=512):
    s, n = y.shape
    return pl.pallas_call(
        _glu_body,
        grid=(s // tm,),
        in_specs=[pl.BlockSpec((tm, n), lambda i: (i, 0)),
                  pl.BlockSpec((n, n), lambda i: (0, 0)),
                  pl.BlockSpec((1, n), lambda i: (0, 0))],
        out_specs=pl.BlockSpec((tm, n), lambda i: (i, 0)),
        out_shape=jax.ShapeDtypeStruct((s, n), BF16),
        compiler_params=_cparams(("parallel",)),
        name="s5_glu",
    )(y, w, b.reshape(1, n).astype(F32))


def _attn_body(q_ref, k_ref, v_ref, lam_ref, sw_ref, o_ref,
               qs_ref, m_ref, acc_ref, s_ref, p_ref, al_ref, *, tq):
    qi = pl.program_id(1)
    hw = 2 * HEAD_DIM
    q = q_ref[...]
    lane = lax.broadcasted_iota(jnp.int32, q.shape, 1)
    zero = jnp.zeros_like(q)
    qs_ref[0] = jnp.where(lane < HEAD_DIM, q, zero)
    qs_ref[1] = jnp.where(lane >= HEAD_DIM, q, zero)
    m_ref[...] = jnp.full(m_ref.shape, -jnp.inf, F32)
    acc_ref[...] = jnp.zeros(acc_ref.shape, F32)

    def scores(c, slot, masked=False):
        off = pl.multiple_of(c * tq, tq)
        k = k_ref[pl.ds(off, tq), :]
        for mp in range(2):
            s = lax.dot_general(qs_ref[mp], k, (((1,), (1,)), ((), ())),
                                preferred_element_type=F32)
            if masked:
                row = lax.broadcasted_iota(jnp.int32, s.shape, 0)
                col = lax.broadcasted_iota(jnp.int32, s.shape, 1)
                s = jnp.where(col <= row, s, -jnp.inf)
            s_ref[slot, mp] = s

    def softmax(slot):
        for mp in range(2):
            s = s_ref[slot, mp]
            m_old = m_ref[mp]
            m_new = jnp.maximum(m_old, jnp.max(s, axis=-1, keepdims=True))
            al_ref[slot, mp] = jnp.exp(m_old - m_new)
            p_ref[slot, mp] = jnp.exp(s - m_new).astype(BF16)
            m_ref[mp] = m_new

    def pv(c, slot):
        off = pl.multiple_of(c * tq, tq)
        v = v_ref[pl.ds(off, tq), :]
        for mp in range(2):
            acc_ref[mp] = al_ref[slot, mp] * acc_ref[mp] + jnp.dot(
                p_ref[slot, mp], v, preferred_element_type=F32)

    @pl.when(qi == 0)
    def _one_chunk():
        scores(0, 0, masked=True)
        softmax(0)
        pv(0, 0)

    @pl.when(qi == 1)
    def _two_chunks():
        scores(0, 1)
        softmax(1)
        pv(0, 1)
        scores(1, 0, masked=True)
        softmax(0)
        pv(1, 0)

    @pl.when(qi >= 2)
    def _pipelined():
        odd = qi % 2

        @pl.when(odd == 0)
        def _fill_even():
            scores(0, 0)
            scores(1, 1)
            softmax(0)

        @pl.when(odd == 1)
        def _fill_odd():
            scores(0, 1)
            scores(1, 0)
            softmax(1)
            scores(2, 1)
            softmax(0)
            pv(0, 1)

        t0 = 2 + odd

        def two_steps(u, carry):
            t = t0 + 2 * u
            scores(t, 0)
            softmax(1)
            pv(t - 2, 0)
            scores(t + 1, 1)
            softmax(0)
            pv(t - 1, 1)
            return carry

        lax.fori_loop(0, (qi - t0) // 2, two_steps, 0)
        scores(qi, 0, masked=True)
        softmax(1)
        pv(qi - 2, 0)
        softmax(0)
        pv(qi - 1, 1)
        pv(qi, 0)

    lam = (jnp.exp(jnp.sum(lam_ref[0:1, :] * lam_ref[1:2, :]))
           - jnp.exp(jnp.sum(lam_ref[2:3, :] * lam_ref[3:4, :])) + LAMBDA_INIT)
    a1 = acc_ref[0]
    a2 = acc_ref[1]
    o = a1[:, :hw] / a1[:, hw:] - lam * (a2[:, :hw] / a2[:, hw:])
    o = o * lax.rsqrt(jnp.mean(o * o, axis=-1, keepdims=True) + SUBLN_EPS)
    o_ref[...] = (o * sw_ref[...] * (1.0 - LAMBDA_INIT)).astype(o_ref.dtype)


def _diff_attention(qk, v1, lam_params, subln_w, tq=512):
    s = qk.shape[0]
    nq = s // tq
    hw = 2 * HEAD_DIM
    kcol0 = ATTN_WIDTH // hw
    return pl.pallas_call(
        functools.partial(_attn_body, tq=tq),
        grid=(ATTN_HEADS, nq),
        in_specs=[pl.BlockSpec((tq, hw), lambda h, i: (i, h)),
                  pl.BlockSpec((s, hw), lambda h, i: (0, kcol0 + h)),
                  pl.BlockSpec((s, 2 * hw), lambda h, i: (0, h)),
                  pl.BlockSpec((4, HEAD_DIM), lambda h, i: (0, 0)),
                  pl.BlockSpec((1, hw), lambda h, i: (0, 0))],
        out_specs=pl.BlockSpec((tq, hw), lambda h, i: (i, h)),
        out_shape=jax.ShapeDtypeStruct((s, ATTN_WIDTH), BF16),
        scratch_shapes=[pltpu.VMEM((2, tq, hw), BF16),
                        pltpu.VMEM((2, tq, 1), F32),
                        pltpu.VMEM((2, tq, 2 * hw), F32),
                        pltpu.VMEM((2, 2, tq, tq), F32),
                        pltpu.VMEM((2, 2, tq, tq), BF16),
                        pltpu.VMEM((2, 2, tq, 1), F32)],
        compiler_params=_cparams(("parallel", "arbitrary")),
        name="diff_attn",
    )(qk, qk, v1, lam_params, subln_w.reshape(1, hw).astype(F32))


def _merge_body(x_ref, ys_ref, ya_ref, gs_ref, ga_ref, wps_ref, wpa_ref, wo_ref, n2_ref, wr_ref,
                x1_ref, h2_ref, eid_ref, gate_ref, cnt_ref):
    ps = jnp.dot(ys_ref[...], wps_ref[...], preferred_element_type=F32)
    pa = jnp.dot(ya_ref[...], wpa_ref[...], preferred_element_type=F32)
    merged = gs_ref[...].astype(F32) * ps + ga_ref[...].astype(F32) * pa
    x1 = x_ref[...] + jnp.dot(merged.astype(BF16), wo_ref[...], preferred_element_type=F32)
    x1_ref[...] = x1
    h2 = x1 * lax.rsqrt(jnp.mean(x1 * x1, axis=-1, keepdims=True) + NORM_EPS) * n2_ref[...]
    h2_ref[...] = h2
    wr = wr_ref[...]
    h_hi = h2.astype(BF16)
    h_lo = (h2 - h_hi.astype(F32)).astype(BF16)
    w_hi = wr.astype(BF16)
    w_lo = (wr - w_hi.astype(F32)).astype(BF16)
    logits = (jnp.dot(h_hi, w_hi, preferred_element_type=F32)
              + jnp.dot(h_lo, w_hi, preferred_element_type=F32)
              + jnp.dot(h_hi, w_lo, preferred_element_type=F32))
    lane = lax.broadcasted_iota(jnp.int32, logits.shape, 1)
    big = jnp.int32(1 << 20)
    ninf = jnp.float32(-jnp.inf)
    is_g = lane < N_GROUPS
    gl = jnp.where(is_g, logits, ninf)
    gm = jnp.max(gl, axis=-1, keepdims=True)
    g_idx = jnp.min(jnp.where(gl == gm, lane, big), axis=-1, keepdims=True)
    g_val = 1.0 / jnp.sum(jnp.where(is_g, jnp.exp(gl - gm), 0.0), axis=-1, keepdims=True)
    lo = N_GROUPS + g_idx * EXPERTS_PER_GROUP
    in_grp = (lane >= lo) & (lane < lo + EXPERTS_PER_GROUP)
    el = jnp.where(in_grp, logits, ninf)
    e1 = jnp.max(el, axis=-1, keepdims=True)
    i1 = jnp.min(jnp.where(el == e1, lane, big), axis=-1, keepdims=True)
    el2 = jnp.where(lane == i1, ninf, el)
    e2 = jnp.max(el2, axis=-1, keepdims=True)
    i2 = jnp.min(jnp.where(el2 == e2, lane, big), axis=-1, keepdims=True)
    t = jnp.exp(e2 - e1)
    p1 = 1.0 / (1.0 + t)
    p2 = t / (1.0 + t)
    gate_ref[...] = jnp.where(lane == 0, p1 * g_val, jnp.where(lane == 1, p2 * g_val, 0.0))

    @pl.when(pl.program_id(0) == 0)
    def _zero_counts():
        cnt_ref[...] = jnp.zeros(cnt_ref.shape, F32)

    ex1 = i1 - N_GROUPS
    ex2 = i2 - N_GROUPS
    oh1 = lane == ex1
    oh2 = lane == ex2
    onehot = jnp.where(oh1 | oh2, 1.0, 0.0)
    tm = logits.shape[0]
    r_i = lax.broadcasted_iota(jnp.int32, (tm, tm), 0)
    c_i = lax.broadcasted_iota(jnp.int32, (tm, tm), 1)
    tri = jnp.where(c_i < r_i, 1.0, 0.0).astype(BF16)
    before = jnp.dot(tri, onehot.astype(BF16), preferred_element_type=F32) + cnt_ref[...]
    rank1 = jnp.sum(jnp.where(oh1, before, 0.0), axis=-1, keepdims=True).astype(jnp.int32)
    rank2 = jnp.sum(jnp.where(oh2, before, 0.0), axis=-1, keepdims=True).astype(jnp.int32)
    cnt_ref[...] = cnt_ref[...] + jnp.sum(onehot, axis=0, keepdims=True)
    eid_ref[...] = jnp.where(lane == 0, ex1, jnp.where(lane == 1, ex2,
                             jnp.where(lane == 2, rank1, jnp.where(lane == 3, rank2, 0))))


def _merge_route(x, y_ssm, y_attn, gates, wps, wpa, wo, n2w, wr, tm=256):
    s, d = x.shape
    nw = y_ssm.shape[1]
    const = lambda i: (0, 0)
    return pl.pallas_call(
        _merge_body,
        grid=(s // tm,),
        in_specs=[pl.BlockSpec((tm, d), lambda i: (i, 0)),
                  pl.BlockSpec((tm, nw), lambda i: (i, 0)),
                  pl.BlockSpec((tm, nw), lambda i: (i, 0)),
                  pl.BlockSpec((tm, d), lambda i: (i, 0)),
                  pl.BlockSpec((tm, d), lambda i: (i, 1)),
                  pl.BlockSpec((nw, d), const),
                  pl.BlockSpec((nw, d), const),
                  pl.BlockSpec((d, d), const),
                  pl.BlockSpec((1, d), const),
                  pl.BlockSpec((d, LANES), const)],
        out_specs=[pl.BlockSpec((tm, d), lambda i: (i, 0)),
                   pl.BlockSpec((tm, d), lambda i: (i, 0)),
                   pl.BlockSpec((tm, LANES), lambda i: (i, 0)),
                   pl.BlockSpec((tm, LANES), lambda i: (i, 0)),
                   pl.BlockSpec((1, LANES), const)],
        out_shape=[jax.ShapeDtypeStruct((s, d), F32),
                   jax.ShapeDtypeStruct((s, d), F32),
                   jax.ShapeDtypeStruct((s, LANES), jnp.int32),
                   jax.ShapeDtypeStruct((s, LANES), F32),
                   jax.ShapeDtypeStruct((1, LANES), F32)],
        compiler_params=_cparams(("arbitrary",)),
        name="merge_route",
    )(x, y_ssm, y_attn, gates, gates, wps, wpa, wo, n2w.reshape(1, d).astype(F32), wr)


def _row_copy(src_hbm, row, dst_ref, r, sem):
    return pltpu.make_async_copy(src_hbm.at[pl.ds(row, 1), :], dst_ref.at[pl.ds(r, 1), :], sem)


def _gather_rows(src_hbm, idx_ref, base, stride, dst_ref, sem, n):
    def start(r, c):
        _row_copy(src_hbm, idx_ref[base + stride * r], dst_ref, r, sem).start()
        return c

    lax.fori_loop(0, n, start, 0, unroll=8)


def _wait_rows(src_hbm, dst_ref, sem, n):
    pltpu.make_async_copy(src_hbm.at[pl.ds(0, n), :], dst_ref, sem).wait()


def _scatter_tok_body(dest_ref, tok_ref):
    def zero(i, c):
        tok_ref[i] = 0
        return c

    lax.fori_loop(0, tok_ref.shape[0], zero, 0, unroll=8)

    def put(i, c):
        tok_ref[dest_ref[i]] = lax.shift_right_logical(i, TOP_K_LOG2)
        return c

    lax.fori_loop(0, dest_ref.shape[0], put, 0, unroll=8)


def _scatter_tok(dest, buf_len):
    return pl.pallas_call(
        _scatter_tok_body,
        in_specs=[pl.BlockSpec(memory_space=pltpu.SMEM)],
        out_specs=pl.BlockSpec(memory_space=pltpu.SMEM),
        out_shape=jax.ShapeDtypeStruct((buf_len,), jnp.int32),
        name="scatter_tok",
    )(dest)


def _expert_body(be_ref, nb_ref, tok_ref, h2_hbm, wg_ref, wu_ref, wd_ref, o_ref,
                 xb_ref, wgb_ref, wub_ref, wdb_ref, sem):
    b = pl.program_id(0)
    nb = nb_ref[0]

    @pl.when(b == 0)
    def _first():
        _gather_rows(h2_hbm, tok_ref, 0, 1, xb_ref.at[0], sem.at[0], MOE_BLK)

    @pl.when((b == 0) | (be_ref[b] != be_ref[jnp.maximum(b - 1, 0)]))
    def _new_expert():
        wgb_ref[...] = wg_ref[0].astype(BF16)
        wub_ref[...] = wu_ref[0].astype(BF16)
        wdb_ref[...] = wd_ref[0].astype(BF16)

    @pl.when(b < nb)
    def _run():
        cur = b % 2
        nxt = 1 - cur
        _wait_rows(h2_hbm, xb_ref.at[cur], sem.at[cur], MOE_BLK)
        base = (b + 1) * MOE_BLK
        for r in range(MOE_BLK):
            _row_copy(h2_hbm, tok_ref[base + r], xb_ref.at[nxt], r, sem.at[nxt]).start()
        xb = xb_ref[cur].astype(BF16)
        g = jnp.dot(xb, wgb_ref[...], preferred_element_type=F32)
        u = jnp.dot(xb, wub_ref[...], preferred_element_type=F32)
        a = (jax.nn.silu(g) * u).astype(BF16)
        o_ref[...] = jnp.dot(a, wdb_ref[...], preferred_element_type=F32)

        @pl.when(b == nb - 1)
        def _drain():
            _wait_rows(h2_hbm, xb_ref.at[nxt], sem.at[nxt], MOE_BLK)

    @pl.when(b >= nb)
    def _unused():
        o_ref[...] = jnp.zeros(o_ref.shape, o_ref.dtype)


def _experts(h2, block_e, n_used, buf_tok, wg, wu, wd):
    s, d = h2.shape
    buf_len = buf_tok.shape[0]
    n_blocks = buf_len // MOE_BLK
    wmap = lambda b, be, nb, tok: (be[b], 0, 0)
    return pl.pallas_call(
        _expert_body,
        grid_spec=pltpu.PrefetchScalarGridSpec(
            num_scalar_prefetch=3,
            grid=(n_blocks,),
            in_specs=[pl.BlockSpec(memory_space=pl.ANY),
                      pl.BlockSpec((1, d, D_FF), wmap),
                      pl.BlockSpec((1, d, D_FF), wmap),
                      pl.BlockSpec((1, D_FF, d), wmap)],
            out_specs=pl.BlockSpec((MOE_BLK, d), lambda b, be, nb, tok: (b, 0)),
            scratch_shapes=[pltpu.VMEM((2, MOE_BLK, d), F32),
                            pltpu.VMEM((d, D_FF), BF16), pltpu.VMEM((d, D_FF), BF16),
                            pltpu.VMEM((D_FF, d), BF16),
                            pltpu.SemaphoreType.DMA((2,))]),
        out_shape=jax.ShapeDtypeStruct((buf_len, d), F32),
        compiler_params=_cparams(("arbitrary",)),
        name="experts",
    )(block_e, n_used, buf_tok, h2, wg, wu, wd)


def _combine_body(pos_ref, x1_ref, gate_ref, fw_ref, ys_hbm, o_ref, g_ref, sem, *, tm):
    i = pl.program_id(0)

    def gather(tile, slot):
        for k in range(TOP_K):
            _gather_rows(ys_hbm, pos_ref, TOP_K * tile * tm + k, TOP_K, g_ref.at[slot, k],
                         sem.at[slot], tm)

    @pl.when(i == 0)
    def _first():
        gather(0, 0)

    @pl.when(i + 1 < pl.num_programs(0))
    def _prefetch():
        gather(i + 1, (i + 1) % 2)

    cur = i % 2
    for k in range(TOP_K):
        _wait_rows(ys_hbm, g_ref.at[cur, k], sem.at[cur], tm)
    gate = gate_ref[...]
    x = x1_ref[...] + (gate[:, 0:1] * g_ref[cur, 0] + gate[:, 1:2] * g_ref[cur, 1])
    y = x * lax.rsqrt(jnp.mean(x * x, axis=-1, keepdims=True) + NORM_EPS)
    o_ref[...] = y * fw_ref[...]


def _combine(x1, ys, pos, gate, fw, tm=256):
    s, d = x1.shape
    return pl.pallas_call(
        functools.partial(_combine_body, tm=tm),
        grid_spec=pltpu.PrefetchScalarGridSpec(
            num_scalar_prefetch=1,
            grid=(s // tm,),
            in_specs=[pl.BlockSpec((tm, d), lambda i, p: (i, 0)),
                      pl.BlockSpec((tm, LANES), lambda i, p: (i, 0)),
                      pl.BlockSpec((1, d), lambda i, p: (0, 0)),
                      pl.BlockSpec(memory_space=pl.ANY)],
            out_specs=pl.BlockSpec((tm, d), lambda i, p: (i, 0)),
            scratch_shapes=[pltpu.VMEM((2, TOP_K, tm, d), F32),
                            pltpu.SemaphoreType.DMA((2,))]),
        out_shape=jax.ShapeDtypeStruct((s, d), F32),
        compiler_params=_cparams(("arbitrary",)),
        name="combine_norm",
    )(pos, x1, gate, fw.reshape(1, d).astype(F32), ys)


def _dispatch_tables(eid, counts):
    n_tok = eid.shape[0]
    n_slots = n_tok * TOP_K
    buf_len = n_slots + N_EXPERTS * MOE_BLK
    n_blocks = buf_len // MOE_BLK
    counts = counts.astype(jnp.int32)
    padded = ((counts + MOE_BLK - 1) // MOE_BLK) * MOE_BLK
    padded_ends = jnp.cumsum(padded)
    padded_starts = padded_ends - padded
    experts = eid[:, :TOP_K]
    onehot = experts[:, :, None] == jnp.arange(N_EXPERTS, dtype=jnp.int32)
    dest = jnp.sum(jnp.where(onehot, padded_starts, 0), axis=-1) + eid[:, TOP_K:2 * TOP_K]
    block_start = jnp.arange(n_blocks, dtype=jnp.int32) * MOE_BLK
    n_used = (padded_ends[-1] // MOE_BLK).astype(jnp.int32).reshape(1)
    last_start = (n_used[0] - 1) * MOE_BLK
    block_e = jnp.sum(padded_ends[None, :] <= jnp.minimum(block_start, last_start)[:, None],
                      axis=-1).astype(jnp.int32)
    return block_e, n_used, dest.reshape(n_slots).astype(jnp.int32), buf_len


def kernel(x, positions, norm1_w, w_in, ssm_lambda_re, ssm_lambda_im, ssm_log_dt, ssm_b_re, ssm_b_im, ssm_c_re, ssm_c_im, ssm_d, ssm_glu_w, ssm_glu_b, attn_lambda_q1, attn_lambda_k1, attn_lambda_q2, attn_lambda_k2, attn_subln_w, w_proj_ssm, w_proj_attn, w_out, norm2_w, router_group_w, router_expert_w, expert_w_gate, expert_w_up, expert_w_down, final_norm_w):
    bsz, seq, d = x.shape
    depth = norm1_w.shape[0]
    xs = x.reshape(bsz * seq, d)
    pos = positions.reshape(bsz * seq)
    o_q = SSM_WIDTH
    o_v = o_q + 2 * ATTN_WIDTH
    o_g = o_v + ATTN_WIDTH
    for l in range(depth):
        h = _rmsnorm(xs, norm1_w[l].astype(F32), BF16)
        w_in_b = w_in[l].astype(F32)
        u = _mm(h, w_in_b, 0, SSM_WIDTH, "none", F32)
        qk = _mm_rope(h, w_in_b, pos, o_q)
        v = _mm_vones(h, w_in_b, o_v)
        gates = _mm(h, w_in_b, o_g, 2 * D_MODEL, "sigmoid", F32)

        tables = _s5_tables(ssm_lambda_re[l], ssm_lambda_im[l], ssm_log_dt[l],
                            ssm_b_re[l], ssm_b_im[l], ssm_c_re[l], ssm_c_im[l])
        y = _s5_core(u, tables, ssm_d[l])
        y_ssm = _glu(y, ssm_glu_w[l].astype(BF16), ssm_glu_b[l])

        lam_params = jnp.stack([attn_lambda_q1[l], attn_lambda_k1[l],
                                attn_lambda_q2[l], attn_lambda_k2[l]]).astype(F32)
        y_attn = _diff_attention(qk, v, lam_params, attn_subln_w[l])

        wr = jnp.concatenate([router_group_w[l], router_expert_w[l]], axis=1).astype(F32)
        wr = jnp.pad(wr, ((0, 0), (0, LANES - wr.shape[1])))
        x1, h2, eid, gate, counts = _merge_route(
            xs, y_ssm, y_attn, gates, w_proj_ssm[l].astype(BF16), w_proj_attn[l].astype(BF16),
            w_out[l].astype(BF16), norm2_w[l], wr)

        block_e, n_used, dest, buf_len = _dispatch_tables(eid, counts[0, :N_EXPERTS])
        buf_tok = _scatter_tok(dest, buf_len)
        ys = _experts(h2, block_e, n_used, buf_tok, expert_w_gate[l].astype(F32),
                      expert_w_up[l].astype(F32), expert_w_down[l].astype(F32))
        last = l == depth - 1
        assert last, "DEPTH > 1 needs an un-normalised combine"
        xs = _combine(x1, ys, dest, gate, final_norm_w)
    return xs.reshape(bsz, seq, d)
```

```python
import functools
import math

import jax
import jax.numpy as jnp
from jax import lax
from jax.experimental import pallas as pl
from jax.experimental.pallas import tpu as pltpu

F32 = jnp.float32
BF16 = jnp.bfloat16

D_MODEL = 2048
SSM_WIDTH = 1024
SSM_GROUP = 16
SSM_GROUPS = 64
SSM_STATE = 64
ATTN_WIDTH = 1024
ATTN_HEADS = 8
HEAD_DIM = 64
ROPE_THETA = 10000.0
N_GROUPS = 4
EXPERTS_PER_GROUP = 8
N_EXPERTS = 32
TOP_K = 2
TOP_K_LOG2 = 1
D_FF = 512
NORM_EPS = 1e-6
SUBLN_EPS = 1e-5
LAMBDA_INIT = 0.8 - 0.6 * math.exp(-0.3 * 0)

LANES = 128
VMEM_LIMIT = 48 * 1024 * 1024

SSM_CHUNK = 16
SSM_GB = 8
MOE_BLK = 128


def _cparams(sem):
    return pltpu.CompilerParams(dimension_semantics=sem, vmem_limit_bytes=VMEM_LIMIT)


def _rmsnorm_body(x_ref, w_ref, o_ref):
    x = x_ref[...]
    y = x * lax.rsqrt(jnp.mean(x * x, axis=-1, keepdims=True) + NORM_EPS)
    o_ref[...] = (y * w_ref[...]).astype(o_ref.dtype)


def _rmsnorm(x, w, out_dtype, tm=512):
    s, d = x.shape
    return pl.pallas_call(
        _rmsnorm_body,
        grid=(s // tm,),
        in_specs=[pl.BlockSpec((tm, d), lambda i: (i, 0)),
                  pl.BlockSpec((1, d), lambda i: (0, 0))],
        out_specs=pl.BlockSpec((tm, d), lambda i: (i, 0)),
        out_shape=jax.ShapeDtypeStruct((s, d), out_dtype),
        compiler_params=_cparams(("parallel",)),
        name="rmsnorm",
    )(x, w.reshape(1, d))


def _proj(h_ref, w_ref, wb_ref, row_axis):
    @pl.when(pl.program_id(row_axis) == 0)
    def _cast():
        wb_ref[...] = w_ref[...].astype(BF16)

    return jnp.dot(h_ref[...], wb_ref[...], preferred_element_type=F32)


def _mm_body(h_ref, w_ref, o_ref, wb_ref, *, act):
    acc = _proj(h_ref, w_ref, wb_ref, 1)
    if act == "sigmoid":
        acc = jax.nn.sigmoid(acc)
    o_ref[...] = acc.astype(o_ref.dtype)


def _mm(h, w, col0, ncols, act, out_dtype, tm=512, tn=1024):
    s, k = h.shape
    jb = col0 // tn
    return pl.pallas_call(
        functools.partial(_mm_body, act=act),
        grid=(ncols // tn, s // tm),
        in_specs=[pl.BlockSpec((tm, k), lambda j, i: (i, 0)),
                  pl.BlockSpec((k, tn), lambda j, i: (0, j + jb))],
        out_specs=pl.BlockSpec((tm, tn), lambda j, i: (i, j)),
        out_shape=jax.ShapeDtypeStruct((s, ncols), out_dtype),
        scratch_shapes=[pltpu.VMEM((k, tn), BF16)],
        compiler_params=_cparams(("arbitrary", "arbitrary")),
        name="mm_" + act,
    )(h, w)


def _mm_vones_body(h_ref, w_ref, o_ref, wb_ref):
    acc = _proj(h_ref, w_ref, wb_ref, 0)
    hw = 2 * HEAD_DIM
    ones = jnp.ones((acc.shape[0], hw), o_ref.dtype)
    for hd in range(acc.shape[1] // hw):
        o_ref[:, 2 * hd * hw:(2 * hd + 1) * hw] = acc[:, hd * hw:(hd + 1) * hw].astype(o_ref.dtype)
        o_ref[:, (2 * hd + 1) * hw:(2 * hd + 2) * hw] = ones


def _mm_vones(h, w, col0, tm=512):
    s, k = h.shape
    tn = ATTN_WIDTH
    jb = col0 // tn
    return pl.pallas_call(
        _mm_vones_body,
        grid=(s // tm,),
        in_specs=[pl.BlockSpec((tm, k), lambda i: (i, 0)),
                  pl.BlockSpec((k, tn), lambda i: (0, jb))],
        out_specs=pl.BlockSpec((tm, 2 * tn), lambda i: (i, 0)),
        out_shape=jax.ShapeDtypeStruct((s, 2 * tn), BF16),
        scratch_shapes=[pltpu.VMEM((k, tn), BF16)],
        compiler_params=_cparams(("arbitrary",)),
        name="mm_vones",
    )(h, w)


def _mm_rope_body(h_ref, w_ref, pos_ref, invf_ref, o_ref, wb_ref, *, tn):
    j = pl.program_id(0)
    acc = _proj(h_ref, w_ref, wb_ref, 1)
    scale = jnp.where(j == 0, HEAD_DIM ** -0.5, 1.0).astype(F32)
    ang = pos_ref[...].astype(F32) * invf_ref[...]
    lane = lax.broadcasted_iota(jnp.int32, ang.shape, 1)
    low = (lane % HEAD_DIM) < (HEAD_DIM // 2)
    cos = jnp.cos(ang) * scale
    sin = jnp.where(low, -jnp.sin(ang), jnp.sin(ang)) * scale
    for c in range(tn // LANES):
        t = acc[:, c * LANES:(c + 1) * LANES]
        partner = jnp.where(low, pltpu.roll(t, LANES - HEAD_DIM // 2, 1),
                            pltpu.roll(t, HEAD_DIM // 2, 1))
        o_ref[:, c * LANES:(c + 1) * LANES] = (t * cos + partner * sin).astype(o_ref.dtype)


def _mm_rope(h, w, positions, col0, tm=512, tn=1024):
    s, k = h.shape
    jb = col0 // tn
    inv_freq = 1.0 / (ROPE_THETA ** (jnp.arange(0, HEAD_DIM, 2, dtype=F32) / HEAD_DIM))
    invf = jnp.tile(inv_freq, LANES // (HEAD_DIM // 2)).reshape(1, LANES)
    return pl.pallas_call(
        functools.partial(_mm_rope_body, tn=tn),
        grid=(2, s // tm),
        in_specs=[pl.BlockSpec((tm, k), lambda j, i: (i, 0)),
                  pl.BlockSpec((k, tn), lambda j, i: (0, j + jb)),
                  pl.BlockSpec((tm, 1), lambda j, i: (i, 0)),
                  pl.BlockSpec((1, LANES), lambda j, i: (0, 0))],
        out_specs=pl.BlockSpec((tm, tn), lambda j, i: (i, j)),
        out_shape=jax.ShapeDtypeStruct((s, 2 * tn), BF16),
        scratch_shapes=[pltpu.VMEM((k, tn), BF16)],
        compiler_params=_cparams(("arbitrary", "arbitrary")),
        name="mm_rope",
    )(h, w, positions.reshape(s, 1), invf)


def _s5_tables(lam_re, lam_im, log_dt, b_re, b_im, c_re, c_im):
    L, G, P, M = SSM_CHUNK, SSM_GROUPS, SSM_STATE, SSM_GROUP
    hi = lax.Precision.HIGHEST
    lr, li = lam_re.astype(F32), lam_im.astype(F32)
    dt = jnp.exp(log_dt.astype(F32))[:, None]
    n = jnp.arange(L + 1, dtype=F32)[:, None, None]
    mag = jnp.exp(lr * dt * n)
    pr = mag * jnp.cos(li * dt * n)
    pi = mag * jnp.sin(li * dt * n)
    nr, ni = pr[1] - 1.0, pi[1]
    den = lr * lr + li * li
    f_re = (nr * lr + ni * li) / den
    f_im = (ni * lr - nr * li) / den
    br, bi = b_re.astype(F32), b_im.astype(F32)
    bbr = f_re[..., None] * br - f_im[..., None] * bi
    bbi = f_re[..., None] * bi + f_im[..., None] * br
    cr, ci = c_re.astype(F32), c_im.astype(F32)

    nb, gb = G // SSM_GB, SSM_GB
    eye = jnp.eye(gb, dtype=F32)
    pr_g = jnp.transpose(pr, (1, 0, 2))
    pi_g = jnp.transpose(pi, (1, 0, 2))
    bbr_t = jnp.transpose(bbr, (0, 2, 1))
    bbi_t = jnp.transpose(bbi, (0, 2, 1))

    abr = pr_g[:, :L, None, :] * bbr_t[:, None] - pi_g[:, :L, None, :] * bbi_t[:, None]
    abi = pr_g[:, :L, None, :] * bbi_t[:, None] + pi_g[:, :L, None, :] * bbr_t[:, None]
    kmat = (jnp.einsum('gnip,gop->gnio', abr, cr, precision=hi)
            - jnp.einsum('gnip,gop->gnio', abi, ci, precision=hi))
    kpad = jnp.pad(kmat, ((0, 0), (1, 0), (0, 0), (0, 0)))
    kq = jnp.stack([jnp.stack([kpad[:, 1 + c2 - r2::2][:, :L // 2] for c2 in range(2)], axis=3)
                    for r2 in range(2)], axis=2)
    kq = jnp.transpose(kq.reshape(nb, gb, L // 2, 2, M, 2, M), (0, 2, 3, 1, 4, 5, 6))
    kk_tab = (kq[:, :, :, :, :, :, None, :] * eye[None, None, None, :, None, None, :, None]
              ).reshape(nb, L // 2, 2 * gb * M, 2 * gb * M)

    ab1r = pr_g[:, 1, None, :] * bbr_t - pi_g[:, 1, None, :] * bbi_t
    ab1i = pr_g[:, 1, None, :] * bbi_t + pi_g[:, 1, None, :] * bbr_t
    b4 = jnp.stack([jnp.stack([ab1r, ab1i], axis=2), jnp.stack([bbr_t, bbi_t], axis=2)],
                   axis=1)
    b4 = jnp.transpose(b4.reshape(nb, gb, 2, M, 2, P), (0, 2, 1, 3, 4, 5))
    bb_tab = (b4[:, :, :, :, :, None, :] * eye[None, None, :, None, None, :, None]
              ).reshape(nb, 2 * gb * M, 2 * gb * P)

    cr_t = jnp.transpose(cr, (0, 2, 1))
    ci_t = jnp.transpose(ci, (0, 2, 1))
    car = cr_t * pr_g[:, 1, :, None] - ci_t * pi_g[:, 1, :, None]
    cai = cr_t * pi_g[:, 1, :, None] + ci_t * pr_g[:, 1, :, None]
    c4 = jnp.stack([jnp.stack([cr_t, car], axis=2), jnp.stack([-ci_t, -cai], axis=2)],
                   axis=1)
    c4 = jnp.transpose(c4.reshape(nb, gb, 2, P, 2, M), (0, 2, 1, 3, 4, 5))
    cc_tab = (c4[:, :, :, :, :, None, :] * eye[None, None, :, None, None, :, None]
              ).reshape(nb, 2 * gb * P, 2 * gb * M)

    ap = jnp.stack([pr.reshape(L + 1, nb, gb * P), pi.reshape(L + 1, nb, gb * P)], axis=2)
    return (kk_tab.astype(BF16), bb_tab.astype(BF16), cc_tab.astype(BF16),
            ap.reshape(L + 1, 2 * G * P))


def _cmul(ar, ai, zr, zi):
    return ar * zr - ai * zi, ar * zi + ai * zr


def _s5_body(u_ref, kk_ref, bb_ref, cc_ref, ap_ref, d_ref, y_ref,
             lhs_ref, er_ref, ei_ref, xr_ref, xi_ref):
    L = SSM_CHUNK
    nc = u_ref.shape[0] // L
    hp = SSM_GB * SSM_STATE
    nq = L // 2
    for q in range(nq):
        lhs_ref[q] = jnp.concatenate(
            [u_ref[pl.ds(2 * q, nc, stride=L), :], u_ref[pl.ds(2 * q + 1, nc, stride=L), :]],
            axis=1).astype(BF16)

    er = jnp.zeros((nc, hp), F32)
    ei = jnp.zeros((nc, hp), F32)
    for q in range(nq):
        z = jnp.dot(lhs_ref[q], bb_ref[0], preferred_element_type=F32)
        n = L - 2 - 2 * q
        dr, di = _cmul(ap_ref[n:n + 1, :hp], ap_ref[n:n + 1, hp:], z[:, :hp], z[:, hp:])
        er = er + dr
        ei = ei + di
    er_ref[...] = er
    ei_ref[...] = ei

    ar = ap_ref[L:L + 1, :hp]
    ai = ap_ref[L:L + 1, hp:]

    def step(c, carry):
        xr, xi = carry
        xr_ref[pl.ds(c, 1), :] = xr
        xi_ref[pl.ds(c, 1), :] = xi
        nr, ni = _cmul(ar, ai, xr, xi)
        return nr + er_ref[pl.ds(c, 1), :], ni + ei_ref[pl.ds(c, 1), :]

    zero = jnp.zeros((1, hp), F32)
    lax.fori_loop(0, nc, step, (zero, zero))

    for qq in range(nq):
        n = 2 * qq + 1
        wr, wi = _cmul(ap_ref[n:n + 1, :hp], ap_ref[n:n + 1, hp:], xr_ref[...], xi_ref[...])
        w = jnp.concatenate([wr, wi], axis=1).astype(BF16)
        acc = jnp.dot(w, cc_ref[0], preferred_element_type=F32)
        for q in range(qq + 1):
            acc = acc + jnp.dot(lhs_ref[q], kk_ref[0, qq - q], preferred_element_type=F32)
        for r in range(2):
            j = 2 * qq + r
            y = acc[:, r * LANES:(r + 1) * LANES] + d_ref[...] * u_ref[pl.ds(j, nc, stride=L), :]
            y_ref[pl.ds(j, nc, stride=L), :] = jax.nn.gelu(y)


def _s5_core(u, tables, d_skip):
    s, width = u.shape
    L, P, M, gb = SSM_CHUNK, SSM_STATE, SSM_GROUP, SSM_GB
    nc = s // L
    nb = width // (gb * M)
    kk_tab, bb_tab, cc_tab, ap = tables
    return pl.pallas_call(
        _s5_body,
        grid=(nb,),
        in_specs=[pl.BlockSpec((s, gb * M), lambda b: (0, b)),
                  pl.BlockSpec((1, L // 2, 2 * gb * M, 2 * gb * M), lambda b: (b, 0, 0, 0)),
                  pl.BlockSpec((1, 2 * gb * M, 2 * gb * P), lambda b: (b, 0, 0)),
                  pl.BlockSpec((1, 2 * gb * P, 2 * gb * M), lambda b: (b, 0, 0)),
                  pl.BlockSpec((L + 1, 2 * gb * P), lambda b: (0, b)),
                  pl.BlockSpec((1, gb * M), lambda b: (0, b))],
        out_specs=pl.BlockSpec((s, gb * M), lambda b: (0, b)),
        out_shape=jax.ShapeDtypeStruct((s, width), F32),
        scratch_shapes=[pltpu.VMEM((L // 2, nc, 2 * gb * M), BF16)]
        + [pltpu.VMEM((nc, gb * P), F32) for _ in range(4)],
        compiler_params=_cparams(("parallel",)),
        name="s5_scan",
    )(u, kk_tab, bb_tab, cc_tab, ap, d_skip.astype(F32).reshape(1, width))


def _glu_body(y_ref, w_ref, b_ref, o_ref):
    y = y_ref[...]
    z = jnp.dot(y.astype(BF16), w_ref[...], preferred_element_type=F32) + b_ref[...]
    o_ref[...] = (y * jax.nn.sigmoid(z)).astype(o_ref.dtype)


def _glu(y, w, b, tm=512):
    s, n = y.shape
    return pl.pallas_call(
        _glu_body,
        grid=(s // tm,),
        in_specs=[pl.BlockSpec((tm, n), lambda i: (i, 0)),
                  pl.BlockSpec((n, n), lambda i: (0, 0)),
                  pl.BlockSpec((1, n), lambda i: (0, 0))],
        out_specs=pl.BlockSpec((tm, n), lambda i: (i, 0)),
        out_shape=jax.ShapeDtypeStruct((s, n), BF16),
        compiler_params=_cparams(("parallel",)),
        name="s5_glu",
    )(y, w, b.reshape(1, n).astype(F32))


def _attn_body(q_ref, k_ref, v_ref, lam_ref, sw_ref, o_ref,
               qs_ref, m_ref, acc_ref, s_ref, p_ref, al_ref, *, tq):
    qi = pl.program_id(1)
    hw = 2 * HEAD_DIM
    q = q_ref[...]
    lane = lax.broadcasted_iota(jnp.int32, q.shape, 1)
    zero = jnp.zeros_like(q)
    qs_ref[0] = jnp.where(lane < HEAD_DIM, q, zero)
    qs_ref[1] = jnp.where(lane >= HEAD_DIM, q, zero)
    m_ref[...] = jnp.full(m_ref.shape, -jnp.inf, F32)
    acc_ref[...] = jnp.zeros(acc_ref.shape, F32)

    def scores(c, slot, masked=False):
        off = pl.multiple_of(c * tq, tq)
        k = k_ref[pl.ds(off, tq), :]
        for mp in range(2):
            s = lax.dot_general(qs_ref[mp], k, (((1,), (1,)), ((), ())),
                                preferred_element_type=F32)
            if masked:
                row = lax.broadcasted_iota(jnp.int32, s.shape, 0)
                col = lax.broadcasted_iota(jnp.int32, s.shape, 1)
                s = jnp.where(col <= row, s, -jnp.inf)
            s_ref[slot, mp] = s

    def softmax(slot):
        for mp in range(2):
            s = s_ref[slot, mp]
            m_old = m_ref[mp]
            m_new = jnp.maximum(m_old, jnp.max(s, axis=-1, keepdims=True))
            al_ref[slot, mp] = jnp.exp(m_old - m_new)
            p_ref[slot, mp] = jnp.exp(s - m_new).astype(BF16)
            m_ref[mp] = m_new

    def pv(c, slot):
        off = pl.multiple_of(c * tq, tq)
        v = v_ref[pl.ds(off, tq), :]
        for mp in range(2):
            acc_ref[mp] = al_ref[slot, mp] * acc_ref[mp] + jnp.dot(
                p_ref[slot, mp], v, preferred_element_type=F32)

    @pl.when(qi == 0)
    def _one_chunk():
        scores(0, 0, masked=True)
        softmax(0)
        pv(0, 0)

    @pl.when(qi == 1)
    def _two_chunks():
        scores(0, 1)
        softmax(1)
        pv(0, 1)
        scores(1, 0, masked=True)
        softmax(0)
        pv(1, 0)

    @pl.when(qi >= 2)
    def _pipelined():
        odd = qi % 2

        @pl.when(odd == 0)
        def _fill_even():
            scores(0, 0)
            scores(1, 1)
            softmax(0)

        @pl.when(odd == 1)
        def _fill_odd():
            scores(0, 1)
            scores(1, 0)
            softmax(1)
            scores(2, 1)
            softmax(0)
            pv(0, 1)

        t0 = 2 + odd

        def two_steps(u, carry):
            t = t0 + 2 * u
            scores(t, 0)
            softmax(1)
            pv(t - 2, 0)
            scores(t + 1, 1)
            softmax(0)
            pv(t - 1, 1)
            return carry

        lax.fori_loop(0, (qi - t0) // 2, two_steps, 0)
        scores(qi, 0, masked=True)
        softmax(1)
        pv(qi - 2, 0)
        softmax(0)
        pv(qi - 1, 1)
        pv(qi, 0)

    lam = (jnp.exp(jnp.sum(lam_ref[0:1, :] * lam_ref[1:2, :]))
           - jnp.exp(jnp.sum(lam_ref[2:3, :] * lam_ref[3:4, :])) + LAMBDA_INIT)
    a1 = acc_ref[0]
    a2 = acc_ref[1]
    o = a1[:, :hw] / a1[:, hw:] - lam * (a2[:, :hw] / a2[:, hw:])
    o = o * lax.rsqrt(jnp.mean(o * o, axis=-1, keepdims=True) + SUBLN_EPS)
    o_ref[...] = (o * sw_ref[...] * (1.0 - LAMBDA_INIT)).astype(o_ref.dtype)


def _diff_attention(qk, v1, lam_params, subln_w, tq=512):
    s = qk.shape[0]
    nq = s // tq
    hw = 2 * HEAD_DIM
    kcol0 = ATTN_WIDTH // hw
    return pl.pallas_call(
        functools.partial(_attn_body, tq=tq),
        grid=(ATTN_HEADS, nq),
        in_specs=[pl.BlockSpec((tq, hw), lambda h, i: (i, h)),
                  pl.BlockSpec((s, hw), lambda h, i: (0, kcol0 + h)),
                  pl.BlockSpec((s, 2 * hw), lambda h, i: (0, h)),
                  pl.BlockSpec((4, HEAD_DIM), lambda h, i: (0, 0)),
                  pl.BlockSpec((1, hw), lambda h, i: (0, 0))],
        out_specs=pl.BlockSpec((tq, hw), lambda h, i: (i, h)),
        out_shape=jax.ShapeDtypeStruct((s, ATTN_WIDTH), BF16),
        scratch_shapes=[pltpu.VMEM((2, tq, hw), BF16),
                        pltpu.VMEM((2, tq, 1), F32),
                        pltpu.VMEM((2, tq, 2 * hw), F32),
                        pltpu.VMEM((2, 2, tq, tq), F32),
                        pltpu.VMEM((2, 2, tq, tq), BF16),
                        pltpu.VMEM((2, 2, tq, 1), F32)],
        compiler_params=_cparams(("parallel", "arbitrary")),
        name="diff_attn",
    )(qk, qk, v1, lam_params, subln_w.reshape(1, hw).astype(F32))


def _merge_body(x_ref, ys_ref, ya_ref, gs_ref, ga_ref, wps_ref, wpa_ref, wo_ref, n2_ref, wr_ref,
                x1_ref, h2_ref, eid_ref, gate_ref, cnt_ref):
    ps = jnp.dot(ys_ref[...], wps_ref[...], preferred_element_type=F32)
    pa = jnp.dot(ya_ref[...], wpa_ref[...], preferred_element_type=F32)
    merged = gs_ref[...].astype(F32) * ps + ga_ref[...].astype(F32) * pa
    x1 = x_ref[...] + jnp.dot(merged.astype(BF16), wo_ref[...], preferred_element_type=F32)
    x1_ref[...] = x1
    h2 = x1 * lax.rsqrt(jnp.mean(x1 * x1, axis=-1, keepdims=True) + NORM_EPS) * n2_ref[...]
    h2_ref[...] = h2
    wr = wr_ref[...]
    h_hi = h2.astype(BF16)
    h_lo = (h2 - h_hi.astype(F32)).astype(BF16)
    w_hi = wr.astype(BF16)
    w_lo = (wr - w_hi.astype(F32)).astype(BF16)
    logits = (jnp.dot(h_hi, w_hi, preferred_element_type=F32)
              + jnp.dot(h_lo, w_hi, preferred_element_type=F32)
              + jnp.dot(h_hi, w_lo, preferred_element_type=F32))
    lane = lax.broadcasted_iota(jnp.int32, logits.shape, 1)
    big = jnp.int32(1 << 20)
    ninf = jnp.float32(-jnp.inf)
    is_g = lane < N_GROUPS
    gl = jnp.where(is_g, logits, ninf)
    gm = jnp.max(gl, axis=-1, keepdims=True)
    g_idx = jnp.min(jnp.where(gl == gm, lane, big), axis=-1, keepdims=True)
    g_val = 1.0 / jnp.sum(jnp.where(is_g, jnp.exp(gl - gm), 0.0), axis=-1, keepdims=True)
    lo = N_GROUPS + g_idx * EXPERTS_PER_GROUP
    in_grp = (lane >= lo) & (lane < lo + EXPERTS_PER_GROUP)
    el = jnp.where(in_grp, logits, ninf)
    e1 = jnp.max(el, axis=-1, keepdims=True)
    i1 = jnp.min(jnp.where(el == e1, lane, big), axis=-1, keepdims=True)
    el2 = jnp.where(lane == i1, ninf, el)
    e2 = jnp.max(el2, axis=-1, keepdims=True)
    i2 = jnp.min(jnp.where(el2 == e2, lane, big), axis=-1, keepdims=True)
    t = jnp.exp(e2 - e1)
    p1 = 1.0 / (1.0 + t)
    p2 = t / (1.0 + t)
    gate_ref[...] = jnp.where(lane == 0, p1 * g_val, jnp.where(lane == 1, p2 * g_val, 0.0))

    @pl.when(pl.program_id(0) == 0)
    def _zero_counts():
        cnt_ref[...] = jnp.zeros(cnt_ref.shape, F32)

    ex1 = i1 - N_GROUPS
    ex2 = i2 - N_GROUPS
    oh1 = lane == ex1
    oh2 = lane == ex2
    onehot = jnp.where(oh1 | oh2, 1.0, 0.0)
    tm = logits.shape[0]
    r_i = lax.broadcasted_iota(jnp.int32, (tm, tm), 0)
    c_i = lax.broadcasted_iota(jnp.int32, (tm, tm), 1)
    tri = jnp.where(c_i < r_i, 1.0, 0.0).astype(BF16)
    before = jnp.dot(tri, onehot.astype(BF16), preferred_element_type=F32) + cnt_ref[...]
    rank1 = jnp.sum(jnp.where(oh1, before, 0.0), axis=-1, keepdims=True).astype(jnp.int32)
    rank2 = jnp.sum(jnp.where(oh2, before, 0.0), axis=-1, keepdims=True).astype(jnp.int32)
    cnt_ref[...] = cnt_ref[...] + jnp.sum(onehot, axis=0, keepdims=True)
    eid_ref[...] = jnp.where(lane == 0, ex1, jnp.where(lane == 1, ex2,
                             jnp.where(lane == 2, rank1, jnp.where(lane == 3, rank2, 0))))


def _merge_route(x, y_ssm, y_attn, gates, wps, wpa, wo, n2w, wr, tm=256):
    s, d = x.shape
    nw = y_ssm.shape[1]
    const = lambda i: (0, 0)
    return pl.pallas_call(
        _merge_body,
        grid=(s // tm,),
        in_specs=[pl.BlockSpec((tm, d), lambda i: (i, 0)),
                  pl.BlockSpec((tm, nw), lambda i: (i, 0)),
                  pl.BlockSpec((tm, nw), lambda i: (i, 0)),
                  pl.BlockSpec((tm, d), lambda i: (i, 0)),
                  pl.BlockSpec((tm, d), lambda i: (i, 1)),
                  pl.BlockSpec((nw, d), const),
                  pl.BlockSpec((nw, d), const),
                  pl.BlockSpec((d, d), const),
                  pl.BlockSpec((1, d), const),
                  pl.BlockSpec((d, LANES), const)],
        out_specs=[pl.BlockSpec((tm, d), lambda i: (i, 0)),
                   pl.BlockSpec((tm, d), lambda i: (i, 0)),
                   pl.BlockSpec((tm, LANES), lambda i: (i, 0)),
                   pl.BlockSpec((tm, LANES), lambda i: (i, 0)),
                   pl.BlockSpec((1, LANES), const)],
        out_shape=[jax.ShapeDtypeStruct((s, d), F32),
                   jax.ShapeDtypeStruct((s, d), F32),
                   jax.ShapeDtypeStruct((s, LANES), jnp.int32),
                   jax.ShapeDtypeStruct((s, LANES), F32),
                   jax.ShapeDtypeStruct((1, LANES), F32)],
        compiler_params=_cparams(("arbitrary",)),
        name="merge_route",
    )(x, y_ssm, y_attn, gates, gates, wps, wpa, wo, n2w.reshape(1, d).astype(F32), wr)


def _row_copy(src_hbm, row, dst_ref, r, sem):
    return pltpu.make_async_copy(src_hbm.at[pl.ds(row, 1), :], dst_ref.at[pl.ds(r, 1), :], sem)


def _gather_rows(src_hbm, idx_ref, base, stride, dst_ref, sem, n):
    def start(r, c):
        _row_copy(src_hbm, idx_ref[base + stride * r], dst_ref, r, sem).start()
        return c

    lax.fori_loop(0, n, start, 0, unroll=8)


def _wait_rows(src_hbm, dst_ref, sem, n):
    pltpu.make_async_copy(src_hbm.at[pl.ds(0, n), :], dst_ref, sem).wait()


def _scatter_tok_body(dest_ref, tok_ref):
    def zero(i, c):
        tok_ref[i] = 0
        return c

    lax.fori_loop(0, tok_ref.shape[0], zero, 0, unroll=8)

    def put(i, c):
        tok_ref[dest_ref[i]] = lax.shift_right_logical(i, TOP_K_LOG2)
        return c

    lax.fori_loop(0, dest_ref.shape[0], put, 0, unroll=8)


def _scatter_tok(dest, buf_len):
    return pl.pallas_call(
        _scatter_tok_body,
        in_specs=[pl.BlockSpec(memory_space=pltpu.SMEM)],
        out_specs=pl.BlockSpec(memory_space=pltpu.SMEM),
        out_shape=jax.ShapeDtypeStruct((buf_len,), jnp.int32),
        name="scatter_tok",
    )(dest)


def _expert_body(be_ref, nb_ref, tok_ref, h2_hbm, wg_ref, wu_ref, wd_ref, o_ref,
                 xb_ref, wgb_ref, wub_ref, wdb_ref, sem):
    b = pl.program_id(0)
    nb = nb_ref[0]

    @pl.when(b == 0)
    def _first():
        _gather_rows(h2_hbm, tok_ref, 0, 1, xb_ref.at[0], sem.at[0], MOE_BLK)

    @pl.when((b == 0) | (be_ref[b] != be_ref[jnp.maximum(b - 1, 0)]))
    def _new_expert():
        wgb_ref[...] = wg_ref[0].astype(BF16)
        wub_ref[...] = wu_ref[0].astype(BF16)
        wdb_ref[...] = wd_ref[0].astype(BF16)

    @pl.when(b < nb)
    def _run():
        cur = b % 2
        nxt = 1 - cur
        _wait_rows(h2_hbm, xb_ref.at[cur], sem.at[cur], MOE_BLK)
        base = (b + 1) * MOE_BLK
        for r in range(MOE_BLK):
            _row_copy(h2_hbm, tok_ref[base + r], xb_ref.at[nxt], r, sem.at[nxt]).start()
        xb = xb_ref[cur].astype(BF16)
        g = jnp.dot(xb, wgb_ref[...], preferred_element_type=F32)
        u = jnp.dot(xb, wub_ref[...], preferred_element_type=F32)
        a = (jax.nn.silu(g) * u).astype(BF16)
        o_ref[...] = jnp.dot(a, wdb_ref[...], preferred_element_type=F32)

        @pl.when(b == nb - 1)
        def _drain():
            _wait_rows(h2_hbm, xb_ref.at[nxt], sem.at[nxt], MOE_BLK)

    @pl.when(b >= nb)
    def _unused():
        o_ref[...] = jnp.zeros(o_ref.shape, o_ref.dtype)


def _experts(h2, block_e, n_used, buf_tok, wg, wu, wd):
    s, d = h2.shape
    buf_len = buf_tok.shape[0]
    n_blocks = buf_len // MOE_BLK
    wmap = lambda b, be, nb, tok: (be[b], 0, 0)
    return pl.pallas_call(
        _expert_body,
        grid_spec=pltpu.PrefetchScalarGridSpec(
            num_scalar_prefetch=3,
            grid=(n_blocks,),
            in_specs=[pl.BlockSpec(memory_space=pl.ANY),
                      pl.BlockSpec((1, d, D_FF), wmap),
                      pl.BlockSpec((1, d, D_FF), wmap),
                      pl.BlockSpec((1, D_FF, d), wmap)],
            out_specs=pl.BlockSpec((MOE_BLK, d), lambda b, be, nb, tok: (b, 0)),
            scratch_shapes=[pltpu.VMEM((2, MOE_BLK, d), F32),
                            pltpu.VMEM((d, D_FF), BF16), pltpu.VMEM((d, D_FF), BF16),
                            pltpu.VMEM((D_FF, d), BF16),
                            pltpu.SemaphoreType.DMA((2,))]),
        out_shape=jax.ShapeDtypeStruct((buf_len, d), F32),
        compiler_params=_cparams(("arbitrary",)),
        name="experts",
    )(block_e, n_used, buf_tok, h2, wg, wu, wd)


def _combine_body(pos_ref, x1_ref, gate_ref, fw_ref, ys_hbm, o_ref, g_ref, sem, *, tm):
    i = pl.program_id(0)

    def gather(tile, slot):
        for k in range(TOP_K):
            _gather_rows(ys_hbm, pos_ref, TOP_K * tile * tm + k, TOP_K, g_ref.at[slot, k],
                         sem.at[slot], tm)

    @pl.when(i == 0)
    def _first():
        gather(0, 0)

    @pl.when(i + 1 < pl.num_programs(0))
    def _prefetch():
        gather(i + 1, (i + 1) % 2)

    cur = i % 2
    for k in range(TOP_K):
        _wait_rows(ys_hbm, g_ref.at[cur, k], sem.at[cur], tm)
    gate = gate_ref[...]
    x = x1_ref[...] + (gate[:, 0:1] * g_ref[cur, 0] + gate[:, 1:2] * g_ref[cur, 1])
    y = x * lax.rsqrt(jnp.mean(x * x, axis=-1, keepdims=True) + NORM_EPS)
    o_ref[...] = y * fw_ref[...]


def _combine(x1, ys, pos, gate, fw, tm=256):
    s, d = x1.shape
    return pl.pallas_call(
        functools.partial(_combine_body, tm=tm),
        grid_spec=pltpu.PrefetchScalarGridSpec(
            num_scalar_prefetch=1,
            grid=(s // tm,),
            in_specs=[pl.BlockSpec((tm, d), lambda i, p: (i, 0)),
                      pl.BlockSpec((tm, LANES), lambda i, p: (i, 0)),
                      pl.BlockSpec((1, d), lambda i, p: (0, 0)),
                      pl.BlockSpec(memory_space=pl.ANY)],
            out_specs=pl.BlockSpec((tm, d), lambda i, p: (i, 0)),
            scratch_shapes=[pltpu.VMEM((2, TOP_K, tm, d), F32),
                            pltpu.SemaphoreType.DMA((2,))]),
        out_shape=jax.ShapeDtypeStruct((s, d), F32),
        compiler_params=_cparams(("arbitrary",)),
        name="combine_norm",
    )(pos, x1, gate, fw.reshape(1, d).astype(F32), ys)


def _dispatch_tables(eid, counts):
    n_tok = eid.shape[0]
    n_slots = n_tok * TOP_K
    buf_len = n_slots + N_EXPERTS * MOE_BLK
    n_blocks = buf_len // MOE_BLK
    counts = counts.astype(jnp.int32)
    padded = ((counts + MOE_BLK - 1) // MOE_BLK) * MOE_BLK
    padded_ends = jnp.cumsum(padded)
    padded_starts = padded_ends - padded
    experts = eid[:, :TOP_K]
    onehot = experts[:, :, None] == jnp.arange(N_EXPERTS, dtype=jnp.int32)
    dest = jnp.sum(jnp.where(onehot, padded_starts, 0), axis=-1) + eid[:, TOP_K:2 * TOP_K]
    block_start = jnp.arange(n_blocks, dtype=jnp.int32) * MOE_BLK
    n_used = (padded_ends[-1] // MOE_BLK).astype(jnp.int32).reshape(1)
    last_start = (n_used[0] - 1) * MOE_BLK
    block_e = jnp.sum(padded_ends[None, :] <= jnp.minimum(block_start, last_start)[:, None],
                      axis=-1).astype(jnp.int32)
    return block_e, n_used, dest.reshape(n_slots).astype(jnp.int32), buf_len


def kernel(x, positions, norm1_w, w_in, ssm_lambda_re, ssm_lambda_im, ssm_log_dt, ssm_b_re, ssm_b_im, ssm_c_re, ssm_c_im, ssm_d, ssm_glu_w, ssm_glu_b, attn_lambda_q1, attn_lambda_k1, attn_lambda_q2, attn_lambda_k2, attn_subln_w, w_proj_ssm, w_proj_attn, w_out, norm2_w, router_group_w, router_expert_w, expert_w_gate, expert_w_up, expert_w_down, final_norm_w):
    bsz, seq, d = x.shape
    depth = norm1_w.shape[0]
    xs = x.reshape(bsz * seq, d)
    pos = positions.reshape(bsz * seq)
    o_q = SSM_WIDTH
    o_v = o_q + 2 * ATTN_WIDTH
    o_g = o_v + ATTN_WIDTH
    for l in range(depth):
        h = _rmsnorm(xs, norm1_w[l].astype(F32), BF16)
        w_in_b = w_in[l].astype(F32)
        u = _mm(h, w_in_b, 0, SSM_WIDTH, "none", F32)
        qk = _mm_rope(h, w_in_b, pos, o_q)
        v = _mm_vones(h, w_in_b, o_v)
        gates = _mm(h, w_in_b, o_g, 2 * D_MODEL, "sigmoid", F32)

        tables = _s5_tables(ssm_lambda_re[l], ssm_lambda_im[l], ssm_log_dt[l],
                            ssm_b_re[l], ssm_b_im[l], ssm_c_re[l], ssm_c_im[l])
        y = _s5_core(u, tables, ssm_d[l])
        y_ssm = _glu(y, ssm_glu_w[l].astype(BF16), ssm_glu_b[l])

        lam_params = jnp.stack([attn_lambda_q1[l], attn_lambda_k1[l],
                                attn_lambda_q2[l], attn_lambda_k2[l]]).astype(F32)
        y_attn = _diff_attention(qk, v, lam_params, attn_subln_w[l])

        wr = jnp.concatenate([router_group_w[l], router_expert_w[l]], axis=1).astype(F32)
        wr = jnp.pad(wr, ((0, 0), (0, LANES - wr.shape[1])))
        x1, h2, eid, gate, counts = _merge_route(
            xs, y_ssm, y_attn, gates, w_proj_ssm[l].astype(BF16), w_proj_attn[l].astype(BF16),
            w_out[l].astype(BF16), norm2_w[l], wr)

        block_e, n_used, dest, buf_len = _dispatch_tables(eid, counts[0, :N_EXPERTS])
        buf_tok = _scatter_tok(dest, buf_len)
        ys = _experts(h2, block_e, n_used, buf_tok, expert_w_gate[l].astype(F32),
                      expert_w_up[l].astype(F32), expert_w_down[l].astype(F32))
        last = l == depth - 1
        assert last, "DEPTH > 1 needs an un-normalised combine"
        xs = _combine(x1, ys, dest, gate, final_norm_w)
    return xs.reshape(bsz, seq, d)
```

```python
import functools
import math

import jax
import jax.numpy as jnp
from jax import lax
from jax.experimental import pallas as pl
from jax.experimental.pallas import tpu as pltpu

F32 = jnp.float32
BF16 = jnp.bfloat16

D_MODEL = 2048
SSM_WIDTH = 1024
SSM_GROUP = 16
SSM_GROUPS = 64
SSM_STATE = 64
ATTN_WIDTH = 1024
ATTN_HEADS = 8
HEAD_DIM = 64
ROPE_THETA = 10000.0
N_GROUPS = 4
EXPERTS_PER_GROUP = 8
N_EXPERTS = 32
TOP_K = 2
TOP_K_LOG2 = 1
D_FF = 512
NORM_EPS = 1e-6
SUBLN_EPS = 1e-5
LAMBDA_INIT = 0.8 - 0.6 * math.exp(-0.3 * 0)

LANES = 128
VMEM_LIMIT = 48 * 1024 * 1024

SSM_CHUNK = 16
SSM_GB = 8
MOE_BLK = 128


def _cparams(sem):
    return pltpu.CompilerParams(dimension_semantics=sem, vmem_limit_bytes=VMEM_LIMIT)


def _rmsnorm_body(x_ref, w_ref, o_ref):
    x = x_ref[...]
    y = x * lax.rsqrt(jnp.mean(x * x, axis=-1, keepdims=True) + NORM_EPS)
    o_ref[...] = (y * w_ref[...]).astype(o_ref.dtype)


def _rmsnorm(x, w, out_dtype, tm=512):
    s, d = x.shape
    return pl.pallas_call(
        _rmsnorm_body,
        grid=(s // tm,),
        in_specs=[pl.BlockSpec((tm, d), lambda i: (i, 0)),
                  pl.BlockSpec((1, d), lambda i: (0, 0))],
        out_specs=pl.BlockSpec((tm, d), lambda i: (i, 0)),
        out_shape=jax.ShapeDtypeStruct((s, d), out_dtype),
        compiler_params=_cparams(("parallel",)),
        name="rmsnorm",
    )(x, w.reshape(1, d))


def _proj(h_ref, w_ref, wb_ref, row_axis):
    @pl.when(pl.program_id(row_axis) == 0)
    def _cast():
        wb_ref[...] = w_ref[...].astype(BF16)

    return jnp.dot(h_ref[...], wb_ref[...], preferred_element_type=F32)


def _mm_body(h_ref, w_ref, o_ref, wb_ref, *, act):
    acc = _proj(h_ref, w_ref, wb_ref, 1)
    if act == "sigmoid":
        acc = jax.nn.sigmoid(acc)
    o_ref[...] = acc.astype(o_ref.dtype)


def _mm(h, w, col0, ncols, act, out_dtype, tm=512, tn=1024):
    s, k = h.shape
    jb = col0 // tn
    return pl.pallas_call(
        functools.partial(_mm_body, act=act),
        grid=(ncols // tn, s // tm),
        in_specs=[pl.BlockSpec((tm, k), lambda j, i: (i, 0)),
                  pl.BlockSpec((k, tn), lambda j, i: (0, j + jb))],
        out_specs=pl.BlockSpec((tm, tn), lambda j, i: (i, j)),
        out_shape=jax.ShapeDtypeStruct((s, ncols), out_dtype),
        scratch_shapes=[pltpu.VMEM((k, tn), BF16)],
        compiler_params=_cparams(("arbitrary", "arbitrary")),
        name="mm_" + act,
    )(h, w)


def _mm_vones_body(h_ref, w_ref, o_ref, wb_ref):
    acc = _proj(h_ref, w_ref, wb_ref, 0)
    hw = 2 * HEAD_DIM
    ones = jnp.ones((acc.shape[0], hw), o_ref.dtype)
    for hd in range(acc.shape[1] // hw):
        o_ref[:, 2 * hd * hw:(2 * hd + 1) * hw] = acc[:, hd * hw:(hd + 1) * hw].astype(o_ref.dtype)
        o_ref[:, (2 * hd + 1) * hw:(2 * hd + 2) * hw] = ones


def _mm_vones(h, w, col0, tm=512):
    s, k = h.shape
    tn = ATTN_WIDTH
    jb = col0 // tn
    return pl.pallas_call(
        _mm_vones_body,
        grid=(s // tm,),
        in_specs=[pl.BlockSpec((tm, k), lambda i: (i, 0)),
                  pl.BlockSpec((k, tn), lambda i: (0, jb))],
        out_specs=pl.BlockSpec((tm, 2 * tn), lambda i: (i, 0)),
        out_shape=jax.ShapeDtypeStruct((s, 2 * tn), BF16),
        scratch_shapes=[pltpu.VMEM((k, tn), BF16)],
        compiler_params=_cparams(("arbitrary",)),
        name="mm_vones",
    )(h, w)


def _mm_rope_body(h_ref, w_ref, pos_ref, invf_ref, o_ref, wb_ref, *, tn):
    j = pl.program_id(0)
    acc = _proj(h_ref, w_ref, wb_ref, 1)
    scale = jnp.where(j == 0, HEAD_DIM ** -0.5, 1.0).astype(F32)
    ang = pos_ref[...].astype(F32) * invf_ref[...]
    lane = lax.broadcasted_iota(jnp.int32, ang.shape, 1)
    low = (lane % HEAD_DIM) < (HEAD_DIM // 2)
    cos = jnp.cos(ang) * scale
    sin = jnp.where(low, -jnp.sin(ang), jnp.sin(ang)) * scale
    for c in range(tn // LANES):
        t = acc[:, c * LANES:(c + 1) * LANES]
        partner = jnp.where(low, pltpu.roll(t, LANES - HEAD_DIM // 2, 1),
                            pltpu.roll(t, HEAD_DIM // 2, 1))
        o_ref[:, c * LANES:(c + 1) * LANES] = (t * cos + partner * sin).astype(o_ref.dtype)


def _mm_rope(h, w, positions, col0, tm=512, tn=1024):
    s, k = h.shape
    jb = col0 // tn
    inv_freq = 1.0 / (ROPE_THETA ** (jnp.arange(0, HEAD_DIM, 2, dtype=F32) / HEAD_DIM))
    invf = jnp.tile(inv_freq, LANES // (HEAD_DIM // 2)).reshape(1, LANES)
    return pl.pallas_call(
        functools.partial(_mm_rope_body, tn=tn),
        grid=(2, s // tm),
        in_specs=[pl.BlockSpec((tm, k), lambda j, i: (i, 0)),
                  pl.BlockSpec((k, tn), lambda j, i: (0, j + jb)),
                  pl.BlockSpec((tm, 1), lambda j, i: (i, 0)),
                  pl.BlockSpec((1, LANES), lambda j, i: (0, 0))],
        out_specs=pl.BlockSpec((tm, tn), lambda j, i: (i, j)),
        out_shape=jax.ShapeDtypeStruct((s, 2 * tn), BF16),
        scratch_shapes=[pltpu.VMEM((k, tn), BF16)],
        compiler_params=_cparams(("arbitrary", "arbitrary")),
        name="mm_rope",
    )(h, w, positions.reshape(s, 1), invf)


def _s5_tables(lam_re, lam_im, log_dt, b_re, b_im, c_re, c_im):
    L, G, P, M = SSM_CHUNK, SSM_GROUPS, SSM_STATE, SSM_GROUP
    hi = lax.Precision.HIGHEST
    lr, li = lam_re.astype(F32), lam_im.astype(F32)
    dt = jnp.exp(log_dt.astype(F32))[:, None]
    n = jnp.arange(L + 1, dtype=F32)[:, None, None]
    mag = jnp.exp(lr * dt * n)
    pr = mag * jnp.cos(li * dt * n)
    pi = mag * jnp.sin(li * dt * n)
    nr, ni = pr[1] - 1.0, pi[1]
    den = lr * lr + li * li
    f_re = (nr * lr + ni * li) / den
    f_im = (ni * lr - nr * li) / den
    br, bi = b_re.astype(F32), b_im.astype(F32)
    bbr = f_re[..., None] * br - f_im[..., None] * bi
    bbi = f_re[..., None] * bi + f_im[..., None] * br
    cr, ci = c_re.astype(F32), c_im.astype(F32)

    nb, gb = G // SSM_GB, SSM_GB
    pr_g = jnp.transpose(pr, (1, 0, 2))
    pi_g = jnp.transpose(pi, (1, 0, 2))
    bbr_t = jnp.transpose(bbr, (0, 2, 1))
    bbi_t = jnp.transpose(bbi, (0, 2, 1))

    abr = pr_g[:, :L, None, :] * bbr_t[:, None] - pi_g[:, :L, None, :] * bbi_t[:, None]
    abi = pr_g[:, :L, None, :] * bbi_t[:, None] + pi_g[:, :L, None, :] * bbr_t[:, None]
    kmat = (jnp.einsum('gnip,gop->gnio', abr, cr, precision=hi)
            - jnp.einsum('gnip,gop->gnio', abi, ci, precision=hi))
    kpad = jnp.pad(kmat, ((0, 0), (1, 0), (0, 0), (0, 0)))
    kq = jnp.stack([jnp.stack([kpad[:, 1 + c2 - r2::2][:, :L // 2] for c2 in range(2)], axis=3)
                    for r2 in range(2)], axis=2)
    kk_src = jnp.transpose(kq.reshape(nb, gb, L // 2, 2, M, 2, M),
                           (0, 2, 3, 5, 4, 1, 6)).reshape(nb, L // 2, 2, 2, M, gb * M)

    ab1r = pr_g[:, 1, None, :] * bbr_t - pi_g[:, 1, None, :] * bbi_t
    ab1i = pr_g[:, 1, None, :] * bbi_t + pi_g[:, 1, None, :] * bbr_t
    b4 = jnp.stack([jnp.stack([ab1r, ab1i], axis=2), jnp.stack([bbr_t, bbi_t], axis=2)],
                   axis=1)
    bb_src = jnp.transpose(b4.reshape(nb, gb, 2, M, 2, P),
                           (0, 2, 4, 3, 1, 5)).reshape(nb, 2, 2, M, gb * P)

    cr_t = jnp.transpose(cr, (0, 2, 1))
    ci_t = jnp.transpose(ci, (0, 2, 1))
    car = cr_t * pr_g[:, 1, :, None] - ci_t * pi_g[:, 1, :, None]
    cai = cr_t * pi_g[:, 1, :, None] + ci_t * pr_g[:, 1, :, None]
    c4 = jnp.stack([jnp.stack([cr_t, car], axis=2), jnp.stack([-ci_t, -cai], axis=2)],
                   axis=1)
    cc_src = jnp.transpose(c4.reshape(nb, gb, 2, P, 2, M),
                           (0, 2, 4, 3, 1, 5)).reshape(nb, 2, 2, P, gb * M)

    ap = jnp.stack([pr.reshape(L + 1, nb, gb * P), pi.reshape(L + 1, nb, gb * P)], axis=2)
    return kk_src, bb_src, cc_src, ap.reshape(L + 1, 2 * G * P)


def _cmul(ar, ai, zr, zi):
    return ar * zr - ai * zi, ar * zi + ai * zr


def _block_diag(src, rows_per_group, cols_per_group):
    t = jnp.concatenate([src] * SSM_GB, axis=0)
    row = lax.broadcasted_iota(jnp.int32, t.shape, 0)
    col = lax.broadcasted_iota(jnp.int32, t.shape, 1)
    return jnp.where(row // rows_per_group == col // cols_per_group, t, 0.0).astype(BF16)


def _s5_body(u_ref, kk_src, bb_src, cc_src, ap_ref, d_ref, y_ref,
             kk_ref, bb_ref, cc_ref, lhs_ref, er_ref, ei_ref, xr_ref, xi_ref):
    L, M, P = SSM_CHUNK, SSM_GROUP, SSM_STATE
    nc = u_ref.shape[0] // L
    hp = SSM_GB * P
    gm = SSM_GB * M
    nq = L // 2
    for a in range(2):
        for b in range(2):
            for dd in range(nq):
                kk_ref[dd, a * gm:(a + 1) * gm, b * gm:(b + 1) * gm] = _block_diag(
                    kk_src[0, dd, a, b], M, M)
            bb_ref[a * gm:(a + 1) * gm, b * hp:(b + 1) * hp] = _block_diag(bb_src[0, a, b], M, P)
            cc_ref[a * hp:(a + 1) * hp, b * gm:(b + 1) * gm] = _block_diag(cc_src[0, a, b], P, M)
    for q in range(nq):
        lhs_ref[q] = jnp.concatenate(
            [u_ref[pl.ds(2 * q, nc, stride=L), :], u_ref[pl.ds(2 * q + 1, nc, stride=L), :]],
            axis=1).astype(BF16)

    er = jnp.zeros((nc, hp), F32)
    ei = jnp.zeros((nc, hp), F32)
    for q in range(nq):
        z = jnp.dot(lhs_ref[q], bb_ref[...], preferred_element_type=F32)
        n = L - 2 - 2 * q
        dr, di = _cmul(ap_ref[n:n + 1, :hp], ap_ref[n:n + 1, hp:], z[:, :hp], z[:, hp:])
        er = er + dr
        ei = ei + di
    er_ref[...] = er
    ei_ref[...] = ei

    ar = ap_ref[L:L + 1, :hp]
    ai = ap_ref[L:L + 1, hp:]

    def step(c, carry):
        xr, xi = carry
        xr_ref[pl.ds(c, 1), :] = xr
        xi_ref[pl.ds(c, 1), :] = xi
        nr, ni = _cmul(ar, ai, xr, xi)
        return nr + er_ref[pl.ds(c, 1), :], ni + ei_ref[pl.ds(c, 1), :]

    zero = jnp.zeros((1, hp), F32)
    lax.fori_loop(0, nc, step, (zero, zero))

    for qq in range(nq):
        n = 2 * qq + 1
        wr, wi = _cmul(ap_ref[n:n + 1, :hp], ap_ref[n:n + 1, hp:], xr_ref[...], xi_ref[...])
        w = jnp.concatenate([wr, wi], axis=1).astype(BF16)
        acc = jnp.dot(w, cc_ref[...], preferred_element_type=F32)
        for q in range(qq + 1):
            acc = acc + jnp.dot(lhs_ref[q], kk_ref[qq - q], preferred_element_type=F32)
        for r in range(2):
            j = 2 * qq + r
            y = acc[:, r * LANES:(r + 1) * LANES] + d_ref[...] * u_ref[pl.ds(j, nc, stride=L), :]
            y_ref[pl.ds(j, nc, stride=L), :] = jax.nn.gelu(y)


def _s5_core(u, tables, d_skip):
    s, width = u.shape
    L, P, M, gb = SSM_CHUNK, SSM_STATE, SSM_GROUP, SSM_GB
    nc = s // L
    nb = width // (gb * M)
    kk_src, bb_src, cc_src, ap = tables
    return pl.pallas_call(
        _s5_body,
        grid=(nb,),
        in_specs=[pl.BlockSpec((s, gb * M), lambda b: (0, b)),
                  pl.BlockSpec((1, L // 2, 2, 2, M, gb * M), lambda b: (b, 0, 0, 0, 0, 0)),
                  pl.BlockSpec((1, 2, 2, M, gb * P), lambda b: (b, 0, 0, 0, 0)),
                  pl.BlockSpec((1, 2, 2, P, gb * M), lambda b: (b, 0, 0, 0, 0)),
                  pl.BlockSpec((L + 1, 2 * gb * P), lambda b: (0, b)),
                  pl.BlockSpec((1, gb * M), lambda b: (0, b))],
        out_specs=pl.BlockSpec((s, gb * M), lambda b: (0, b)),
        out_shape=jax.ShapeDtypeStruct((s, width), F32),
        scratch_shapes=[pltpu.VMEM((L // 2, 2 * gb * M, 2 * gb * M), BF16),
                        pltpu.VMEM((2 * gb * M, 2 * gb * P), BF16),
                        pltpu.VMEM((2 * gb * P, 2 * gb * M), BF16),
                        pltpu.VMEM((L // 2, nc, 2 * gb * M), BF16)]
        + [pltpu.VMEM((nc, gb * P), F32) for _ in range(4)],
        compiler_params=_cparams(("parallel",)),
        name="s5_scan",
    )(u, kk_src, bb_src, cc_src, ap, d_skip.astype(F32).reshape(1, width))


def _glu_body(y_ref, w_ref, b_ref, o_ref):
    y = y_ref[...]
    z = jnp.dot(y.astype(BF16), w_ref[...], preferred_element_type=F32) + b_ref[...]
    o_ref[...] = (y * jax.nn.sigmoid(z)).astype(o_ref.dtype)


def _glu(y, w, b, tm=512):
    s, n = y.shape
    return pl.pallas_call(
        _glu_body,
        grid=(s // tm,),
        in_specs=[pl.BlockSpec((tm, n), lambda i: (i, 0)),
                  pl.BlockSpec((n, n), lambda i: (0, 0)),
                  pl.BlockSpec((1, n), lambda i: (0, 0))],
        out_specs=pl.BlockSpec((tm, n), lambda i: (i, 0)),
        out_shape=jax.ShapeDtypeStruct((s, n), BF16),
        compiler_params=_cparams(("parallel",)),
        name="s5_glu",
    )(y, w, b.reshape(1, n).astype(F32))


def _attn_body(q_ref, k_ref, v_ref, lam_ref, sw_ref, o_ref,
               qs_ref, m_ref, acc_ref, s_ref, p_ref, al_ref, *, tq):
    qi = pl.program_id(1)
    hw = 2 * HEAD_DIM
    q = q_ref[...]
    lane = lax.broadcasted_iota(jnp.int32, q.shape, 1)
    zero = jnp.zeros_like(q)
    qs_ref[0] = jnp.where(lane < HEAD_DIM, q, zero)
    qs_ref[1] = jnp.where(lane >= HEAD_DIM, q, zero)
    m_ref[...] = jnp.full(m_ref.shape, -jnp.inf, F32)
    acc_ref[...] = jnp.zeros(acc_ref.shape, F32)

    def scores(c, slot, masked=False):
        off = pl.multiple_of(c * tq, tq)
        k = k_ref[pl.ds(off, tq), :]
        for mp in range(2):
            s = lax.dot_general(qs_ref[mp], k, (((1,), (1,)), ((), ())),
                                preferred_element_type=F32)
            if masked:
                row = lax.broadcasted_iota(jnp.int32, s.shape, 0)
                col = lax.broadcasted_iota(jnp.int32, s.shape, 1)
                s = jnp.where(col <= row, s, -jnp.inf)
            s_ref[slot, mp] = s

    def softmax(slot):
        for mp in range(2):
            s = s_ref[slot, mp]
            m_old = m_ref[mp]
            m_new = jnp.maximum(m_old, jnp.max(s, axis=-1, keepdims=True))
            al_ref[slot, mp] = jnp.exp(m_old - m_new)
            p_ref[slot, mp] = jnp.exp(s - m_new).astype(BF16)
            m_ref[mp] = m_new

    def pv(c, slot):
        off = pl.multiple_of(c * tq, tq)
        v = v_ref[pl.ds(off, tq), :]
        for mp in range(2):
            acc_ref[mp] = al_ref[slot, mp] * acc_ref[mp] + jnp.dot(
                p_ref[slot, mp], v, preferred_element_type=F32)

    @pl.when(qi == 0)
    def _one_chunk():
        scores(0, 0, masked=True)
        softmax(0)
        pv(0, 0)

    @pl.when(qi == 1)
    def _two_chunks():
        scores(0, 1)
        softmax(1)
        pv(0, 1)
        scores(1, 0, masked=True)
        softmax(0)
        pv(1, 0)

    @pl.when(qi >= 2)
    def _pipelined():
        odd = qi % 2

        @pl.when(odd == 0)
        def _fill_even():
            scores(0, 0)
            scores(1, 1)
            softmax(0)

        @pl.when(odd == 1)
        def _fill_odd():
            scores(0, 1)
            scores(1, 0)
            softmax(1)
            scores(2, 1)
            softmax(0)
            pv(0, 1)

        t0 = 2 + odd

        def two_steps(u, carry):
            t = t0 + 2 * u
            scores(t, 0)
            softmax(1)
            pv(t - 2, 0)
            scores(t + 1, 1)
            softmax(0)
            pv(t - 1, 1)
            return carry

        lax.fori_loop(0, (qi - t0) // 2, two_steps, 0)
        scores(qi, 0, masked=True)
        softmax(1)
        pv(qi - 2, 0)
        softmax(0)
        pv(qi - 1, 1)
        pv(qi, 0)

    lam = (jnp.exp(jnp.sum(lam_ref[0:1, :] * lam_ref[1:2, :]))
           - jnp.exp(jnp.sum(lam_ref[2:3, :] * lam_ref[3:4, :])) + LAMBDA_INIT)
    a1 = acc_ref[0]
    a2 = acc_ref[1]
    o = a1[:, :hw] / a1[:, hw:] - lam * (a2[:, :hw] / a2[:, hw:])
    o = o * lax.rsqrt(jnp.mean(o * o, axis=-1, keepdims=True) + SUBLN_EPS)
    o_ref[...] = (o * sw_ref[...] * (1.0 - LAMBDA_INIT)).astype(o_ref.dtype)


def _diff_attention(qk, v1, lam_params, subln_w, tq=512):
    s = qk.shape[0]
    nq = s // tq
    hw = 2 * HEAD_DIM
    kcol0 = ATTN_WIDTH // hw
    return pl.pallas_call(
        functools.partial(_attn_body, tq=tq),
        grid=(ATTN_HEADS, nq),
        in_specs=[pl.BlockSpec((tq, hw), lambda h, i: (i, h)),
                  pl.BlockSpec((s, hw), lambda h, i: (0, kcol0 + h)),
                  pl.BlockSpec((s, 2 * hw), lambda h, i: (0, h)),
                  pl.BlockSpec((4, HEAD_DIM), lambda h, i: (0, 0)),
                  pl.BlockSpec((1, hw), lambda h, i: (0, 0))],
        out_specs=pl.BlockSpec((tq, hw), lambda h, i: (i, h)),
        out_shape=jax.ShapeDtypeStruct((s, ATTN_WIDTH), BF16),
        scratch_shapes=[pltpu.VMEM((2, tq, hw), BF16),
                        pltpu.VMEM((2, tq, 1), F32),
                        pltpu.VMEM((2, tq, 2 * hw), F32),
                        pltpu.VMEM((2, 2, tq, tq), F32),
                        pltpu.VMEM((2, 2, tq, tq), BF16),
                        pltpu.VMEM((2, 2, tq, 1), F32)],
        compiler_params=_cparams(("parallel", "arbitrary")),
        name="diff_attn",
    )(qk, qk, v1, lam_params, subln_w.reshape(1, hw).astype(F32))


def _merge_body(x_ref, ys_ref, ya_ref, gs_ref, ga_ref, wps_ref, wpa_ref, wo_ref, n2_ref, wr_ref,
                x1_ref, h2_ref, eid_ref, gate_ref, cnt_ref):
    ps = jnp.dot(ys_ref[...], wps_ref[...], preferred_element_type=F32)
    pa = jnp.dot(ya_ref[...], wpa_ref[...], preferred_element_type=F32)
    merged = gs_ref[...].astype(F32) * ps + ga_ref[...].astype(F32) * pa
    x1 = x_ref[...] + jnp.dot(merged.astype(BF16), wo_ref[...], preferred_element_type=F32)
    x1_ref[...] = x1
    h2 = x1 * lax.rsqrt(jnp.mean(x1 * x1, axis=-1, keepdims=True) + NORM_EPS) * n2_ref[...]
    h2_ref[...] = h2
    wr = wr_ref[...]
    h_hi = h2.astype(BF16)
    h_lo = (h2 - h_hi.astype(F32)).astype(BF16)
    w_hi = wr.astype(BF16)
    w_lo = (wr - w_hi.astype(F32)).astype(BF16)
    logits = (jnp.dot(h_hi, w_hi, preferred_element_type=F32)
              + jnp.dot(h_lo, w_hi, preferred_element_type=F32)
              + jnp.dot(h_hi, w_lo, preferred_element_type=F32))
    lane = lax.broadcasted_iota(jnp.int32, logits.shape, 1)
    big = jnp.int32(1 << 20)
    ninf = jnp.float32(-jnp.inf)
    is_g = lane < N_GROUPS
    gl = jnp.where(is_g, logits, ninf)
    gm = jnp.max(gl, axis=-1, keepdims=True)
    g_idx = jnp.min(jnp.where(gl == gm, lane, big), axis=-1, keepdims=True)
    g_val = 1.0 / jnp.sum(jnp.where(is_g, jnp.exp(gl - gm), 0.0), axis=-1, keepdims=True)
    lo = N_GROUPS + g_idx * EXPERTS_PER_GROUP
    in_grp = (lane >= lo) & (lane < lo + EXPERTS_PER_GROUP)
    el = jnp.where(in_grp, logits, ninf)
    e1 = jnp.max(el, axis=-1, keepdims=True)
    i1 = jnp.min(jnp.where(el == e1, lane, big), axis=-1, keepdims=True)
    el2 = jnp.where(lane == i1, ninf, el)
    e2 = jnp.max(el2, axis=-1, keepdims=True)
    i2 = jnp.min(jnp.where(el2 == e2, lane, big), axis=-1, keepdims=True)
    t = jnp.exp(e2 - e1)
    p1 = 1.0 / (1.0 + t)
    p2 = t / (1.0 + t)
    gate_ref[...] = jnp.where(lane == 0, p1 * g_val, jnp.where(lane == 1, p2 * g_val, 0.0))

    @pl.when(pl.program_id(0) == 0)
    def _zero_counts():
        cnt_ref[...] = jnp.zeros(cnt_ref.shape, F32)

    ex1 = i1 - N_GROUPS
    ex2 = i2 - N_GROUPS
    oh1 = lane == ex1
    oh2 = lane == ex2
    onehot = jnp.where(oh1 | oh2, 1.0, 0.0)
    tm = logits.shape[0]
    r_i = lax.broadcasted_iota(jnp.int32, (tm, tm), 0)
    c_i = lax.broadcasted_iota(jnp.int32, (tm, tm), 1)
    tri = jnp.where(c_i < r_i, 1.0, 0.0).astype(BF16)
    before = jnp.dot(tri, onehot.astype(BF16), preferred_element_type=F32) + cnt_ref[...]
    rank1 = jnp.sum(jnp.where(oh1, before, 0.0), axis=-1, keepdims=True).astype(jnp.int32)
    rank2 = jnp.sum(jnp.where(oh2, before, 0.0), axis=-1, keepdims=True).astype(jnp.int32)
    cnt_ref[...] = cnt_ref[...] + jnp.sum(onehot, axis=0, keepdims=True)
    eid_ref[...] = jnp.where(lane == 0, ex1, jnp.where(lane == 1, ex2,
                             jnp.where(lane == 2, rank1, jnp.where(lane == 3, rank2, 0))))


def _merge_route(x, y_ssm, y_attn, gates, wps, wpa, wo, n2w, wr, tm=256):
    s, d = x.shape
    nw = y_ssm.shape[1]
    const = lambda i: (0, 0)
    return pl.pallas_call(
        _merge_body,
        grid=(s // tm,),
        in_specs=[pl.BlockSpec((tm, d), lambda i: (i, 0)),
                  pl.BlockSpec((tm, nw), lambda i: (i, 0)),
                  pl.BlockSpec((tm, nw), lambda i: (i, 0)),
                  pl.BlockSpec((tm, d), lambda i: (i, 0)),
                  pl.BlockSpec((tm, d), lambda i: (i, 1)),
                  pl.BlockSpec((nw, d), const),
                  pl.BlockSpec((nw, d), const),
                  pl.BlockSpec((d, d), const),
                  pl.BlockSpec((1, d), const),
                  pl.BlockSpec((d, LANES), const)],
        out_specs=[pl.BlockSpec((tm, d), lambda i: (i, 0)),
                   pl.BlockSpec((tm, d), lambda i: (i, 0)),
                   pl.BlockSpec((tm, LANES), lambda i: (i, 0)),
                   pl.BlockSpec((tm, LANES), lambda i: (i, 0)),
                   pl.BlockSpec((1, LANES), const)],
        out_shape=[jax.ShapeDtypeStruct((s, d), F32),
                   jax.ShapeDtypeStruct((s, d), F32),
                   jax.ShapeDtypeStruct((s, LANES), jnp.int32),
                   jax.ShapeDtypeStruct((s, LANES), F32),
                   jax.ShapeDtypeStruct((1, LANES), F32)],
        compiler_params=_cparams(("arbitrary",)),
        name="merge_route",
    )(x, y_ssm, y_attn, gates, gates, wps, wpa, wo, n2w.reshape(1, d).astype(F32), wr)


def _row_copy(src_hbm, row, dst_ref, r, sem):
    return pltpu.make_async_copy(src_hbm.at[pl.ds(row, 1), :], dst_ref.at[pl.ds(r, 1), :], sem)


def _gather_rows(src_hbm, idx_ref, base, stride, dst_ref, sem, n):
    def start(r, c):
        _row_copy(src_hbm, idx_ref[base + stride * r], dst_ref, r, sem).start()
        return c

    lax.fori_loop(0, n, start, 0, unroll=8)


def _wait_rows(src_hbm, dst_ref, sem, n):
    pltpu.make_async_copy(src_hbm.at[pl.ds(0, n), :], dst_ref, sem).wait()


def _scatter_tok_body(dest_ref, tok_ref):
    def zero(i, c):
        tok_ref[i] = 0
        return c

    lax.fori_loop(0, tok_ref.shape[0], zero, 0, unroll=8)

    def put(i, c):
        tok_ref[dest_ref[i]] = lax.shift_right_logical(i, TOP_K_LOG2)
        return c

    lax.fori_loop(0, dest_ref.shape[0], put, 0, unroll=8)


def _scatter_tok(dest, buf_len):
    return pl.pallas_call(
        _scatter_tok_body,
        in_specs=[pl.BlockSpec(memory_space=pltpu.SMEM)],
        out_specs=pl.BlockSpec(memory_space=pltpu.SMEM),
        out_shape=jax.ShapeDtypeStruct((buf_len,), jnp.int32),
        name="scatter_tok",
    )(dest)


def _expert_body(be_ref, nb_ref, tok_ref, h2_hbm, wg_ref, wu_ref, wd_ref, o_ref,
                 xb_ref, wgb_ref, wub_ref, wdb_ref, sem):
    b = pl.program_id(0)
    nb = nb_ref[0]

    @pl.when(b == 0)
    def _first():
        _gather_rows(h2_hbm, tok_ref, 0, 1, xb_ref.at[0], sem.at[0], MOE_BLK)

    @pl.when((b == 0) | (be_ref[b] != be_ref[jnp.maximum(b - 1, 0)]))
    def _new_expert():
        wgb_ref[...] = wg_ref[0].astype(BF16)
        wub_ref[...] = wu_ref[0].astype(BF16)
        wdb_ref[...] = wd_ref[0].astype(BF16)

    @pl.when(b < nb)
    def _run():
        cur = b % 2
        nxt = 1 - cur
        _wait_rows(h2_hbm, xb_ref.at[cur], sem.at[cur], MOE_BLK)
        base = (b + 1) * MOE_BLK
        for r in range(MOE_BLK):
            _row_copy(h2_hbm, tok_ref[base + r], xb_ref.at[nxt], r, sem.at[nxt]).start()
        xb = xb_ref[cur].astype(BF16)
        g = jnp.dot(xb, wgb_ref[...], preferred_element_type=F32)
        u = jnp.dot(xb, wub_ref[...], preferred_element_type=F32)
        a = (jax.nn.silu(g) * u).astype(BF16)
        o_ref[...] = jnp.dot(a, wdb_ref[...], preferred_element_type=F32)

        @pl.when(b == nb - 1)
        def _drain():
            _wait_rows(h2_hbm, xb_ref.at[nxt], sem.at[nxt], MOE_BLK)

    @pl.when(b >= nb)
    def _unused():
        o_ref[...] = jnp.zeros(o_ref.shape, o_ref.dtype)


def _experts(h2, block_e, n_used, buf_tok, wg, wu, wd):
    s, d = h2.shape
    buf_len = buf_tok.shape[0]
    n_blocks = buf_len // MOE_BLK
    wmap = lambda b, be, nb, tok: (be[b], 0, 0)
    return pl.pallas_call(
        _expert_body,
        grid_spec=pltpu.PrefetchScalarGridSpec(
            num_scalar_prefetch=3,
            grid=(n_blocks,),
            in_specs=[pl.BlockSpec(memory_space=pl.ANY),
                      pl.BlockSpec((1, d, D_FF), wmap),
                      pl.BlockSpec((1, d, D_FF), wmap),
                      pl.BlockSpec((1, D_FF, d), wmap)],
            out_specs=pl.BlockSpec((MOE_BLK, d), lambda b, be, nb, tok: (b, 0)),
            scratch_shapes=[pltpu.VMEM((2, MOE_BLK, d), F32),
                            pltpu.VMEM((d, D_FF), BF16), pltpu.VMEM((d, D_FF), BF16),
                            pltpu.VMEM((D_FF, d), BF16),
                            pltpu.SemaphoreType.DMA((2,))]),
        out_shape=jax.ShapeDtypeStruct((buf_len, d), F32),
        compiler_params=_cparams(("arbitrary",)),
        name="experts",
    )(block_e, n_used, buf_tok, h2, wg, wu, wd)


def _combine_body(pos_ref, x1_ref, gate_ref, fw_ref, ys_hbm, o_ref, g_ref, sem, *, tm):
    i = pl.program_id(0)

    def gather(tile, slot):
        for k in range(TOP_K):
            _gather_rows(ys_hbm, pos_ref, TOP_K * tile * tm + k, TOP_K, g_ref.at[slot, k],
                         sem.at[slot], tm)

    @pl.when(i == 0)
    def _first():
        gather(0, 0)

    @pl.when(i + 1 < pl.num_programs(0))
    def _prefetch():
        gather(i + 1, (i + 1) % 2)

    cur = i % 2
    for k in range(TOP_K):
        _wait_rows(ys_hbm, g_ref.at[cur, k], sem.at[cur], tm)
    gate = gate_ref[...]
    x = x1_ref[...] + (gate[:, 0:1] * g_ref[cur, 0] + gate[:, 1:2] * g_ref[cur, 1])
    y = x * lax.rsqrt(jnp.mean(x * x, axis=-1, keepdims=True) + NORM_EPS)
    o_ref[...] = y * fw_ref[...]


def _combine(x1, ys, pos, gate, fw, tm=256):
    s, d = x1.shape
    return pl.pallas_call(
        functools.partial(_combine_body, tm=tm),
        grid_spec=pltpu.PrefetchScalarGridSpec(
            num_scalar_prefetch=1,
            grid=(s // tm,),
            in_specs=[pl.BlockSpec((tm, d), lambda i, p: (i, 0)),
                      pl.BlockSpec((tm, LANES), lambda i, p: (i, 0)),
                      pl.BlockSpec((1, d), lambda i, p: (0, 0)),
                      pl.BlockSpec(memory_space=pl.ANY)],
            out_specs=pl.BlockSpec((tm, d), lambda i, p: (i, 0)),
            scratch_shapes=[pltpu.VMEM((2, TOP_K, tm, d), F32),
                            pltpu.SemaphoreType.DMA((2,))]),
        out_shape=jax.ShapeDtypeStruct((s, d), F32),
        compiler_params=_cparams(("arbitrary",)),
        name="combine_norm",
    )(pos, x1, gate, fw.reshape(1, d).astype(F32), ys)


def _dispatch_tables(eid, counts):
    n_tok = eid.shape[0]
    n_slots = n_tok * TOP_K
    buf_len = n_slots + N_EXPERTS * MOE_BLK
    n_blocks = buf_len // MOE_BLK
    counts = counts.astype(jnp.int32)
    padded = ((counts + MOE_BLK - 1) // MOE_BLK) * MOE_BLK
    padded_ends = jnp.cumsum(padded)
    padded_starts = padded_ends - padded
    experts = eid[:, :TOP_K]
    onehot = experts[:, :, None] == jnp.arange(N_EXPERTS, dtype=jnp.int32)
    dest = jnp.sum(jnp.where(onehot, padded_starts, 0), axis=-1) + eid[:, TOP_K:2 * TOP_K]
    block_start = jnp.arange(n_blocks, dtype=jnp.int32) * MOE_BLK
    n_used = (padded_ends[-1] // MOE_BLK).astype(jnp.int32).reshape(1)
    last_start = (n_used[0] - 1) * MOE_BLK
    block_e = jnp.sum(padded_ends[None, :] <= jnp.minimum(block_start, last_start)[:, None],
                      axis=-1).astype(jnp.int32)
    return block_e, n_used, dest.reshape(n_slots).astype(jnp.int32), buf_len


def kernel(x, positions, norm1_w, w_in, ssm_lambda_re, ssm_lambda_im, ssm_log_dt, ssm_b_re, ssm_b_im, ssm_c_re, ssm_c_im, ssm_d, ssm_glu_w, ssm_glu_b, attn_lambda_q1, attn_lambda_k1, attn_lambda_q2, attn_lambda_k2, attn_subln_w, w_proj_ssm, w_proj_attn, w_out, norm2_w, router_group_w, router_expert_w, expert_w_gate, expert_w_up, expert_w_down, final_norm_w):
    bsz, seq, d = x.shape
    depth = norm1_w.shape[0]
    xs = x.reshape(bsz * seq, d)
    pos = positions.reshape(bsz * seq)
    o_q = SSM_WIDTH
    o_v = o_q + 2 * ATTN_WIDTH
    o_g = o_v + ATTN_WIDTH
    for l in range(depth):
        h = _rmsnorm(xs, norm1_w[l].astype(F32), BF16)
        w_in_b = w_in[l].astype(F32)
        u = _mm(h, w_in_b, 0, SSM_WIDTH, "none", F32)
        qk = _mm_rope(h, w_in_b, pos, o_q)
        v = _mm_vones(h, w_in_b, o_v)
        gates = _mm(h, w_in_b, o_g, 2 * D_MODEL, "sigmoid", F32)

        tables = _s5_tables(ssm_lambda_re[l], ssm_lambda_im[l], ssm_log_dt[l],
                            ssm_b_re[l], ssm_b_im[l], ssm_c_re[l], ssm_c_im[l])
        y = _s5_core(u, tables, ssm_d[l])
        y_ssm = _glu(y, ssm_glu_w[l].astype(BF16), ssm_glu_b[l])

        lam_params = jnp.stack([attn_lambda_q1[l], attn_lambda_k1[l],
                                attn_lambda_q2[l], attn_lambda_k2[l]]).astype(F32)
        y_attn = _diff_attention(qk, v, lam_params, attn_subln_w[l])

        wr = jnp.concatenate([router_group_w[l], router_expert_w[l]], axis=1).astype(F32)
        wr = jnp.pad(wr, ((0, 0), (0, LANES - wr.shape[1])))
        x1, h2, eid, gate, counts = _merge_route(
            xs, y_ssm, y_attn, gates, w_proj_ssm[l].astype(BF16), w_proj_attn[l].astype(BF16),
            w_out[l].astype(BF16), norm2_w[l], wr)

        block_e, n_used, dest, buf_len = _dispatch_tables(eid, counts[0, :N_EXPERTS])
        buf_tok = _scatter_tok(dest, buf_len)
        ys = _experts(h2, block_e, n_used, buf_tok, expert_w_gate[l].astype(F32),
                      expert_w_up[l].astype(F32), expert_w_down[l].astype(F32))
        last = l == depth - 1
        assert last, "DEPTH > 1 needs an un-normalised combine"
        xs = _combine(x1, ys, dest, gate, final_norm_w)
    return xs.reshape(bsz, seq, d)
```

```python
import functools
import math

import jax
import jax.numpy as jnp
from jax import lax
from jax.experimental import pallas as pl
from jax.experimental.pallas import tpu as pltpu

F32 = jnp.float32
BF16 = jnp.bfloat16

D_MODEL = 2048
SSM_WIDTH = 1024
SSM_GROUP = 16
SSM_GROUPS = 64
SSM_STATE = 64
ATTN_WIDTH = 1024
ATTN_HEADS = 8
HEAD_DIM = 64
ROPE_THETA = 10000.0
N_GROUPS = 4
EXPERTS_PER_GROUP = 8
N_EXPERTS = 32
TOP_K = 2
TOP_K_LOG2 = 1
D_FF = 512
NORM_EPS = 1e-6
SUBLN_EPS = 1e-5
LAMBDA_INIT = 0.8 - 0.6 * math.exp(-0.3 * 0)

LANES = 128
VMEM_LIMIT = 48 * 1024 * 1024

SSM_CHUNK = 16
SSM_GB = 8
MOE_BLK = 128


def _cparams(sem):
    return pltpu.CompilerParams(dimension_semantics=sem, vmem_limit_bytes=VMEM_LIMIT)


def _rmsnorm_body(x_ref, w_ref, o_ref):
    x = x_ref[...]
    y = x * lax.rsqrt(jnp.mean(x * x, axis=-1, keepdims=True) + NORM_EPS)
    o_ref[...] = (y * w_ref[...]).astype(o_ref.dtype)


def _rmsnorm(x, w, out_dtype, tm=512):
    s, d = x.shape
    return pl.pallas_call(
        _rmsnorm_body,
        grid=(s // tm,),
        in_specs=[pl.BlockSpec((tm, d), lambda i: (i, 0)),
                  pl.BlockSpec((1, d), lambda i: (0, 0))],
        out_specs=pl.BlockSpec((tm, d), lambda i: (i, 0)),
        out_shape=jax.ShapeDtypeStruct((s, d), out_dtype),
        compiler_params=_cparams(("parallel",)),
        name="rmsnorm",
    )(x, w.reshape(1, d))


def _proj(h_ref, w_ref, wb_ref, row_axis):
    @pl.when(pl.program_id(row_axis) == 0)
    def _cast():
        wb_ref[...] = w_ref[...].astype(BF16)

    return jnp.dot(h_ref[...], wb_ref[...], preferred_element_type=F32)


def _mm_body(h_ref, w_ref, o_ref, wb_ref, *, act):
    acc = _proj(h_ref, w_ref, wb_ref, 1)
    if act == "sigmoid":
        acc = jax.nn.sigmoid(acc)
    o_ref[...] = acc.astype(o_ref.dtype)


def _mm(h, w, col0, ncols, act, out_dtype, tm=512, tn=1024):
    s, k = h.shape
    jb = col0 // tn
    return pl.pallas_call(
        functools.partial(_mm_body, act=act),
        grid=(ncols // tn, s // tm),
        in_specs=[pl.BlockSpec((tm, k), lambda j, i: (i, 0)),
                  pl.BlockSpec((k, tn), lambda j, i: (0, j + jb))],
        out_specs=pl.BlockSpec((tm, tn), lambda j, i: (i, j)),
        out_shape=jax.ShapeDtypeStruct((s, ncols), out_dtype),
        scratch_shapes=[pltpu.VMEM((k, tn), BF16)],
        compiler_params=_cparams(("arbitrary", "arbitrary")),
        name="mm_" + act,
    )(h, w)


def _mm_vones_body(h_ref, w_ref, o_ref, wb_ref):
    acc = _proj(h_ref, w_ref, wb_ref, 0)
    hw = 2 * HEAD_DIM
    ones = jnp.ones((acc.shape[0], hw), o_ref.dtype)
    for hd in range(acc.shape[1] // hw):
        o_ref[:, 2 * hd * hw:(2 * hd + 1) * hw] = acc[:, hd * hw:(hd + 1) * hw].astype(o_ref.dtype)
        o_ref[:, (2 * hd + 1) * hw:(2 * hd + 2) * hw] = ones


def _mm_vones(h, w, col0, tm=512):
    s, k = h.shape
    tn = ATTN_WIDTH
    jb = col0 // tn
    return pl.pallas_call(
        _mm_vones_body,
        grid=(s // tm,),
        in_specs=[pl.BlockSpec((tm, k), lambda i: (i, 0)),
                  pl.BlockSpec((k, tn), lambda i: (0, jb))],
        out_specs=pl.BlockSpec((tm, 2 * tn), lambda i: (i, 0)),
        out_shape=jax.ShapeDtypeStruct((s, 2 * tn), BF16),
        scratch_shapes=[pltpu.VMEM((k, tn), BF16)],
        compiler_params=_cparams(("arbitrary",)),
        name="mm_vones",
    )(h, w)


def _mm_rope_body(h_ref, w_ref, pos_ref, invf_ref, o_ref, wb_ref, *, tn):
    j = pl.program_id(0)
    acc = _proj(h_ref, w_ref, wb_ref, 1)
    scale = jnp.where(j == 0, HEAD_DIM ** -0.5, 1.0).astype(F32)
    ang = pos_ref[...].astype(F32) * invf_ref[...]
    lane = lax.broadcasted_iota(jnp.int32, ang.shape, 1)
    low = (lane % HEAD_DIM) < (HEAD_DIM // 2)
    cos = jnp.cos(ang) * scale
    sin = jnp.where(low, -jnp.sin(ang), jnp.sin(ang)) * scale
    for c in range(tn // LANES):
        t = acc[:, c * LANES:(c + 1) * LANES]
        partner = jnp.where(low, pltpu.roll(t, LANES - HEAD_DIM // 2, 1),
                            pltpu.roll(t, HEAD_DIM // 2, 1))
        o_ref[:, c * LANES:(c + 1) * LANES] = (t * cos + partner * sin).astype(o_ref.dtype)


def _mm_rope(h, w, positions, col0, tm=512, tn=1024):
    s, k = h.shape
    jb = col0 // tn
    inv_freq = 1.0 / (ROPE_THETA ** (jnp.arange(0, HEAD_DIM, 2, dtype=F32) / HEAD_DIM))
    invf = jnp.tile(inv_freq, LANES // (HEAD_DIM // 2)).reshape(1, LANES)
    return pl.pallas_call(
        functools.partial(_mm_rope_body, tn=tn),
        grid=(2, s // tm),
        in_specs=[pl.BlockSpec((tm, k), lambda j, i: (i, 0)),
                  pl.BlockSpec((k, tn), lambda j, i: (0, j + jb)),
                  pl.BlockSpec((tm, 1), lambda j, i: (i, 0)),
                  pl.BlockSpec((1, LANES), lambda j, i: (0, 0))],
        out_specs=pl.BlockSpec((tm, tn), lambda j, i: (i, j)),
        out_shape=jax.ShapeDtypeStruct((s, 2 * tn), BF16),
        scratch_shapes=[pltpu.VMEM((k, tn), BF16)],
        compiler_params=_cparams(("arbitrary", "arbitrary")),
        name="mm_rope",
    )(h, w, positions.reshape(s, 1), invf)


def _s5_tables(lam_re, lam_im, log_dt, b_re, b_im, c_re, c_im):
    L, G, P, M = SSM_CHUNK, SSM_GROUPS, SSM_STATE, SSM_GROUP
    hi = lax.Precision.HIGHEST
    lr, li = lam_re.astype(F32), lam_im.astype(F32)
    dt = jnp.exp(log_dt.astype(F32))[:, None]
    n = jnp.arange(L + 1, dtype=F32)[:, None, None]
    mag = jnp.exp(lr * dt * n)
    pr = mag * jnp.cos(li * dt * n)
    pi = mag * jnp.sin(li * dt * n)
    nr, ni = pr[1] - 1.0, pi[1]
    den = lr * lr + li * li
    f_re = (nr * lr + ni * li) / den
    f_im = (ni * lr - nr * li) / den
    br, bi = b_re.astype(F32), b_im.astype(F32)
    bbr = f_re[..., None] * br - f_im[..., None] * bi
    bbi = f_re[..., None] * bi + f_im[..., None] * br
    cr, ci = c_re.astype(F32), c_im.astype(F32)

    nb, gb = G // SSM_GB, SSM_GB
    pr_g = jnp.transpose(pr, (1, 0, 2))
    pi_g = jnp.transpose(pi, (1, 0, 2))
    bbr_t = jnp.transpose(bbr, (0, 2, 1))
    bbi_t = jnp.transpose(bbi, (0, 2, 1))

    abr = pr_g[:, :L, None, :] * bbr_t[:, None] - pi_g[:, :L, None, :] * bbi_t[:, None]
    abi = pr_g[:, :L, None, :] * bbi_t[:, None] + pi_g[:, :L, None, :] * bbr_t[:, None]
    kmat = (jnp.einsum('gnip,gop->gnio', abr, cr, precision=hi)
            - jnp.einsum('gnip,gop->gnio', abi, ci, precision=hi))
    kpad = jnp.pad(kmat, ((0, 0), (1, 0), (0, 0), (0, 0)))
    kq = jnp.stack([jnp.stack([kpad[:, 1 + c2 - r2::2][:, :L // 2] for c2 in range(2)], axis=3)
                    for r2 in range(2)], axis=2)
    kk_src = jnp.transpose(kq.reshape(nb, gb, L // 2, 2, M, 2, M),
                           (0, 2, 3, 5, 4, 1, 6)).reshape(nb, L // 2, 2, 2, M, gb * M)

    ab1r = pr_g[:, 1, None, :] * bbr_t - pi_g[:, 1, None, :] * bbi_t
    ab1i = pr_g[:, 1, None, :] * bbi_t + pi_g[:, 1, None, :] * bbr_t
    b4 = jnp.stack([jnp.stack([ab1r, ab1i], axis=2), jnp.stack([bbr_t, bbi_t], axis=2)],
                   axis=1)
    bb_src = jnp.transpose(b4.reshape(nb, gb, 2, M, 2, P),
                           (0, 2, 4, 3, 1, 5)).reshape(nb, 2, 2, M, gb * P)

    cr_t = jnp.transpose(cr, (0, 2, 1))
    ci_t = jnp.transpose(ci, (0, 2, 1))
    car = cr_t * pr_g[:, 1, :, None] - ci_t * pi_g[:, 1, :, None]
    cai = cr_t * pi_g[:, 1, :, None] + ci_t * pr_g[:, 1, :, None]
    c4 = jnp.stack([jnp.stack([cr_t, car], axis=2), jnp.stack([-ci_t, -cai], axis=2)],
                   axis=1)
    cc_src = jnp.transpose(c4.reshape(nb, gb, 2, P, 2, M),
                           (0, 2, 4, 3, 1, 5)).reshape(nb, 2, 2, P, gb * M)

    ap = jnp.stack([pr.reshape(L + 1, nb, gb * P), pi.reshape(L + 1, nb, gb * P)], axis=2)
    return kk_src, bb_src, cc_src, ap.reshape(L + 1, 2 * G * P)


def _cmul(ar, ai, zr, zi):
    return ar * zr - ai * zi, ar * zi + ai * zr


def _block_diag(src, rows_per_group, cols_per_group):
    t = jnp.concatenate([src] * SSM_GB, axis=0)
    row = lax.broadcasted_iota(jnp.int32, t.shape, 0)
    col = lax.broadcasted_iota(jnp.int32, t.shape, 1)
    return jnp.where(row // rows_per_group == col // cols_per_group, t, 0.0).astype(BF16)


def _s5_body(u_ref, kk_src, bb_src, cc_src, ap_ref, d_ref, y_ref,
             kk_ref, bb_ref, cc_ref, lhs_ref, er_ref, ei_ref, xr_ref, xi_ref):
    L, M, P = SSM_CHUNK, SSM_GROUP, SSM_STATE
    nc = u_ref.shape[0] // L
    hp = SSM_GB * P
    gm = SSM_GB * M
    nq = L // 2
    for a in range(2):
        for b in range(2):
            for dd in range(nq):
                kk_ref[dd, a * gm:(a + 1) * gm, b * gm:(b + 1) * gm] = _block_diag(
                    kk_src[0, dd, a, b], M, M)
            bb_ref[a * gm:(a + 1) * gm, b * hp:(b + 1) * hp] = _block_diag(bb_src[0, a, b], M, P)
            cc_ref[a * hp:(a + 1) * hp, b * gm:(b + 1) * gm] = _block_diag(cc_src[0, a, b], P, M)
    for q in range(nq):
        lhs_ref[q] = jnp.concatenate(
            [u_ref[pl.ds(2 * q, nc, stride=L), :], u_ref[pl.ds(2 * q + 1, nc, stride=L), :]],
            axis=1).astype(BF16)

    er = jnp.zeros((nc, hp), F32)
    ei = jnp.zeros((nc, hp), F32)
    for q in range(nq):
        z = jnp.dot(lhs_ref[q], bb_ref[...], preferred_element_type=F32)
        n = L - 2 - 2 * q
        dr, di = _cmul(ap_ref[n:n + 1, :hp], ap_ref[n:n + 1, hp:], z[:, :hp], z[:, hp:])
        er = er + dr
        ei = ei + di
    er_ref[...] = er
    ei_ref[...] = ei

    ar = ap_ref[L:L + 1, :hp]
    ai = ap_ref[L:L + 1, hp:]

    def step(c, carry):
        xr, xi = carry
        xr_ref[pl.ds(c, 1), :] = xr
        xi_ref[pl.ds(c, 1), :] = xi
        nr, ni = _cmul(ar, ai, xr, xi)
        return nr + er_ref[pl.ds(c, 1), :], ni + ei_ref[pl.ds(c, 1), :]

    zero = jnp.zeros((1, hp), F32)
    lax.fori_loop(0, nc, step, (zero, zero))

    for qq in range(nq):
        n = 2 * qq + 1
        wr, wi = _cmul(ap_ref[n:n + 1, :hp], ap_ref[n:n + 1, hp:], xr_ref[...], xi_ref[...])
        w = jnp.concatenate([wr, wi], axis=1).astype(BF16)
        acc = jnp.dot(w, cc_ref[...], preferred_element_type=F32)
        for q in range(qq + 1):
            acc = acc + jnp.dot(lhs_ref[q], kk_ref[qq - q], preferred_element_type=F32)
        for r in range(2):
            j = 2 * qq + r
            y = acc[:, r * LANES:(r + 1) * LANES] + d_ref[...] * u_ref[pl.ds(j, nc, stride=L), :]
            y_ref[pl.ds(j, nc, stride=L), :] = jax.nn.gelu(y)


def _s5_core(u, tables, d_skip):
    s, width = u.shape
    L, P, M, gb = SSM_CHUNK, SSM_STATE, SSM_GROUP, SSM_GB
    nc = s // L
    nb = width // (gb * M)
    kk_src, bb_src, cc_src, ap = tables
    return pl.pallas_call(
        _s5_body,
        grid=(nb,),
        in_specs=[pl.BlockSpec((s, gb * M), lambda b: (0, b)),
                  pl.BlockSpec((1, L // 2, 2, 2, M, gb * M), lambda b: (b, 0, 0, 0, 0, 0)),
                  pl.BlockSpec((1, 2, 2, M, gb * P), lambda b: (b, 0, 0, 0, 0)),
                  pl.BlockSpec((1, 2, 2, P, gb * M), lambda b: (b, 0, 0, 0, 0)),
                  pl.BlockSpec((L + 1, 2 * gb * P), lambda b: (0, b)),
                  pl.BlockSpec((1, gb * M), lambda b: (0, b))],
        out_specs=pl.BlockSpec((s, gb * M), lambda b: (0, b)),
        out_shape=jax.ShapeDtypeStruct((s, width), F32),
        scratch_shapes=[pltpu.VMEM((L // 2, 2 * gb * M, 2 * gb * M), BF16),
                        pltpu.VMEM((2 * gb * M, 2 * gb * P), BF16),
                        pltpu.VMEM((2 * gb * P, 2 * gb * M), BF16),
                        pltpu.VMEM((L // 2, nc, 2 * gb * M), BF16)]
        + [pltpu.VMEM((nc, gb * P), F32) for _ in range(4)],
        compiler_params=_cparams(("parallel",)),
        name="s5_scan",
    )(u, kk_src, bb_src, cc_src, ap, d_skip.astype(F32).reshape(1, width))


def _glu_body(y_ref, w_ref, b_ref, o_ref):
    y = y_ref[...]
    z = jnp.dot(y.astype(BF16), w_ref[...], preferred_element_type=F32) + b_ref[...]
    o_ref[...] = (y * jax.nn.sigmoid(z)).astype(o_ref.dtype)


def _glu(y, w, b, tm=512):
    s, n = y.shape
    return pl.pallas_call(
        _glu_body,
        grid=(s // tm,),
        in_specs=[pl.BlockSpec((tm, n), lambda i: (i, 0)),
                  pl.BlockSpec((n, n), lambda i: (0, 0)),
                  pl.BlockSpec((1, n), lambda i: (0, 0))],
        out_specs=pl.BlockSpec((tm, n), lambda i: (i, 0)),
        out_shape=jax.ShapeDtypeStruct((s, n), BF16),
        compiler_params=_cparams(("parallel",)),
        name="s5_glu",
    )(y, w, b.reshape(1, n).astype(F32))


def _attn_body(q_ref, k_ref, v_ref, lam_ref, sw_ref, o_ref,
               qs_ref, m_ref, acc_ref, s_ref, p_ref, al_ref, *, tq):
    qi = pl.program_id(1)
    hw = 2 * HEAD_DIM
    q = q_ref[...]
    lane = lax.broadcasted_iota(jnp.int32, q.shape, 1)
    zero = jnp.zeros_like(q)
    qs_ref[0] = jnp.where(lane < HEAD_DIM, q, zero)
    qs_ref[1] = jnp.where(lane >= HEAD_DIM, q, zero)
    m_ref[...] = jnp.full(m_ref.shape, -jnp.inf, F32)
    acc_ref[...] = jnp.zeros(acc_ref.shape, F32)

    def scores(c, slot, masked=False):
        off = pl.multiple_of(c * tq, tq)
        k = k_ref[pl.ds(off, tq), :]
        for mp in range(2):
            s = lax.dot_general(qs_ref[mp], k, (((1,), (1,)), ((), ())),
                                preferred_element_type=F32)
            if masked:
                row = lax.broadcasted_iota(jnp.int32, s.shape, 0)
                col = lax.broadcasted_iota(jnp.int32, s.shape, 1)
                s = jnp.where(col <= row, s, -jnp.inf)
            s_ref[slot, mp] = s

    def softmax(slot):
        for mp in range(2):
            s = s_ref[slot, mp]
            m_old = m_ref[mp]
            m_new = jnp.maximum(m_old, jnp.max(s, axis=-1, keepdims=True))
            al_ref[slot, mp] = jnp.exp(m_old - m_new)
            p_ref[slot, mp] = jnp.exp(s - m_new).astype(BF16)
            m_ref[mp] = m_new

    def pv(c, slot):
        off = pl.multiple_of(c * tq, tq)
        v = v_ref[pl.ds(off, tq), :]
        for mp in range(2):
            acc_ref[mp] = al_ref[slot, mp] * acc_ref[mp] + jnp.dot(
                p_ref[slot, mp], v, preferred_element_type=F32)

    @pl.when(qi == 0)
    def _one_chunk():
        scores(0, 0, masked=True)
        softmax(0)
        pv(0, 0)

    @pl.when(qi == 1)
    def _two_chunks():
        scores(0, 1)
        softmax(1)
        pv(0, 1)
        scores(1, 0, masked=True)
        softmax(0)
        pv(1, 0)

    @pl.when(qi >= 2)
    def _pipelined():
        odd = qi % 2

        @pl.when(odd == 0)
        def _fill_even():
            scores(0, 0)
            scores(1, 1)
            softmax(0)

        @pl.when(odd == 1)
        def _fill_odd():
            scores(0, 1)
            scores(1, 0)
            softmax(1)
            scores(2, 1)
            softmax(0)
            pv(0, 1)

        t0 = 2 + odd

        def two_steps(u, carry):
            t = t0 + 2 * u
            scores(t, 0)
            softmax(1)
            pv(t - 2, 0)
            scores(t + 1, 1)
            softmax(0)
            pv(t - 1, 1)
            return carry

        lax.fori_loop(0, (qi - t0) // 2, two_steps, 0)
        scores(qi, 0, masked=True)
        softmax(1)
        pv(qi - 2, 0)
        softmax(0)
        pv(qi - 1, 1)
        pv(qi, 0)

    lam = (jnp.exp(jnp.sum(lam_ref[0:1, :] * lam_ref[1:2, :]))
           - jnp.exp(jnp.sum(lam_ref[2:3, :] * lam_ref[3:4, :])) + LAMBDA_INIT)
    a1 = acc_ref[0]
    a2 = acc_ref[1]
    o = a1[:, :hw] / a1[:, hw:] - lam * (a2[:, :hw] / a2[:, hw:])
    o = o * lax.rsqrt(jnp.mean(o * o, axis=-1, keepdims=True) + SUBLN_EPS)
    o_ref[...] = (o * sw_ref[...] * (1.0 - LAMBDA_INIT)).astype(o_ref.dtype)


def _diff_attention(qk, v1, lam_params, subln_w, tq=512):
    s = qk.shape[0]
    nq = s // tq
    hw = 2 * HEAD_DIM
    kcol0 = ATTN_WIDTH // hw
    return pl.pallas_call(
        functools.partial(_attn_body, tq=tq),
        grid=(ATTN_HEADS, nq),
        in_specs=[pl.BlockSpec((tq, hw), lambda h, i: (i, h)),
                  pl.BlockSpec((s, hw), lambda h, i: (0, kcol0 + h)),
                  pl.BlockSpec((s, 2 * hw), lambda h, i: (0, h)),
                  pl.BlockSpec((4, HEAD_DIM), lambda h, i: (0, 0)),
                  pl.BlockSpec((1, hw), lambda h, i: (0, 0))],
        out_specs=pl.BlockSpec((tq, hw), lambda h, i: (i, h)),
        out_shape=jax.ShapeDtypeStruct((s, ATTN_WIDTH), BF16),
        scratch_shapes=[pltpu.VMEM((2, tq, hw), BF16),
                        pltpu.VMEM((2, tq, 1), F32),
                        pltpu.VMEM((2, tq, 2 * hw), F32),
                        pltpu.VMEM((2, 2, tq, tq), F32),
                        pltpu.VMEM((2, 2, tq, tq), BF16),
                        pltpu.VMEM((2, 2, tq, 1), F32)],
        compiler_params=_cparams(("parallel", "arbitrary")),
        name="diff_attn",
    )(qk, qk, v1, lam_params, subln_w.reshape(1, hw).astype(F32))


def _merge_body(x_ref, ys_ref, ya_ref, gs_ref, ga_ref, wps_ref, wpa_ref, wo_ref, n2_ref, wr_ref,
                x1_ref, h2_ref, eid_ref, gate_ref, cnt_ref):
    ps = jnp.dot(ys_ref[...], wps_ref[...], preferred_element_type=F32)
    pa = jnp.dot(ya_ref[...], wpa_ref[...], preferred_element_type=F32)
    merged = gs_ref[...].astype(F32) * ps + ga_ref[...].astype(F32) * pa
    x1 = x_ref[...] + jnp.dot(merged.astype(BF16), wo_ref[...], preferred_element_type=F32)
    x1_ref[...] = x1
    h2 = x1 * lax.rsqrt(jnp.mean(x1 * x1, axis=-1, keepdims=True) + NORM_EPS) * n2_ref[...]
    h2_ref[...] = h2
    wr = wr_ref[...]
    h_hi = h2.astype(BF16)
    h_lo = (h2 - h_hi.astype(F32)).astype(BF16)
    w_hi = wr.astype(BF16)
    w_lo = (wr - w_hi.astype(F32)).astype(BF16)
    logits = (jnp.dot(h_hi, w_hi, preferred_element_type=F32)
              + jnp.dot(h_lo, w_hi, preferred_element_type=F32)
              + jnp.dot(h_hi, w_lo, preferred_element_type=F32))
    lane = lax.broadcasted_iota(jnp.int32, logits.shape, 1)
    big = jnp.int32(1 << 20)
    ninf = jnp.float32(-jnp.inf)
    is_g = lane < N_GROUPS
    gl = jnp.where(is_g, logits, ninf)
    gm = jnp.max(gl, axis=-1, keepdims=True)
    g_idx = jnp.min(jnp.where(gl == gm, lane, big), axis=-1, keepdims=True)
    g_val = 1.0 / jnp.sum(jnp.where(is_g, jnp.exp(gl - gm), 0.0), axis=-1, keepdims=True)
    lo = N_GROUPS + g_idx * EXPERTS_PER_GROUP
    in_grp = (lane >= lo) & (lane < lo + EXPERTS_PER_GROUP)
    el = jnp.where(in_grp, logits, ninf)
    e1 = jnp.max(el, axis=-1, keepdims=True)
    i1 = jnp.min(jnp.where(el == e1, lane, big), axis=-1, keepdims=True)
    el2 = jnp.where(lane == i1, ninf, el)
    e2 = jnp.max(el2, axis=-1, keepdims=True)
    i2 = jnp.min(jnp.where(el2 == e2, lane, big), axis=-1, keepdims=True)
    t = jnp.exp(e2 - e1)
    p1 = 1.0 / (1.0 + t)
    p2 = t / (1.0 + t)
    gate_ref[...] = jnp.where(lane == 0, p1 * g_val, jnp.where(lane == 1, p2 * g_val, 0.0))

    @pl.when(pl.program_id(0) == 0)
    def _zero_counts():
        cnt_ref[...] = jnp.zeros(cnt_ref.shape, F32)

    ex1 = i1 - N_GROUPS
    ex2 = i2 - N_GROUPS
    oh1 = lane == ex1
    oh2 = lane == ex2
    onehot = jnp.where(oh1 | oh2, 1.0, 0.0)
    tm = logits.shape[0]
    r_i = lax.broadcasted_iota(jnp.int32, (tm, tm), 0)
    c_i = lax.broadcasted_iota(jnp.int32, (tm, tm), 1)
    tri = jnp.where(c_i < r_i, 1.0, 0.0).astype(BF16)
    before = jnp.dot(tri, onehot.astype(BF16), preferred_element_type=F32) + cnt_ref[...]
    rank1 = jnp.sum(jnp.where(oh1, before, 0.0), axis=-1, keepdims=True).astype(jnp.int32)
    rank2 = jnp.sum(jnp.where(oh2, before, 0.0), axis=-1, keepdims=True).astype(jnp.int32)
    cnt_ref[...] = cnt_ref[...] + jnp.sum(onehot, axis=0, keepdims=True)
    eid_ref[...] = jnp.where(lane == 0, ex1, jnp.where(lane == 1, ex2,
                             jnp.where(lane == 2, rank1, jnp.where(lane == 3, rank2, 0))))


def _merge_route(x, y_ssm, y_attn, gates, wps, wpa, wo, n2w, wr, tm=256):
    s, d = x.shape
    nw = y_ssm.shape[1]
    const = lambda i: (0, 0)
    return pl.pallas_call(
        _merge_body,
        grid=(s // tm,),
        in_specs=[pl.BlockSpec((tm, d), lambda i: (i, 0)),
                  pl.BlockSpec((tm, nw), lambda i: (i, 0)),
                  pl.BlockSpec((tm, nw), lambda i: (i, 0)),
                  pl.BlockSpec((tm, d), lambda i: (i, 0)),
                  pl.BlockSpec((tm, d), lambda i: (i, 1)),
                  pl.BlockSpec((nw, d), const),
                  pl.BlockSpec((nw, d), const),
                  pl.BlockSpec((d, d), const),
                  pl.BlockSpec((1, d), const),
                  pl.BlockSpec((d, LANES), const)],
        out_specs=[pl.BlockSpec((tm, d), lambda i: (i, 0)),
                   pl.BlockSpec((tm, d), lambda i: (i, 0)),
                   pl.BlockSpec((tm, LANES), lambda i: (i, 0)),
                   pl.BlockSpec((tm, LANES), lambda i: (i, 0)),
                   pl.BlockSpec((1, LANES), const)],
        out_shape=[jax.ShapeDtypeStruct((s, d), F32),
                   jax.ShapeDtypeStruct((s, d), F32),
                   jax.ShapeDtypeStruct((s, LANES), jnp.int32),
                   jax.ShapeDtypeStruct((s, LANES), F32),
                   jax.ShapeDtypeStruct((1, LANES), F32)],
        compiler_params=_cparams(("arbitrary",)),
        name="merge_route",
    )(x, y_ssm, y_attn, gates, gates, wps, wpa, wo, n2w.reshape(1, d).astype(F32), wr)


def _row_copy(src_hbm, row, dst_ref, r, sem):
    return pltpu.make_async_copy(src_hbm.at[pl.ds(row, 1), :], dst_ref.at[pl.ds(r, 1), :], sem)


def _gather_rows(src_hbm, idx_ref, base, stride, dst_ref, sem, n):
    def start(r, c):
        _row_copy(src_hbm, idx_ref[base + stride * r], dst_ref, r, sem).start()
        return c

    lax.fori_loop(0, n, start, 0, unroll=8)


def _wait_rows(src_hbm, dst_ref, sem, n):
    pltpu.make_async_copy(src_hbm.at[pl.ds(0, n), :], dst_ref, sem).wait()


def _scatter_tok_body(dest_ref, tok_ref):
    def zero(i, c):
        tok_ref[i] = 0
        return c

    lax.fori_loop(0, tok_ref.shape[0], zero, 0, unroll=8)

    def put(i, c):
        tok_ref[dest_ref[i]] = lax.shift_right_logical(i, TOP_K_LOG2)
        return c

    lax.fori_loop(0, dest_ref.shape[0], put, 0, unroll=8)


def _scatter_tok(dest, buf_len):
    return pl.pallas_call(
        _scatter_tok_body,
        in_specs=[pl.BlockSpec(memory_space=pltpu.SMEM)],
        out_specs=pl.BlockSpec(memory_space=pltpu.SMEM),
        out_shape=jax.ShapeDtypeStruct((buf_len,), jnp.int32),
        name="scatter_tok",
    )(dest)


def _block_out_copy(ob_ref, slot, ys_hbm, blk, sem):
    return pltpu.make_async_copy(ob_ref.at[slot], ys_hbm.at[pl.ds(blk * MOE_BLK, MOE_BLK), :],
                                 sem.at[slot])


def _expert_body(fb_ref, nblk_ref, nb_ref, tok_ref, h2_hbm, wg_ref, wu_ref, wd_ref, ys_hbm,
                 xb_ref, ob_ref, wgb_ref, wub_ref, wdb_ref, gsem, osem, *, n_blocks):
    e = pl.program_id(0)
    nb = nb_ref[0]
    first = fb_ref[e]
    count = nblk_ref[e]

    @pl.when(count > 0)
    def _expert():
        wgb_ref[...] = wg_ref[0].astype(BF16)
        wub_ref[...] = wu_ref[0].astype(BF16)
        wdb_ref[...] = wd_ref[0].astype(BF16)

        @pl.when(first == 0)
        def _first_rows():
            _gather_rows(h2_hbm, tok_ref, 0, 1, xb_ref.at[0], gsem.at[0], MOE_BLK)

        def block(t, carry):
            g = first + t
            cur = g % 2
            nxt = 1 - cur
            _wait_rows(h2_hbm, xb_ref.at[cur], gsem.at[cur], MOE_BLK)

            @pl.when(g >= 2)
            def _staging_free():
                _block_out_copy(ob_ref, cur, ys_hbm, g - 2, osem).wait()

            base = (g + 1) * MOE_BLK
            for r in range(MOE_BLK):
                _row_copy(h2_hbm, tok_ref[base + r], xb_ref.at[nxt], r, gsem.at[nxt]).start()
            xb = xb_ref[cur].astype(BF16)
            hg = jnp.dot(xb, wgb_ref[...], preferred_element_type=F32)
            hu = jnp.dot(xb, wub_ref[...], preferred_element_type=F32)
            act = (jax.nn.silu(hg) * hu).astype(BF16)
            ob_ref[cur] = jnp.dot(act, wdb_ref[...], preferred_element_type=F32)
            _block_out_copy(ob_ref, cur, ys_hbm, g, osem).start()
            return carry

        lax.fori_loop(0, count, block, 0)

    @pl.when(e == pl.num_programs(0) - 1)
    def _finish():
        _wait_rows(h2_hbm, xb_ref.at[nb % 2], gsem.at[nb % 2], MOE_BLK)
        _block_out_copy(ob_ref, (nb - 1) % 2, ys_hbm, nb - 1, osem).wait()

        @pl.when(nb >= 2)
        def _():
            _block_out_copy(ob_ref, nb % 2, ys_hbm, nb - 2, osem).wait()

        ob_ref[0] = jnp.zeros(ob_ref.shape[1:], ob_ref.dtype)

        def zero_block(g, carry):
            _block_out_copy(ob_ref, 0, ys_hbm, g, osem).start()
            return carry

        lax.fori_loop(nb, n_blocks, zero_block, 0)

        def zero_wait(g, carry):
            _block_out_copy(ob_ref, 0, ys_hbm, g, osem).wait()
            return carry

        lax.fori_loop(nb, n_blocks, zero_wait, 0)


def _experts(h2, first_blk, n_blk, n_used, buf_tok, wg, wu, wd):
    s, d = h2.shape
    buf_len = buf_tok.shape[0]
    n_blocks = buf_len // MOE_BLK
    wmap = lambda e, fb, nk, nb, tok: (e, 0, 0)
    return pl.pallas_call(
        functools.partial(_expert_body, n_blocks=n_blocks),
        grid_spec=pltpu.PrefetchScalarGridSpec(
            num_scalar_prefetch=4,
            grid=(N_EXPERTS,),
            in_specs=[pl.BlockSpec(memory_space=pl.ANY),
                      pl.BlockSpec((1, d, D_FF), wmap),
                      pl.BlockSpec((1, d, D_FF), wmap),
                      pl.BlockSpec((1, D_FF, d), wmap)],
            out_specs=pl.BlockSpec(memory_space=pl.ANY),
            scratch_shapes=[pltpu.VMEM((2, MOE_BLK, d), F32), pltpu.VMEM((2, MOE_BLK, d), F32),
                            pltpu.VMEM((d, D_FF), BF16), pltpu.VMEM((d, D_FF), BF16),
                            pltpu.VMEM((D_FF, d), BF16),
                            pltpu.SemaphoreType.DMA((2,)), pltpu.SemaphoreType.DMA((2,))]),
        out_shape=jax.ShapeDtypeStruct((buf_len, d), F32),
        compiler_params=_cparams(("arbitrary",)),
        name="experts",
    )(first_blk, n_blk, n_used, buf_tok, h2, wg, wu, wd)


def _combine_body(pos_ref, x1_ref, gate_ref, fw_ref, ys_hbm, o_ref, g_ref, sem, *, tm):
    i = pl.program_id(0)

    def gather(tile, slot):
        for k in range(TOP_K):
            _gather_rows(ys_hbm, pos_ref, TOP_K * tile * tm + k, TOP_K, g_ref.at[slot, k],
                         sem.at[slot], tm)

    @pl.when(i == 0)
    def _first():
        gather(0, 0)

    @pl.when(i + 1 < pl.num_programs(0))
    def _prefetch():
        gather(i + 1, (i + 1) % 2)

    cur = i % 2
    for k in range(TOP_K):
        _wait_rows(ys_hbm, g_ref.at[cur, k], sem.at[cur], tm)
    gate = gate_ref[...]
    x = x1_ref[...] + (gate[:, 0:1] * g_ref[cur, 0] + gate[:, 1:2] * g_ref[cur, 1])
    y = x * lax.rsqrt(jnp.mean(x * x, axis=-1, keepdims=True) + NORM_EPS)
    o_ref[...] = y * fw_ref[...]


def _combine(x1, ys, pos, gate, fw, tm=256):
    s, d = x1.shape
    return pl.pallas_call(
        functools.partial(_combine_body, tm=tm),
        grid_spec=pltpu.PrefetchScalarGridSpec(
            num_scalar_prefetch=1,
            grid=(s // tm,),
            in_specs=[pl.BlockSpec((tm, d), lambda i, p: (i, 0)),
                      pl.BlockSpec((tm, LANES), lambda i, p: (i, 0)),
                      pl.BlockSpec((1, d), lambda i, p: (0, 0)),
                      pl.BlockSpec(memory_space=pl.ANY)],
            out_specs=pl.BlockSpec((tm, d), lambda i, p: (i, 0)),
            scratch_shapes=[pltpu.VMEM((2, TOP_K, tm, d), F32),
                            pltpu.SemaphoreType.DMA((2,))]),
        out_shape=jax.ShapeDtypeStruct((s, d), F32),
        compiler_params=_cparams(("arbitrary",)),
        name="combine_norm",
    )(pos, x1, gate, fw.reshape(1, d).astype(F32), ys)


def _dispatch_tables(eid, counts):
    n_tok = eid.shape[0]
    n_slots = n_tok * TOP_K
    buf_len = n_slots + N_EXPERTS * MOE_BLK
    counts = counts.astype(jnp.int32)
    padded = ((counts + MOE_BLK - 1) // MOE_BLK) * MOE_BLK
    padded_ends = jnp.cumsum(padded)
    padded_starts = padded_ends - padded
    experts = eid[:, :TOP_K]
    onehot = experts[:, :, None] == jnp.arange(N_EXPERTS, dtype=jnp.int32)
    dest = jnp.sum(jnp.where(onehot, padded_starts, 0), axis=-1) + eid[:, TOP_K:2 * TOP_K]
    n_used = (padded_ends[-1] // MOE_BLK).astype(jnp.int32).reshape(1)
    first_blk = (padded_starts // MOE_BLK).astype(jnp.int32)
    n_blk = (padded // MOE_BLK).astype(jnp.int32)
    return first_blk, n_blk, n_used, dest.reshape(n_slots).astype(jnp.int32), buf_len


def kernel(x, positions, norm1_w, w_in, ssm_lambda_re, ssm_lambda_im, ssm_log_dt, ssm_b_re, ssm_b_im, ssm_c_re, ssm_c_im, ssm_d, ssm_glu_w, ssm_glu_b, attn_lambda_q1, attn_lambda_k1, attn_lambda_q2, attn_lambda_k2, attn_subln_w, w_proj_ssm, w_proj_attn, w_out, norm2_w, router_group_w, router_expert_w, expert_w_gate, expert_w_up, expert_w_down, final_norm_w):
    bsz, seq, d = x.shape
    depth = norm1_w.shape[0]
    xs = x.reshape(bsz * seq, d)
    pos = positions.reshape(bsz * seq)
    o_q = SSM_WIDTH
    o_v = o_q + 2 * ATTN_WIDTH
    o_g = o_v + ATTN_WIDTH
    for l in range(depth):
        h = _rmsnorm(xs, norm1_w[l].astype(F32), BF16)
        w_in_b = w_in[l].astype(F32)
        u = _mm(h, w_in_b, 0, SSM_WIDTH, "none", F32)
        qk = _mm_rope(h, w_in_b, pos, o_q)
        v = _mm_vones(h, w_in_b, o_v)
        gates = _mm(h, w_in_b, o_g, 2 * D_MODEL, "sigmoid", F32)

        tables = _s5_tables(ssm_lambda_re[l], ssm_lambda_im[l], ssm_log_dt[l],
                            ssm_b_re[l], ssm_b_im[l], ssm_c_re[l], ssm_c_im[l])
        y = _s5_core(u, tables, ssm_d[l])
        y_ssm = _glu(y, ssm_glu_w[l].astype(BF16), ssm_glu_b[l])

        lam_params = jnp.stack([attn_lambda_q1[l], attn_lambda_k1[l],
                                attn_lambda_q2[l], attn_lambda_k2[l]]).astype(F32)
        y_attn = _diff_attention(qk, v, lam_params, attn_subln_w[l])

        wr = jnp.concatenate([router_group_w[l], router_expert_w[l]], axis=1).astype(F32)
        wr = jnp.pad(wr, ((0, 0), (0, LANES - wr.shape[1])))
        x1, h2, eid, gate, counts = _merge_route(
            xs, y_ssm, y_attn, gates, w_proj_ssm[l].astype(BF16), w_proj_attn[l].astype(BF16),
            w_out[l].astype(BF16), norm2_w[l], wr)

        first_blk, n_blk, n_used, dest, buf_len = _dispatch_tables(eid, counts[0, :N_EXPERTS])
        buf_tok = _scatter_tok(dest, buf_len)
        ys = _experts(h2, first_blk, n_blk, n_used, buf_tok, expert_w_gate[l].astype(F32),
                      expert_w_up[l].astype(F32), expert_w_down[l].astype(F32))
        last = l == depth - 1
        assert last, "DEPTH > 1 needs an un-normalised combine"
        xs = _combine(x1, ys, dest, gate, final_norm_w)
    return xs.reshape(bsz, seq, d)
```

```python
import functools
import math

import jax
import jax.numpy as jnp
from jax import lax
from jax.experimental import pallas as pl
from jax.experimental.pallas import tpu as pltpu

F32 = jnp.float32
BF16 = jnp.bfloat16

D_MODEL = 2048
SSM_WIDTH = 1024
SSM_GROUP = 16
SSM_GROUPS = 64
SSM_STATE = 64
ATTN_WIDTH = 1024
ATTN_HEADS = 8
HEAD_DIM = 64
ROPE_THETA = 10000.0
N_GROUPS = 4
EXPERTS_PER_GROUP = 8
N_EXPERTS = 32
TOP_K = 2
TOP_K_LOG2 = 1
D_FF = 512
NORM_EPS = 1e-6
SUBLN_EPS = 1e-5
LAMBDA_INIT = 0.8 - 0.6 * math.exp(-0.3 * 0)

LANES = 128
VMEM_LIMIT = 48 * 1024 * 1024

SSM_CHUNK = 16
SSM_GB = 8
MOE_BLK = 128


def _cparams(sem):
    return pltpu.CompilerParams(dimension_semantics=sem, vmem_limit_bytes=VMEM_LIMIT)


def _rmsnorm_body(x_ref, w_ref, o_ref):
    x = x_ref[...]
    y = x * lax.rsqrt(jnp.mean(x * x, axis=-1, keepdims=True) + NORM_EPS)
    o_ref[...] = (y * w_ref[...]).astype(o_ref.dtype)


def _rmsnorm(x, w, out_dtype, tm=512):
    s, d = x.shape
    return pl.pallas_call(
        _rmsnorm_body,
        grid=(s // tm,),
        in_specs=[pl.BlockSpec((tm, d), lambda i: (i, 0)),
                  pl.BlockSpec((1, d), lambda i: (0, 0))],
        out_specs=pl.BlockSpec((tm, d), lambda i: (i, 0)),
        out_shape=jax.ShapeDtypeStruct((s, d), out_dtype),
        compiler_params=_cparams(("parallel",)),
        name="rmsnorm",
    )(x, w.reshape(1, d))


def _proj(h_ref, w_ref, wb_ref, row_axis):
    @pl.when(pl.program_id(row_axis) == 0)
    def _cast():
        wb_ref[...] = w_ref[...].astype(BF16)

    return jnp.dot(h_ref[...], wb_ref[...], preferred_element_type=F32)


def _mm_body(h_ref, w_ref, o_ref, wb_ref, *, act):
    acc = _proj(h_ref, w_ref, wb_ref, 1)
    if act == "sigmoid":
        acc = jax.nn.sigmoid(acc)
    o_ref[...] = acc.astype(o_ref.dtype)


def _mm(h, w, col0, ncols, act, out_dtype, tm=512, tn=1024):
    s, k = h.shape
    jb = col0 // tn
    return pl.pallas_call(
        functools.partial(_mm_body, act=act),
        grid=(ncols // tn, s // tm),
        in_specs=[pl.BlockSpec((tm, k), lambda j, i: (i, 0)),
                  pl.BlockSpec((k, tn), lambda j, i: (0, j + jb))],
        out_specs=pl.BlockSpec((tm, tn), lambda j, i: (i, j)),
        out_shape=jax.ShapeDtypeStruct((s, ncols), out_dtype),
        scratch_shapes=[pltpu.VMEM((k, tn), BF16)],
        compiler_params=_cparams(("arbitrary", "arbitrary")),
        name="mm_" + act,
    )(h, w)


def _mm_vones_body(h_ref, w_ref, o_ref, wb_ref):
    acc = _proj(h_ref, w_ref, wb_ref, 0)
    hw = 2 * HEAD_DIM
    ones = jnp.ones((acc.shape[0], hw), o_ref.dtype)
    for hd in range(acc.shape[1] // hw):
        o_ref[:, 2 * hd * hw:(2 * hd + 1) * hw] = acc[:, hd * hw:(hd + 1) * hw].astype(o_ref.dtype)
        o_ref[:, (2 * hd + 1) * hw:(2 * hd + 2) * hw] = ones


def _mm_vones(h, w, col0, tm=512):
    s, k = h.shape
    tn = ATTN_WIDTH
    jb = col0 // tn
    return pl.pallas_call(
        _mm_vones_body,
        grid=(s // tm,),
        in_specs=[pl.BlockSpec((tm, k), lambda i: (i, 0)),
                  pl.BlockSpec((k, tn), lambda i: (0, jb))],
        out_specs=pl.BlockSpec((tm, 2 * tn), lambda i: (i, 0)),
        out_shape=jax.ShapeDtypeStruct((s, 2 * tn), BF16),
        scratch_shapes=[pltpu.VMEM((k, tn), BF16)],
        compiler_params=_cparams(("arbitrary",)),
        name="mm_vones",
    )(h, w)


def _mm_rope_body(h_ref, w_ref, pos_ref, invf_ref, o_ref, wb_ref, *, tn):
    j = pl.program_id(0)
    acc = _proj(h_ref, w_ref, wb_ref, 1)
    scale = jnp.where(j == 0, HEAD_DIM ** -0.5, 1.0).astype(F32)
    ang = pos_ref[...].astype(F32) * invf_ref[...]
    lane = lax.broadcasted_iota(jnp.int32, ang.shape, 1)
    low = (lane % HEAD_DIM) < (HEAD_DIM // 2)
    cos = jnp.cos(ang) * scale
    sin = jnp.where(low, -jnp.sin(ang), jnp.sin(ang)) * scale
    for c in range(tn // LANES):
        t = acc[:, c * LANES:(c + 1) * LANES]
        partner = jnp.where(low, pltpu.roll(t, LANES - HEAD_DIM // 2, 1),
                            pltpu.roll(t, HEAD_DIM // 2, 1))
        o_ref[:, c * LANES:(c + 1) * LANES] = (t * cos + partner * sin).astype(o_ref.dtype)


def _mm_rope(h, w, positions, col0, tm=512, tn=1024):
    s, k = h.shape
    jb = col0 // tn
    inv_freq = 1.0 / (ROPE_THETA ** (jnp.arange(0, HEAD_DIM, 2, dtype=F32) / HEAD_DIM))
    invf = jnp.tile(inv_freq, LANES // (HEAD_DIM // 2)).reshape(1, LANES)
    return pl.pallas_call(
        functools.partial(_mm_rope_body, tn=tn),
        grid=(2, s // tm),
        in_specs=[pl.BlockSpec((tm, k), lambda j, i: (i, 0)),
                  pl.BlockSpec((k, tn), lambda j, i: (0, j + jb)),
                  pl.BlockSpec((tm, 1), lambda j, i: (i, 0)),
                  pl.BlockSpec((1, LANES), lambda j, i: (0, 0))],
        out_specs=pl.BlockSpec((tm, tn), lambda j, i: (i, j)),
        out_shape=jax.ShapeDtypeStruct((s, 2 * tn), BF16),
        scratch_shapes=[pltpu.VMEM((k, tn), BF16)],
        compiler_params=_cparams(("arbitrary", "arbitrary")),
        name="mm_rope",
    )(h, w, positions.reshape(s, 1), invf)


def _s5_tables(lam_re, lam_im, log_dt, b_re, b_im, c_re, c_im):
    L, G, P, M = SSM_CHUNK, SSM_GROUPS, SSM_STATE, SSM_GROUP
    hi = lax.Precision.HIGHEST
    lr, li = lam_re.astype(F32), lam_im.astype(F32)
    dt = jnp.exp(log_dt.astype(F32))[:, None]
    n = jnp.arange(L + 1, dtype=F32)[:, None, None]
    mag = jnp.exp(lr * dt * n)
    pr = mag * jnp.cos(li * dt * n)
    pi = mag * jnp.sin(li * dt * n)
    nr, ni = pr[1] - 1.0, pi[1]
    den = lr * lr + li * li
    f_re = (nr * lr + ni * li) / den
    f_im = (ni * lr - nr * li) / den
    br, bi = b_re.astype(F32), b_im.astype(F32)
    bbr = f_re[..., None] * br - f_im[..., None] * bi
    bbi = f_re[..., None] * bi + f_im[..., None] * br
    cr, ci = c_re.astype(F32), c_im.astype(F32)

    nb, gb = G // SSM_GB, SSM_GB
    pr_g = jnp.transpose(pr, (1, 0, 2))
    pi_g = jnp.transpose(pi, (1, 0, 2))
    bbr_t = jnp.transpose(bbr, (0, 2, 1))
    bbi_t = jnp.transpose(bbi, (0, 2, 1))

    abr = pr_g[:, :L, None, :] * bbr_t[:, None] - pi_g[:, :L, None, :] * bbi_t[:, None]
    abi = pr_g[:, :L, None, :] * bbi_t[:, None] + pi_g[:, :L, None, :] * bbr_t[:, None]
    kmat = (jnp.einsum('gnip,gop->gnio', abr, cr, precision=hi)
            - jnp.einsum('gnip,gop->gnio', abi, ci, precision=hi))
    kpad = jnp.pad(kmat, ((0, 0), (1, 0), (0, 0), (0, 0)))
    kq = jnp.stack([jnp.stack([kpad[:, 1 + c2 - r2::2][:, :L // 2] for c2 in range(2)], axis=3)
                    for r2 in range(2)], axis=2)
    kk_src = jnp.transpose(kq.reshape(nb, gb, L // 2, 2, M, 2, M),
                           (0, 2, 3, 5, 4, 1, 6)).reshape(nb, L // 2, 2, 2, M, gb * M)

    ab1r = pr_g[:, 1, None, :] * bbr_t - pi_g[:, 1, None, :] * bbi_t
    ab1i = pr_g[:, 1, None, :] * bbi_t + pi_g[:, 1, None, :] * bbr_t
    b4 = jnp.stack([jnp.stack([ab1r, ab1i], axis=2), jnp.stack([bbr_t, bbi_t], axis=2)],
                   axis=1)
    bb_src = jnp.transpose(b4.reshape(nb, gb, 2, M, 2, P),
                           (0, 2, 4, 3, 1, 5)).reshape(nb, 2, 2, M, gb * P)

    cr_t = jnp.transpose(cr, (0, 2, 1))
    ci_t = jnp.transpose(ci, (0, 2, 1))
    car = cr_t * pr_g[:, 1, :, None] - ci_t * pi_g[:, 1, :, None]
    cai = cr_t * pi_g[:, 1, :, None] + ci_t * pr_g[:, 1, :, None]
    c4 = jnp.stack([jnp.stack([cr_t, car], axis=2), jnp.stack([-ci_t, -cai], axis=2)],
                   axis=1)
    cc_src = jnp.transpose(c4.reshape(nb, gb, 2, P, 2, M),
                           (0, 2, 4, 3, 1, 5)).reshape(nb, 2, 2, P, gb * M)

    ap = jnp.stack([pr.reshape(L + 1, nb, gb * P), pi.reshape(L + 1, nb, gb * P)], axis=2)
    return kk_src, bb_src, cc_src, ap.reshape(L + 1, 2 * G * P)


def _cmul(ar, ai, zr, zi):
    return ar * zr - ai * zi, ar * zi + ai * zr


def _block_diag(src, rows_per_group, cols_per_group):
    t = jnp.concatenate([src] * SSM_GB, axis=0)
    row = lax.broadcasted_iota(jnp.int32, t.shape, 0)
    col = lax.broadcasted_iota(jnp.int32, t.shape, 1)
    return jnp.where(row // rows_per_group == col // cols_per_group, t, 0.0).astype(BF16)


def _s5_body(u_ref, kk_src, bb_src, cc_src, ap_ref, d_ref, y_ref,
             kk_ref, bb_ref, cc_ref, lhs_ref, er_ref, ei_ref, xr_ref, xi_ref):
    L, M, P = SSM_CHUNK, SSM_GROUP, SSM_STATE
    nc = u_ref.shape[0] // L
    hp = SSM_GB * P
    gm = SSM_GB * M
    nq = L // 2
    for a in range(2):
        for b in range(2):
            for dd in range(nq):
                kk_ref[dd, a * gm:(a + 1) * gm, b * gm:(b + 1) * gm] = _block_diag(
                    kk_src[0, dd, a, b], M, M)
            bb_ref[a * gm:(a + 1) * gm, b * hp:(b + 1) * hp] = _block_diag(bb_src[0, a, b], M, P)
            cc_ref[a * hp:(a + 1) * hp, b * gm:(b + 1) * gm] = _block_diag(cc_src[0, a, b], P, M)
    for q in range(nq):
        lhs_ref[q] = jnp.concatenate(
            [u_ref[pl.ds(2 * q, nc, stride=L), :], u_ref[pl.ds(2 * q + 1, nc, stride=L), :]],
            axis=1).astype(BF16)

    er = jnp.zeros((nc, hp), F32)
    ei = jnp.zeros((nc, hp), F32)
    for q in range(nq):
        z = jnp.dot(lhs_ref[q], bb_ref[...], preferred_element_type=F32)
        n = L - 2 - 2 * q
        dr, di = _cmul(ap_ref[n:n + 1, :hp], ap_ref[n:n + 1, hp:], z[:, :hp], z[:, hp:])
        er = er + dr
        ei = ei + di
    er_ref[...] = er
    ei_ref[...] = ei

    ar = ap_ref[L:L + 1, :hp]
    ai = ap_ref[L:L + 1, hp:]

    def step(c, carry):
        xr, xi = carry
        xr_ref[pl.ds(c, 1), :] = xr
        xi_ref[pl.ds(c, 1), :] = xi
        nr, ni = _cmul(ar, ai, xr, xi)
        return nr + er_ref[pl.ds(c, 1), :], ni + ei_ref[pl.ds(c, 1), :]

    zero = jnp.zeros((1, hp), F32)
    lax.fori_loop(0, nc, step, (zero, zero))

    for qq in range(nq):
        n = 2 * qq + 1
        wr, wi = _cmul(ap_ref[n:n + 1, :hp], ap_ref[n:n + 1, hp:], xr_ref[...], xi_ref[...])
        w = jnp.concatenate([wr, wi], axis=1).astype(BF16)
        acc = jnp.dot(w, cc_ref[...], preferred_element_type=F32)
        for q in range(qq + 1):
            acc = acc + jnp.dot(lhs_ref[q], kk_ref[qq - q], preferred_element_type=F32)
        for r in range(2):
            j = 2 * qq + r
            y = acc[:, r * LANES:(r + 1) * LANES] + d_ref[...] * u_ref[pl.ds(j, nc, stride=L), :]
            y_ref[pl.ds(j, nc, stride=L), :] = jax.nn.gelu(y)


def _s5_core(u, tables, d_skip):
    s, width = u.shape
    L, P, M, gb = SSM_CHUNK, SSM_STATE, SSM_GROUP, SSM_GB
    nc = s // L
    nb = width // (gb * M)
    kk_src, bb_src, cc_src, ap = tables
    return pl.pallas_call(
        _s5_body,
        grid=(nb,),
        in_specs=[pl.BlockSpec((s, gb * M), lambda b: (0, b)),
                  pl.BlockSpec((1, L // 2, 2, 2, M, gb * M), lambda b: (b, 0, 0, 0, 0, 0)),
                  pl.BlockSpec((1, 2, 2, M, gb * P), lambda b: (b, 0, 0, 0, 0)),
                  pl.BlockSpec((1, 2, 2, P, gb * M), lambda b: (b, 0, 0, 0, 0)),
                  pl.BlockSpec((L + 1, 2 * gb * P), lambda b: (0, b)),
                  pl.BlockSpec((1, gb * M), lambda b: (0, b))],
        out_specs=pl.BlockSpec((s, gb * M), lambda b: (0, b)),
        out_shape=jax.ShapeDtypeStruct((s, width), F32),
        scratch_shapes=[pltpu.VMEM((L // 2, 2 * gb * M, 2 * gb * M), BF16),
                        pltpu.VMEM((2 * gb * M, 2 * gb * P), BF16),
                        pltpu.VMEM((2 * gb * P, 2 * gb * M), BF16),
                        pltpu.VMEM((L // 2, nc, 2 * gb * M), BF16)]
        + [pltpu.VMEM((nc, gb * P), F32) for _ in range(4)],
        compiler_params=_cparams(("parallel",)),
        name="s5_scan",
    )(u, kk_src, bb_src, cc_src, ap, d_skip.astype(F32).reshape(1, width))


def _glu_body(y_ref, w_ref, b_ref, o_ref):
    y = y_ref[...]
    z = jnp.dot(y.astype(BF16), w_ref[...], preferred_element_type=F32) + b_ref[...]
    o_ref[...] = (y * jax.nn.sigmoid(z)).astype(o_ref.dtype)


def _glu(y, w, b, tm=512):
    s, n = y.shape
    return pl.pallas_call(
        _glu_body,
        grid=(s // tm,),
        in_specs=[pl.BlockSpec((tm, n), lambda i: (i, 0)),
                  pl.BlockSpec((n, n), lambda i: (0, 0)),
                  pl.BlockSpec((1, n), lambda i: (0, 0))],
        out_specs=pl.BlockSpec((tm, n), lambda i: (i, 0)),
        out_shape=jax.ShapeDtypeStruct((s, n), BF16),
        compiler_params=_cparams(("parallel",)),
        name="s5_glu",
    )(y, w, b.reshape(1, n).astype(F32))


def _attn_body(q_ref, k_ref, v_ref, lam_ref, sw_ref, o_ref,
               qs_ref, m_ref, acc_ref, s_ref, p_ref, al_ref, *, tq):
    qi = pl.program_id(1)
    hw = 2 * HEAD_DIM
    q = q_ref[...]
    lane = lax.broadcasted_iota(jnp.int32, q.shape, 1)
    zero = jnp.zeros_like(q)
    qs_ref[0] = jnp.where(lane < HEAD_DIM, q, zero)
    qs_ref[1] = jnp.where(lane >= HEAD_DIM, q, zero)
    m_ref[...] = jnp.full(m_ref.shape, -jnp.inf, F32)
    acc_ref[...] = jnp.zeros(acc_ref.shape, F32)

    def scores(c, slot, masked=False):
        off = pl.multiple_of(c * tq, tq)
        k = k_ref[pl.ds(off, tq), :]
        for mp in range(2):
            s = lax.dot_general(qs_ref[mp], k, (((1,), (1,)), ((), ())),
                                preferred_element_type=F32)
            if masked:
                row = lax.broadcasted_iota(jnp.int32, s.shape, 0)
                col = lax.broadcasted_iota(jnp.int32, s.shape, 1)
                s = jnp.where(col <= row, s, -jnp.inf)
            s_ref[slot, mp] = s

    def softmax(slot):
        for mp in range(2):
            s = s_ref[slot, mp]
            m_old = m_ref[mp]
            m_new = jnp.maximum(m_old, jnp.max(s, axis=-1, keepdims=True))
            al_ref[slot, mp] = jnp.exp(m_old - m_new)
            p_ref[slot, mp] = jnp.exp(s - m_new).astype(BF16)
            m_ref[mp] = m_new

    def pv(c, slot):
        off = pl.multiple_of(c * tq, tq)
        v = v_ref[pl.ds(off, tq), :]
        for mp in range(2):
            acc_ref[mp] = al_ref[slot, mp] * acc_ref[mp] + jnp.dot(
                p_ref[slot, mp], v, preferred_element_type=F32)

    @pl.when(qi == 0)
    def _one_chunk():
        scores(0, 0, masked=True)
        softmax(0)
        pv(0, 0)

    @pl.when(qi == 1)
    def _two_chunks():
        scores(0, 1)
        softmax(1)
        pv(0, 1)
        scores(1, 0, masked=True)
        softmax(0)
        pv(1, 0)

    @pl.when(qi >= 2)
    def _pipelined():
        odd = qi % 2

        @pl.when(odd == 0)
        def _fill_even():
            scores(0, 0)
            scores(1, 1)
            softmax(0)

        @pl.when(odd == 1)
        def _fill_odd():
            scores(0, 1)
            scores(1, 0)
            softmax(1)
            scores(2, 1)
            softmax(0)
            pv(0, 1)

        t0 = 2 + odd

        def two_steps(u, carry):
            t = t0 + 2 * u
            scores(t, 0)
            softmax(1)
            pv(t - 2, 0)
            scores(t + 1, 1)
            softmax(0)
            pv(t - 1, 1)
            return carry

        lax.fori_loop(0, (qi - t0) // 2, two_steps, 0)
        scores(qi, 0, masked=True)
        softmax(1)
        pv(qi - 2, 0)
        softmax(0)
        pv(qi - 1, 1)
        pv(qi, 0)

    lam = (jnp.exp(jnp.sum(lam_ref[0:1, :] * lam_ref[1:2, :]))
           - jnp.exp(jnp.sum(lam_ref[2:3, :] * lam_ref[3:4, :])) + LAMBDA_INIT)
    a1 = acc_ref[0]
    a2 = acc_ref[1]
    o = a1[:, :hw] / a1[:, hw:] - lam * (a2[:, :hw] / a2[:, hw:])
    o = o * lax.rsqrt(jnp.mean(o * o, axis=-1, keepdims=True) + SUBLN_EPS)
    o_ref[...] = (o * sw_ref[...] * (1.0 - LAMBDA_INIT)).astype(o_ref.dtype)


def _diff_attention(qk, v1, lam_params, subln_w, tq=512):
    s = qk.shape[0]
    nq = s // tq
    hw = 2 * HEAD_DIM
    kcol0 = ATTN_WIDTH // hw
    return pl.pallas_call(
        functools.partial(_attn_body, tq=tq),
        grid=(ATTN_HEADS, nq),
        in_specs=[pl.BlockSpec((tq, hw), lambda h, i: (i, h)),
                  pl.BlockSpec((s, hw), lambda h, i: (0, kcol0 + h)),
                  pl.BlockSpec((s, 2 * hw), lambda h, i: (0, h)),
                  pl.BlockSpec((4, HEAD_DIM), lambda h, i: (0, 0)),
                  pl.BlockSpec((1, hw), lambda h, i: (0, 0))],
        out_specs=pl.BlockSpec((tq, hw), lambda h, i: (i, h)),
        out_shape=jax.ShapeDtypeStruct((s, ATTN_WIDTH), BF16),
        scratch_shapes=[pltpu.VMEM((2, tq, hw), BF16),
                        pltpu.VMEM((2, tq, 1), F32),
                        pltpu.VMEM((2, tq, 2 * hw), F32),
                        pltpu.VMEM((2, 2, tq, tq), F32),
                        pltpu.VMEM((2, 2, tq, tq), BF16),
                        pltpu.VMEM((2, 2, tq, 1), F32)],
        compiler_params=_cparams(("parallel", "arbitrary")),
        name="diff_attn",
    )(qk, qk, v1, lam_params, subln_w.reshape(1, hw).astype(F32))


def _merge_body(x_ref, ys_ref, ya_ref, gs_ref, ga_ref, wps_ref, wpa_ref, wo_ref, n2_ref, wr_ref,
                x1_ref, h2_ref, eid_ref, gate_ref, cnt_ref):
    ps = jnp.dot(ys_ref[...], wps_ref[...], preferred_element_type=F32)
    pa = jnp.dot(ya_ref[...], wpa_ref[...], preferred_element_type=F32)
    merged = gs_ref[...].astype(F32) * ps + ga_ref[...].astype(F32) * pa
    x1 = x_ref[...] + jnp.dot(merged.astype(BF16), wo_ref[...], preferred_element_type=F32)
    x1_ref[...] = x1
    h2 = x1 * lax.rsqrt(jnp.mean(x1 * x1, axis=-1, keepdims=True) + NORM_EPS) * n2_ref[...]
    h2_ref[...] = h2
    wr = wr_ref[...]
    h_hi = h2.astype(BF16)
    h_lo = (h2 - h_hi.astype(F32)).astype(BF16)
    w_hi = wr.astype(BF16)
    w_lo = (wr - w_hi.astype(F32)).astype(BF16)
    logits = (jnp.dot(h_hi, w_hi, preferred_element_type=F32)
              + jnp.dot(h_lo, w_hi, preferred_element_type=F32)
              + jnp.dot(h_hi, w_lo, preferred_element_type=F32))
    lane = lax.broadcasted_iota(jnp.int32, logits.shape, 1)
    big = jnp.int32(1 << 20)
    ninf = jnp.float32(-jnp.inf)
    is_g = lane < N_GROUPS
    gl = jnp.where(is_g, logits, ninf)
    gm = jnp.max(gl, axis=-1, keepdims=True)
    g_idx = jnp.min(jnp.where(gl == gm, lane, big), axis=-1, keepdims=True)
    g_val = 1.0 / jnp.sum(jnp.where(is_g, jnp.exp(gl - gm), 0.0), axis=-1, keepdims=True)
    lo = N_GROUPS + g_idx * EXPERTS_PER_GROUP
    in_grp = (lane >= lo) & (lane < lo + EXPERTS_PER_GROUP)
    el = jnp.where(in_grp, logits, ninf)
    e1 = jnp.max(el, axis=-1, keepdims=True)
    i1 = jnp.min(jnp.where(el == e1, lane, big), axis=-1, keepdims=True)
    el2 = jnp.where(lane == i1, ninf, el)
    e2 = jnp.max(el2, axis=-1, keepdims=True)
    i2 = jnp.min(jnp.where(el2 == e2, lane, big), axis=-1, keepdims=True)
    t = jnp.exp(e2 - e1)
    p1 = 1.0 / (1.0 + t)
    p2 = t / (1.0 + t)
    gate_ref[...] = jnp.where(lane == 0, p1 * g_val, jnp.where(lane == 1, p2 * g_val, 0.0))

    @pl.when(pl.program_id(0) == 0)
    def _zero_counts():
        cnt_ref[...] = jnp.zeros(cnt_ref.shape, F32)

    ex1 = i1 - N_GROUPS
    ex2 = i2 - N_GROUPS
    oh1 = lane == ex1
    oh2 = lane == ex2
    onehot = jnp.where(oh1 | oh2, 1.0, 0.0)
    tm = logits.shape[0]
    r_i = lax.broadcasted_iota(jnp.int32, (tm, tm), 0)
    c_i = lax.broadcasted_iota(jnp.int32, (tm, tm), 1)
    tri = jnp.where(c_i < r_i, 1.0, 0.0).astype(BF16)
    before = jnp.dot(tri, onehot.astype(BF16), preferred_element_type=F32) + cnt_ref[...]
    rank1 = jnp.sum(jnp.where(oh1, before, 0.0), axis=-1, keepdims=True).astype(jnp.int32)
    rank2 = jnp.sum(jnp.where(oh2, before, 0.0), axis=-1, keepdims=True).astype(jnp.int32)
    cnt_ref[...] = cnt_ref[...] + jnp.sum(onehot, axis=0, keepdims=True)
    eid_ref[...] = jnp.where(lane == 0, ex1, jnp.where(lane == 1, ex2,
                             jnp.where(lane == 2, rank1, jnp.where(lane == 3, rank2, 0))))


def _merge_route(x, y_ssm, y_attn, gates, wps, wpa, wo, n2w, wr, tm=256):
    s, d = x.shape
    nw = y_ssm.shape[1]
    const = lambda i: (0, 0)
    return pl.pallas_call(
        _merge_body,
        grid=(s // tm,),
        in_specs=[pl.BlockSpec((tm, d), lambda i: (i, 0)),
                  pl.BlockSpec((tm, nw), lambda i: (i, 0)),
                  pl.BlockSpec((tm, nw), lambda i: (i, 0)),
                  pl.BlockSpec((tm, d), lambda i: (i, 0)),
                  pl.BlockSpec((tm, d), lambda i: (i, 1)),
                  pl.BlockSpec((nw, d), const),
                  pl.BlockSpec((nw, d), const),
                  pl.BlockSpec((d, d), const),
                  pl.BlockSpec((1, d), const),
                  pl.BlockSpec((d, LANES), const)],
        out_specs=[pl.BlockSpec((tm, d), lambda i: (i, 0)),
                   pl.BlockSpec((tm, d), lambda i: (i, 0)),
                   pl.BlockSpec((tm, LANES), lambda i: (i, 0)),
                   pl.BlockSpec((tm, LANES), lambda i: (i, 0)),
                   pl.BlockSpec((1, LANES), const)],
        out_shape=[jax.ShapeDtypeStruct((s, d), F32),
                   jax.ShapeDtypeStruct((s, d), F32),
                   jax.ShapeDtypeStruct((s, LANES), jnp.int32),
                   jax.ShapeDtypeStruct((s, LANES), F32),
                   jax.ShapeDtypeStruct((1, LANES), F32)],
        compiler_params=_cparams(("arbitrary",)),
        name="merge_route",
    )(x, y_ssm, y_attn, gates, gates, wps, wpa, wo, n2w.reshape(1, d).astype(F32), wr)


def _row_copy(src_hbm, row, dst_ref, r, sem):
    return pltpu.make_async_copy(src_hbm.at[pl.ds(row, 1), :], dst_ref.at[pl.ds(r, 1), :], sem)


ROW_DMA_PRIORITY = 1


def _gather_rows(src_hbm, idx_ref, base, stride, dst_ref, sem, n, priority=ROW_DMA_PRIORITY):
    def start(r, c):
        _row_copy(src_hbm, idx_ref[base + stride * r], dst_ref, r, sem).start(priority=priority)
        return c

    lax.fori_loop(0, n, start, 0, unroll=8)


def _wait_rows(src_hbm, dst_ref, sem, n):
    pltpu.make_async_copy(src_hbm.at[pl.ds(0, n), :], dst_ref, sem).wait()


def _scatter_tok_body(dest_ref, tok_ref):
    def zero(i, c):
        tok_ref[i] = 0
        return c

    lax.fori_loop(0, tok_ref.shape[0], zero, 0, unroll=8)

    def put(i, c):
        tok_ref[dest_ref[i]] = lax.shift_right_logical(i, TOP_K_LOG2)
        return c

    lax.fori_loop(0, dest_ref.shape[0], put, 0, unroll=8)


def _scatter_tok(dest, buf_len):
    return pl.pallas_call(
        _scatter_tok_body,
        in_specs=[pl.BlockSpec(memory_space=pltpu.SMEM)],
        out_specs=pl.BlockSpec(memory_space=pltpu.SMEM),
        out_shape=jax.ShapeDtypeStruct((buf_len,), jnp.int32),
        name="scatter_tok",
    )(dest)


def _block_out_copy(ob_ref, slot, ys_hbm, blk, sem):
    return pltpu.make_async_copy(ob_ref.at[slot], ys_hbm.at[pl.ds(blk * MOE_BLK, MOE_BLK), :],
                                 sem.at[slot])


def _expert_body(fb_ref, nblk_ref, nb_ref, tok_ref, h2_hbm, wg_ref, wu_ref, wd_ref, ys_hbm,
                 xb_ref, ob_ref, wgb_ref, wub_ref, wdb_ref, gsem, osem, *, n_blocks):
    e = pl.program_id(0)
    nb = nb_ref[0]
    first = fb_ref[e]
    count = nblk_ref[e]

    @pl.when(count > 0)
    def _expert():
        wgb_ref[...] = wg_ref[0].astype(BF16)
        wub_ref[...] = wu_ref[0].astype(BF16)
        wdb_ref[...] = wd_ref[0].astype(BF16)

        @pl.when(first == 0)
        def _first_rows():
            _gather_rows(h2_hbm, tok_ref, 0, 1, xb_ref.at[0], gsem.at[0], MOE_BLK)

        def block(t, carry):
            g = first + t
            cur = g % 2
            nxt = 1 - cur
            _wait_rows(h2_hbm, xb_ref.at[cur], gsem.at[cur], MOE_BLK)

            @pl.when(g >= 2)
            def _staging_free():
                _block_out_copy(ob_ref, cur, ys_hbm, g - 2, osem).wait()

            base = (g + 1) * MOE_BLK
            for r in range(MOE_BLK):
                _row_copy(h2_hbm, tok_ref[base + r], xb_ref.at[nxt], r, gsem.at[nxt]).start(
                    priority=ROW_DMA_PRIORITY)
            xb = xb_ref[cur].astype(BF16)
            hg = jnp.dot(xb, wgb_ref[...], preferred_element_type=F32)
            hu = jnp.dot(xb, wub_ref[...], preferred_element_type=F32)
            act = (jax.nn.silu(hg) * hu).astype(BF16)
            ob_ref[cur] = jnp.dot(act, wdb_ref[...], preferred_element_type=F32)
            _block_out_copy(ob_ref, cur, ys_hbm, g, osem).start()
            return carry

        lax.fori_loop(0, count, block, 0)

    @pl.when(e == pl.num_programs(0) - 1)
    def _finish():
        _wait_rows(h2_hbm, xb_ref.at[nb % 2], gsem.at[nb % 2], MOE_BLK)
        _block_out_copy(ob_ref, (nb - 1) % 2, ys_hbm, nb - 1, osem).wait()

        @pl.when(nb >= 2)
        def _():
            _block_out_copy(ob_ref, nb % 2, ys_hbm, nb - 2, osem).wait()

        ob_ref[0] = jnp.zeros(ob_ref.shape[1:], ob_ref.dtype)

        def zero_block(g, carry):
            _block_out_copy(ob_ref, 0, ys_hbm, g, osem).start()
            return carry

        lax.fori_loop(nb, n_blocks, zero_block, 0)

        def zero_wait(g, carry):
            _block_out_copy(ob_ref, 0, ys_hbm, g, osem).wait()
            return carry

        lax.fori_loop(nb, n_blocks, zero_wait, 0)


def _experts(h2, first_blk, n_blk, n_used, buf_tok, wg, wu, wd):
    s, d = h2.shape
    buf_len = buf_tok.shape[0]
    n_blocks = buf_len // MOE_BLK
    wmap = lambda e, fb, nk, nb, tok: (e, 0, 0)
    return pl.pallas_call(
        functools.partial(_expert_body, n_blocks=n_blocks),
        grid_spec=pltpu.PrefetchScalarGridSpec(
            num_scalar_prefetch=4,
            grid=(N_EXPERTS,),
            in_specs=[pl.BlockSpec(memory_space=pl.ANY),
                      pl.BlockSpec((1, d, D_FF), wmap),
                      pl.BlockSpec((1, d, D_FF), wmap),
                      pl.BlockSpec((1, D_FF, d), wmap)],
            out_specs=pl.BlockSpec(memory_space=pl.ANY),
            scratch_shapes=[pltpu.VMEM((2, MOE_BLK, d), F32), pltpu.VMEM((2, MOE_BLK, d), F32),
                            pltpu.VMEM((d, D_FF), BF16), pltpu.VMEM((d, D_FF), BF16),
                            pltpu.VMEM((D_FF, d), BF16),
                            pltpu.SemaphoreType.DMA((2,)), pltpu.SemaphoreType.DMA((2,))]),
        out_shape=jax.ShapeDtypeStruct((buf_len, d), F32),
        compiler_params=_cparams(("arbitrary",)),
        name="experts",
    )(first_blk, n_blk, n_used, buf_tok, h2, wg, wu, wd)


def _combine_body(pos_ref, x1_ref, gate_ref, fw_ref, ys_hbm, o_ref, g_ref, sem, *, tm):
    i = pl.program_id(0)

    def gather(tile, slot):
        for k in range(TOP_K):
            _gather_rows(ys_hbm, pos_ref, TOP_K * tile * tm + k, TOP_K, g_ref.at[slot, k],
                         sem.at[slot], tm, priority=k % 2)

    @pl.when(i == 0)
    def _first():
        gather(0, 0)

    @pl.when(i + 1 < pl.num_programs(0))
    def _prefetch():
        gather(i + 1, (i + 1) % 2)

    cur = i % 2
    for k in range(TOP_K):
        _wait_rows(ys_hbm, g_ref.at[cur, k], sem.at[cur], tm)
    gate = gate_ref[...]
    x = x1_ref[...] + (gate[:, 0:1] * g_ref[cur, 0] + gate[:, 1:2] * g_ref[cur, 1])
    y = x * lax.rsqrt(jnp.mean(x * x, axis=-1, keepdims=True) + NORM_EPS)
    o_ref[...] = y * fw_ref[...]


def _combine(x1, ys, pos, gate, fw, tm=256):
    s, d = x1.shape
    return pl.pallas_call(
        functools.partial(_combine_body, tm=tm),
        grid_spec=pltpu.PrefetchScalarGridSpec(
            num_scalar_prefetch=1,
            grid=(s // tm,),
            in_specs=[pl.BlockSpec((tm, d), lambda i, p: (i, 0)),
                      pl.BlockSpec((tm, LANES), lambda i, p: (i, 0)),
                      pl.BlockSpec((1, d), lambda i, p: (0, 0)),
                      pl.BlockSpec(memory_space=pl.ANY)],
            out_specs=pl.BlockSpec((tm, d), lambda i, p: (i, 0)),
            scratch_shapes=[pltpu.VMEM((2, TOP_K, tm, d), F32),
                            pltpu.SemaphoreType.DMA((2,))]),
        out_shape=jax.ShapeDtypeStruct((s, d), F32),
        compiler_params=_cparams(("arbitrary",)),
        name="combine_norm",
    )(pos, x1, gate, fw.reshape(1, d).astype(F32), ys)


def _dispatch_tables(eid, counts):
    n_tok = eid.shape[0]
    n_slots = n_tok * TOP_K
    buf_len = n_slots + N_EXPERTS * MOE_BLK
    counts = counts.astype(jnp.int32)
    padded = ((counts + MOE_BLK - 1) // MOE_BLK) * MOE_BLK
    padded_ends = jnp.cumsum(padded)
    padded_starts = padded_ends - padded
    experts = eid[:, :TOP_K]
    onehot = experts[:, :, None] == jnp.arange(N_EXPERTS, dtype=jnp.int32)
    dest = jnp.sum(jnp.where(onehot, padded_starts, 0), axis=-1) + eid[:, TOP_K:2 * TOP_K]
    n_used = (padded_ends[-1] // MOE_BLK).astype(jnp.int32).reshape(1)
    first_blk = (padded_starts // MOE_BLK).astype(jnp.int32)
    n_blk = (padded // MOE_BLK).astype(jnp.int32)
    return first_blk, n_blk, n_used, dest.reshape(n_slots).astype(jnp.int32), buf_len


def kernel(x, positions, norm1_w, w_in, ssm_lambda_re, ssm_lambda_im, ssm_log_dt, ssm_b_re, ssm_b_im, ssm_c_re, ssm_c_im, ssm_d, ssm_glu_w, ssm_glu_b, attn_lambda_q1, attn_lambda_k1, attn_lambda_q2, attn_lambda_k2, attn_subln_w, w_proj_ssm, w_proj_attn, w_out, norm2_w, router_group_w, router_expert_w, expert_w_gate, expert_w_up, expert_w_down, final_norm_w):
    bsz, seq, d = x.shape
    depth = norm1_w.shape[0]
    xs = x.reshape(bsz * seq, d)
    pos = positions.reshape(bsz * seq)
    o_q = SSM_WIDTH
    o_v = o_q + 2 * ATTN_WIDTH
    o_g = o_v + ATTN_WIDTH
    for l in range(depth):
        h = _rmsnorm(xs, norm1_w[l].astype(F32), BF16)
        w_in_b = w_in[l].astype(F32)
        u = _mm(h, w_in_b, 0, SSM_WIDTH, "none", F32)
        qk = _mm_rope(h, w_in_b, pos, o_q)
        v = _mm_vones(h, w_in_b, o_v)
        gates = _mm(h, w_in_b, o_g, 2 * D_MODEL, "sigmoid", F32)

        tables = _s5_tables(ssm_lambda_re[l], ssm_lambda_im[l], ssm_log_dt[l],
                            ssm_b_re[l], ssm_b_im[l], ssm_c_re[l], ssm_c_im[l])
        y = _s5_core(u, tables, ssm_d[l])
        y_ssm = _glu(y, ssm_glu_w[l].astype(BF16), ssm_glu_b[l])

        lam_params = jnp.stack([attn_lambda_q1[l], attn_lambda_k1[l],
                                attn_lambda_q2[l], attn_lambda_k2[l]]).astype(F32)
        y_attn = _diff_attention(qk, v, lam_params, attn_subln_w[l])

        wr = jnp.concatenate([router_group_w[l], router_expert_w[l]], axis=1).astype(F32)
        wr = jnp.pad(wr, ((0, 0), (0, LANES - wr.shape[1])))
        x1, h2, eid, gate, counts = _merge_route(
            xs, y_ssm, y_attn, gates, w_proj_ssm[l].astype(BF16), w_proj_attn[l].astype(BF16),
            w_out[l].astype(BF16), norm2_w[l], wr)

        first_blk, n_blk, n_used, dest, buf_len = _dispatch_tables(eid, counts[0, :N_EXPERTS])
        buf_tok = _scatter_tok(dest, buf_len)
        ys = _experts(h2, first_blk, n_blk, n_used, buf_tok, expert_w_gate[l].astype(F32),
                      expert_w_up[l].astype(F32), expert_w_down[l].astype(F32))
        last = l == depth - 1
        assert last, "DEPTH > 1 needs an un-normalised combine"
        xs = _combine(x1, ys, dest, gate, final_norm_w)
    return xs.reshape(bsz, seq, d)
```

```python
import functools
import math

import jax
import jax.numpy as jnp
from jax import lax
from jax.experimental import pallas as pl
from jax.experimental.pallas import tpu as pltpu

F32 = jnp.float32
BF16 = jnp.bfloat16

D_MODEL = 2048
SSM_WIDTH = 1024
SSM_GROUP = 16
SSM_GROUPS = 64
SSM_STATE = 64
ATTN_WIDTH = 1024
ATTN_HEADS = 8
HEAD_DIM = 64
ROPE_THETA = 10000.0
N_GROUPS = 4
EXPERTS_PER_GROUP = 8
N_EXPERTS = 32
TOP_K = 2
TOP_K_LOG2 = 1
D_FF = 512
NORM_EPS = 1e-6
SUBLN_EPS = 1e-5
LAMBDA_INIT = 0.8 - 0.6 * math.exp(-0.3 * 0)

LANES = 128
VMEM_LIMIT = 48 * 1024 * 1024

SSM_CHUNK = 16
SSM_GB = 8
MOE_BLK = 128


def _cparams(sem):
    return pltpu.CompilerParams(dimension_semantics=sem, vmem_limit_bytes=VMEM_LIMIT)


def _rmsnorm_body(x_ref, w_ref, o_ref):
    x = x_ref[...]
    y = x * lax.rsqrt(jnp.mean(x * x, axis=-1, keepdims=True) + NORM_EPS)
    o_ref[...] = (y * w_ref[...]).astype(o_ref.dtype)


def _rmsnorm(x, w, out_dtype, tm=512):
    s, d = x.shape
    return pl.pallas_call(
        _rmsnorm_body,
        grid=(s // tm,),
        in_specs=[pl.BlockSpec((tm, d), lambda i: (i, 0)),
                  pl.BlockSpec((1, d), lambda i: (0, 0))],
        out_specs=pl.BlockSpec((tm, d), lambda i: (i, 0)),
        out_shape=jax.ShapeDtypeStruct((s, d), out_dtype),
        compiler_params=_cparams(("parallel",)),
        name="rmsnorm",
    )(x, w.reshape(1, d))


def _proj(h_ref, w_ref, wb_ref, row_axis):
    @pl.when(pl.program_id(row_axis) == 0)
    def _cast():
        wb_ref[...] = w_ref[...].astype(BF16)

    return jnp.dot(h_ref[...], wb_ref[...], preferred_element_type=F32)


def _mm_body(h_ref, w_ref, o_ref, wb_ref, *, act):
    acc = _proj(h_ref, w_ref, wb_ref, 1)
    if act == "sigmoid":
        acc = jax.nn.sigmoid(acc)
    o_ref[...] = acc.astype(o_ref.dtype)


def _mm(h, w, col0, ncols, act, out_dtype, tm=512, tn=1024):
    s, k = h.shape
    jb = col0 // tn
    return pl.pallas_call(
        functools.partial(_mm_body, act=act),
        grid=(ncols // tn, s // tm),
        in_specs=[pl.BlockSpec((tm, k), lambda j, i: (i, 0)),
                  pl.BlockSpec((k, tn), lambda j, i: (0, j + jb))],
        out_specs=pl.BlockSpec((tm, tn), lambda j, i: (i, j)),
        out_shape=jax.ShapeDtypeStruct((s, ncols), out_dtype),
        scratch_shapes=[pltpu.VMEM((k, tn), BF16)],
        compiler_params=_cparams(("arbitrary", "arbitrary")),
        name="mm_" + act,
    )(h, w)


def _mm_vones_body(h_ref, w_ref, o_ref, wb_ref):
    acc = _proj(h_ref, w_ref, wb_ref, 0)
    hw = 2 * HEAD_DIM
    ones = jnp.ones((acc.shape[0], hw), o_ref.dtype)
    for hd in range(acc.shape[1] // hw):
        o_ref[:, 2 * hd * hw:(2 * hd + 1) * hw] = acc[:, hd * hw:(hd + 1) * hw].astype(o_ref.dtype)
        o_ref[:, (2 * hd + 1) * hw:(2 * hd + 2) * hw] = ones


def _mm_vones(h, w, col0, tm=512):
    s, k = h.shape
    tn = ATTN_WIDTH
    jb = col0 // tn
    return pl.pallas_call(
        _mm_vones_body,
        grid=(s // tm,),
        in_specs=[pl.BlockSpec((tm, k), lambda i: (i, 0)),
                  pl.BlockSpec((k, tn), lambda i: (0, jb))],
        out_specs=pl.BlockSpec((tm, 2 * tn), lambda i: (i, 0)),
        out_shape=jax.ShapeDtypeStruct((s, 2 * tn), BF16),
        scratch_shapes=[pltpu.VMEM((k, tn), BF16)],
        compiler_params=_cparams(("arbitrary",)),
        name="mm_vones",
    )(h, w)


def _mm_rope_body(h_ref, w_ref, pos_ref, invf_ref, o_ref, wb_ref, *, tn):
    j = pl.program_id(0)
    acc = _proj(h_ref, w_ref, wb_ref, 1)
    scale = jnp.where(j == 0, HEAD_DIM ** -0.5, 1.0).astype(F32)
    ang = pos_ref[...].astype(F32) * invf_ref[...]
    lane = lax.broadcasted_iota(jnp.int32, ang.shape, 1)
    low = (lane % HEAD_DIM) < (HEAD_DIM // 2)
    cos = jnp.cos(ang) * scale
    sin = jnp.where(low, -jnp.sin(ang), jnp.sin(ang)) * scale
    for c in range(tn // LANES):
        t = acc[:, c * LANES:(c + 1) * LANES]
        partner = jnp.where(low, pltpu.roll(t, LANES - HEAD_DIM // 2, 1),
                            pltpu.roll(t, HEAD_DIM // 2, 1))
        o_ref[:, c * LANES:(c + 1) * LANES] = (t * cos + partner * sin).astype(o_ref.dtype)


def _mm_rope(h, w, positions, col0, tm=512, tn=1024):
    s, k = h.shape
    jb = col0 // tn
    inv_freq = 1.0 / (ROPE_THETA ** (jnp.arange(0, HEAD_DIM, 2, dtype=F32) / HEAD_DIM))
    invf = jnp.tile(inv_freq, LANES // (HEAD_DIM // 2)).reshape(1, LANES)
    return pl.pallas_call(
        functools.partial(_mm_rope_body, tn=tn),
        grid=(2, s // tm),
        in_specs=[pl.BlockSpec((tm, k), lambda j, i: (i, 0)),
                  pl.BlockSpec((k, tn), lambda j, i: (0, j + jb)),
                  pl.BlockSpec((tm, 1), lambda j, i: (i, 0)),
                  pl.BlockSpec((1, LANES), lambda j, i: (0, 0))],
        out_specs=pl.BlockSpec((tm, tn), lambda j, i: (i, j)),
        out_shape=jax.ShapeDtypeStruct((s, 2 * tn), BF16),
        scratch_shapes=[pltpu.VMEM((k, tn), BF16)],
        compiler_params=_cparams(("arbitrary", "arbitrary")),
        name="mm_rope",
    )(h, w, positions.reshape(s, 1), invf)


def _s5_tables(lam_re, lam_im, log_dt, b_re, b_im, c_re, c_im):
    L, G, P, M = SSM_CHUNK, SSM_GROUPS, SSM_STATE, SSM_GROUP
    hi = lax.Precision.HIGHEST
    lr, li = lam_re.astype(F32), lam_im.astype(F32)
    dt = jnp.exp(log_dt.astype(F32))[:, None]
    n = jnp.arange(L + 1, dtype=F32)[:, None, None]
    mag = jnp.exp(lr * dt * n)
    pr = mag * jnp.cos(li * dt * n)
    pi = mag * jnp.sin(li * dt * n)
    nr, ni = pr[1] - 1.0, pi[1]
    den = lr * lr + li * li
    f_re = (nr * lr + ni * li) / den
    f_im = (ni * lr - nr * li) / den
    br, bi = b_re.astype(F32), b_im.astype(F32)
    bbr = f_re[..., None] * br - f_im[..., None] * bi
    bbi = f_re[..., None] * bi + f_im[..., None] * br
    cr, ci = c_re.astype(F32), c_im.astype(F32)

    nb, gb = G // SSM_GB, SSM_GB
    pr_g = jnp.transpose(pr, (1, 0, 2))
    pi_g = jnp.transpose(pi, (1, 0, 2))
    bbr_t = jnp.transpose(bbr, (0, 2, 1))
    bbi_t = jnp.transpose(bbi, (0, 2, 1))

    abr = pr_g[:, :L, None, :] * bbr_t[:, None] - pi_g[:, :L, None, :] * bbi_t[:, None]
    abi = pr_g[:, :L, None, :] * bbi_t[:, None] + pi_g[:, :L, None, :] * bbr_t[:, None]
    kmat = (jnp.einsum('gnip,gop->gnio', abr, cr, precision=hi)
            - jnp.einsum('gnip,gop->gnio', abi, ci, precision=hi))
    kpad = jnp.pad(kmat, ((0, 0), (1, 0), (0, 0), (0, 0)))
    kq = jnp.stack([jnp.stack([kpad[:, 1 + c2 - r2::2][:, :L // 2] for c2 in range(2)], axis=3)
                    for r2 in range(2)], axis=2)
    kk_src = jnp.transpose(kq.reshape(nb, gb, L // 2, 2, M, 2, M),
                           (0, 2, 3, 5, 4, 1, 6)).reshape(nb, L // 2, 2, 2, M, gb * M)

    ab1r = pr_g[:, 1, None, :] * bbr_t - pi_g[:, 1, None, :] * bbi_t
    ab1i = pr_g[:, 1, None, :] * bbi_t + pi_g[:, 1, None, :] * bbr_t
    b4 = jnp.stack([jnp.stack([ab1r, ab1i], axis=2), jnp.stack([bbr_t, bbi_t], axis=2)],
                   axis=1)
    bb_src = jnp.transpose(b4.reshape(nb, gb, 2, M, 2, P),
                           (0, 2, 4, 3, 1, 5)).reshape(nb, 2, 2, M, gb * P)

    cr_t = jnp.transpose(cr, (0, 2, 1))
    ci_t = jnp.transpose(ci, (0, 2, 1))
    car = cr_t * pr_g[:, 1, :, None] - ci_t * pi_g[:, 1, :, None]
    cai = cr_t * pi_g[:, 1, :, None] + ci_t * pr_g[:, 1, :, None]
    c4 = jnp.stack([jnp.stack([cr_t, car], axis=2), jnp.stack([-ci_t, -cai], axis=2)],
                   axis=1)
    cc_src = jnp.transpose(c4.reshape(nb, gb, 2, P, 2, M),
                           (0, 2, 4, 3, 1, 5)).reshape(nb, 2, 2, P, gb * M)

    ap = jnp.stack([pr.reshape(L + 1, nb, gb * P), pi.reshape(L + 1, nb, gb * P)], axis=2)
    return kk_src, bb_src, cc_src, ap.reshape(L + 1, 2 * G * P)


def _cmul(ar, ai, zr, zi):
    return ar * zr - ai * zi, ar * zi + ai * zr


def _block_diag(src, rows_per_group, cols_per_group):
    t = jnp.concatenate([src] * SSM_GB, axis=0)
    row = lax.broadcasted_iota(jnp.int32, t.shape, 0)
    col = lax.broadcasted_iota(jnp.int32, t.shape, 1)
    return jnp.where(row // rows_per_group == col // cols_per_group, t, 0.0).astype(BF16)


def _s5_body(u_ref, kk_src, bb_src, cc_src, ap_ref, d_ref, y_ref,
             kk_ref, bb_ref, cc_ref, lhs_ref, er_ref, ei_ref, xr_ref, xi_ref):
    L, M, P = SSM_CHUNK, SSM_GROUP, SSM_STATE
    nc = u_ref.shape[0] // L
    hp = SSM_GB * P
    gm = SSM_GB * M
    nq = L // 2
    for a in range(2):
        for b in range(2):
            for dd in range(nq):
                kk_ref[dd, a * gm:(a + 1) * gm, b * gm:(b + 1) * gm] = _block_diag(
                    kk_src[0, dd, a, b], M, M)
            bb_ref[a * gm:(a + 1) * gm, b * hp:(b + 1) * hp] = _block_diag(bb_src[0, a, b], M, P)
            cc_ref[a * hp:(a + 1) * hp, b * gm:(b + 1) * gm] = _block_diag(cc_src[0, a, b], P, M)
    for q in range(nq):
        lhs_ref[q] = jnp.concatenate(
            [u_ref[pl.ds(2 * q, nc, stride=L), :], u_ref[pl.ds(2 * q + 1, nc, stride=L), :]],
            axis=1).astype(BF16)

    er = jnp.zeros((nc, hp), F32)
    ei = jnp.zeros((nc, hp), F32)
    for q in range(nq):
        z = jnp.dot(lhs_ref[q], bb_ref[...], preferred_element_type=F32)
        n = L - 2 - 2 * q
        dr, di = _cmul(ap_ref[n:n + 1, :hp], ap_ref[n:n + 1, hp:], z[:, :hp], z[:, hp:])
        er = er + dr
        ei = ei + di
    er_ref[...] = er
    ei_ref[...] = ei

    ar = ap_ref[L:L + 1, :hp]
    ai = ap_ref[L:L + 1, hp:]

    def step(c, carry):
        xr, xi = carry
        xr_ref[pl.ds(c, 1), :] = xr
        xi_ref[pl.ds(c, 1), :] = xi
        nr, ni = _cmul(ar, ai, xr, xi)
        return nr + er_ref[pl.ds(c, 1), :], ni + ei_ref[pl.ds(c, 1), :]

    zero = jnp.zeros((1, hp), F32)
    lax.fori_loop(0, nc, step, (zero, zero))

    for qq in range(nq):
        n = 2 * qq + 1
        wr, wi = _cmul(ap_ref[n:n + 1, :hp], ap_ref[n:n + 1, hp:], xr_ref[...], xi_ref[...])
        w = jnp.concatenate([wr, wi], axis=1).astype(BF16)
        acc = jnp.dot(w, cc_ref[...], preferred_element_type=F32)
        for q in range(qq + 1):
            acc = acc + jnp.dot(lhs_ref[q], kk_ref[qq - q], preferred_element_type=F32)
        for r in range(2):
            j = 2 * qq + r
            y = acc[:, r * LANES:(r + 1) * LANES] + d_ref[...] * u_ref[pl.ds(j, nc, stride=L), :]
            y_ref[pl.ds(j, nc, stride=L), :] = jax.nn.gelu(y)


def _s5_core(u, tables, d_skip):
    s, width = u.shape
    L, P, M, gb = SSM_CHUNK, SSM_STATE, SSM_GROUP, SSM_GB
    nc = s // L
    nb = width // (gb * M)
    kk_src, bb_src, cc_src, ap = tables
    return pl.pallas_call(
        _s5_body,
        grid=(nb,),
        in_specs=[pl.BlockSpec((s, gb * M), lambda b: (0, b)),
                  pl.BlockSpec((1, L // 2, 2, 2, M, gb * M), lambda b: (b, 0, 0, 0, 0, 0)),
                  pl.BlockSpec((1, 2, 2, M, gb * P), lambda b: (b, 0, 0, 0, 0)),
                  pl.BlockSpec((1, 2, 2, P, gb * M), lambda b: (b, 0, 0, 0, 0)),
                  pl.BlockSpec((L + 1, 2 * gb * P), lambda b: (0, b)),
                  pl.BlockSpec((1, gb * M), lambda b: (0, b))],
        out_specs=pl.BlockSpec((s, gb * M), lambda b: (0, b)),
        out_shape=jax.ShapeDtypeStruct((s, width), F32),
        scratch_shapes=[pltpu.VMEM((L // 2, 2 * gb * M, 2 * gb * M), BF16),
                        pltpu.VMEM((2 * gb * M, 2 * gb * P), BF16),
                        pltpu.VMEM((2 * gb * P, 2 * gb * M), BF16),
                        pltpu.VMEM((L // 2, nc, 2 * gb * M), BF16)]
        + [pltpu.VMEM((nc, gb * P), F32) for _ in range(4)],
        compiler_params=_cparams(("parallel",)),
        name="s5_scan",
    )(u, kk_src, bb_src, cc_src, ap, d_skip.astype(F32).reshape(1, width))


def _glu_body(y_ref, w_ref, b_ref, o_ref):
    y = y_ref[...]
    z = jnp.dot(y.astype(BF16), w_ref[...], preferred_element_type=F32) + b_ref[...]
    o_ref[...] = (y * jax.nn.sigmoid(z)).astype(o_ref.dtype)


def _glu(y, w, b, tm=512):
    s, n = y.shape
    return pl.pallas_call(
        _glu_body,
        grid=(s // tm,),
        in_specs=[pl.BlockSpec((tm, n), lambda i: (i, 0)),
                  pl.BlockSpec((n, n), lambda i: (0, 0)),
                  pl.BlockSpec((1, n), lambda i: (0, 0))],
        out_specs=pl.BlockSpec((tm, n), lambda i: (i, 0)),
        out_shape=jax.ShapeDtypeStruct((s, n), BF16),
        compiler_params=_cparams(("parallel",)),
        name="s5_glu",
    )(y, w, b.reshape(1, n).astype(F32))


def _attn_body(q_ref, k_ref, v_ref, lam_ref, sw_ref, o_ref,
               qs_ref, m_ref, acc_ref, s_ref, p_ref, al_ref, *, tq):
    qi = pl.program_id(1)
    hw = 2 * HEAD_DIM
    q = q_ref[...]
    lane = lax.broadcasted_iota(jnp.int32, q.shape, 1)
    zero = jnp.zeros_like(q)
    qs_ref[0] = jnp.where(lane < HEAD_DIM, q, zero)
    qs_ref[1] = jnp.where(lane >= HEAD_DIM, q, zero)
    m_ref[...] = jnp.full(m_ref.shape, -jnp.inf, F32)
    acc_ref[...] = jnp.zeros(acc_ref.shape, F32)

    def scores(c, slot, masked=False):
        off = pl.multiple_of(c * tq, tq)
        k = k_ref[pl.ds(off, tq), :]
        for mp in range(2):
            s = lax.dot_general(qs_ref[mp], k, (((1,), (1,)), ((), ())),
                                preferred_element_type=F32)
            if masked:
                row = lax.broadcasted_iota(jnp.int32, s.shape, 0)
                col = lax.broadcasted_iota(jnp.int32, s.shape, 1)
                s = jnp.where(col <= row, s, -jnp.inf)
            s_ref[slot, mp] = s

    def softmax(slot):
        for mp in range(2):
            s = s_ref[slot, mp]
            m_old = m_ref[mp]
            m_new = jnp.maximum(m_old, jnp.max(s, axis=-1, keepdims=True))
            al_ref[slot, mp] = jnp.exp(m_old - m_new)
            p_ref[slot, mp] = jnp.exp(s - m_new).astype(BF16)
            m_ref[mp] = m_new

    def pv(c, slot):
        off = pl.multiple_of(c * tq, tq)
        v = v_ref[pl.ds(off, tq), :]
        for mp in range(2):
            acc_ref[mp] = al_ref[slot, mp] * acc_ref[mp] + jnp.dot(
                p_ref[slot, mp], v, preferred_element_type=F32)

    @pl.when(qi == 0)
    def _one_chunk():
        scores(0, 0, masked=True)
        softmax(0)
        pv(0, 0)

    @pl.when(qi == 1)
    def _two_chunks():
        scores(0, 1)
        softmax(1)
        pv(0, 1)
        scores(1, 0, masked=True)
        softmax(0)
        pv(1, 0)

    @pl.when(qi >= 2)
    def _pipelined():
        odd = qi % 2

        @pl.when(odd == 0)
        def _fill_even():
            scores(0, 0)
            scores(1, 1)
            softmax(0)

        @pl.when(odd == 1)
        def _fill_odd():
            scores(0, 1)
            scores(1, 0)
            softmax(1)
            scores(2, 1)
            softmax(0)
            pv(0, 1)

        t0 = 2 + odd

        def two_steps(u, carry):
            t = t0 + 2 * u
            scores(t, 0)
            softmax(1)
            pv(t - 2, 0)
            scores(t + 1, 1)
            softmax(0)
            pv(t - 1, 1)
            return carry

        lax.fori_loop(0, (qi - t0) // 2, two_steps, 0)
        scores(qi, 0, masked=True)
        softmax(1)
        pv(qi - 2, 0)
        softmax(0)
        pv(qi - 1, 1)
        pv(qi, 0)

    lam = (jnp.exp(jnp.sum(lam_ref[0:1, :] * lam_ref[1:2, :]))
           - jnp.exp(jnp.sum(lam_ref[2:3, :] * lam_ref[3:4, :])) + LAMBDA_INIT)
    a1 = acc_ref[0]
    a2 = acc_ref[1]
    o = a1[:, :hw] / a1[:, hw:] - lam * (a2[:, :hw] / a2[:, hw:])
    o = o * lax.rsqrt(jnp.mean(o * o, axis=-1, keepdims=True) + SUBLN_EPS)
    o_ref[...] = (o * sw_ref[...] * (1.0 - LAMBDA_INIT)).astype(o_ref.dtype)


def _diff_attention(qk, v1, lam_params, subln_w, tq=512):
    s = qk.shape[0]
    nq = s // tq
    hw = 2 * HEAD_DIM
    kcol0 = ATTN_WIDTH // hw
    return pl.pallas_call(
        functools.partial(_attn_body, tq=tq),
        grid=(ATTN_HEADS, nq),
        in_specs=[pl.BlockSpec((tq, hw), lambda h, i: (i, h)),
                  pl.BlockSpec((s, hw), lambda h, i: (0, kcol0 + h)),
                  pl.BlockSpec((s, 2 * hw), lambda h, i: (0, h)),
                  pl.BlockSpec((4, HEAD_DIM), lambda h, i: (0, 0)),
                  pl.BlockSpec((1, hw), lambda h, i: (0, 0))],
        out_specs=pl.BlockSpec((tq, hw), lambda h, i: (i, h)),
        out_shape=jax.ShapeDtypeStruct((s, ATTN_WIDTH), BF16),
        scratch_shapes=[pltpu.VMEM((2, tq, hw), BF16),
                        pltpu.VMEM((2, tq, 1), F32),
                        pltpu.VMEM((2, tq, 2 * hw), F32),
                        pltpu.VMEM((2, 2, tq, tq), F32),
                        pltpu.VMEM((2, 2, tq, tq), BF16),
                        pltpu.VMEM((2, 2, tq, 1), F32)],
        compiler_params=_cparams(("parallel", "arbitrary")),
        name="diff_attn",
    )(qk, qk, v1, lam_params, subln_w.reshape(1, hw).astype(F32))


def _merge_body(x_ref, ys_ref, ya_ref, gs_ref, ga_ref, wps_ref, wpa_ref, wo_ref, n2_ref, wr_ref,
                x1_ref, h2_ref, eid_ref, gate_ref, cnt_ref):
    ps = jnp.dot(ys_ref[...], wps_ref[...], preferred_element_type=F32)
    pa = jnp.dot(ya_ref[...], wpa_ref[...], preferred_element_type=F32)
    merged = gs_ref[...].astype(F32) * ps + ga_ref[...].astype(F32) * pa
    x1 = x_ref[...] + jnp.dot(merged.astype(BF16), wo_ref[...], preferred_element_type=F32)
    x1_ref[...] = x1
    h2 = x1 * lax.rsqrt(jnp.mean(x1 * x1, axis=-1, keepdims=True) + NORM_EPS) * n2_ref[...]
    h2_ref[...] = h2
    wr = wr_ref[...]
    h_hi = h2.astype(BF16)
    h_lo = (h2 - h_hi.astype(F32)).astype(BF16)
    w_hi = wr.astype(BF16)
    w_lo = (wr - w_hi.astype(F32)).astype(BF16)
    logits = (jnp.dot(h_hi, w_hi, preferred_element_type=F32)
              + jnp.dot(h_lo, w_hi, preferred_element_type=F32)
              + jnp.dot(h_hi, w_lo, preferred_element_type=F32))
    lane = lax.broadcasted_iota(jnp.int32, logits.shape, 1)
    big = jnp.int32(1 << 20)
    ninf = jnp.float32(-jnp.inf)
    is_g = lane < N_GROUPS
    gl = jnp.where(is_g, logits, ninf)
    gm = jnp.max(gl, axis=-1, keepdims=True)
    g_idx = jnp.min(jnp.where(gl == gm, lane, big), axis=-1, keepdims=True)
    g_val = 1.0 / jnp.sum(jnp.where(is_g, jnp.exp(gl - gm), 0.0), axis=-1, keepdims=True)
    lo = N_GROUPS + g_idx * EXPERTS_PER_GROUP
    in_grp = (lane >= lo) & (lane < lo + EXPERTS_PER_GROUP)
    el = jnp.where(in_grp, logits, ninf)
    e1 = jnp.max(el, axis=-1, keepdims=True)
    i1 = jnp.min(jnp.where(el == e1, lane, big), axis=-1, keepdims=True)
    el2 = jnp.where(lane == i1, ninf, el)
    e2 = jnp.max(el2, axis=-1, keepdims=True)
    i2 = jnp.min(jnp.where(el2 == e2, lane, big), axis=-1, keepdims=True)
    t = jnp.exp(e2 - e1)
    p1 = 1.0 / (1.0 + t)
    p2 = t / (1.0 + t)
    gate_ref[...] = jnp.where(lane == 0, p1 * g_val, jnp.where(lane == 1, p2 * g_val, 0.0))

    @pl.when(pl.program_id(0) == 0)
    def _zero_counts():
        cnt_ref[...] = jnp.zeros(cnt_ref.shape, F32)

    ex1 = i1 - N_GROUPS
    ex2 = i2 - N_GROUPS
    oh1 = lane == ex1
    oh2 = lane == ex2
    onehot = jnp.where(oh1 | oh2, 1.0, 0.0)
    tm = logits.shape[0]
    r_i = lax.broadcasted_iota(jnp.int32, (tm, tm), 0)
    c_i = lax.broadcasted_iota(jnp.int32, (tm, tm), 1)
    tri = jnp.where(c_i < r_i, 1.0, 0.0).astype(BF16)
    before = jnp.dot(tri, onehot.astype(BF16), preferred_element_type=F32) + cnt_ref[...]
    rank1 = jnp.sum(jnp.where(oh1, before, 0.0), axis=-1, keepdims=True).astype(jnp.int32)
    rank2 = jnp.sum(jnp.where(oh2, before, 0.0), axis=-1, keepdims=True).astype(jnp.int32)
    cnt_ref[...] = cnt_ref[...] + jnp.sum(onehot, axis=0, keepdims=True)
    eid_ref[...] = jnp.where(lane == 0, ex1, jnp.where(lane == 1, ex2,
                             jnp.where(lane == 2, rank1, jnp.where(lane == 3, rank2, 0))))


def _merge_route(x, y_ssm, y_attn, gates, wps, wpa, wo, n2w, wr, tm=256):
    s, d = x.shape
    nw = y_ssm.shape[1]
    const = lambda i: (0, 0)
    return pl.pallas_call(
        _merge_body,
        grid=(s // tm,),
        in_specs=[pl.BlockSpec((tm, d), lambda i: (i, 0)),
                  pl.BlockSpec((tm, nw), lambda i: (i, 0)),
                  pl.BlockSpec((tm, nw), lambda i: (i, 0)),
                  pl.BlockSpec((tm, d), lambda i: (i, 0)),
                  pl.BlockSpec((tm, d), lambda i: (i, 1)),
                  pl.BlockSpec((nw, d), const),
                  pl.BlockSpec((nw, d), const),
                  pl.BlockSpec((d, d), const),
                  pl.BlockSpec((1, d), const),
                  pl.BlockSpec((d, LANES), const)],
        out_specs=[pl.BlockSpec((tm, d), lambda i: (i, 0)),
                   pl.BlockSpec((tm, d), lambda i: (i, 0)),
                   pl.BlockSpec((tm, LANES), lambda i: (i, 0)),
                   pl.BlockSpec((tm, LANES), lambda i: (i, 0)),
                   pl.BlockSpec((1, LANES), const)],
        out_shape=[jax.ShapeDtypeStruct((s, d), F32),
                   jax.ShapeDtypeStruct((s, d), F32),
                   jax.ShapeDtypeStruct((s, LANES), jnp.int32),
                   jax.ShapeDtypeStruct((s, LANES), F32),
                   jax.ShapeDtypeStruct((1, LANES), F32)],
        compiler_params=_cparams(("arbitrary",)),
        name="merge_route",
    )(x, y_ssm, y_attn, gates, gates, wps, wpa, wo, n2w.reshape(1, d).astype(F32), wr)


def _row_copy(src_hbm, row, dst_ref, r, sem):
    return pltpu.make_async_copy(src_hbm.at[pl.ds(row, 1), :], dst_ref.at[pl.ds(r, 1), :], sem)


def _gather_rows(src_hbm, idx_ref, base, stride, dst_ref, sem, n):
    def start(r, c):
        _row_copy(src_hbm, idx_ref[base + stride * r], dst_ref, r, sem).start()
        return c

    lax.fori_loop(0, n, start, 0, unroll=8)


def _wait_rows(src_hbm, dst_ref, sem, n):
    pltpu.make_async_copy(src_hbm.at[pl.ds(0, n), :], dst_ref, sem).wait()


def _scatter_tok_body(dest_ref, tok_ref):
    def zero(i, c):
        tok_ref[i] = 0
        return c

    lax.fori_loop(0, tok_ref.shape[0], zero, 0, unroll=8)

    def put(i, c):
        tok_ref[dest_ref[i]] = lax.shift_right_logical(i, TOP_K_LOG2)
        return c

    lax.fori_loop(0, dest_ref.shape[0], put, 0, unroll=8)


def _scatter_tok(dest, buf_len):
    return pl.pallas_call(
        _scatter_tok_body,
        in_specs=[pl.BlockSpec(memory_space=pltpu.SMEM)],
        out_specs=pl.BlockSpec(memory_space=pltpu.SMEM),
        out_shape=jax.ShapeDtypeStruct((buf_len,), jnp.int32),
        name="scatter_tok",
    )(dest)


def _block_out_copy(ob_ref, slot, ys_hbm, blk, sem):
    return pltpu.make_async_copy(ob_ref.at[slot], ys_hbm.at[pl.ds(blk * MOE_BLK, MOE_BLK), :],
                                 sem.at[slot])


WEIGHT_DMA_PRIORITY = 1


def _weight_copies(e, slot, w_hbm, w_buf, wsem):
    return [pltpu.make_async_copy(h.at[e], b.at[slot], wsem.at[slot])
            for h, b in zip(w_hbm, w_buf)]


def _expert_body(fb_ref, nblk_ref, nb_ref, tok_ref, h2_hbm, wg_hbm, wu_hbm, wd_hbm, ys_hbm,
                 xb_ref, ob_ref, wg_ref, wu_ref, wd_ref, wgb_ref, wub_ref, wdb_ref,
                 gsem, osem, wsem, *, n_blocks):
    e = pl.program_id(0)
    nb = nb_ref[0]
    first = fb_ref[e]
    count = nblk_ref[e]
    w_hbm = (wg_hbm, wu_hbm, wd_hbm)
    w_buf = (wg_ref, wu_ref, wd_ref)

    @pl.when(e == 0)
    def _first_weights():
        for c in _weight_copies(0, 0, w_hbm, w_buf, wsem):
            c.start(priority=WEIGHT_DMA_PRIORITY)

    @pl.when(e + 1 < pl.num_programs(0))
    def _next_weights():
        for c in _weight_copies(e + 1, (e + 1) % 2, w_hbm, w_buf, wsem):
            c.start(priority=WEIGHT_DMA_PRIORITY)

    for c in _weight_copies(e, e % 2, w_hbm, w_buf, wsem):
        c.wait()

    @pl.when(count > 0)
    def _expert():
        wgb_ref[...] = wg_ref[e % 2].astype(BF16)
        wub_ref[...] = wu_ref[e % 2].astype(BF16)
        wdb_ref[...] = wd_ref[e % 2].astype(BF16)

        @pl.when(first == 0)
        def _first_rows():
            _gather_rows(h2_hbm, tok_ref, 0, 1, xb_ref.at[0], gsem.at[0], MOE_BLK)

        def block(t, carry):
            g = first + t
            cur = g % 2
            nxt = 1 - cur
            _wait_rows(h2_hbm, xb_ref.at[cur], gsem.at[cur], MOE_BLK)

            @pl.when(g >= 2)
            def _staging_free():
                _block_out_copy(ob_ref, cur, ys_hbm, g - 2, osem).wait()

            base = (g + 1) * MOE_BLK
            for r in range(MOE_BLK):
                _row_copy(h2_hbm, tok_ref[base + r], xb_ref.at[nxt], r, gsem.at[nxt]).start()
            xb = xb_ref[cur].astype(BF16)
            hg = jnp.dot(xb, wgb_ref[...], preferred_element_type=F32)
            hu = jnp.dot(xb, wub_ref[...], preferred_element_type=F32)
            act = (jax.nn.silu(hg) * hu).astype(BF16)
            ob_ref[cur] = jnp.dot(act, wdb_ref[...], preferred_element_type=F32)
            _block_out_copy(ob_ref, cur, ys_hbm, g, osem).start()
            return carry

        lax.fori_loop(0, count, block, 0)

    @pl.when(e == pl.num_programs(0) - 1)
    def _finish():
        _wait_rows(h2_hbm, xb_ref.at[nb % 2], gsem.at[nb % 2], MOE_BLK)
        _block_out_copy(ob_ref, (nb - 1) % 2, ys_hbm, nb - 1, osem).wait()

        @pl.when(nb >= 2)
        def _():
            _block_out_copy(ob_ref, nb % 2, ys_hbm, nb - 2, osem).wait()

        ob_ref[0] = jnp.zeros(ob_ref.shape[1:], ob_ref.dtype)

        def zero_block(g, carry):
            _block_out_copy(ob_ref, 0, ys_hbm, g, osem).start()
            return carry

        lax.fori_loop(nb, n_blocks, zero_block, 0)

        def zero_wait(g, carry):
            _block_out_copy(ob_ref, 0, ys_hbm, g, osem).wait()
            return carry

        lax.fori_loop(nb, n_blocks, zero_wait, 0)


def _experts(h2, first_blk, n_blk, n_used, buf_tok, wg, wu, wd):
    s, d = h2.shape
    buf_len = buf_tok.shape[0]
    n_blocks = buf_len // MOE_BLK
    return pl.pallas_call(
        functools.partial(_expert_body, n_blocks=n_blocks),
        grid_spec=pltpu.PrefetchScalarGridSpec(
            num_scalar_prefetch=4,
            grid=(N_EXPERTS,),
            in_specs=[pl.BlockSpec(memory_space=pl.ANY)] * 4,
            out_specs=pl.BlockSpec(memory_space=pl.ANY),
            scratch_shapes=[pltpu.VMEM((2, MOE_BLK, d), F32), pltpu.VMEM((2, MOE_BLK, d), F32),
                            pltpu.VMEM((2, d, D_FF), F32), pltpu.VMEM((2, d, D_FF), F32),
                            pltpu.VMEM((2, D_FF, d), F32),
                            pltpu.VMEM((d, D_FF), BF16), pltpu.VMEM((d, D_FF), BF16),
                            pltpu.VMEM((D_FF, d), BF16),
                            pltpu.SemaphoreType.DMA((2,)), pltpu.SemaphoreType.DMA((2,)),
                            pltpu.SemaphoreType.DMA((2,))]),
        out_shape=jax.ShapeDtypeStruct((buf_len, d), F32),
        compiler_params=_cparams(("arbitrary",)),
        name="experts",
    )(first_blk, n_blk, n_used, buf_tok, h2, wg, wu, wd)


def _combine_body(pos_ref, x1_ref, gate_ref, fw_ref, ys_hbm, o_ref, g_ref, sem, *, tm):
    i = pl.program_id(0)

    def gather(tile, slot):
        for k in range(TOP_K):
            _gather_rows(ys_hbm, pos_ref, TOP_K * tile * tm + k, TOP_K, g_ref.at[slot, k],
                         sem.at[slot], tm)

    @pl.when(i == 0)
    def _first():
        gather(0, 0)

    @pl.when(i + 1 < pl.num_programs(0))
    def _prefetch():
        gather(i + 1, (i + 1) % 2)

    cur = i % 2
    for k in range(TOP_K):
        _wait_rows(ys_hbm, g_ref.at[cur, k], sem.at[cur], tm)
    gate = gate_ref[...]
    x = x1_ref[...] + (gate[:, 0:1] * g_ref[cur, 0] + gate[:, 1:2] * g_ref[cur, 1])
    y = x * lax.rsqrt(jnp.mean(x * x, axis=-1, keepdims=True) + NORM_EPS)
    o_ref[...] = y * fw_ref[...]


def _combine(x1, ys, pos, gate, fw, tm=256):
    s, d = x1.shape
    return pl.pallas_call(
        functools.partial(_combine_body, tm=tm),
        grid_spec=pltpu.PrefetchScalarGridSpec(
            num_scalar_prefetch=1,
            grid=(s // tm,),
            in_specs=[pl.BlockSpec((tm, d), lambda i, p: (i, 0)),
                      pl.BlockSpec((tm, LANES), lambda i, p: (i, 0)),
                      pl.BlockSpec((1, d), lambda i, p: (0, 0)),
                      pl.BlockSpec(memory_space=pl.ANY)],
            out_specs=pl.BlockSpec((tm, d), lambda i, p: (i, 0)),
            scratch_shapes=[pltpu.VMEM((2, TOP_K, tm, d), F32),
                            pltpu.SemaphoreType.DMA((2,))]),
        out_shape=jax.ShapeDtypeStruct((s, d), F32),
        compiler_params=_cparams(("arbitrary",)),
        name="combine_norm",
    )(pos, x1, gate, fw.reshape(1, d).astype(F32), ys)


def _dispatch_tables(eid, counts):
    n_tok = eid.shape[0]
    n_slots = n_tok * TOP_K
    buf_len = n_slots + N_EXPERTS * MOE_BLK
    counts = counts.astype(jnp.int32)
    padded = ((counts + MOE_BLK - 1) // MOE_BLK) * MOE_BLK
    padded_ends = jnp.cumsum(padded)
    padded_starts = padded_ends - padded
    experts = eid[:, :TOP_K]
    onehot = experts[:, :, None] == jnp.arange(N_EXPERTS, dtype=jnp.int32)
    dest = jnp.sum(jnp.where(onehot, padded_starts, 0), axis=-1) + eid[:, TOP_K:2 * TOP_K]
    n_used = (padded_ends[-1] // MOE_BLK).astype(jnp.int32).reshape(1)
    first_blk = (padded_starts // MOE_BLK).astype(jnp.int32)
    n_blk = (padded // MOE_BLK).astype(jnp.int32)
    return first_blk, n_blk, n_used, dest.reshape(n_slots).astype(jnp.int32), buf_len


def kernel(x, positions, norm1_w, w_in, ssm_lambda_re, ssm_lambda_im, ssm_log_dt, ssm_b_re, ssm_b_im, ssm_c_re, ssm_c_im, ssm_d, ssm_glu_w, ssm_glu_b, attn_lambda_q1, attn_lambda_k1, attn_lambda_q2, attn_lambda_k2, attn_subln_w, w_proj_ssm, w_proj_attn, w_out, norm2_w, router_group_w, router_expert_w, expert_w_gate, expert_w_up, expert_w_down, final_norm_w):
    bsz, seq, d = x.shape
    depth = norm1_w.shape[0]
    xs = x.reshape(bsz * seq, d)
    pos = positions.reshape(bsz * seq)
    o_q = SSM_WIDTH
    o_v = o_q + 2 * ATTN_WIDTH
    o_g = o_v + ATTN_WIDTH
    for l in range(depth):
        h = _rmsnorm(xs, norm1_w[l].astype(F32), BF16)
        w_in_b = w_in[l].astype(F32)
        u = _mm(h, w_in_b, 0, SSM_WIDTH, "none", F32)
        qk = _mm_rope(h, w_in_b, pos, o_q)
        v = _mm_vones(h, w_in_b, o_v)
        gates = _mm(h, w_in_b, o_g, 2 * D_MODEL, "sigmoid", F32)

        tables = _s5_tables(ssm_lambda_re[l], ssm_lambda_im[l], ssm_log_dt[l],
                            ssm_b_re[l], ssm_b_im[l], ssm_c_re[l], ssm_c_im[l])
        y = _s5_core(u, tables, ssm_d[l])
        y_ssm = _glu(y, ssm_glu_w[l].astype(BF16), ssm_glu_b[l])

        lam_params = jnp.stack([attn_lambda_q1[l], attn_lambda_k1[l],
                                attn_lambda_q2[l], attn_lambda_k2[l]]).astype(F32)
        y_attn = _diff_attention(qk, v, lam_params, attn_subln_w[l])

        wr = jnp.concatenate([router_group_w[l], router_expert_w[l]], axis=1).astype(F32)
        wr = jnp.pad(wr, ((0, 0), (0, LANES - wr.shape[1])))
        x1, h2, eid, gate, counts = _merge_route(
            xs, y_ssm, y_attn, gates, w_proj_ssm[l].astype(BF16), w_proj_attn[l].astype(BF16),
            w_out[l].astype(BF16), norm2_w[l], wr)

        first_blk, n_blk, n_used, dest, buf_len = _dispatch_tables(eid, counts[0, :N_EXPERTS])
        buf_tok = _scatter_tok(dest, buf_len)
        ys = _experts(h2, first_blk, n_blk, n_used, buf_tok, expert_w_gate[l].astype(F32),
                      expert_w_up[l].astype(F32), expert_w_down[l].astype(F32))
        last = l == depth - 1
        assert last, "DEPTH > 1 needs an un-normalised combine"
        xs = _combine(x1, ys, dest, gate, final_norm_w)
    return xs.reshape(bsz, seq, d)
```

```python
import functools
import math

import jax
import jax.numpy as jnp
from jax import lax
from jax.experimental import pallas as pl
from jax.experimental.pallas import tpu as pltpu

F32 = jnp.float32
BF16 = jnp.bfloat16

D_MODEL = 2048
SSM_WIDTH = 1024
SSM_GROUP = 16
SSM_GROUPS = 64
SSM_STATE = 64
ATTN_WIDTH = 1024
ATTN_HEADS = 8
HEAD_DIM = 64
ROPE_THETA = 10000.0
N_GROUPS = 4
EXPERTS_PER_GROUP = 8
N_EXPERTS = 32
TOP_K = 2
TOP_K_LOG2 = 1
D_FF = 512
NORM_EPS = 1e-6
SUBLN_EPS = 1e-5
LAMBDA_INIT = 0.8 - 0.6 * math.exp(-0.3 * 0)

LANES = 128
VMEM_LIMIT = 48 * 1024 * 1024

SSM_CHUNK = 16
SSM_GB = 8
MOE_BLK = 128


def _cparams(sem):
    return pltpu.CompilerParams(dimension_semantics=sem, vmem_limit_bytes=VMEM_LIMIT)


def _rmsnorm_body(x_ref, w_ref, o_ref):
    x = x_ref[...]
    y = x * lax.rsqrt(jnp.mean(x * x, axis=-1, keepdims=True) + NORM_EPS)
    o_ref[...] = (y * w_ref[...]).astype(o_ref.dtype)


def _rmsnorm(x, w, out_dtype, tm=512):
    s, d = x.shape
    return pl.pallas_call(
        _rmsnorm_body,
        grid=(s // tm,),
        in_specs=[pl.BlockSpec((tm, d), lambda i: (i, 0)),
                  pl.BlockSpec((1, d), lambda i: (0, 0))],
        out_specs=pl.BlockSpec((tm, d), lambda i: (i, 0)),
        out_shape=jax.ShapeDtypeStruct((s, d), out_dtype),
        compiler_params=_cparams(("parallel",)),
        name="rmsnorm",
    )(x, w.reshape(1, d))


def _proj(h_ref, w_ref, wb_ref, row_axis):
    @pl.when(pl.program_id(row_axis) == 0)
    def _cast():
        wb_ref[...] = w_ref[...].astype(BF16)

    return jnp.dot(h_ref[...], wb_ref[...], preferred_element_type=F32)


def _mm_body(h_ref, w_ref, o_ref, wb_ref, *, act):
    acc = _proj(h_ref, w_ref, wb_ref, 1)
    if act == "sigmoid":
        acc = jax.nn.sigmoid(acc)
    o_ref[...] = acc.astype(o_ref.dtype)


def _mm(h, w, col0, ncols, act, out_dtype, tm=512, tn=1024):
    s, k = h.shape
    jb = col0 // tn
    return pl.pallas_call(
        functools.partial(_mm_body, act=act),
        grid=(ncols // tn, s // tm),
        in_specs=[pl.BlockSpec((tm, k), lambda j, i: (i, 0)),
                  pl.BlockSpec((k, tn), lambda j, i: (0, j + jb))],
        out_specs=pl.BlockSpec((tm, tn), lambda j, i: (i, j)),
        out_shape=jax.ShapeDtypeStruct((s, ncols), out_dtype),
        scratch_shapes=[pltpu.VMEM((k, tn), BF16)],
        compiler_params=_cparams(("arbitrary", "arbitrary")),
        name="mm_" + act,
    )(h, w)


def _mm_vones_body(h_ref, w_ref, o_ref, wb_ref):
    acc = _proj(h_ref, w_ref, wb_ref, 0)
    hw = 2 * HEAD_DIM
    ones = jnp.ones((acc.shape[0], hw), o_ref.dtype)
    for hd in range(acc.shape[1] // hw):
        o_ref[:, 2 * hd * hw:(2 * hd + 1) * hw] = acc[:, hd * hw:(hd + 1) * hw].astype(o_ref.dtype)
        o_ref[:, (2 * hd + 1) * hw:(2 * hd + 2) * hw] = ones


def _mm_vones(h, w, col0, tm=512):
    s, k = h.shape
    tn = ATTN_WIDTH
    jb = col0 // tn
    return pl.pallas_call(
        _mm_vones_body,
        grid=(s // tm,),
        in_specs=[pl.BlockSpec((tm, k), lambda i: (i, 0)),
                  pl.BlockSpec((k, tn), lambda i: (0, jb))],
        out_specs=pl.BlockSpec((tm, 2 * tn), lambda i: (i, 0)),
        out_shape=jax.ShapeDtypeStruct((s, 2 * tn), BF16),
        scratch_shapes=[pltpu.VMEM((k, tn), BF16)],
        compiler_params=_cparams(("arbitrary",)),
        name="mm_vones",
    )(h, w)


def _mm_rope_body(h_ref, w_ref, pos_ref, invf_ref, o_ref, wb_ref, *, tn):
    j = pl.program_id(0)
    acc = _proj(h_ref, w_ref, wb_ref, 1)
    scale = jnp.where(j == 0, HEAD_DIM ** -0.5, 1.0).astype(F32)
    ang = pos_ref[...].astype(F32) * invf_ref[...]
    lane = lax.broadcasted_iota(jnp.int32, ang.shape, 1)
    low = (lane % HEAD_DIM) < (HEAD_DIM // 2)
    cos = jnp.cos(ang) * scale
    sin = jnp.where(low, -jnp.sin(ang), jnp.sin(ang)) * scale
    for c in range(tn // LANES):
        t = acc[:, c * LANES:(c + 1) * LANES]
        partner = jnp.where(low, pltpu.roll(t, LANES - HEAD_DIM // 2, 1),
                            pltpu.roll(t, HEAD_DIM // 2, 1))
        o_ref[:, c * LANES:(c + 1) * LANES] = (t * cos + partner * sin).astype(o_ref.dtype)


def _mm_rope(h, w, positions, col0, tm=512, tn=1024):
    s, k = h.shape
    jb = col0 // tn
    inv_freq = 1.0 / (ROPE_THETA ** (jnp.arange(0, HEAD_DIM, 2, dtype=F32) / HEAD_DIM))
    invf = jnp.tile(inv_freq, LANES // (HEAD_DIM // 2)).reshape(1, LANES)
    return pl.pallas_call(
        functools.partial(_mm_rope_body, tn=tn),
        grid=(2, s // tm),
        in_specs=[pl.BlockSpec((tm, k), lambda j, i: (i, 0)),
                  pl.BlockSpec((k, tn), lambda j, i: (0, j + jb)),
                  pl.BlockSpec((tm, 1), lambda j, i: (i, 0)),
                  pl.BlockSpec((1, LANES), lambda j, i: (0, 0))],
        out_specs=pl.BlockSpec((tm, tn), lambda j, i: (i, j)),
        out_shape=jax.ShapeDtypeStruct((s, 2 * tn), BF16),
        scratch_shapes=[pltpu.VMEM((k, tn), BF16)],
        compiler_params=_cparams(("arbitrary", "arbitrary")),
        name="mm_rope",
    )(h, w, positions.reshape(s, 1), invf)


def _s5_tables(lam_re, lam_im, log_dt, b_re, b_im, c_re, c_im):
    L, G, P, M = SSM_CHUNK, SSM_GROUPS, SSM_STATE, SSM_GROUP
    hi = lax.Precision.HIGHEST
    lr, li = lam_re.astype(F32), lam_im.astype(F32)
    dt = jnp.exp(log_dt.astype(F32))[:, None]
    n = jnp.arange(L + 1, dtype=F32)[:, None, None]
    mag = jnp.exp(lr * dt * n)
    pr = mag * jnp.cos(li * dt * n)
    pi = mag * jnp.sin(li * dt * n)
    nr, ni = pr[1] - 1.0, pi[1]
    den = lr * lr + li * li
    f_re = (nr * lr + ni * li) / den
    f_im = (ni * lr - nr * li) / den
    br, bi = b_re.astype(F32), b_im.astype(F32)
    bbr = f_re[..., None] * br - f_im[..., None] * bi
    bbi = f_re[..., None] * bi + f_im[..., None] * br
    cr, ci = c_re.astype(F32), c_im.astype(F32)

    nb, gb = G // SSM_GB, SSM_GB
    pr_g = jnp.transpose(pr, (1, 0, 2))
    pi_g = jnp.transpose(pi, (1, 0, 2))
    bbr_t = jnp.transpose(bbr, (0, 2, 1))
    bbi_t = jnp.transpose(bbi, (0, 2, 1))

    abr = pr_g[:, :L, None, :] * bbr_t[:, None] - pi_g[:, :L, None, :] * bbi_t[:, None]
    abi = pr_g[:, :L, None, :] * bbi_t[:, None] + pi_g[:, :L, None, :] * bbr_t[:, None]
    kmat = (jnp.einsum('gnip,gop->gnio', abr, cr, precision=hi)
            - jnp.einsum('gnip,gop->gnio', abi, ci, precision=hi))
    kpad = jnp.pad(kmat, ((0, 0), (1, 0), (0, 0), (0, 0)))
    kq = jnp.stack([jnp.stack([kpad[:, 1 + c2 - r2::2][:, :L // 2] for c2 in range(2)], axis=3)
                    for r2 in range(2)], axis=2)
    kk_src = jnp.transpose(kq.reshape(nb, gb, L // 2, 2, M, 2, M),
                           (0, 2, 3, 5, 4, 1, 6)).reshape(nb, L // 2, 2, 2, M, gb * M)

    ab1r = pr_g[:, 1, None, :] * bbr_t - pi_g[:, 1, None, :] * bbi_t
    ab1i = pr_g[:, 1, None, :] * bbi_t + pi_g[:, 1, None, :] * bbr_t
    b4 = jnp.stack([jnp.stack([ab1r, ab1i], axis=2), jnp.stack([bbr_t, bbi_t], axis=2)],
                   axis=1)
    bb_src = jnp.transpose(b4.reshape(nb, gb, 2, M, 2, P),
                           (0, 2, 4, 3, 1, 5)).reshape(nb, 2, 2, M, gb * P)

    cr_t = jnp.transpose(cr, (0, 2, 1))
    ci_t = jnp.transpose(ci, (0, 2, 1))
    car = cr_t * pr_g[:, 1, :, None] - ci_t * pi_g[:, 1, :, None]
    cai = cr_t * pi_g[:, 1, :, None] + ci_t * pr_g[:, 1, :, None]
    c4 = jnp.stack([jnp.stack([cr_t, car], axis=2), jnp.stack([-ci_t, -cai], axis=2)],
                   axis=1)
    cc_src = jnp.transpose(c4.reshape(nb, gb, 2, P, 2, M),
                           (0, 2, 4, 3, 1, 5)).reshape(nb, 2, 2, P, gb * M)

    ap = jnp.stack([pr.reshape(L + 1, nb, gb * P), pi.reshape(L + 1, nb, gb * P)], axis=2)
    return kk_src, bb_src, cc_src, ap.reshape(L + 1, 2 * G * P)


def _cmul(ar, ai, zr, zi):
    return ar * zr - ai * zi, ar * zi + ai * zr


def _block_diag(src, rows_per_group, cols_per_group):
    t = jnp.concatenate([src] * SSM_GB, axis=0)
    row = lax.broadcasted_iota(jnp.int32, t.shape, 0)
    col = lax.broadcasted_iota(jnp.int32, t.shape, 1)
    return jnp.where(row // rows_per_group == col // cols_per_group, t, 0.0).astype(BF16)


def _s5_body(u_ref, kk_src, bb_src, cc_src, ap_ref, d_ref, y_ref,
             kk_ref, bb_ref, cc_ref, lhs_ref, er_ref, ei_ref, xr_ref, xi_ref):
    L, M, P = SSM_CHUNK, SSM_GROUP, SSM_STATE
    nc = u_ref.shape[0] // L
    hp = SSM_GB * P
    gm = SSM_GB * M
    nq = L // 2
    for a in range(2):
        for b in range(2):
            for dd in range(nq):
                kk_ref[dd, a * gm:(a + 1) * gm, b * gm:(b + 1) * gm] = _block_diag(
                    kk_src[0, dd, a, b], M, M)
            bb_ref[a * gm:(a + 1) * gm, b * hp:(b + 1) * hp] = _block_diag(bb_src[0, a, b], M, P)
            cc_ref[a * hp:(a + 1) * hp, b * gm:(b + 1) * gm] = _block_diag(cc_src[0, a, b], P, M)
    for q in range(nq):
        lhs_ref[q] = jnp.concatenate(
            [u_ref[pl.ds(2 * q, nc, stride=L), :], u_ref[pl.ds(2 * q + 1, nc, stride=L), :]],
            axis=1).astype(BF16)

    er = jnp.zeros((nc, hp), F32)
    ei = jnp.zeros((nc, hp), F32)
    for q in range(nq):
        z = jnp.dot(lhs_ref[q], bb_ref[...], preferred_element_type=F32)
        n = L - 2 - 2 * q
        dr, di = _cmul(ap_ref[n:n + 1, :hp], ap_ref[n:n + 1, hp:], z[:, :hp], z[:, hp:])
        er = er + dr
        ei = ei + di
    er_ref[...] = er
    ei_ref[...] = ei

    ar = ap_ref[L:L + 1, :hp]
    ai = ap_ref[L:L + 1, hp:]

    def step(c, carry):
        xr, xi = carry
        xr_ref[pl.ds(c, 1), :] = xr
        xi_ref[pl.ds(c, 1), :] = xi
        nr, ni = _cmul(ar, ai, xr, xi)
        return nr + er_ref[pl.ds(c, 1), :], ni + ei_ref[pl.ds(c, 1), :]

    zero = jnp.zeros((1, hp), F32)
    lax.fori_loop(0, nc, step, (zero, zero))

    for qq in range(nq):
        n = 2 * qq + 1
        wr, wi = _cmul(ap_ref[n:n + 1, :hp], ap_ref[n:n + 1, hp:], xr_ref[...], xi_ref[...])
        w = jnp.concatenate([wr, wi], axis=1).astype(BF16)
        acc = jnp.dot(w, cc_ref[...], preferred_element_type=F32)
        for q in range(qq + 1):
            acc = acc + jnp.dot(lhs_ref[q], kk_ref[qq - q], preferred_element_type=F32)
        for r in range(2):
            j = 2 * qq + r
            y = acc[:, r * LANES:(r + 1) * LANES] + d_ref[...] * u_ref[pl.ds(j, nc, stride=L), :]
            y_ref[pl.ds(j, nc, stride=L), :] = jax.nn.gelu(y)


def _s5_core(u, tables, d_skip):
    s, width = u.shape
    L, P, M, gb = SSM_CHUNK, SSM_STATE, SSM_GROUP, SSM_GB
    nc = s // L
    nb = width // (gb * M)
    kk_src, bb_src, cc_src, ap = tables
    return pl.pallas_call(
        _s5_body,
        grid=(nb,),
        in_specs=[pl.BlockSpec((s, gb * M), lambda b: (0, b)),
                  pl.BlockSpec((1, L // 2, 2, 2, M, gb * M), lambda b: (b, 0, 0, 0, 0, 0)),
                  pl.BlockSpec((1, 2, 2, M, gb * P), lambda b: (b, 0, 0, 0, 0)),
                  pl.BlockSpec((1, 2, 2, P, gb * M), lambda b: (b, 0, 0, 0, 0)),
                  pl.BlockSpec((L + 1, 2 * gb * P), lambda b: (0, b)),
                  pl.BlockSpec((1, gb * M), lambda b: (0, b))],
        out_specs=pl.BlockSpec((s, gb * M), lambda b: (0, b)),
        out_shape=jax.ShapeDtypeStruct((s, width), F32),
        scratch_shapes=[pltpu.VMEM((L // 2, 2 * gb * M, 2 * gb * M), BF16),
                        pltpu.VMEM((2 * gb * M, 2 * gb * P), BF16),
                        pltpu.VMEM((2 * gb * P, 2 * gb * M), BF16),
                        pltpu.VMEM((L // 2, nc, 2 * gb * M), BF16)]
        + [pltpu.VMEM((nc, gb * P), F32) for _ in range(4)],
        compiler_params=_cparams(("parallel",)),
        name="s5_scan",
    )(u, kk_src, bb_src, cc_src, ap, d_skip.astype(F32).reshape(1, width))


def _glu_body(y_ref, w_ref, b_ref, o_ref):
    y = y_ref[...]
    z = jnp.dot(y.astype(BF16), w_ref[...], preferred_element_type=F32) + b_ref[...]
    o_ref[...] = (y * jax.nn.sigmoid(z)).astype(o_ref.dtype)


def _glu(y, w, b, tm=512):
    s, n = y.shape
    return pl.pallas_call(
        _glu_body,
        grid=(s // tm,),
        in_specs=[pl.BlockSpec((tm, n), lambda i: (i, 0)),
                  pl.BlockSpec((n, n), lambda i: (0, 0)),
                  pl.BlockSpec((1, n), lambda i: (0, 0))],
        out_specs=pl.BlockSpec((tm, n), lambda i: (i, 0)),
        out_shape=jax.ShapeDtypeStruct((s, n), BF16),
        compiler_params=_cparams(("parallel",)),
        name="s5_glu",
    )(y, w, b.reshape(1, n).astype(F32))


def _attn_body(q_ref, k_ref, v_ref, lam_ref, sw_ref, o_ref,
               qs_ref, m_ref, acc_ref, s_ref, p_ref, al_ref, *, tq):
    qi = pl.program_id(1)
    hw = 2 * HEAD_DIM
    q = q_ref[...]
    lane = lax.broadcasted_iota(jnp.int32, q.shape, 1)
    zero = jnp.zeros_like(q)
    qs_ref[0] = jnp.where(lane < HEAD_DIM, q, zero)
    qs_ref[1] = jnp.where(lane >= HEAD_DIM, q, zero)
    m_ref[...] = jnp.full(m_ref.shape, -jnp.inf, F32)
    acc_ref[...] = jnp.zeros(acc_ref.shape, F32)

    def scores(c, slot, masked=False):
        off = pl.multiple_of(c * tq, tq)
        k = k_ref[pl.ds(off, tq), :]
        for mp in range(2):
            s = lax.dot_general(qs_ref[mp], k, (((1,), (1,)), ((), ())),
                                preferred_element_type=F32)
            if masked:
                row = lax.broadcasted_iota(jnp.int32, s.shape, 0)
                col = lax.broadcasted_iota(jnp.int32, s.shape, 1)
                s = jnp.where(col <= row, s, -jnp.inf)
            s_ref[slot, mp] = s

    def softmax(slot):
        for mp in range(2):
            s = s_ref[slot, mp]
            m_old = m_ref[mp]
            m_new = jnp.maximum(m_old, jnp.max(s, axis=-1, keepdims=True))
            al_ref[slot, mp] = jnp.exp(m_old - m_new)
            p_ref[slot, mp] = jnp.exp(s - m_new).astype(BF16)
            m_ref[mp] = m_new

    def pv(c, slot):
        off = pl.multiple_of(c * tq, tq)
        v = v_ref[pl.ds(off, tq), :]
        for mp in range(2):
            acc_ref[mp] = al_ref[slot, mp] * acc_ref[mp] + jnp.dot(
                p_ref[slot, mp], v, preferred_element_type=F32)

    @pl.when(qi == 0)
    def _one_chunk():
        scores(0, 0, masked=True)
        softmax(0)
        pv(0, 0)

    @pl.when(qi == 1)
    def _two_chunks():
        scores(0, 1)
        softmax(1)
        pv(0, 1)
        scores(1, 0, masked=True)
        softmax(0)
        pv(1, 0)

    @pl.when(qi >= 2)
    def _pipelined():
        odd = qi % 2

        @pl.when(odd == 0)
        def _fill_even():
            scores(0, 0)
            scores(1, 1)
            softmax(0)

        @pl.when(odd == 1)
        def _fill_odd():
            scores(0, 1)
            scores(1, 0)
            softmax(1)
            scores(2, 1)
            softmax(0)
            pv(0, 1)

        t0 = 2 + odd

        def two_steps(u, carry):
            t = t0 + 2 * u
            scores(t, 0)
            softmax(1)
            pv(t - 2, 0)
            scores(t + 1, 1)
            softmax(0)
            pv(t - 1, 1)
            return carry

        lax.fori_loop(0, (qi - t0) // 2, two_steps, 0)
        scores(qi, 0, masked=True)
        softmax(1)
        pv(qi - 2, 0)
        softmax(0)
        pv(qi - 1, 1)
        pv(qi, 0)

    lam = (jnp.exp(jnp.sum(lam_ref[0:1, :] * lam_ref[1:2, :]))
           - jnp.exp(jnp.sum(lam_ref[2:3, :] * lam_ref[3:4, :])) + LAMBDA_INIT)
    a1 = acc_ref[0]
    a2 = acc_ref[1]
    o = a1[:, :hw] / a1[:, hw:] - lam * (a2[:, :hw] / a2[:, hw:])
    o = o * lax.rsqrt(jnp.mean(o * o, axis=-1, keepdims=True) + SUBLN_EPS)
    o_ref[...] = (o * sw_ref[...] * (1.0 - LAMBDA_INIT)).astype(o_ref.dtype)


def _diff_attention(qk, v1, lam_params, subln_w, tq=512):
    s = qk.shape[0]
    nq = s // tq
    hw = 2 * HEAD_DIM
    kcol0 = ATTN_WIDTH // hw
    return pl.pallas_call(
        functools.partial(_attn_body, tq=tq),
        grid=(ATTN_HEADS, nq),
        in_specs=[pl.BlockSpec((tq, hw), lambda h, i: (i, h)),
                  pl.BlockSpec((s, hw), lambda h, i: (0, kcol0 + h)),
                  pl.BlockSpec((s, 2 * hw), lambda h, i: (0, h)),
                  pl.BlockSpec((4, HEAD_DIM), lambda h, i: (0, 0)),
                  pl.BlockSpec((1, hw), lambda h, i: (0, 0))],
        out_specs=pl.BlockSpec((tq, hw), lambda h, i: (i, h)),
        out_shape=jax.ShapeDtypeStruct((s, ATTN_WIDTH), BF16),
        scratch_shapes=[pltpu.VMEM((2, tq, hw), BF16),
                        pltpu.VMEM((2, tq, 1), F32),
                        pltpu.VMEM((2, tq, 2 * hw), F32),
                        pltpu.VMEM((2, 2, tq, tq), F32),
                        pltpu.VMEM((2, 2, tq, tq), BF16),
                        pltpu.VMEM((2, 2, tq, 1), F32)],
        compiler_params=_cparams(("parallel", "arbitrary")),
        name="diff_attn",
    )(qk, qk, v1, lam_params, subln_w.reshape(1, hw).astype(F32))


def _merge_body(x_ref, ys_ref, ya_ref, gs_ref, ga_ref, wps_ref, wpa_ref, wo_ref, n2_ref, wr_ref,
                x1_ref, h2_ref, eid_ref, gate_ref, cnt_ref):
    ps = jnp.dot(ys_ref[...], wps_ref[...], preferred_element_type=F32)
    pa = jnp.dot(ya_ref[...], wpa_ref[...], preferred_element_type=F32)
    merged = gs_ref[...].astype(F32) * ps + ga_ref[...].astype(F32) * pa
    x1 = x_ref[...] + jnp.dot(merged.astype(BF16), wo_ref[...], preferred_element_type=F32)
    x1_ref[...] = x1
    h2 = x1 * lax.rsqrt(jnp.mean(x1 * x1, axis=-1, keepdims=True) + NORM_EPS) * n2_ref[...]
    for sl in range(h2.shape[1] // LANES):
        h2_ref[:, sl, :] = h2[:, sl * LANES:(sl + 1) * LANES]
    wr = wr_ref[...]
    h_hi = h2.astype(BF16)
    h_lo = (h2 - h_hi.astype(F32)).astype(BF16)
    w_hi = wr.astype(BF16)
    w_lo = (wr - w_hi.astype(F32)).astype(BF16)
    logits = (jnp.dot(h_hi, w_hi, preferred_element_type=F32)
              + jnp.dot(h_lo, w_hi, preferred_element_type=F32)
              + jnp.dot(h_hi, w_lo, preferred_element_type=F32))
    lane = lax.broadcasted_iota(jnp.int32, logits.shape, 1)
    big = jnp.int32(1 << 20)
    ninf = jnp.float32(-jnp.inf)
    is_g = lane < N_GROUPS
    gl = jnp.where(is_g, logits, ninf)
    gm = jnp.max(gl, axis=-1, keepdims=True)
    g_idx = jnp.min(jnp.where(gl == gm, lane, big), axis=-1, keepdims=True)
    g_val = 1.0 / jnp.sum(jnp.where(is_g, jnp.exp(gl - gm), 0.0), axis=-1, keepdims=True)
    lo = N_GROUPS + g_idx * EXPERTS_PER_GROUP
    in_grp = (lane >= lo) & (lane < lo + EXPERTS_PER_GROUP)
    el = jnp.where(in_grp, logits, ninf)
    e1 = jnp.max(el, axis=-1, keepdims=True)
    i1 = jnp.min(jnp.where(el == e1, lane, big), axis=-1, keepdims=True)
    el2 = jnp.where(lane == i1, ninf, el)
    e2 = jnp.max(el2, axis=-1, keepdims=True)
    i2 = jnp.min(jnp.where(el2 == e2, lane, big), axis=-1, keepdims=True)
    t = jnp.exp(e2 - e1)
    p1 = 1.0 / (1.0 + t)
    p2 = t / (1.0 + t)
    gate_ref[...] = jnp.where(lane == 0, p1 * g_val, jnp.where(lane == 1, p2 * g_val, 0.0))

    @pl.when(pl.program_id(0) == 0)
    def _zero_counts():
        cnt_ref[...] = jnp.zeros(cnt_ref.shape, F32)

    ex1 = i1 - N_GROUPS
    ex2 = i2 - N_GROUPS
    oh1 = lane == ex1
    oh2 = lane == ex2
    onehot = jnp.where(oh1 | oh2, 1.0, 0.0)
    tm = logits.shape[0]
    r_i = lax.broadcasted_iota(jnp.int32, (tm, tm), 0)
    c_i = lax.broadcasted_iota(jnp.int32, (tm, tm), 1)
    tri = jnp.where(c_i < r_i, 1.0, 0.0).astype(BF16)
    before = jnp.dot(tri, onehot.astype(BF16), preferred_element_type=F32) + cnt_ref[...]
    rank1 = jnp.sum(jnp.where(oh1, before, 0.0), axis=-1, keepdims=True).astype(jnp.int32)
    rank2 = jnp.sum(jnp.where(oh2, before, 0.0), axis=-1, keepdims=True).astype(jnp.int32)
    cnt_ref[...] = cnt_ref[...] + jnp.sum(onehot, axis=0, keepdims=True)
    eid_ref[...] = jnp.where(lane == 0, ex1, jnp.where(lane == 1, ex2,
                             jnp.where(lane == 2, rank1, jnp.where(lane == 3, rank2, 0))))


def _merge_route(x, y_ssm, y_attn, gates, wps, wpa, wo, n2w, wr, tm=256):
    s, d = x.shape
    nw = y_ssm.shape[1]
    const = lambda i: (0, 0)
    return pl.pallas_call(
        _merge_body,
        grid=(s // tm,),
        in_specs=[pl.BlockSpec((tm, d), lambda i: (i, 0)),
                  pl.BlockSpec((tm, nw), lambda i: (i, 0)),
                  pl.BlockSpec((tm, nw), lambda i: (i, 0)),
                  pl.BlockSpec((tm, d), lambda i: (i, 0)),
                  pl.BlockSpec((tm, d), lambda i: (i, 1)),
                  pl.BlockSpec((nw, d), const),
                  pl.BlockSpec((nw, d), const),
                  pl.BlockSpec((d, d), const),
                  pl.BlockSpec((1, d), const),
                  pl.BlockSpec((d, LANES), const)],
        out_specs=[pl.BlockSpec((tm, d), lambda i: (i, 0)),
                   pl.BlockSpec((tm, d // LANES, LANES), lambda i: (i, 0, 0)),
                   pl.BlockSpec((tm, LANES), lambda i: (i, 0)),
                   pl.BlockSpec((tm, LANES), lambda i: (i, 0)),
                   pl.BlockSpec((1, LANES), const)],
        out_shape=[jax.ShapeDtypeStruct((s, d), F32),
                   jax.ShapeDtypeStruct((s, d // LANES, LANES), F32),
                   jax.ShapeDtypeStruct((s, LANES), jnp.int32),
                   jax.ShapeDtypeStruct((s, LANES), F32),
                   jax.ShapeDtypeStruct((1, LANES), F32)],
        compiler_params=_cparams(("arbitrary",)),
        name="merge_route",
    )(x, y_ssm, y_attn, gates, gates, wps, wpa, wo, n2w.reshape(1, d).astype(F32), wr)


def _row_copy(src_hbm, row, dst_ref, r, sem):
    return pltpu.make_async_copy(src_hbm.at[pl.ds(row, 1)], dst_ref.at[pl.ds(r, 1)], sem)


def _gather_rows(src_hbm, idx_ref, base, stride, dst_ref, sem, n):
    def start(r, c):
        _row_copy(src_hbm, idx_ref[base + stride * r], dst_ref, r, sem).start()
        return c

    lax.fori_loop(0, n, start, 0, unroll=8)


def _wait_rows(src_hbm, dst_ref, sem, n):
    pltpu.make_async_copy(src_hbm.at[pl.ds(0, n)], dst_ref, sem).wait()


def _scatter_tok_body(dest_ref, tok_ref):
    def zero(i, c):
        tok_ref[i] = 0
        return c

    lax.fori_loop(0, tok_ref.shape[0], zero, 0, unroll=8)

    def put(i, c):
        tok_ref[dest_ref[i]] = lax.shift_right_logical(i, TOP_K_LOG2)
        return c

    lax.fori_loop(0, dest_ref.shape[0], put, 0, unroll=8)


def _scatter_tok(dest, buf_len):
    return pl.pallas_call(
        _scatter_tok_body,
        in_specs=[pl.BlockSpec(memory_space=pltpu.SMEM)],
        out_specs=pl.BlockSpec(memory_space=pltpu.SMEM),
        out_shape=jax.ShapeDtypeStruct((buf_len,), jnp.int32),
        name="scatter_tok",
    )(dest)


def _block_out_copy(ob_ref, slot, ys_hbm, blk, sem):
    return pltpu.make_async_copy(ob_ref.at[slot], ys_hbm.at[pl.ds(blk * MOE_BLK, MOE_BLK), :],
                                 sem.at[slot])


WEIGHT_DMA_PRIORITY = 1


def _weight_copies(e, slot, w_hbm, w_buf, wsem):
    return [pltpu.make_async_copy(h.at[e], b.at[slot], wsem.at[slot])
            for h, b in zip(w_hbm, w_buf)]


def _expert_body(fb_ref, nblk_ref, nb_ref, tok_ref, h2_hbm, wg_hbm, wu_hbm, wd_hbm, ys_hbm,
                 xb_ref, ob_ref, wg_ref, wu_ref, wd_ref, wgb_ref, wub_ref, wdb_ref,
                 gsem, osem, wsem, *, n_blocks):
    e = pl.program_id(0)
    nb = nb_ref[0]
    first = fb_ref[e]
    count = nblk_ref[e]
    w_hbm = (wg_hbm, wu_hbm, wd_hbm)
    w_buf = (wg_ref, wu_ref, wd_ref)

    @pl.when(e == 0)
    def _first_weights():
        for c in _weight_copies(0, 0, w_hbm, w_buf, wsem):
            c.start(priority=WEIGHT_DMA_PRIORITY)

    @pl.when(e + 1 < pl.num_programs(0))
    def _next_weights():
        for c in _weight_copies(e + 1, (e + 1) % 2, w_hbm, w_buf, wsem):
            c.start(priority=WEIGHT_DMA_PRIORITY)

    for c in _weight_copies(e, e % 2, w_hbm, w_buf, wsem):
        c.wait()

    @pl.when(count > 0)
    def _expert():
        wgb_ref[...] = wg_ref[e % 2].astype(BF16)
        wub_ref[...] = wu_ref[e % 2].astype(BF16)
        wdb_ref[...] = wd_ref[e % 2].astype(BF16)

        @pl.when(first == 0)
        def _first_rows():
            _gather_rows(h2_hbm, tok_ref, 0, 1, xb_ref.at[0], gsem.at[0], MOE_BLK)

        def block(t, carry):
            g = first + t
            cur = g % 2
            nxt = 1 - cur
            _wait_rows(h2_hbm, xb_ref.at[cur], gsem.at[cur], MOE_BLK)

            @pl.when(g >= 2)
            def _staging_free():
                _block_out_copy(ob_ref, cur, ys_hbm, g - 2, osem).wait()

            base = (g + 1) * MOE_BLK
            for r in range(MOE_BLK):
                _row_copy(h2_hbm, tok_ref[base + r], xb_ref.at[nxt], r, gsem.at[nxt]).start()
            kw = 2 * LANES
            hg = jnp.zeros((MOE_BLK, D_FF), F32)
            hu = jnp.zeros((MOE_BLK, D_FF), F32)
            for kc in range(D_MODEL // kw):
                xk = jnp.concatenate([xb_ref[cur, :, 2 * kc, :], xb_ref[cur, :, 2 * kc + 1, :]],
                                     axis=1).astype(BF16)
                hg = hg + jnp.dot(xk, wgb_ref[kc * kw:(kc + 1) * kw, :],
                                  preferred_element_type=F32)
                hu = hu + jnp.dot(xk, wub_ref[kc * kw:(kc + 1) * kw, :],
                                  preferred_element_type=F32)
            act = (jax.nn.silu(hg) * hu).astype(BF16)
            ob_ref[cur] = jnp.dot(act, wdb_ref[...], preferred_element_type=F32)
            _block_out_copy(ob_ref, cur, ys_hbm, g, osem).start()
            return carry

        lax.fori_loop(0, count, block, 0)

    @pl.when(e == pl.num_programs(0) - 1)
    def _finish():
        _wait_rows(h2_hbm, xb_ref.at[nb % 2], gsem.at[nb % 2], MOE_BLK)
        _block_out_copy(ob_ref, (nb - 1) % 2, ys_hbm, nb - 1, osem).wait()

        @pl.when(nb >= 2)
        def _():
            _block_out_copy(ob_ref, nb % 2, ys_hbm, nb - 2, osem).wait()

        ob_ref[0] = jnp.zeros(ob_ref.shape[1:], ob_ref.dtype)

        def zero_block(g, carry):
            _block_out_copy(ob_ref, 0, ys_hbm, g, osem).start()
            return carry

        lax.fori_loop(nb, n_blocks, zero_block, 0)

        def zero_wait(g, carry):
            _block_out_copy(ob_ref, 0, ys_hbm, g, osem).wait()
            return carry

        lax.fori_loop(nb, n_blocks, zero_wait, 0)


def _experts(h2, first_blk, n_blk, n_used, buf_tok, wg, wu, wd):
    s, nsl, _ = h2.shape
    d = nsl * LANES
    buf_len = buf_tok.shape[0]
    n_blocks = buf_len // MOE_BLK
    return pl.pallas_call(
        functools.partial(_expert_body, n_blocks=n_blocks),
        grid_spec=pltpu.PrefetchScalarGridSpec(
            num_scalar_prefetch=4,
            grid=(N_EXPERTS,),
            in_specs=[pl.BlockSpec(memory_space=pl.ANY)] * 4,
            out_specs=pl.BlockSpec(memory_space=pl.ANY),
            scratch_shapes=[pltpu.VMEM((2, MOE_BLK, nsl, LANES), F32),
                            pltpu.VMEM((2, MOE_BLK, d), F32),
                            pltpu.VMEM((2, d, D_FF), F32), pltpu.VMEM((2, d, D_FF), F32),
                            pltpu.VMEM((2, D_FF, d), F32),
                            pltpu.VMEM((d, D_FF), BF16), pltpu.VMEM((d, D_FF), BF16),
                            pltpu.VMEM((D_FF, d), BF16),
                            pltpu.SemaphoreType.DMA((2,)), pltpu.SemaphoreType.DMA((2,)),
                            pltpu.SemaphoreType.DMA((2,))]),
        out_shape=jax.ShapeDtypeStruct((buf_len, d), F32),
        compiler_params=_cparams(("arbitrary",)),
        name="experts",
    )(first_blk, n_blk, n_used, buf_tok, h2, wg, wu, wd)


def _combine_body(pos_ref, x1_ref, gate_ref, fw_ref, ys_hbm, o_ref, g_ref, sem, *, tm):
    i = pl.program_id(0)

    def gather(tile, slot):
        for k in range(TOP_K):
            _gather_rows(ys_hbm, pos_ref, TOP_K * tile * tm + k, TOP_K, g_ref.at[slot, k],
                         sem.at[slot], tm)

    @pl.when(i == 0)
    def _first():
        gather(0, 0)

    @pl.when(i + 1 < pl.num_programs(0))
    def _prefetch():
        gather(i + 1, (i + 1) % 2)

    cur = i % 2
    for k in range(TOP_K):
        _wait_rows(ys_hbm, g_ref.at[cur, k], sem.at[cur], tm)
    gate = gate_ref[...]
    x = x1_ref[...] + (gate[:, 0:1] * g_ref[cur, 0] + gate[:, 1:2] * g_ref[cur, 1])
    y = x * lax.rsqrt(jnp.mean(x * x, axis=-1, keepdims=True) + NORM_EPS)
    o_ref[...] = y * fw_ref[...]


def _combine(x1, ys, pos, gate, fw, tm=256):
    s, d = x1.shape
    return pl.pallas_call(
        functools.partial(_combine_body, tm=tm),
        grid_spec=pltpu.PrefetchScalarGridSpec(
            num_scalar_prefetch=1,
            grid=(s // tm,),
            in_specs=[pl.BlockSpec((tm, d), lambda i, p: (i, 0)),
                      pl.BlockSpec((tm, LANES), lambda i, p: (i, 0)),
                      pl.BlockSpec((1, d), lambda i, p: (0, 0)),
                      pl.BlockSpec(memory_space=pl.ANY)],
            out_specs=pl.BlockSpec((tm, d), lambda i, p: (i, 0)),
            scratch_shapes=[pltpu.VMEM((2, TOP_K, tm, d), F32),
                            pltpu.SemaphoreType.DMA((2,))]),
        out_shape=jax.ShapeDtypeStruct((s, d), F32),
        compiler_params=_cparams(("arbitrary",)),
        name="combine_norm",
    )(pos, x1, gate, fw.reshape(1, d).astype(F32), ys)


def _dispatch_tables(eid, counts):
    n_tok = eid.shape[0]
    n_slots = n_tok * TOP_K
    buf_len = n_slots + N_EXPERTS * MOE_BLK
    counts = counts.astype(jnp.int32)
    padded = ((counts + MOE_BLK - 1) // MOE_BLK) * MOE_BLK
    padded_ends = jnp.cumsum(padded)
    padded_starts = padded_ends - padded
    experts = eid[:, :TOP_K]
    onehot = experts[:, :, None] == jnp.arange(N_EXPERTS, dtype=jnp.int32)
    dest = jnp.sum(jnp.where(onehot, padded_starts, 0), axis=-1) + eid[:, TOP_K:2 * TOP_K]
    n_used = (padded_ends[-1] // MOE_BLK).astype(jnp.int32).reshape(1)
    first_blk = (padded_starts // MOE_BLK).astype(jnp.int32)
    n_blk = (padded // MOE_BLK).astype(jnp.int32)
    return first_blk, n_blk, n_used, dest.reshape(n_slots).astype(jnp.int32), buf_len


def kernel(x, positions, norm1_w, w_in, ssm_lambda_re, ssm_lambda_im, ssm_log_dt, ssm_b_re, ssm_b_im, ssm_c_re, ssm_c_im, ssm_d, ssm_glu_w, ssm_glu_b, attn_lambda_q1, attn_lambda_k1, attn_lambda_q2, attn_lambda_k2, attn_subln_w, w_proj_ssm, w_proj_attn, w_out, norm2_w, router_group_w, router_expert_w, expert_w_gate, expert_w_up, expert_w_down, final_norm_w):
    bsz, seq, d = x.shape
    depth = norm1_w.shape[0]
    xs = x.reshape(bsz * seq, d)
    pos = positions.reshape(bsz * seq)
    o_q = SSM_WIDTH
    o_v = o_q + 2 * ATTN_WIDTH
    o_g = o_v + ATTN_WIDTH
    for l in range(depth):
        h = _rmsnorm(xs, norm1_w[l].astype(F32), BF16)
        w_in_b = w_in[l].astype(F32)
        u = _mm(h, w_in_b, 0, SSM_WIDTH, "none", F32)
        qk = _mm_rope(h, w_in_b, pos, o_q)
        v = _mm_vones(h, w_in_b, o_v)
        gates = _mm(h, w_in_b, o_g, 2 * D_MODEL, "sigmoid", F32)

        tables = _s5_tables(ssm_lambda_re[l], ssm_lambda_im[l], ssm_log_dt[l],
                            ssm_b_re[l], ssm_b_im[l], ssm_c_re[l], ssm_c_im[l])
        y = _s5_core(u, tables, ssm_d[l])
        y_ssm = _glu(y, ssm_glu_w[l].astype(BF16), ssm_glu_b[l])

        lam_params = jnp.stack([attn_lambda_q1[l], attn_lambda_k1[l],
                                attn_lambda_q2[l], attn_lambda_k2[l]]).astype(F32)
        y_attn = _diff_attention(qk, v, lam_params, attn_subln_w[l])

        wr = jnp.concatenate([router_group_w[l], router_expert_w[l]], axis=1).astype(F32)
        wr = jnp.pad(wr, ((0, 0), (0, LANES - wr.shape[1])))
        x1, h2, eid, gate, counts = _merge_route(
            xs, y_ssm, y_attn, gates, w_proj_ssm[l].astype(BF16), w_proj_attn[l].astype(BF16),
            w_out[l].astype(BF16), norm2_w[l], wr)

        first_blk, n_blk, n_used, dest, buf_len = _dispatch_tables(eid, counts[0, :N_EXPERTS])
        buf_tok = _scatter_tok(dest, buf_len)
        ys = _experts(h2, first_blk, n_blk, n_used, buf_tok, expert_w_gate[l].astype(F32),
                      expert_w_up[l].astype(F32), expert_w_down[l].astype(F32))
        last = l == depth - 1
        assert last, "DEPTH > 1 needs an un-normalised combine"
        xs = _combine(x1, ys, dest, gate, final_norm_w)
    return xs.reshape(bsz, seq, d)
```

```python
import functools
import math

import jax
import jax.numpy as jnp
from jax import lax
from jax.experimental import pallas as pl
from jax.experimental.pallas import tpu as pltpu

F32 = jnp.float32
BF16 = jnp.bfloat16

D_MODEL = 2048
SSM_WIDTH = 1024
SSM_GROUP = 16
SSM_GROUPS = 64
SSM_STATE = 64
ATTN_WIDTH = 1024
ATTN_HEADS = 8
HEAD_DIM = 64
ROPE_THETA = 10000.0
N_GROUPS = 4
EXPERTS_PER_GROUP = 8
N_EXPERTS = 32
TOP_K = 2
TOP_K_LOG2 = 1
D_FF = 512
NORM_EPS = 1e-6
SUBLN_EPS = 1e-5
LAMBDA_INIT = 0.8 - 0.6 * math.exp(-0.3 * 0)

LANES = 128
VMEM_LIMIT = 48 * 1024 * 1024

SSM_CHUNK = 16
SSM_GB = 8
MOE_BLK = 128


def _cparams(sem):
    return pltpu.CompilerParams(dimension_semantics=sem, vmem_limit_bytes=VMEM_LIMIT)


def _rmsnorm_body(x_ref, w_ref, pos_ref, invf_ref, o_ref, cos_ref, sin_ref):
    x = x_ref[...]
    y = x * lax.rsqrt(jnp.mean(x * x, axis=-1, keepdims=True) + NORM_EPS)
    o_ref[...] = (y * w_ref[...]).astype(o_ref.dtype)
    ang = pos_ref[...].astype(F32) * invf_ref[...]
    lane = lax.broadcasted_iota(jnp.int32, ang.shape, 1)
    low = (lane % HEAD_DIM) < (HEAD_DIM // 2)
    cos_ref[...] = jnp.cos(ang)
    sin_ref[...] = jnp.where(low, -jnp.sin(ang), jnp.sin(ang))


def _rmsnorm_rope(x, w, positions, out_dtype, tm=512):
    s, d = x.shape
    inv_freq = 1.0 / (ROPE_THETA ** (jnp.arange(0, HEAD_DIM, 2, dtype=F32) / HEAD_DIM))
    invf = jnp.tile(inv_freq, LANES // (HEAD_DIM // 2)).reshape(1, LANES)
    return pl.pallas_call(
        _rmsnorm_body,
        grid=(s // tm,),
        in_specs=[pl.BlockSpec((tm, d), lambda i: (i, 0)),
                  pl.BlockSpec((1, d), lambda i: (0, 0)),
                  pl.BlockSpec((tm, 1), lambda i: (i, 0)),
                  pl.BlockSpec((1, LANES), lambda i: (0, 0))],
        out_specs=[pl.BlockSpec((tm, d), lambda i: (i, 0)),
                   pl.BlockSpec((tm, LANES), lambda i: (i, 0)),
                   pl.BlockSpec((tm, LANES), lambda i: (i, 0))],
        out_shape=[jax.ShapeDtypeStruct((s, d), out_dtype),
                   jax.ShapeDtypeStruct((s, LANES), F32),
                   jax.ShapeDtypeStruct((s, LANES), F32)],
        compiler_params=_cparams(("parallel",)),
        name="rmsnorm",
    )(x, w.reshape(1, d), positions.reshape(s, 1), invf)


def _proj(h_ref, w_ref, wb_ref, row_axis):
    @pl.when(pl.program_id(row_axis) == 0)
    def _cast():
        wb_ref[...] = w_ref[...].astype(BF16)

    return jnp.dot(h_ref[...], wb_ref[...], preferred_element_type=F32)


def _mm_body(h_ref, w_ref, o_ref, wb_ref, *, act):
    acc = _proj(h_ref, w_ref, wb_ref, 1)
    if act == "sigmoid":
        acc = jax.nn.sigmoid(acc)
    o_ref[...] = acc.astype(o_ref.dtype)


def _mm(h, w, col0, ncols, act, out_dtype, tm=512, tn=1024):
    s, k = h.shape
    jb = col0 // tn
    return pl.pallas_call(
        functools.partial(_mm_body, act=act),
        grid=(ncols // tn, s // tm),
        in_specs=[pl.BlockSpec((tm, k), lambda j, i: (i, 0)),
                  pl.BlockSpec((k, tn), lambda j, i: (0, j + jb))],
        out_specs=pl.BlockSpec((tm, tn), lambda j, i: (i, j)),
        out_shape=jax.ShapeDtypeStruct((s, ncols), out_dtype),
        scratch_shapes=[pltpu.VMEM((k, tn), BF16)],
        compiler_params=_cparams(("arbitrary", "arbitrary")),
        name="mm_" + act,
    )(h, w)


def _mm_vones_body(h_ref, w_ref, o_ref, wb_ref):
    acc = _proj(h_ref, w_ref, wb_ref, 0)
    hw = 2 * HEAD_DIM
    ones = jnp.ones((acc.shape[0], hw), o_ref.dtype)
    for hd in range(acc.shape[1] // hw):
        o_ref[:, 2 * hd * hw:(2 * hd + 1) * hw] = acc[:, hd * hw:(hd + 1) * hw].astype(o_ref.dtype)
        o_ref[:, (2 * hd + 1) * hw:(2 * hd + 2) * hw] = ones


def _mm_vones(h, w, col0, tm=512):
    s, k = h.shape
    tn = ATTN_WIDTH
    jb = col0 // tn
    return pl.pallas_call(
        _mm_vones_body,
        grid=(s // tm,),
        in_specs=[pl.BlockSpec((tm, k), lambda i: (i, 0)),
                  pl.BlockSpec((k, tn), lambda i: (0, jb))],
        out_specs=pl.BlockSpec((tm, 2 * tn), lambda i: (i, 0)),
        out_shape=jax.ShapeDtypeStruct((s, 2 * tn), BF16),
        scratch_shapes=[pltpu.VMEM((k, tn), BF16)],
        compiler_params=_cparams(("arbitrary",)),
        name="mm_vones",
    )(h, w)


def _mm_rope_body(h_ref, w_ref, cos_ref, sin_ref, o_ref, wb_ref, *, tn):
    j = pl.program_id(0)
    acc = _proj(h_ref, w_ref, wb_ref, 1)
    scale = jnp.where(j == 0, HEAD_DIM ** -0.5, 1.0).astype(F32)
    lane = lax.broadcasted_iota(jnp.int32, cos_ref.shape, 1)
    low = (lane % HEAD_DIM) < (HEAD_DIM // 2)
    cos = cos_ref[...] * scale
    sin = sin_ref[...] * scale
    for c in range(tn // LANES):
        t = acc[:, c * LANES:(c + 1) * LANES]
        partner = jnp.where(low, pltpu.roll(t, LANES - HEAD_DIM // 2, 1),
                            pltpu.roll(t, HEAD_DIM // 2, 1))
        o_ref[:, c * LANES:(c + 1) * LANES] = (t * cos + partner * sin).astype(o_ref.dtype)


def _mm_rope(h, w, cos, sin, col0, tm=512, tn=1024):
    s, k = h.shape
    jb = col0 // tn
    return pl.pallas_call(
        functools.partial(_mm_rope_body, tn=tn),
        grid=(2, s // tm),
        in_specs=[pl.BlockSpec((tm, k), lambda j, i: (i, 0)),
                  pl.BlockSpec((k, tn), lambda j, i: (0, j + jb)),
                  pl.BlockSpec((tm, LANES), lambda j, i: (i, 0)),
                  pl.BlockSpec((tm, LANES), lambda j, i: (i, 0))],
        out_specs=pl.BlockSpec((tm, tn), lambda j, i: (i, j)),
        out_shape=jax.ShapeDtypeStruct((s, 2 * tn), BF16),
        scratch_shapes=[pltpu.VMEM((k, tn), BF16)],
        compiler_params=_cparams(("arbitrary", "arbitrary")),
        name="mm_rope",
    )(h, w, cos, sin)


def _s5_tables(lam_re, lam_im, log_dt, b_re, b_im, c_re, c_im):
    L, G, P, M = SSM_CHUNK, SSM_GROUPS, SSM_STATE, SSM_GROUP
    hi = lax.Precision.HIGHEST
    lr, li = lam_re.astype(F32), lam_im.astype(F32)
    dt = jnp.exp(log_dt.astype(F32))[:, None]
    n = jnp.arange(L + 1, dtype=F32)[:, None, None]
    mag = jnp.exp(lr * dt * n)
    pr = mag * jnp.cos(li * dt * n)
    pi = mag * jnp.sin(li * dt * n)
    nr, ni = pr[1] - 1.0, pi[1]
    den = lr * lr + li * li
    f_re = (nr * lr + ni * li) / den
    f_im = (ni * lr - nr * li) / den
    br, bi = b_re.astype(F32), b_im.astype(F32)
    bbr = f_re[..., None] * br - f_im[..., None] * bi
    bbi = f_re[..., None] * bi + f_im[..., None] * br
    cr, ci = c_re.astype(F32), c_im.astype(F32)

    nb, gb = G // SSM_GB, SSM_GB
    pr_g = jnp.transpose(pr, (1, 0, 2))
    pi_g = jnp.transpose(pi, (1, 0, 2))
    bbr_t = jnp.transpose(bbr, (0, 2, 1))
    bbi_t = jnp.transpose(bbi, (0, 2, 1))

    abr = pr_g[:, :L, None, :] * bbr_t[:, None] - pi_g[:, :L, None, :] * bbi_t[:, None]
    abi = pr_g[:, :L, None, :] * bbi_t[:, None] + pi_g[:, :L, None, :] * bbr_t[:, None]
    kmat = (jnp.einsum('gnip,gop->gnio', abr, cr, precision=hi)
            - jnp.einsum('gnip,gop->gnio', abi, ci, precision=hi))
    kpad = jnp.pad(kmat, ((0, 0), (1, 0), (0, 0), (0, 0)))
    kq = jnp.stack([jnp.stack([kpad[:, 1 + c2 - r2::2][:, :L // 2] for c2 in range(2)], axis=3)
                    for r2 in range(2)], axis=2)
    kk_src = jnp.transpose(kq.reshape(nb, gb, L // 2, 2, M, 2, M),
                           (0, 2, 3, 5, 4, 1, 6)).reshape(nb, L // 2, 2, 2, M, gb * M)

    ab1r = pr_g[:, 1, None, :] * bbr_t - pi_g[:, 1, None, :] * bbi_t
    ab1i = pr_g[:, 1, None, :] * bbi_t + pi_g[:, 1, None, :] * bbr_t
    b4 = jnp.stack([jnp.stack([ab1r, ab1i], axis=2), jnp.stack([bbr_t, bbi_t], axis=2)],
                   axis=1)
    bb_src = jnp.transpose(b4.reshape(nb, gb, 2, M, 2, P),
                           (0, 2, 4, 3, 1, 5)).reshape(nb, 2, 2, M, gb * P)

    cr_t = jnp.transpose(cr, (0, 2, 1))
    ci_t = jnp.transpose(ci, (0, 2, 1))
    car = cr_t * pr_g[:, 1, :, None] - ci_t * pi_g[:, 1, :, None]
    cai = cr_t * pi_g[:, 1, :, None] + ci_t * pr_g[:, 1, :, None]
    c4 = jnp.stack([jnp.stack([cr_t, car], axis=2), jnp.stack([-ci_t, -cai], axis=2)],
                   axis=1)
    cc_src = jnp.transpose(c4.reshape(nb, gb, 2, P, 2, M),
                           (0, 2, 4, 3, 1, 5)).reshape(nb, 2, 2, P, gb * M)

    ap = jnp.stack([pr.reshape(L + 1, nb, gb * P), pi.reshape(L + 1, nb, gb * P)], axis=2)
    return kk_src, bb_src, cc_src, ap.reshape(L + 1, 2 * G * P)


def _cmul(ar, ai, zr, zi):
    return ar * zr - ai * zi, ar * zi + ai * zr


def _block_diag(src, rows_per_group, cols_per_group):
    t = jnp.concatenate([src] * SSM_GB, axis=0)
    row = lax.broadcasted_iota(jnp.int32, t.shape, 0)
    col = lax.broadcasted_iota(jnp.int32, t.shape, 1)
    return jnp.where(row // rows_per_group == col // cols_per_group, t, 0.0).astype(BF16)


def _s5_body(u_ref, kk_src, bb_src, cc_src, ap_ref, d_ref, y_ref,
             kk_ref, bb_ref, cc_ref, lhs_ref, er_ref, ei_ref, xr_ref, xi_ref):
    L, M, P = SSM_CHUNK, SSM_GROUP, SSM_STATE
    nc = u_ref.shape[0] // L
    hp = SSM_GB * P
    gm = SSM_GB * M
    nq = L // 2
    for a in range(2):
        for b in range(2):
            for dd in range(nq):
                kk_ref[dd, a * gm:(a + 1) * gm, b * gm:(b + 1) * gm] = _block_diag(
                    kk_src[0, dd, a, b], M, M)
            bb_ref[a * gm:(a + 1) * gm, b * hp:(b + 1) * hp] = _block_diag(bb_src[0, a, b], M, P)
            cc_ref[a * hp:(a + 1) * hp, b * gm:(b + 1) * gm] = _block_diag(cc_src[0, a, b], P, M)
    for q in range(nq):
        lhs_ref[q] = jnp.concatenate(
            [u_ref[pl.ds(2 * q, nc, stride=L), :], u_ref[pl.ds(2 * q + 1, nc, stride=L), :]],
            axis=1).astype(BF16)

    er = jnp.zeros((nc, hp), F32)
    ei = jnp.zeros((nc, hp), F32)
    for q in range(nq):
        z = jnp.dot(lhs_ref[q], bb_ref[...], preferred_element_type=F32)
        n = L - 2 - 2 * q
        dr, di = _cmul(ap_ref[n:n + 1, :hp], ap_ref[n:n + 1, hp:], z[:, :hp], z[:, hp:])
        er = er + dr
        ei = ei + di
    er_ref[...] = er
    ei_ref[...] = ei

    ar = ap_ref[L:L + 1, :hp]
    ai = ap_ref[L:L + 1, hp:]

    def step(c, carry):
        xr, xi = carry
        xr_ref[pl.ds(c, 1), :] = xr
        xi_ref[pl.ds(c, 1), :] = xi
        nr, ni = _cmul(ar, ai, xr, xi)
        return nr + er_ref[pl.ds(c, 1), :], ni + ei_ref[pl.ds(c, 1), :]

    zero = jnp.zeros((1, hp), F32)
    lax.fori_loop(0, nc, step, (zero, zero))

    for qq in range(nq):
        n = 2 * qq + 1
        wr, wi = _cmul(ap_ref[n:n + 1, :hp], ap_ref[n:n + 1, hp:], xr_ref[...], xi_ref[...])
        w = jnp.concatenate([wr, wi], axis=1).astype(BF16)
        acc = jnp.dot(w, cc_ref[...], preferred_element_type=F32)
        for q in range(qq + 1):
            acc = acc + jnp.dot(lhs_ref[q], kk_ref[qq - q], preferred_element_type=F32)
        for r in range(2):
            j = 2 * qq + r
            y = acc[:, r * LANES:(r + 1) * LANES] + d_ref[...] * u_ref[pl.ds(j, nc, stride=L), :]
            y_ref[pl.ds(j, nc, stride=L), :] = jax.nn.gelu(y)


def _s5_core(u, tables, d_skip):
    s, width = u.shape
    L, P, M, gb = SSM_CHUNK, SSM_STATE, SSM_GROUP, SSM_GB
    nc = s // L
    nb = width // (gb * M)
    kk_src, bb_src, cc_src, ap = tables
    return pl.pallas_call(
        _s5_body,
        grid=(nb,),
        in_specs=[pl.BlockSpec((s, gb * M), lambda b: (0, b)),
                  pl.BlockSpec((1, L // 2, 2, 2, M, gb * M), lambda b: (b, 0, 0, 0, 0, 0)),
                  pl.BlockSpec((1, 2, 2, M, gb * P), lambda b: (b, 0, 0, 0, 0)),
                  pl.BlockSpec((1, 2, 2, P, gb * M), lambda b: (b, 0, 0, 0, 0)),
                  pl.BlockSpec((L + 1, 2 * gb * P), lambda b: (0, b)),
                  pl.BlockSpec((1, gb * M), lambda b: (0, b))],
        out_specs=pl.BlockSpec((s, gb * M), lambda b: (0, b)),
        out_shape=jax.ShapeDtypeStruct((s, width), F32),
        scratch_shapes=[pltpu.VMEM((L // 2, 2 * gb * M, 2 * gb * M), BF16),
                        pltpu.VMEM((2 * gb * M, 2 * gb * P), BF16),
                        pltpu.VMEM((2 * gb * P, 2 * gb * M), BF16),
                        pltpu.VMEM((L // 2, nc, 2 * gb * M), BF16)]
        + [pltpu.VMEM((nc, gb * P), F32) for _ in range(4)],
        compiler_params=_cparams(("parallel",)),
        name="s5_scan",
    )(u, kk_src, bb_src, cc_src, ap, d_skip.astype(F32).reshape(1, width))


def _glu_body(y_ref, w_ref, b_ref, o_ref):
    y = y_ref[...]
    z = jnp.dot(y.astype(BF16), w_ref[...], preferred_element_type=F32) + b_ref[...]
    o_ref[...] = (y * jax.nn.sigmoid(z)).astype(o_ref.dtype)


def _glu(y, w, b, tm=512):
    s, n = y.shape
    return pl.pallas_call(
        _glu_body,
        grid=(s // tm,),
        in_specs=[pl.BlockSpec((tm, n), lambda i: (i, 0)),
                  pl.BlockSpec((n, n), lambda i: (0, 0)),
                  pl.BlockSpec((1, n), lambda i: (0, 0))],
        out_specs=pl.BlockSpec((tm, n), lambda i: (i, 0)),
        out_shape=jax.ShapeDtypeStruct((s, n), BF16),
        compiler_params=_cparams(("parallel",)),
        name="s5_glu",
    )(y, w, b.reshape(1, n).astype(F32))


def _attn_body(q_ref, k_ref, v_ref, lam_ref, sw_ref, o_ref,
               qs_ref, m_ref, acc_ref, s_ref, p_ref, al_ref, *, tq):
    qi = pl.program_id(1)
    hw = 2 * HEAD_DIM
    q = q_ref[...]
    lane = lax.broadcasted_iota(jnp.int32, q.shape, 1)
    zero = jnp.zeros_like(q)
    qs_ref[0] = jnp.where(lane < HEAD_DIM, q, zero)
    qs_ref[1] = jnp.where(lane >= HEAD_DIM, q, zero)
    m_ref[...] = jnp.full(m_ref.shape, -jnp.inf, F32)
    acc_ref[...] = jnp.zeros(acc_ref.shape, F32)

    def scores(c, slot, masked=False):
        off = pl.multiple_of(c * tq, tq)
        k = k_ref[pl.ds(off, tq), :]
        for mp in range(2):
            s = lax.dot_general(qs_ref[mp], k, (((1,), (1,)), ((), ())),
                                preferred_element_type=F32)
            if masked:
                row = lax.broadcasted_iota(jnp.int32, s.shape, 0)
                col = lax.broadcasted_iota(jnp.int32, s.shape, 1)
                s = jnp.where(col <= row, s, -jnp.inf)
            s_ref[slot, mp] = s

    def softmax(slot):
        for mp in range(2):
            s = s_ref[slot, mp]
            m_old = m_ref[mp]
            m_new = jnp.maximum(m_old, jnp.max(s, axis=-1, keepdims=True))
            al_ref[slot, mp] = jnp.exp(m_old - m_new)
            p_ref[slot, mp] = jnp.exp(s - m_new).astype(BF16)
            m_ref[mp] = m_new

    def pv(c, slot):
        off = pl.multiple_of(c * tq, tq)
        v = v_ref[pl.ds(off, tq), :]
        for mp in range(2):
            acc_ref[mp] = al_ref[slot, mp] * acc_ref[mp] + jnp.dot(
                p_ref[slot, mp], v, preferred_element_type=F32)

    @pl.when(qi == 0)
    def _one_chunk():
        scores(0, 0, masked=True)
        softmax(0)
        pv(0, 0)

    @pl.when(qi == 1)
    def _two_chunks():
        scores(0, 1)
        softmax(1)
        pv(0, 1)
        scores(1, 0, masked=True)
        softmax(0)
        pv(1, 0)

    @pl.when(qi >= 2)
    def _pipelined():
        odd = qi % 2

        @pl.when(odd == 0)
        def _fill_even():
            scores(0, 0)
            scores(1, 1)
            softmax(0)

        @pl.when(odd == 1)
        def _fill_odd():
            scores(0, 1)
            scores(1, 0)
            softmax(1)

        def odd_step(_, carry):
            scores(2, 1)
            softmax(0)
            pv(0, 1)
            return carry

        lax.fori_loop(0, odd, odd_step, 0)
        t0 = 2 + odd

        def two_steps(u, carry):
            t = t0 + 2 * u
            scores(t, 0)
            softmax(1)
            pv(t - 2, 0)
            scores(t + 1, 1)
            softmax(0)
            pv(t - 1, 1)
            return carry

        lax.fori_loop(0, (qi - t0) // 2, two_steps, 0)
        scores(qi, 0, masked=True)
        softmax(1)
        pv(qi - 2, 0)
        softmax(0)
        pv(qi - 1, 1)
        pv(qi, 0)

    lam = (jnp.exp(jnp.sum(lam_ref[0:1, :] * lam_ref[1:2, :]))
           - jnp.exp(jnp.sum(lam_ref[2:3, :] * lam_ref[3:4, :])) + LAMBDA_INIT)
    a1 = acc_ref[0]
    a2 = acc_ref[1]
    o = a1[:, :hw] / a1[:, hw:] - lam * (a2[:, :hw] / a2[:, hw:])
    o = o * lax.rsqrt(jnp.mean(o * o, axis=-1, keepdims=True) + SUBLN_EPS)
    o_ref[...] = (o * sw_ref[...] * (1.0 - LAMBDA_INIT)).astype(o_ref.dtype)


def _diff_attention(qk, v1, lam_params, subln_w, tq=512):
    s = qk.shape[0]
    nq = s // tq
    hw = 2 * HEAD_DIM
    kcol0 = ATTN_WIDTH // hw
    return pl.pallas_call(
        functools.partial(_attn_body, tq=tq),
        grid=(ATTN_HEADS, nq),
        in_specs=[pl.BlockSpec((tq, hw), lambda h, i: (i, h)),
                  pl.BlockSpec((s, hw), lambda h, i: (0, kcol0 + h)),
                  pl.BlockSpec((s, 2 * hw), lambda h, i: (0, h)),
                  pl.BlockSpec((4, HEAD_DIM), lambda h, i: (0, 0)),
                  pl.BlockSpec((1, hw), lambda h, i: (0, 0))],
        out_specs=pl.BlockSpec((tq, hw), lambda h, i: (i, h)),
        out_shape=jax.ShapeDtypeStruct((s, ATTN_WIDTH), BF16),
        scratch_shapes=[pltpu.VMEM((2, tq, hw), BF16),
                        pltpu.VMEM((2, tq, 1), F32),
                        pltpu.VMEM((2, tq, 2 * hw), F32),
                        pltpu.VMEM((2, 2, tq, tq), F32),
                        pltpu.VMEM((2, 2, tq, tq), BF16),
                        pltpu.VMEM((2, 2, tq, 1), F32)],
        compiler_params=_cparams(("parallel", "arbitrary")),
        name="diff_attn",
    )(qk, qk, v1, lam_params, subln_w.reshape(1, hw).astype(F32))


def _merge_body(x_ref, ys_ref, ya_ref, gs_ref, ga_ref, wps_ref, wpa_ref, wo_ref, n2_ref, wr_ref,
                x1_ref, h2_ref, eid_ref, gate_ref, cnt_ref):
    ps = jnp.dot(ys_ref[...], wps_ref[...], preferred_element_type=F32)
    pa = jnp.dot(ya_ref[...], wpa_ref[...], preferred_element_type=F32)
    merged = gs_ref[...].astype(F32) * ps + ga_ref[...].astype(F32) * pa
    x1 = x_ref[...] + jnp.dot(merged.astype(BF16), wo_ref[...], preferred_element_type=F32)
    x1_ref[...] = x1
    h2 = x1 * lax.rsqrt(jnp.mean(x1 * x1, axis=-1, keepdims=True) + NORM_EPS) * n2_ref[...]
    h2_ref[...] = h2
    wr = wr_ref[...]
    h_hi = h2.astype(BF16)
    h_lo = (h2 - h_hi.astype(F32)).astype(BF16)
    w_hi = wr.astype(BF16)
    w_lo = (wr - w_hi.astype(F32)).astype(BF16)
    tm = h2.shape[0]
    parts = jnp.dot(jnp.concatenate([h_hi, h_lo], axis=0), jnp.concatenate([w_hi, w_lo], axis=1),
                    preferred_element_type=F32)
    logits = (parts[:tm, :LANES] + parts[:tm, LANES:]) + (parts[tm:, :LANES] + parts[tm:, LANES:])
    lane = lax.broadcasted_iota(jnp.int32, logits.shape, 1)
    big = jnp.int32(1 << 20)
    ninf = jnp.float32(-jnp.inf)
    is_g = lane < N_GROUPS
    gl = jnp.where(is_g, logits, ninf)
    gm = jnp.max(gl, axis=-1, keepdims=True)
    g_idx = jnp.min(jnp.where(gl == gm, lane, big), axis=-1, keepdims=True)
    g_val = 1.0 / jnp.sum(jnp.where(is_g, jnp.exp(gl - gm), 0.0), axis=-1, keepdims=True)
    lo = N_GROUPS + g_idx * EXPERTS_PER_GROUP
    in_grp = (lane >= lo) & (lane < lo + EXPERTS_PER_GROUP)
    el = jnp.where(in_grp, logits, ninf)
    e1 = jnp.max(el, axis=-1, keepdims=True)
    i1 = jnp.min(jnp.where(el == e1, lane, big), axis=-1, keepdims=True)
    el2 = jnp.where(lane == i1, ninf, el)
    e2 = jnp.max(el2, axis=-1, keepdims=True)
    i2 = jnp.min(jnp.where(el2 == e2, lane, big), axis=-1, keepdims=True)
    t = jnp.exp(e2 - e1)
    p1 = 1.0 / (1.0 + t)
    p2 = t / (1.0 + t)
    gate_ref[...] = jnp.where(lane == 0, p1 * g_val, jnp.where(lane == 1, p2 * g_val, 0.0))

    @pl.when(pl.program_id(0) == 0)
    def _zero_counts():
        cnt_ref[...] = jnp.zeros(cnt_ref.shape, F32)

    ex1 = i1 - N_GROUPS
    ex2 = i2 - N_GROUPS
    oh1 = lane == ex1
    oh2 = lane == ex2
    onehot = jnp.where(oh1 | oh2, 1.0, 0.0)
    r_i =lax.broadcasted_iota(jnp.int32, (tm, tm), 0)
    c_i = lax.broadcasted_iota(jnp.int32, (tm, tm), 1)
    tri = jnp.where(c_i < r_i, 1.0, 0.0).astype(BF16)
    before = jnp.dot(tri, onehot.astype(BF16), preferred_element_type=F32) + cnt_ref[...]
    rank1 = jnp.sum(jnp.where(oh1, before, 0.0), axis=-1, keepdims=True).astype(jnp.int32)
    rank2 = jnp.sum(jnp.where(oh2, before, 0.0), axis=-1, keepdims=True).astype(jnp.int32)
    cnt_ref[...] = cnt_ref[...] + jnp.sum(onehot, axis=0, keepdims=True)
    eid_ref[...] = jnp.where(lane == 0, ex1, jnp.where(lane == 1, ex2,
                             jnp.where(lane == 2, rank1, jnp.where(lane == 3, rank2, 0))))


def _merge_route(x, y_ssm, y_attn, gates, wps, wpa, wo, n2w, wr, tm=256):
    s, d = x.shape
    nw = y_ssm.shape[1]
    const = lambda i: (0, 0)
    return pl.pallas_call(
        _merge_body,
        grid=(s // tm,),
        in_specs=[pl.BlockSpec((tm, d), lambda i: (i, 0)),
                  pl.BlockSpec((tm, nw), lambda i: (i, 0)),
                  pl.BlockSpec((tm, nw), lambda i: (i, 0)),
                  pl.BlockSpec((tm, d), lambda i: (i, 0)),
                  pl.BlockSpec((tm, d), lambda i: (i, 1)),
                  pl.BlockSpec((nw, d), const),
                  pl.BlockSpec((nw, d), const),
                  pl.BlockSpec((d, d), const),
                  pl.BlockSpec((1, d), const),
                  pl.BlockSpec((d, LANES), const)],
        out_specs=[pl.BlockSpec((tm, d), lambda i: (i, 0)),
                   pl.BlockSpec((tm, d), lambda i: (i, 0)),
                   pl.BlockSpec((tm, LANES), lambda i: (i, 0)),
                   pl.BlockSpec((tm, LANES), lambda i: (i, 0)),
                   pl.BlockSpec((1, LANES), const)],
        out_shape=[jax.ShapeDtypeStruct((s, d), F32),
                   jax.ShapeDtypeStruct((s, d), F32),
                   jax.ShapeDtypeStruct((s, LANES), jnp.int32),
                   jax.ShapeDtypeStruct((s, LANES), F32),
                   jax.ShapeDtypeStruct((1, LANES), F32)],
        compiler_params=_cparams(("arbitrary",)),
        name="merge_route",
    )(x, y_ssm, y_attn, gates, gates, wps, wpa, wo, n2w.reshape(1, d).astype(F32), wr)


def _row_copy(src_hbm, row, dst_ref, r, sem):
    return pltpu.make_async_copy(src_hbm.at[pl.ds(row, 1), :], dst_ref.at[pl.ds(r, 1), :], sem)


def _gather_rows(src_hbm, idx_ref, base, stride, dst_ref, sem, n):
    def start(r, c):
        _row_copy(src_hbm, idx_ref[base + stride * r], dst_ref, r, sem).start()
        return c

    lax.fori_loop(0, n, start, 0, unroll=8)


def _wait_rows(src_hbm, dst_ref, sem, n):
    pltpu.make_async_copy(src_hbm.at[pl.ds(0, n), :], dst_ref, sem).wait()


def _scatter_tok_body(dest_ref, tok_ref):
    def zero(i, c):
        tok_ref[i] = 0
        return c

    lax.fori_loop(0, tok_ref.shape[0], zero, 0, unroll=8)

    def put(i, c):
        tok_ref[dest_ref[i]] = lax.shift_right_logical(i, TOP_K_LOG2)
        return c

    lax.fori_loop(0, dest_ref.shape[0], put, 0, unroll=8)


def _scatter_tok(dest, buf_len):
    return pl.pallas_call(
        _scatter_tok_body,
        in_specs=[pl.BlockSpec(memory_space=pltpu.SMEM)],
        out_specs=pl.BlockSpec(memory_space=pltpu.SMEM),
        out_shape=jax.ShapeDtypeStruct((buf_len,), jnp.int32),
        name="scatter_tok",
    )(dest)


def _block_out_copy(ob_ref, slot, ys_hbm, blk, sem):
    return pltpu.make_async_copy(ob_ref.at[slot], ys_hbm.at[pl.ds(blk * MOE_BLK, MOE_BLK), :],
                                 sem.at[slot])


WEIGHT_DMA_PRIORITY = 1


def _weight_copies(e, slot, w_hbm, w_buf, wsem):
    return [pltpu.make_async_copy(h.at[e], b.at[slot], wsem.at[slot])
            for h, b in zip(w_hbm, w_buf)]


def _expert_body(fb_ref, nblk_ref, nb_ref, tok_ref, h2_hbm, wg_hbm, wu_hbm, wd_hbm, ys_hbm,
                 xb_ref, ob_ref, wg_ref, wu_ref, wd_ref, wgb_ref, wub_ref, wdb_ref,
                 gsem, osem, wsem, *, n_blocks):
    e = pl.program_id(0)
    nb = nb_ref[0]
    first = fb_ref[e]
    count = nblk_ref[e]
    w_hbm = (wg_hbm, wu_hbm, wd_hbm)
    w_buf = (wg_ref, wu_ref, wd_ref)

    @pl.when(e == 0)
    def _first_weights():
        for c in _weight_copies(0, 0, w_hbm, w_buf, wsem):
            c.start(priority=WEIGHT_DMA_PRIORITY)

    @pl.when(e + 1 < pl.num_programs(0))
    def _next_weights():
        for c in _weight_copies(e + 1, (e + 1) % 2, w_hbm, w_buf, wsem):
            c.start(priority=WEIGHT_DMA_PRIORITY)

    for c in _weight_copies(e, e % 2, w_hbm, w_buf, wsem):
        c.wait()

    @pl.when(count > 0)
    def _expert():
        wgb_ref[...] = wg_ref[e % 2].astype(BF16)
        wub_ref[...] = wu_ref[e % 2].astype(BF16)
        wdb_ref[...] = wd_ref[e % 2].astype(BF16)

        @pl.when(first == 0)
        def _first_rows():
            _gather_rows(h2_hbm, tok_ref, 0, 1, xb_ref.at[0], gsem.at[0], MOE_BLK)

        def block(t, carry):
            g = first + t
            cur = g % 2
            nxt = 1 - cur
            _wait_rows(h2_hbm, xb_ref.at[cur], gsem.at[cur], MOE_BLK)

            @pl.when(g >= 2)
            def _staging_free():
                _block_out_copy(ob_ref, cur, ys_hbm, g - 2, osem).wait()

            base = (g + 1) * MOE_BLK
            for r in range(MOE_BLK):
                _row_copy(h2_hbm, tok_ref[base + r], xb_ref.at[nxt], r, gsem.at[nxt]).start()
            xb = xb_ref[cur].astype(BF16)
            hg = jnp.dot(xb, wgb_ref[...], preferred_element_type=F32)
            hu = jnp.dot(xb, wub_ref[...], preferred_element_type=F32)
            act = (jax.nn.silu(hg) * hu).astype(BF16)
            ob_ref[cur] = jnp.dot(act, wdb_ref[...], preferred_element_type=F32)
            _block_out_copy(ob_ref, cur, ys_hbm, g, osem).start()
            return carry

        lax.fori_loop(0, count, block, 0)

    @pl.when(e == pl.num_programs(0) - 1)
    def _finish():
        _wait_rows(h2_hbm, xb_ref.at[nb % 2], gsem.at[nb % 2], MOE_BLK)
        _block_out_copy(ob_ref, (nb - 1) % 2, ys_hbm, nb - 1, osem).wait()

        @pl.when(nb >= 2)
        def _():
            _block_out_copy(ob_ref, nb % 2, ys_hbm, nb - 2, osem).wait()

        ob_ref[0] = jnp.zeros(ob_ref.shape[1:], ob_ref.dtype)

        def zero_block(g, carry):
            _block_out_copy(ob_ref, 0, ys_hbm, g, osem).start()
            return carry

        lax.fori_loop(nb, n_blocks, zero_block, 0)

        def zero_wait(g, carry):
            _block_out_copy(ob_ref, 0, ys_hbm, g, osem).wait()
            return carry

        lax.fori_loop(nb, n_blocks, zero_wait, 0)


def _experts(h2, first_blk, n_blk, n_used, buf_tok, wg, wu, wd):
    s, d = h2.shape
    buf_len = buf_tok.shape[0]
    n_blocks = buf_len // MOE_BLK
    return pl.pallas_call(
        functools.partial(_expert_body, n_blocks=n_blocks),
        grid_spec=pltpu.PrefetchScalarGridSpec(
            num_scalar_prefetch=4,
            grid=(N_EXPERTS,),
            in_specs=[pl.BlockSpec(memory_space=pl.ANY)] * 4,
            out_specs=pl.BlockSpec(memory_space=pl.ANY),
            scratch_shapes=[pltpu.VMEM((2, MOE_BLK, d), F32), pltpu.VMEM((2, MOE_BLK, d), F32),
                            pltpu.VMEM((2, d, D_FF), F32), pltpu.VMEM((2, d, D_FF), F32),
                            pltpu.VMEM((2, D_FF, d), F32),
                            pltpu.VMEM((d, D_FF), BF16), pltpu.VMEM((d, D_FF), BF16),
                            pltpu.VMEM((D_FF, d), BF16),
                            pltpu.SemaphoreType.DMA((2,)), pltpu.SemaphoreType.DMA((2,)),
                            pltpu.SemaphoreType.DMA((2,))]),
        out_shape=jax.ShapeDtypeStruct((buf_len, d), F32),
        compiler_params=_cparams(("arbitrary",)),
        name="experts",
    )(first_blk, n_blk, n_used, buf_tok, h2, wg, wu, wd)


def _combine_body(pos_ref, x1_ref, gate_ref, fw_ref, ys_hbm, o_ref, g_ref, sem, *, tm):
    i = pl.program_id(0)

    def gather(tile, slot):
        for k in range(TOP_K):
            _gather_rows(ys_hbm, pos_ref, TOP_K * tile * tm + k, TOP_K, g_ref.at[slot, k],
                         sem.at[slot], tm)

    @pl.when(i == 0)
    def _first():
        gather(0, 0)

    @pl.when(i + 1 < pl.num_programs(0))
    def _prefetch():
        gather(i + 1, (i + 1) % 2)

    cur = i % 2
    for k in range(TOP_K):
        _wait_rows(ys_hbm, g_ref.at[cur, k], sem.at[cur], tm)
    gate = gate_ref[...]
    x = x1_ref[...] + (gate[:, 0:1] * g_ref[cur, 0] + gate[:, 1:2] * g_ref[cur, 1])
    y = x * lax.rsqrt(jnp.mean(x * x, axis=-1, keepdims=True) + NORM_EPS)
    o_ref[...] = y * fw_ref[...]


def _combine(x1, ys, pos, gate, fw, tm=256):
    s, d = x1.shape
    return pl.pallas_call(
        functools.partial(_combine_body, tm=tm),
        grid_spec=pltpu.PrefetchScalarGridSpec(
            num_scalar_prefetch=1,
            grid=(s // tm,),
            in_specs=[pl.BlockSpec((tm, d), lambda i, p: (i, 0)),
                      pl.BlockSpec((tm, LANES), lambda i, p: (i, 0)),
                      pl.BlockSpec((1, d), lambda i, p: (0, 0)),
                      pl.BlockSpec(memory_space=pl.ANY)],
            out_specs=pl.BlockSpec((tm, d), lambda i, p: (i, 0)),
            scratch_shapes=[pltpu.VMEM((2, TOP_K, tm, d), F32),
                            pltpu.SemaphoreType.DMA((2,))]),
        out_shape=jax.ShapeDtypeStruct((s, d), F32),
        compiler_params=_cparams(("arbitrary",)),
        name="combine_norm",
    )(pos, x1, gate, fw.reshape(1, d).astype(F32), ys)


def _dispatch_tables(eid, counts):
    n_tok = eid.shape[0]
    n_slots = n_tok * TOP_K
    buf_len = n_slots + N_EXPERTS * MOE_BLK
    counts = counts.astype(jnp.int32)
    padded = ((counts + MOE_BLK - 1) // MOE_BLK) * MOE_BLK
    padded_ends = jnp.cumsum(padded)
    padded_starts = padded_ends - padded
    experts = eid[:, :TOP_K]
    onehot = experts[:, :, None] == jnp.arange(N_EXPERTS, dtype=jnp.int32)
    dest = jnp.sum(jnp.where(onehot, padded_starts, 0), axis=-1) + eid[:, TOP_K:2 * TOP_K]
    n_used = (padded_ends[-1] // MOE_BLK).astype(jnp.int32).reshape(1)
    first_blk = (padded_starts // MOE_BLK).astype(jnp.int32)
    n_blk = (padded // MOE_BLK).astype(jnp.int32)
    return first_blk, n_blk, n_used, dest.reshape(n_slots).astype(jnp.int32), buf_len


def kernel(x, positions, norm1_w, w_in, ssm_lambda_re, ssm_lambda_im, ssm_log_dt, ssm_b_re, ssm_b_im, ssm_c_re, ssm_c_im, ssm_d, ssm_glu_w, ssm_glu_b, attn_lambda_q1, attn_lambda_k1, attn_lambda_q2, attn_lambda_k2, attn_subln_w, w_proj_ssm, w_proj_attn, w_out, norm2_w, router_group_w, router_expert_w, expert_w_gate, expert_w_up, expert_w_down, final_norm_w):
    bsz, seq, d = x.shape
    depth = norm1_w.shape[0]
    xs = x.reshape(bsz * seq, d)
    pos = positions.reshape(bsz * seq)
    o_q = SSM_WIDTH
    o_v = o_q + 2 * ATTN_WIDTH
    o_g = o_v + ATTN_WIDTH
    for l in range(depth):
        h, rope_cos, rope_sin = _rmsnorm_rope(xs, norm1_w[l].astype(F32), pos, BF16)
        w_in_b = w_in[l].astype(F32)
        u = _mm(h, w_in_b, 0, SSM_WIDTH, "none", F32)
        qk = _mm_rope(h, w_in_b, rope_cos, rope_sin, o_q)
        v = _mm_vones(h, w_in_b, o_v)
        gates = _mm(h, w_in_b, o_g, 2 * D_MODEL, "sigmoid", F32)

        tables = _s5_tables(ssm_lambda_re[l], ssm_lambda_im[l], ssm_log_dt[l],
                            ssm_b_re[l], ssm_b_im[l], ssm_c_re[l], ssm_c_im[l])
        y = _s5_core(u, tables, ssm_d[l])
        y_ssm = _glu(y, ssm_glu_w[l].astype(BF16), ssm_glu_b[l])

        lam_params = jnp.stack([attn_lambda_q1[l], attn_lambda_k1[l],
                                attn_lambda_q2[l], attn_lambda_k2[l]]).astype(F32)
        y_attn = _diff_attention(qk, v, lam_params, attn_subln_w[l])

        wr = jnp.concatenate([router_group_w[l], router_expert_w[l]], axis=1).astype(F32)
        wr = jnp.pad(wr, ((0, 0), (0, LANES - wr.shape[1])))
        x1, h2, eid, gate, counts = _merge_route(
            xs, y_ssm, y_attn, gates, w_proj_ssm[l].astype(BF16), w_proj_attn[l].astype(BF16),
            w_out[l].astype(BF16), norm2_w[l], wr)

        first_blk, n_blk, n_used, dest, buf_len = _dispatch_tables(eid, counts[0, :N_EXPERTS])
        buf_tok = _scatter_tok(dest, buf_len)
        ys = _experts(h2, first_blk, n_blk, n_used, buf_tok, expert_w_gate[l].astype(F32),
                      expert_w_up[l].astype(F32), expert_w_down[l].astype(F32))
        last = l == depth - 1
        assert last, "DEPTH > 1 needs an un-normalised combine"
        xs = _combine(x1, ys, dest, gate, final_norm_w)
    return xs.reshape(bsz, seq, d)
```

```python
import functools
import math

import jax
import jax.numpy as jnp
from jax import lax
from jax.experimental import pallas as pl
from jax.experimental.pallas import tpu as pltpu

F32 = jnp.float32
BF16 = jnp.bfloat16

D_MODEL = 2048
SSM_WIDTH = 1024
SSM_GROUP = 16
SSM_GROUPS = 64
SSM_STATE = 64
ATTN_WIDTH = 1024
ATTN_HEADS = 8
HEAD_DIM = 64
ROPE_THETA = 10000.0
N_GROUPS = 4
EXPERTS_PER_GROUP = 8
N_EXPERTS = 32
TOP_K = 2
TOP_K_LOG2 = 1
D_FF = 512
NORM_EPS = 1e-6
SUBLN_EPS = 1e-5
LAMBDA_INIT = 0.8 - 0.6 * math.exp(-0.3 * 0)

LANES = 128
VMEM_LIMIT = 48 * 1024 * 1024

SSM_CHUNK = 16
SSM_GB = 8
MOE_BLK = 128


def _cparams(sem):
    return pltpu.CompilerParams(dimension_semantics=sem, vmem_limit_bytes=VMEM_LIMIT)


def _rmsnorm_body(x_ref, w_ref, pos_ref, invf_ref, o_ref, cos_ref, sin_ref):
    x = x_ref[...]
    y = x * lax.rsqrt(jnp.mean(x * x, axis=-1, keepdims=True) + NORM_EPS)
    o_ref[...] = (y * w_ref[...]).astype(o_ref.dtype)
    ang = pos_ref[...].astype(F32) * invf_ref[...]
    lane = lax.broadcasted_iota(jnp.int32, ang.shape, 1)
    low = (lane % HEAD_DIM) < (HEAD_DIM // 2)
    cos_ref[...] = jnp.cos(ang)
    sin_ref[...] = jnp.where(low, -jnp.sin(ang), jnp.sin(ang))


def _rmsnorm_rope(x, w, positions, out_dtype, tm=512):
    s, d = x.shape
    inv_freq = 1.0 / (ROPE_THETA ** (jnp.arange(0, HEAD_DIM, 2, dtype=F32) / HEAD_DIM))
    invf = jnp.tile(inv_freq, LANES // (HEAD_DIM // 2)).reshape(1, LANES)
    return pl.pallas_call(
        _rmsnorm_body,
        grid=(s // tm,),
        in_specs=[pl.BlockSpec((tm, d), lambda i: (i, 0)),
                  pl.BlockSpec((1, d), lambda i: (0, 0)),
                  pl.BlockSpec((tm, 1), lambda i: (i, 0)),
                  pl.BlockSpec((1, LANES), lambda i: (0, 0))],
        out_specs=[pl.BlockSpec((tm, d), lambda i: (i, 0)),
                   pl.BlockSpec((tm, LANES), lambda i: (i, 0)),
                   pl.BlockSpec((tm, LANES), lambda i: (i, 0))],
        out_shape=[jax.ShapeDtypeStruct((s, d), out_dtype),
                   jax.ShapeDtypeStruct((s, LANES), F32),
                   jax.ShapeDtypeStruct((s, LANES), F32)],
        compiler_params=_cparams(("parallel",)),
        name="rmsnorm",
    )(x, w.reshape(1, d), positions.reshape(s, 1), invf)


def _proj(h_ref, w_ref, wb_ref, row_axis):
    @pl.when(pl.program_id(row_axis) == 0)
    def _cast():
        wb_ref[...] = w_ref[...].astype(BF16)

    return jnp.dot(h_ref[...], wb_ref[...], preferred_element_type=F32)


def _mm_body(h_ref, w_ref, o_ref, wb_ref, *, act):
    acc = _proj(h_ref, w_ref, wb_ref, 1)
    if act == "sigmoid":
        acc = jax.nn.sigmoid(acc)
    o_ref[...] = acc.astype(o_ref.dtype)


def _mm(h, w, col0, ncols, act, out_dtype, tm=1024, tn=1024):
    s, k = h.shape
    jb = col0 // tn
    return pl.pallas_call(
        functools.partial(_mm_body, act=act),
        grid=(ncols // tn, s // tm),
        in_specs=[pl.BlockSpec((tm, k), lambda j, i: (i, 0)),
                  pl.BlockSpec((k, tn), lambda j, i: (0, j + jb))],
        out_specs=pl.BlockSpec((tm, tn), lambda j, i: (i, j)),
        out_shape=jax.ShapeDtypeStruct((s, ncols), out_dtype),
        scratch_shapes=[pltpu.VMEM((k, tn), BF16)],
        compiler_params=_cparams(("arbitrary", "arbitrary")),
        name="mm_" + act,
    )(h, w)


def _mm_vones_body(h_ref, w_ref, o_ref, wb_ref):
    acc = _proj(h_ref, w_ref, wb_ref, 0)
    hw = 2 * HEAD_DIM
    ones = jnp.ones((acc.shape[0], hw), o_ref.dtype)
    for hd in range(acc.shape[1] // hw):
        o_ref[:, 2 * hd * hw:(2 * hd + 1) * hw] = acc[:, hd * hw:(hd + 1) * hw].astype(o_ref.dtype)
        o_ref[:, (2 * hd + 1) * hw:(2 * hd + 2) * hw] = ones


def _mm_vones(h, w, col0, tm=512):
    s, k = h.shape
    tn = ATTN_WIDTH
    jb = col0 // tn
    return pl.pallas_call(
        _mm_vones_body,
        grid=(s // tm,),
        in_specs=[pl.BlockSpec((tm, k), lambda i: (i, 0)),
                  pl.BlockSpec((k, tn), lambda i: (0, jb))],
        out_specs=pl.BlockSpec((tm, 2 * tn), lambda i: (i, 0)),
        out_shape=jax.ShapeDtypeStruct((s, 2 * tn), BF16),
        scratch_shapes=[pltpu.VMEM((k, tn), BF16)],
        compiler_params=_cparams(("arbitrary",)),
        name="mm_vones",
    )(h, w)


def _mm_rope_body(h_ref, w_ref, cos_ref, sin_ref, o_ref, wb_ref, *, tn):
    j = pl.program_id(0)
    acc = _proj(h_ref, w_ref, wb_ref, 1)
    scale = jnp.where(j == 0, HEAD_DIM ** -0.5, 1.0).astype(F32)
    lane = lax.broadcasted_iota(jnp.int32, cos_ref.shape, 1)
    low = (lane % HEAD_DIM) < (HEAD_DIM // 2)
    cos = cos_ref[...] * scale
    sin = sin_ref[...] * scale
    for c in range(tn // LANES):
        t = acc[:, c * LANES:(c + 1) * LANES]
        partner = jnp.where(low, pltpu.roll(t, LANES - HEAD_DIM // 2, 1),
                            pltpu.roll(t, HEAD_DIM // 2, 1))
        o_ref[:, c * LANES:(c + 1) * LANES] = (t * cos + partner * sin).astype(o_ref.dtype)


def _mm_rope(h, w, cos, sin, col0, tm=1024, tn=1024):
    s, k = h.shape
    jb = col0 // tn
    return pl.pallas_call(
        functools.partial(_mm_rope_body, tn=tn),
        grid=(2, s // tm),
        in_specs=[pl.BlockSpec((tm, k), lambda j, i: (i, 0)),
                  pl.BlockSpec((k, tn), lambda j, i: (0, j + jb)),
                  pl.BlockSpec((tm, LANES), lambda j, i: (i, 0)),
                  pl.BlockSpec((tm, LANES), lambda j, i: (i, 0))],
        out_specs=pl.BlockSpec((tm, tn), lambda j, i: (i, j)),
        out_shape=jax.ShapeDtypeStruct((s, 2 * tn), BF16),
        scratch_shapes=[pltpu.VMEM((k, tn), BF16)],
        compiler_params=_cparams(("arbitrary", "arbitrary")),
        name="mm_rope",
    )(h, w, cos, sin)


def _s5_tables(lam_re, lam_im, log_dt, b_re, b_im, c_re, c_im):
    L, G, P, M = SSM_CHUNK, SSM_GROUPS, SSM_STATE, SSM_GROUP
    hi = lax.Precision.HIGHEST
    lr, li = lam_re.astype(F32), lam_im.astype(F32)
    dt = jnp.exp(log_dt.astype(F32))[:, None]
    n = jnp.arange(L + 1, dtype=F32)[:, None, None]
    mag = jnp.exp(lr * dt * n)
    pr = mag * jnp.cos(li * dt * n)
    pi = mag * jnp.sin(li * dt * n)
    nr, ni = pr[1] - 1.0, pi[1]
    den = lr * lr + li * li
    f_re = (nr * lr + ni * li) / den
    f_im = (ni * lr - nr * li) / den
    br, bi = b_re.astype(F32), b_im.astype(F32)
    bbr = f_re[..., None] * br - f_im[..., None] * bi
    bbi = f_re[..., None] * bi + f_im[..., None] * br
    cr, ci = c_re.astype(F32), c_im.astype(F32)

    nb, gb = G // SSM_GB, SSM_GB
    pr_g = jnp.transpose(pr, (1, 0, 2))
    pi_g = jnp.transpose(pi, (1, 0, 2))
    bbr_t = jnp.transpose(bbr, (0, 2, 1))
    bbi_t = jnp.transpose(bbi, (0, 2, 1))

    abr = pr_g[:, :L, None, :] * bbr_t[:, None] - pi_g[:, :L, None, :] * bbi_t[:, None]
    abi = pr_g[:, :L, None, :] * bbi_t[:, None] + pi_g[:, :L, None, :] * bbr_t[:, None]
    kmat = (jnp.einsum('gnip,gop->gnio', abr, cr, precision=hi)
            - jnp.einsum('gnip,gop->gnio', abi, ci, precision=hi))
    kpad = jnp.pad(kmat, ((0, 0), (1, 0), (0, 0), (0, 0)))
    kq = jnp.stack([jnp.stack([kpad[:, 1 + c2 - r2::2][:, :L // 2] for c2 in range(2)], axis=3)
                    for r2 in range(2)], axis=2)
    kk_src = jnp.transpose(kq.reshape(nb, gb, L // 2, 2, M, 2, M),
                           (0, 2, 3, 5, 4, 1, 6)).reshape(nb, L // 2, 2, 2, M, gb * M)

    ab1r = pr_g[:, 1, None, :] * bbr_t - pi_g[:, 1, None, :] * bbi_t
    ab1i = pr_g[:, 1, None, :] * bbi_t + pi_g[:, 1, None, :] * bbr_t
    b4 = jnp.stack([jnp.stack([ab1r, ab1i], axis=2), jnp.stack([bbr_t, bbi_t], axis=2)],
                   axis=1)
    bb_src = jnp.transpose(b4.reshape(nb, gb, 2, M, 2, P),
                           (0, 2, 4, 3, 1, 5)).reshape(nb, 2, 2, M, gb * P)

    cr_t = jnp.transpose(cr, (0, 2, 1))
    ci_t = jnp.transpose(ci, (0, 2, 1))
    car = cr_t * pr_g[:, 1, :, None] - ci_t * pi_g[:, 1, :, None]
    cai = cr_t * pi_g[:, 1, :, None] + ci_t * pr_g[:, 1, :, None]
    c4 = jnp.stack([jnp.stack([cr_t, car], axis=2), jnp.stack([-ci_t, -cai], axis=2)],
                   axis=1)
    cc_src = jnp.transpose(c4.reshape(nb, gb, 2, P, 2, M),
                           (0, 2, 4, 3, 1, 5)).reshape(nb, 2, 2, P, gb * M)

    ap = jnp.stack([pr.reshape(L + 1, nb, gb * P), pi.reshape(L + 1, nb, gb * P)], axis=2)
    return kk_src, bb_src, cc_src, ap.reshape(L + 1, 2 * G * P)


def _cmul(ar, ai, zr, zi):
    return ar * zr - ai * zi, ar * zi + ai * zr


def _block_diag(src, rows_per_group, cols_per_group):
    t = jnp.concatenate([src] * SSM_GB, axis=0)
    row = lax.broadcasted_iota(jnp.int32, t.shape, 0)
    col = lax.broadcasted_iota(jnp.int32, t.shape, 1)
    return jnp.where(row // rows_per_group == col // cols_per_group, t, 0.0).astype(BF16)


def _s5_body(u_ref, kk_src, bb_src, cc_src, ap_ref, d_ref, y_ref,
             kk_ref, bb_ref, cc_ref, lhs_ref, er_ref, ei_ref, xr_ref, xi_ref):
    L, M, P = SSM_CHUNK, SSM_GROUP, SSM_STATE
    nc = u_ref.shape[0] // L
    hp = SSM_GB * P
    gm = SSM_GB * M
    nq = L // 2
    for a in range(2):
        for b in range(2):
            for dd in range(nq):
                kk_ref[dd, a * gm:(a + 1) * gm, b * gm:(b + 1) * gm] = _block_diag(
                    kk_src[0, dd, a, b], M, M)
            bb_ref[a * gm:(a + 1) * gm, b * hp:(b + 1) * hp] = _block_diag(bb_src[0, a, b], M, P)
            cc_ref[a * hp:(a + 1) * hp, b * gm:(b + 1) * gm] = _block_diag(cc_src[0, a, b], P, M)
    for q in range(nq):
        lhs_ref[q] = jnp.concatenate(
            [u_ref[pl.ds(2 * q, nc, stride=L), :], u_ref[pl.ds(2 * q + 1, nc, stride=L), :]],
            axis=1).astype(BF16)

    er = jnp.zeros((nc, hp), F32)
    ei = jnp.zeros((nc, hp), F32)
    for q in range(nq):
        z = jnp.dot(lhs_ref[q], bb_ref[...], preferred_element_type=F32)
        n = L - 2 - 2 * q
        dr, di = _cmul(ap_ref[n:n + 1, :hp], ap_ref[n:n + 1, hp:], z[:, :hp], z[:, hp:])
        er = er + dr
        ei = ei + di
    er_ref[...] = er
    ei_ref[...] = ei

    ar = ap_ref[L:L + 1, :hp]
    ai = ap_ref[L:L + 1, hp:]

    def step(c, carry):
        xr, xi = carry
        xr_ref[pl.ds(c, 1), :] = xr
        xi_ref[pl.ds(c, 1), :] = xi
        nr, ni = _cmul(ar, ai, xr, xi)
        return nr + er_ref[pl.ds(c, 1), :], ni + ei_ref[pl.ds(c, 1), :]

    zero = jnp.zeros((1, hp), F32)
    lax.fori_loop(0, nc, step, (zero, zero))

    for qq in range(nq):
        n = 2 * qq + 1
        wr, wi = _cmul(ap_ref[n:n + 1, :hp], ap_ref[n:n + 1, hp:], xr_ref[...], xi_ref[...])
        w = jnp.concatenate([wr, wi], axis=1).astype(BF16)
        acc = jnp.dot(w, cc_ref[...], preferred_element_type=F32)
        for q in range(qq + 1):
            acc = acc + jnp.dot(lhs_ref[q], kk_ref[qq - q], preferred_element_type=F32)
        for r in range(2):
            j = 2 * qq + r
            y = acc[:, r * LANES:(r + 1) * LANES] + d_ref[...] * u_ref[pl.ds(j, nc, stride=L), :]
            y_ref[pl.ds(j, nc, stride=L), :] = jax.nn.gelu(y)


def _s5_core(u, tables, d_skip):
    s, width = u.shape
    L, P, M, gb = SSM_CHUNK, SSM_STATE, SSM_GROUP, SSM_GB
    nc = s // L
    nb = width // (gb * M)
    kk_src, bb_src, cc_src, ap = tables
    return pl.pallas_call(
        _s5_body,
        grid=(nb,),
        in_specs=[pl.BlockSpec((s, gb * M), lambda b: (0, b)),
                  pl.BlockSpec((1, L // 2, 2, 2, M, gb * M), lambda b: (b, 0, 0, 0, 0, 0)),
                  pl.BlockSpec((1, 2, 2, M, gb * P), lambda b: (b, 0, 0, 0, 0)),
                  pl.BlockSpec((1, 2, 2, P, gb * M), lambda b: (b, 0, 0, 0, 0)),
                  pl.BlockSpec((L + 1, 2 * gb * P), lambda b: (0, b)),
                  pl.BlockSpec((1, gb * M), lambda b: (0, b))],
        out_specs=pl.BlockSpec((s, gb * M), lambda b: (0, b)),
        out_shape=jax.ShapeDtypeStruct((s, width), F32),
        scratch_shapes=[pltpu.VMEM((L // 2, 2 * gb * M, 2 * gb * M), BF16),
                        pltpu.VMEM((2 * gb * M, 2 * gb * P), BF16),
                        pltpu.VMEM((2 * gb * P, 2 * gb * M), BF16),
                        pltpu.VMEM((L // 2, nc, 2 * gb * M), BF16)]
        + [pltpu.VMEM((nc, gb * P), F32) for _ in range(4)],
        compiler_params=_cparams(("parallel",)),
        name="s5_scan",
    )(u, kk_src, bb_src, cc_src, ap, d_skip.astype(F32).reshape(1, width))


def _glu_body(y_ref, w_ref, b_ref, o_ref):
    y = y_ref[...]
    z = jnp.dot(y.astype(BF16), w_ref[...], preferred_element_type=F32) + b_ref[...]
    o_ref[...] = (y * jax.nn.sigmoid(z)).astype(o_ref.dtype)


def _glu(y, w, b, tm=512):
    s, n = y.shape
    return pl.pallas_call(
        _glu_body,
        grid=(s // tm,),
        in_specs=[pl.BlockSpec((tm, n), lambda i: (i, 0)),
                  pl.BlockSpec((n, n), lambda i: (0, 0)),
                  pl.BlockSpec((1, n), lambda i: (0, 0))],
        out_specs=pl.BlockSpec((tm, n), lambda i: (i, 0)),
        out_shape=jax.ShapeDtypeStruct((s, n), BF16),
        compiler_params=_cparams(("parallel",)),
        name="s5_glu",
    )(y, w, b.reshape(1, n).astype(F32))


def _attn_body(q_ref, k_ref, v_ref, lam_ref, sw_ref, o_ref,
               qs_ref, m_ref, acc_ref, s_ref, p_ref, al_ref, *, tq):
    qi = pl.program_id(1)
    hw = 2 * HEAD_DIM
    q = q_ref[...]
    lane = lax.broadcasted_iota(jnp.int32, q.shape, 1)
    zero = jnp.zeros_like(q)
    qs_ref[0] = jnp.where(lane < HEAD_DIM, q, zero)
    qs_ref[1] = jnp.where(lane >= HEAD_DIM, q, zero)
    m_ref[...] = jnp.full(m_ref.shape, -jnp.inf, F32)
    acc_ref[...] = jnp.zeros(acc_ref.shape, F32)

    def scores(c, slot, masked=False):
        off = pl.multiple_of(c * tq, tq)
        k = k_ref[pl.ds(off, tq), :]
        for mp in range(2):
            s = lax.dot_general(qs_ref[mp], k, (((1,), (1,)), ((), ())),
                                preferred_element_type=F32)
            if masked:
                row = lax.broadcasted_iota(jnp.int32, s.shape, 0)
                col = lax.broadcasted_iota(jnp.int32, s.shape, 1)
                s = jnp.where(col <= row, s, -jnp.inf)
            s_ref[slot, mp] = s

    def softmax(slot):
        for mp in range(2):
            s = s_ref[slot, mp]
            m_old = m_ref[mp]
            m_new = jnp.maximum(m_old, jnp.max(s, axis=-1, keepdims=True))
            al_ref[slot, mp] = jnp.exp(m_old - m_new)
            p_ref[slot, mp] = jnp.exp(s - m_new).astype(BF16)
            m_ref[mp] = m_new

    def pv(c, slot):
        off = pl.multiple_of(c * tq, tq)
        v = v_ref[pl.ds(off, tq), :]
        for mp in range(2):
            acc_ref[mp] = al_ref[slot, mp] * acc_ref[mp] + jnp.dot(
                p_ref[slot, mp], v, preferred_element_type=F32)

    @pl.when(qi == 0)
    def _one_chunk():
        scores(0, 0, masked=True)
        softmax(0)
        pv(0, 0)

    @pl.when(qi == 1)
    def _two_chunks():
        scores(0, 1)
        softmax(1)
        pv(0, 1)
        scores(1, 0, masked=True)
        softmax(0)
        pv(1, 0)

    @pl.when(qi >= 2)
    def _pipelined():
        odd = qi % 2

        @pl.when(odd == 0)
        def _fill_even():
            scores(0, 0)
            scores(1, 1)
            softmax(0)

        @pl.when(odd == 1)
        def _fill_odd():
            scores(0, 1)
            scores(1, 0)
            softmax(1)

        def odd_step(_, carry):
            scores(2, 1)
            softmax(0)
            pv(0, 1)
            return carry

        lax.fori_loop(0, odd, odd_step, 0)
        t0 = 2 + odd

        def two_steps(u, carry):
            t = t0 + 2 * u
            scores(t, 0)
            softmax(1)
            pv(t - 2, 0)
            scores(t + 1, 1)
            softmax(0)
            pv(t - 1, 1)
            return carry

        lax.fori_loop(0, (qi - t0) // 2, two_steps, 0)
        scores(qi, 0, masked=True)
        softmax(1)
        pv(qi - 2, 0)
        softmax(0)
        pv(qi - 1, 1)
        pv(qi, 0)

    lam = (jnp.exp(jnp.sum(lam_ref[0:1, :] * lam_ref[1:2, :]))
           - jnp.exp(jnp.sum(lam_ref[2:3, :] * lam_ref[3:4, :])) + LAMBDA_INIT)
    a1 = acc_ref[0]
    a2 = acc_ref[1]
    o = a1[:, :hw] / a1[:, hw:] - lam * (a2[:, :hw] / a2[:, hw:])
    o = o * lax.rsqrt(jnp.mean(o * o, axis=-1, keepdims=True) + SUBLN_EPS)
    o_ref[...] = (o * sw_ref[...] * (1.0 - LAMBDA_INIT)).astype(o_ref.dtype)


def _diff_attention(qk, v1, lam_params, subln_w, tq=512):
    s = qk.shape[0]
    nq = s // tq
    hw = 2 * HEAD_DIM
    kcol0 = ATTN_WIDTH // hw
    return pl.pallas_call(
        functools.partial(_attn_body, tq=tq),
        grid=(ATTN_HEADS, nq),
        in_specs=[pl.BlockSpec((tq, hw), lambda h, i: (i, h)),
                  pl.BlockSpec((s, hw), lambda h, i: (0, kcol0 + h)),
                  pl.BlockSpec((s, 2 * hw), lambda h, i: (0, h)),
                  pl.BlockSpec((4, HEAD_DIM), lambda h, i: (0, 0)),
                  pl.BlockSpec((1, hw), lambda h, i: (0, 0))],
        out_specs=pl.BlockSpec((tq, hw), lambda h, i: (i, h)),
        out_shape=jax.ShapeDtypeStruct((s, ATTN_WIDTH), BF16),
        scratch_shapes=[pltpu.VMEM((2, tq, hw), BF16),
                        pltpu.VMEM((2, tq, 1), F32),
                        pltpu.VMEM((2, tq, 2 * hw), F32),
                        pltpu.VMEM((2, 2, tq, tq), F32),
                        pltpu.VMEM((2, 2, tq, tq), BF16),
                        pltpu.VMEM((2, 2, tq, 1), F32)],
        compiler_params=_cparams(("parallel", "arbitrary")),
        name="diff_attn",
    )(qk, qk, v1, lam_params, subln_w.reshape(1, hw).astype(F32))


def _merge_body(x_ref, ys_ref, ya_ref, gs_ref, ga_ref, wps_ref, wpa_ref, wo_ref, n2_ref, wr_ref,
                x1_ref, h2_ref, eid_ref, gate_ref, cnt_ref):
    ps = jnp.dot(ys_ref[...], wps_ref[...], preferred_element_type=F32)
    pa = jnp.dot(ya_ref[...], wpa_ref[...], preferred_element_type=F32)
    merged = gs_ref[...].astype(F32) * ps + ga_ref[...].astype(F32) * pa
    x1 = x_ref[...] + jnp.dot(merged.astype(BF16), wo_ref[...], preferred_element_type=F32)
    x1_ref[...] = x1
    h2 = x1 * lax.rsqrt(jnp.mean(x1 * x1, axis=-1, keepdims=True) + NORM_EPS) * n2_ref[...]
    h2_ref[...] = h2
    wr = wr_ref[...]
    h_hi = h2.astype(BF16)
    h_lo = (h2 - h_hi.astype(F32)).astype(BF16)
    w_hi = wr.astype(BF16)
    w_lo = (wr - w_hi.astype(F32)).astype(BF16)
    tm = h2.shape[0]
    parts = jnp.dot(jnp.concatenate([h_hi, h_lo], axis=0), jnp.concatenate([w_hi, w_lo], axis=1),
                    preferred_element_type=F32)
    logits = (parts[:tm, :LANES] + parts[:tm, LANES:]) + (parts[tm:, :LANES] + parts[tm:, LANES:])
    lane = lax.broadcasted_iota(jnp.int32, logits.shape, 1)
    big = jnp.int32(1 << 20)
    ninf = jnp.float32(-jnp.inf)
    is_g = lane < N_GROUPS
    gl = jnp.where(is_g, logits, ninf)
    gm = jnp.max(gl, axis=-1, keepdims=True)
    g_idx = jnp.min(jnp.where(gl == gm, lane, big), axis=-1, keepdims=True)
    g_val = 1.0 / jnp.sum(jnp.where(is_g, jnp.exp(gl - gm), 0.0), axis=-1, keepdims=True)
    lo = N_GROUPS + g_idx * EXPERTS_PER_GROUP
    in_grp = (lane >= lo) & (lane < lo + EXPERTS_PER_GROUP)
    el = jnp.where(in_grp, logits, ninf)
    e1 = jnp.max(el, axis=-1, keepdims=True)
    i1 = jnp.min(jnp.where(el == e1, lane, big), axis=-1, keepdims=True)
    el2 = jnp.where(lane == i1, ninf, el)
    e2 = jnp.max(el2, axis=-1, keepdims=True)
    i2 = jnp.min(jnp.where(el2 == e2, lane, big), axis=-1, keepdims=True)
    t = jnp.exp(e2 - e1)
    p1 = 1.0 / (1.0 + t)
    p2 = t / (1.0 + t)
    gate_ref[...] = jnp.where(lane == 0, p1 * g_val, jnp.where(lane == 1, p2 * g_val, 0.0))

    @pl.when(pl.program_id(0) == 0)
    def _zero_counts():
        cnt_ref[...] = jnp.zeros(cnt_ref.shape, F32)

    ex1 = i1 - N_GROUPS
    ex2 = i2 - N_GROUPS
    oh1 = lane == ex1
    oh2 = lane == ex2
    onehot = jnp.where(oh1 | oh2, 1.0, 0.0)
    r_i =lax.broadcasted_iota(jnp.int32, (tm, tm), 0)
    c_i = lax.broadcasted_iota(jnp.int32, (tm, tm), 1)
    tri = jnp.where(c_i < r_i, 1.0, 0.0).astype(BF16)
    before = jnp.dot(tri, onehot.astype(BF16), preferred_element_type=F32) + cnt_ref[...]
    rank1 = jnp.sum(jnp.where(oh1, before, 0.0), axis=-1, keepdims=True).astype(jnp.int32)
    rank2 = jnp.sum(jnp.where(oh2, before, 0.0), axis=-1, keepdims=True).astype(jnp.int32)
    cnt_ref[...] = cnt_ref[...] + jnp.sum(onehot, axis=0, keepdims=True)
    eid_ref[...] = jnp.where(lane == 0, ex1, jnp.where(lane == 1, ex2,
                             jnp.where(lane == 2, rank1, jnp.where(lane == 3, rank2, 0))))


def _merge_route(x, y_ssm, y_attn, gates, wps, wpa, wo, n2w, wr, tm=256):
    s, d = x.shape
    nw = y_ssm.shape[1]
    const = lambda i: (0, 0)
    return pl.pallas_call(
        _merge_body,
        grid=(s // tm,),
        in_specs=[pl.BlockSpec((tm, d), lambda i: (i, 0)),
                  pl.BlockSpec((tm, nw), lambda i: (i, 0)),
                  pl.BlockSpec((tm, nw), lambda i: (i, 0)),
                  pl.BlockSpec((tm, d), lambda i: (i, 0)),
                  pl.BlockSpec((tm, d), lambda i: (i, 1)),
                  pl.BlockSpec((nw, d), const),
                  pl.BlockSpec((nw, d), const),
                  pl.BlockSpec((d, d), const),
                  pl.BlockSpec((1, d), const),
                  pl.BlockSpec((d, LANES), const)],
        out_specs=[pl.BlockSpec((tm, d), lambda i: (i, 0)),
                   pl.BlockSpec((tm, d), lambda i: (i, 0)),
                   pl.BlockSpec((tm, LANES), lambda i: (i, 0)),
                   pl.BlockSpec((tm, LANES), lambda i: (i, 0)),
                   pl.BlockSpec((1, LANES), const)],
        out_shape=[jax.ShapeDtypeStruct((s, d), F32),
                   jax.ShapeDtypeStruct((s, d), F32),
                   jax.ShapeDtypeStruct((s, LANES), jnp.int32),
                   jax.ShapeDtypeStruct((s, LANES), F32),
                   jax.ShapeDtypeStruct((1, LANES), F32)],
        compiler_params=_cparams(("arbitrary",)),
        name="merge_route",
    )(x, y_ssm, y_attn, gates, gates, wps, wpa, wo, n2w.reshape(1, d).astype(F32), wr)


def _row_copy(src_hbm, row, dst_ref, r, sem):
    return pltpu.make_async_copy(src_hbm.at[pl.ds(row, 1), :], dst_ref.at[pl.ds(r, 1), :], sem)


def _gather_rows(src_hbm, idx_ref, base, stride, dst_ref, sem, n):
    def start(r, c):
        _row_copy(src_hbm, idx_ref[base + stride * r], dst_ref, r, sem).start()
        return c

    lax.fori_loop(0, n, start, 0, unroll=8)


def _wait_rows(src_hbm, dst_ref, sem, n):
    pltpu.make_async_copy(src_hbm.at[pl.ds(0, n), :], dst_ref, sem).wait()


def _scatter_tok_body(dest_ref, tok_ref):
    def zero(i, c):
        tok_ref[i] = 0
        return c

    lax.fori_loop(0, tok_ref.shape[0], zero, 0, unroll=8)

    def put(i, c):
        tok_ref[dest_ref[i]] = lax.shift_right_logical(i, TOP_K_LOG2)
        return c

    lax.fori_loop(0, dest_ref.shape[0], put, 0, unroll=8)


def _scatter_tok(dest, buf_len):
    return pl.pallas_call(
        _scatter_tok_body,
        in_specs=[pl.BlockSpec(memory_space=pltpu.SMEM)],
        out_specs=pl.BlockSpec(memory_space=pltpu.SMEM),
        out_shape=jax.ShapeDtypeStruct((buf_len,), jnp.int32),
        name="scatter_tok",
    )(dest)


def _block_out_copy(ob_ref, slot, ys_hbm, blk, sem):
    return pltpu.make_async_copy(ob_ref.at[slot], ys_hbm.at[pl.ds(blk * MOE_BLK, MOE_BLK), :],
                                 sem.at[slot])


WEIGHT_DMA_PRIORITY = 1


def _weight_copies(e, slot, w_hbm, w_buf, wsem):
    return [pltpu.make_async_copy(h.at[e], b.at[slot], wsem.at[slot])
            for h, b in zip(w_hbm, w_buf)]


def _expert_body(fb_ref, nblk_ref, nb_ref, tok_ref, h2_hbm, wg_hbm, wu_hbm, wd_hbm, ys_hbm,
                 xb_ref, ob_ref, wg_ref, wu_ref, wd_ref, wgb_ref, wub_ref, wdb_ref,
                 gsem, osem, wsem, *, n_blocks):
    e = pl.program_id(0)
    nb = nb_ref[0]
    first = fb_ref[e]
    count = nblk_ref[e]
    w_hbm = (wg_hbm, wu_hbm, wd_hbm)
    w_buf = (wg_ref, wu_ref, wd_ref)

    @pl.when(e == 0)
    def _first_weights():
        for c in _weight_copies(0, 0, w_hbm, w_buf, wsem):
            c.start(priority=WEIGHT_DMA_PRIORITY)

    @pl.when(e + 1 < pl.num_programs(0))
    def _next_weights():
        for c in _weight_copies(e + 1, (e + 1) % 2, w_hbm, w_buf, wsem):
            c.start(priority=WEIGHT_DMA_PRIORITY)

    for c in _weight_copies(e, e % 2, w_hbm, w_buf, wsem):
        c.wait()

    @pl.when(count > 0)
    def _expert():
        wgb_ref[...] = wg_ref[e % 2].astype(BF16)
        wub_ref[...] = wu_ref[e % 2].astype(BF16)
        wdb_ref[...] = wd_ref[e % 2].astype(BF16)

        @pl.when(first == 0)
        def _first_rows():
            _gather_rows(h2_hbm, tok_ref, 0, 1, xb_ref.at[0], gsem.at[0], MOE_BLK)

        def block(t, carry):
            g = first + t
            cur = g % 2
            nxt = 1 - cur
            _wait_rows(h2_hbm, xb_ref.at[cur], gsem.at[cur], MOE_BLK)

            @pl.when(g >= 2)
            def _staging_free():
                _block_out_copy(ob_ref, cur, ys_hbm, g - 2, osem).wait()

            base = (g + 1) * MOE_BLK
            for r in range(MOE_BLK):
                _row_copy(h2_hbm, tok_ref[base + r], xb_ref.at[nxt], r, gsem.at[nxt]).start()
            xb = xb_ref[cur].astype(BF16)
            hg = jnp.dot(xb, wgb_ref[...], preferred_element_type=F32)
            hu = jnp.dot(xb, wub_ref[...], preferred_element_type=F32)
            act = (jax.nn.silu(hg) * hu).astype(BF16)
            ob_ref[cur] = jnp.dot(act, wdb_ref[...], preferred_element_type=F32)
            _block_out_copy(ob_ref, cur, ys_hbm, g, osem).start()
            return carry

        lax.fori_loop(0, count, block, 0)

    @pl.when(e == pl.num_programs(0) - 1)
    def _finish():
        _wait_rows(h2_hbm, xb_ref.at[nb % 2], gsem.at[nb % 2], MOE_BLK)
        _block_out_copy(ob_ref, (nb - 1) % 2, ys_hbm, nb - 1, osem).wait()

        @pl.when(nb >= 2)
        def _():
            _block_out_copy(ob_ref, nb % 2, ys_hbm, nb - 2, osem).wait()

        ob_ref[0] = jnp.zeros(ob_ref.shape[1:], ob_ref.dtype)

        def zero_block(g, carry):
            _block_out_copy(ob_ref, 0, ys_hbm, g, osem).start()
            return carry

        lax.fori_loop(nb, n_blocks, zero_block, 0)

        def zero_wait(g, carry):
            _block_out_copy(ob_ref, 0, ys_hbm, g, osem).wait()
            return carry

        lax.fori_loop(nb, n_blocks, zero_wait, 0)


def _experts(h2, first_blk, n_blk, n_used, buf_tok, wg, wu, wd):
    s, d = h2.shape
    buf_len = buf_tok.shape[0]
    n_blocks = buf_len // MOE_BLK
    return pl.pallas_call(
        functools.partial(_expert_body, n_blocks=n_blocks),
        grid_spec=pltpu.PrefetchScalarGridSpec(
            num_scalar_prefetch=4,
            grid=(N_EXPERTS,),
            in_specs=[pl.BlockSpec(memory_space=pl.ANY)] * 4,
            out_specs=pl.BlockSpec(memory_space=pl.ANY),
            scratch_shapes=[pltpu.VMEM((2, MOE_BLK, d), F32), pltpu.VMEM((2, MOE_BLK, d), F32),
                            pltpu.VMEM((2, d, D_FF), F32), pltpu.VMEM((2, d, D_FF), F32),
                            pltpu.VMEM((2, D_FF, d), F32),
                            pltpu.VMEM((d, D_FF), BF16), pltpu.VMEM((d, D_FF), BF16),
                            pltpu.VMEM((D_FF, d), BF16),
                            pltpu.SemaphoreType.DMA((2,)), pltpu.SemaphoreType.DMA((2,)),
                            pltpu.SemaphoreType.DMA((2,))]),
        out_shape=jax.ShapeDtypeStruct((buf_len, d), F32),
        compiler_params=_cparams(("arbitrary",)),
        name="experts",
    )(first_blk, n_blk, n_used, buf_tok, h2, wg, wu, wd)


def _combine_body(pos_ref, x1_ref, gate_ref, fw_ref, ys_hbm, o_ref, g_ref, sem, *, tm):
    i = pl.program_id(0)

    def gather(tile, slot):
        for k in range(TOP_K):
            _gather_rows(ys_hbm, pos_ref, TOP_K * tile * tm + k, TOP_K, g_ref.at[slot, k],
                         sem.at[slot], tm)

    @pl.when(i == 0)
    def _first():
        gather(0, 0)

    @pl.when(i + 1 < pl.num_programs(0))
    def _prefetch():
        gather(i + 1, (i + 1) % 2)

    cur = i % 2
    for k in range(TOP_K):
        _wait_rows(ys_hbm, g_ref.at[cur, k], sem.at[cur], tm)
    gate = gate_ref[...]
    x = x1_ref[...] + (gate[:, 0:1] * g_ref[cur, 0] + gate[:, 1:2] * g_ref[cur, 1])
    y = x * lax.rsqrt(jnp.mean(x * x, axis=-1, keepdims=True) + NORM_EPS)
    o_ref[...] = y * fw_ref[...]


def _combine(x1, ys, pos, gate, fw, tm=256):
    s, d = x1.shape
    return pl.pallas_call(
        functools.partial(_combine_body, tm=tm),
        grid_spec=pltpu.PrefetchScalarGridSpec(
            num_scalar_prefetch=1,
            grid=(s // tm,),
            in_specs=[pl.BlockSpec((tm, d), lambda i, p: (i, 0)),
                      pl.BlockSpec((tm, LANES), lambda i, p: (i, 0)),
                      pl.BlockSpec((1, d), lambda i, p: (0, 0)),
                      pl.BlockSpec(memory_space=pl.ANY)],
            out_specs=pl.BlockSpec((tm, d), lambda i, p: (i, 0)),
            scratch_shapes=[pltpu.VMEM((2, TOP_K, tm, d), F32),
                            pltpu.SemaphoreType.DMA((2,))]),
        out_shape=jax.ShapeDtypeStruct((s, d), F32),
        compiler_params=_cparams(("arbitrary",)),
        name="combine_norm",
    )(pos, x1, gate, fw.reshape(1, d).astype(F32), ys)


def _dispatch_tables(eid, counts):
    n_tok = eid.shape[0]
    n_slots = n_tok * TOP_K
    buf_len = n_slots + N_EXPERTS * MOE_BLK
    counts = counts.astype(jnp.int32)
    padded = ((counts + MOE_BLK - 1) // MOE_BLK) * MOE_BLK
    padded_ends = jnp.cumsum(padded)
    padded_starts = padded_ends - padded
    experts = eid[:, :TOP_K]
    onehot = experts[:, :, None] == jnp.arange(N_EXPERTS, dtype=jnp.int32)
    dest = jnp.sum(jnp.where(onehot, padded_starts, 0), axis=-1) + eid[:, TOP_K:2 * TOP_K]
    n_used = (padded_ends[-1] // MOE_BLK).astype(jnp.int32).reshape(1)
    first_blk = (padded_starts // MOE_BLK).astype(jnp.int32)
    n_blk = (padded // MOE_BLK).astype(jnp.int32)
    return first_blk, n_blk, n_used, dest.reshape(n_slots).astype(jnp.int32), buf_len


def kernel(x, positions, norm1_w, w_in, ssm_lambda_re, ssm_lambda_im, ssm_log_dt, ssm_b_re, ssm_b_im, ssm_c_re, ssm_c_im, ssm_d, ssm_glu_w, ssm_glu_b, attn_lambda_q1, attn_lambda_k1, attn_lambda_q2, attn_lambda_k2, attn_subln_w, w_proj_ssm, w_proj_attn, w_out, norm2_w, router_group_w, router_expert_w, expert_w_gate, expert_w_up, expert_w_down, final_norm_w):
    bsz, seq, d = x.shape
    depth = norm1_w.shape[0]
    xs = x.reshape(bsz * seq, d)
    pos = positions.reshape(bsz * seq)
    o_q = SSM_WIDTH
    o_v = o_q + 2 * ATTN_WIDTH
    o_g = o_v + ATTN_WIDTH
    for l in range(depth):
        h, rope_cos, rope_sin = _rmsnorm_rope(xs, norm1_w[l].astype(F32), pos, BF16)
        w_in_b = w_in[l].astype(F32)
        u = _mm(h, w_in_b, 0, SSM_WIDTH, "none", F32)
        qk = _mm_rope(h, w_in_b, rope_cos, rope_sin, o_q)
        v = _mm_vones(h, w_in_b, o_v)
        gates = _mm(h, w_in_b, o_g, 2 * D_MODEL, "sigmoid", F32)

        tables = _s5_tables(ssm_lambda_re[l], ssm_lambda_im[l], ssm_log_dt[l],
                            ssm_b_re[l], ssm_b_im[l], ssm_c_re[l], ssm_c_im[l])
        y = _s5_core(u, tables, ssm_d[l])
        y_ssm = _glu(y, ssm_glu_w[l].astype(BF16), ssm_glu_b[l])

        lam_params = jnp.stack([attn_lambda_q1[l], attn_lambda_k1[l],
                                attn_lambda_q2[l], attn_lambda_k2[l]]).astype(F32)
        y_attn = _diff_attention(qk, v, lam_params, attn_subln_w[l])

        wr = jnp.concatenate([router_group_w[l], router_expert_w[l]], axis=1).astype(F32)
        wr = jnp.pad(wr, ((0, 0), (0, LANES - wr.shape[1])))
        x1, h2, eid, gate, counts = _merge_route(
            xs, y_ssm, y_attn, gates, w_proj_ssm[l].astype(BF16), w_proj_attn[l].astype(BF16),
            w_out[l].astype(BF16), norm2_w[l], wr)

        first_blk, n_blk, n_used, dest, buf_len = _dispatch_tables(eid, counts[0, :N_EXPERTS])
        buf_tok = _scatter_tok(dest, buf_len)
        ys = _experts(h2, first_blk, n_blk, n_used, buf_tok, expert_w_gate[l].astype(F32),
                      expert_w_up[l].astype(F32), expert_w_down[l].astype(F32))
        last = l == depth - 1
        assert last, "DEPTH > 1 needs an un-normalised combine"
        xs = _combine(x1, ys, dest, gate, final_norm_w)
    return xs.reshape(bsz, seq, d)
```

```python
import functools
import math

import jax
import jax.numpy as jnp
from jax import lax
from jax.experimental import pallas as pl
from jax.experimental.pallas import tpu as pltpu

F32 = jnp.float32
BF16 = jnp.bfloat16

D_MODEL = 2048
SSM_WIDTH = 1024
SSM_GROUP = 16
SSM_GROUPS = 64
SSM_STATE = 64
ATTN_WIDTH = 1024
ATTN_HEADS = 8
HEAD_DIM = 64
ROPE_THETA = 10000.0
N_GROUPS = 4
EXPERTS_PER_GROUP = 8
N_EXPERTS = 32
TOP_K = 2
TOP_K_LOG2 = 1
D_FF = 512
NORM_EPS = 1e-6
SUBLN_EPS = 1e-5
LAMBDA_INIT = 0.8 - 0.6 * math.exp(-0.3 * 0)

LANES = 128
VMEM_LIMIT = 48 * 1024 * 1024

SSM_CHUNK = 16
SSM_GB = 8
MOE_BLK = 128


def _cparams(sem):
    return pltpu.CompilerParams(dimension_semantics=sem, vmem_limit_bytes=VMEM_LIMIT)


def _rmsnorm_body(x_ref, w_ref, pos_ref, invf_ref, o_ref, cos_ref, sin_ref):
    x = x_ref[...]
    y = x * lax.rsqrt(jnp.mean(x * x, axis=-1, keepdims=True) + NORM_EPS)
    o_ref[...] = (y * w_ref[...]).astype(o_ref.dtype)
    ang = pos_ref[...].astype(F32) * invf_ref[...]
    lane = lax.broadcasted_iota(jnp.int32, ang.shape, 1)
    low = (lane % HEAD_DIM) < (HEAD_DIM // 2)
    cos_ref[...] = jnp.cos(ang)
    sin_ref[...] = jnp.where(low, -jnp.sin(ang), jnp.sin(ang))


def _rmsnorm_rope(x, w, positions, out_dtype, tm=512):
    s, d = x.shape
    inv_freq = 1.0 / (ROPE_THETA ** (jnp.arange(0, HEAD_DIM, 2, dtype=F32) / HEAD_DIM))
    invf = jnp.tile(inv_freq, LANES // (HEAD_DIM // 2)).reshape(1, LANES)
    return pl.pallas_call(
        _rmsnorm_body,
        grid=(s // tm,),
        in_specs=[pl.BlockSpec((tm, d), lambda i: (i, 0)),
                  pl.BlockSpec((1, d), lambda i: (0, 0)),
                  pl.BlockSpec((tm, 1), lambda i: (i, 0)),
                  pl.BlockSpec((1, LANES), lambda i: (0, 0))],
        out_specs=[pl.BlockSpec((tm, d), lambda i: (i, 0)),
                   pl.BlockSpec((tm, LANES), lambda i: (i, 0)),
                   pl.BlockSpec((tm, LANES), lambda i: (i, 0))],
        out_shape=[jax.ShapeDtypeStruct((s, d), out_dtype),
                   jax.ShapeDtypeStruct((s, LANES), F32),
                   jax.ShapeDtypeStruct((s, LANES), F32)],
        compiler_params=_cparams(("parallel",)),
        name="rmsnorm",
    )(x, w.reshape(1, d), positions.reshape(s, 1), invf)


def _proj(h_ref, w_ref, wb_ref, row_axis):
    @pl.when(pl.program_id(row_axis) == 0)
    def _cast():
        wb_ref[...] = w_ref[...].astype(BF16)

    return jnp.dot(h_ref[...], wb_ref[...], preferred_element_type=F32)


def _mm_body(h_ref, w_ref, o_ref, wb_ref, *, act):
    acc = _proj(h_ref, w_ref, wb_ref, 1)
    if act == "sigmoid":
        acc = jax.nn.sigmoid(acc)
    o_ref[...] = acc.astype(o_ref.dtype)


def _mm(h, w, col0, ncols, act, out_dtype, tm=1024, tn=1024):
    s, k = h.shape
    jb = col0 // tn
    return pl.pallas_call(
        functools.partial(_mm_body, act=act),
        grid=(ncols // tn, s // tm),
        in_specs=[pl.BlockSpec((tm, k), lambda j, i: (i, 0)),
                  pl.BlockSpec((k, tn), lambda j, i: (0, j + jb))],
        out_specs=pl.BlockSpec((tm, tn), lambda j, i: (i, j)),
        out_shape=jax.ShapeDtypeStruct((s, ncols), out_dtype),
        scratch_shapes=[pltpu.VMEM((k, tn), BF16)],
        compiler_params=_cparams(("arbitrary", "arbitrary")),
        name="mm_" + act,
    )(h, w)


def _mm_vones_body(h_ref, w_ref, o_ref, wb_ref):
    acc = _proj(h_ref, w_ref, wb_ref, 0)
    hw = 2 * HEAD_DIM
    ones = jnp.ones((acc.shape[0], hw), o_ref.dtype)
    for hd in range(acc.shape[1] // hw):
        o_ref[:, 2 * hd * hw:(2 * hd + 1) * hw] = acc[:, hd * hw:(hd + 1) * hw].astype(o_ref.dtype)
        o_ref[:, (2 * hd + 1) * hw:(2 * hd + 2) * hw] = ones


def _mm_vones(h, w, col0, tm=1024):
    s, k = h.shape
    tn = ATTN_WIDTH
    jb = col0 // tn
    return pl.pallas_call(
        _mm_vones_body,
        grid=(s // tm,),
        in_specs=[pl.BlockSpec((tm, k), lambda i: (i, 0)),
                  pl.BlockSpec((k, tn), lambda i: (0, jb))],
        out_specs=pl.BlockSpec((tm, 2 * tn), lambda i: (i, 0)),
        out_shape=jax.ShapeDtypeStruct((s, 2 * tn), BF16),
        scratch_shapes=[pltpu.VMEM((k, tn), BF16)],
        compiler_params=_cparams(("arbitrary",)),
        name="mm_vones",
    )(h, w)


def _mm_rope_body(h_ref, w_ref, cos_ref, sin_ref, o_ref, wb_ref, *, tn):
    j = pl.program_id(0)
    acc = _proj(h_ref, w_ref, wb_ref, 1)
    scale = jnp.where(j == 0, HEAD_DIM ** -0.5, 1.0).astype(F32)
    lane = lax.broadcasted_iota(jnp.int32, cos_ref.shape, 1)
    low = (lane % HEAD_DIM) < (HEAD_DIM // 2)
    cos = cos_ref[...] * scale
    sin = sin_ref[...] * scale
    for c in range(tn // LANES):
        t = acc[:, c * LANES:(c + 1) * LANES]
        partner = jnp.where(low, pltpu.roll(t, LANES - HEAD_DIM // 2, 1),
                            pltpu.roll(t, HEAD_DIM // 2, 1))
        o_ref[:, c * LANES:(c + 1) * LANES] = (t * cos + partner * sin).astype(o_ref.dtype)


def _mm_rope(h, w, cos, sin, col0, tm=1024, tn=1024):
    s, k = h.shape
    jb = col0 // tn
    return pl.pallas_call(
        functools.partial(_mm_rope_body, tn=tn),
        grid=(2, s // tm),
        in_specs=[pl.BlockSpec((tm, k), lambda j, i: (i, 0)),
                  pl.BlockSpec((k, tn), lambda j, i: (0, j + jb)),
                  pl.BlockSpec((tm, LANES), lambda j, i: (i, 0)),
                  pl.BlockSpec((tm, LANES), lambda j, i: (i, 0))],
        out_specs=pl.BlockSpec((tm, tn), lambda j, i: (i, j)),
        out_shape=jax.ShapeDtypeStruct((s, 2 * tn), BF16),
        scratch_shapes=[pltpu.VMEM((k, tn), BF16)],
        compiler_params=_cparams(("arbitrary", "arbitrary")),
        name="mm_rope",
    )(h, w, cos, sin)


def _s5_tables(lam_re, lam_im, log_dt, b_re, b_im, c_re, c_im):
    L, G, P, M = SSM_CHUNK, SSM_GROUPS, SSM_STATE, SSM_GROUP
    hi = lax.Precision.HIGHEST
    lr, li = lam_re.astype(F32), lam_im.astype(F32)
    dt = jnp.exp(log_dt.astype(F32))[:, None]
    n = jnp.arange(L + 1, dtype=F32)[:, None, None]
    mag = jnp.exp(lr * dt * n)
    pr = mag * jnp.cos(li * dt * n)
    pi = mag * jnp.sin(li * dt * n)
    nr, ni = pr[1] - 1.0, pi[1]
    den = lr * lr + li * li
    f_re = (nr * lr + ni * li) / den
    f_im = (ni * lr - nr * li) / den
    br, bi = b_re.astype(F32), b_im.astype(F32)
    bbr = f_re[..., None] * br - f_im[..., None] * bi
    bbi = f_re[..., None] * bi + f_im[..., None] * br
    cr, ci = c_re.astype(F32), c_im.astype(F32)

    nb, gb = G // SSM_GB, SSM_GB
    pr_g = jnp.transpose(pr, (1, 0, 2))
    pi_g = jnp.transpose(pi, (1, 0, 2))
    bbr_t = jnp.transpose(bbr, (0, 2, 1))
    bbi_t = jnp.transpose(bbi, (0, 2, 1))

    abr = pr_g[:, :L, None, :] * bbr_t[:, None] - pi_g[:, :L, None, :] * bbi_t[:, None]
    abi = pr_g[:, :L, None, :] * bbi_t[:, None] + pi_g[:, :L, None, :] * bbr_t[:, None]
    kmat = (jnp.einsum('gnip,gop->gnio', abr, cr, precision=hi)
            - jnp.einsum('gnip,gop->gnio', abi, ci, precision=hi))
    kpad = jnp.pad(kmat, ((0, 0), (1, 0), (0, 0), (0, 0)))
    kq = jnp.stack([jnp.stack([kpad[:, 1 + c2 - r2::2][:, :L // 2] for c2 in range(2)], axis=3)
                    for r2 in range(2)], axis=2)
    kk_src = jnp.transpose(kq.reshape(nb, gb, L // 2, 2, M, 2, M),
                           (0, 2, 3, 5, 4, 1, 6)).reshape(nb, L // 2, 2, 2, M, gb * M)

    ab1r = pr_g[:, 1, None, :] * bbr_t - pi_g[:, 1, None, :] * bbi_t
    ab1i = pr_g[:, 1, None, :] * bbi_t + pi_g[:, 1, None, :] * bbr_t
    b4 = jnp.stack([jnp.stack([ab1r, ab1i], axis=2), jnp.stack([bbr_t, bbi_t], axis=2)],
                   axis=1)
    bb_src = jnp.transpose(b4.reshape(nb, gb, 2, M, 2, P),
                           (0, 2, 4, 3, 1, 5)).reshape(nb, 2, 2, M, gb * P)

    cr_t = jnp.transpose(cr, (0, 2, 1))
    ci_t = jnp.transpose(ci, (0, 2, 1))
    car = cr_t * pr_g[:, 1, :, None] - ci_t * pi_g[:, 1, :, None]
    cai = cr_t * pi_g[:, 1, :, None] + ci_t * pr_g[:, 1, :, None]
    c4 = jnp.stack([jnp.stack([cr_t, car], axis=2), jnp.stack([-ci_t, -cai], axis=2)],
                   axis=1)
    cc_src = jnp.transpose(c4.reshape(nb, gb, 2, P, 2, M),
                           (0, 2, 4, 3, 1, 5)).reshape(nb, 2, 2, P, gb * M)

    ap = jnp.stack([pr.reshape(L + 1, nb, gb * P), pi.reshape(L + 1, nb, gb * P)], axis=2)
    return kk_src, bb_src, cc_src, ap.reshape(L + 1, 2 * G * P)


def _cmul(ar, ai, zr, zi):
    return ar * zr - ai * zi, ar * zi + ai * zr


def _block_diag(src, rows_per_group, cols_per_group):
    t = jnp.concatenate([src] * SSM_GB, axis=0)
    row = lax.broadcasted_iota(jnp.int32, t.shape, 0)
    col = lax.broadcasted_iota(jnp.int32, t.shape, 1)
    return jnp.where(row // rows_per_group == col // cols_per_group, t, 0.0).astype(BF16)


def _s5_body(u_ref, kk_src, bb_src, cc_src, ap_ref, d_ref, y_ref,
             kk_ref, bb_ref, cc_ref, lhs_ref, er_ref, ei_ref, xr_ref, xi_ref):
    L, M, P = SSM_CHUNK, SSM_GROUP, SSM_STATE
    nc = u_ref.shape[0] // L
    hp = SSM_GB * P
    gm = SSM_GB * M
    nq = L // 2
    for a in range(2):
        for b in range(2):
            for dd in range(nq):
                kk_ref[dd, a * gm:(a + 1) * gm, b * gm:(b + 1) * gm] = _block_diag(
                    kk_src[0, dd, a, b], M, M)
            bb_ref[a * gm:(a + 1) * gm, b * hp:(b + 1) * hp] = _block_diag(bb_src[0, a, b], M, P)
            cc_ref[a * hp:(a + 1) * hp, b * gm:(b + 1) * gm] = _block_diag(cc_src[0, a, b], P, M)
    for q in range(nq):
        lhs_ref[q] = jnp.concatenate(
            [u_ref[pl.ds(2 * q, nc, stride=L), :], u_ref[pl.ds(2 * q + 1, nc, stride=L), :]],
            axis=1).astype(BF16)

    er = jnp.zeros((nc, hp), F32)
    ei = jnp.zeros((nc, hp), F32)
    for q in range(nq):
        z = jnp.dot(lhs_ref[q], bb_ref[...], preferred_element_type=F32)
        n = L - 2 - 2 * q
        dr, di = _cmul(ap_ref[n:n + 1, :hp], ap_ref[n:n + 1, hp:], z[:, :hp], z[:, hp:])
        er = er + dr
        ei = ei + di
    er_ref[...] = er
    ei_ref[...] = ei

    ar = ap_ref[L:L + 1, :hp]
    ai = ap_ref[L:L + 1, hp:]

    def step(c, carry):
        xr, xi = carry
        xr_ref[pl.ds(c, 1), :] = xr
        xi_ref[pl.ds(c, 1), :] = xi
        nr, ni = _cmul(ar, ai, xr, xi)
        return nr + er_ref[pl.ds(c, 1), :], ni + ei_ref[pl.ds(c, 1), :]

    zero = jnp.zeros((1, hp), F32)
    lax.fori_loop(0, nc, step, (zero, zero))

    for qq in range(nq):
        n = 2 * qq + 1
        wr, wi = _cmul(ap_ref[n:n + 1, :hp], ap_ref[n:n + 1, hp:], xr_ref[...], xi_ref[...])
        w = jnp.concatenate([wr, wi], axis=1).astype(BF16)
        acc = jnp.dot(w, cc_ref[...], preferred_element_type=F32)
        for q in range(qq + 1):
            acc = acc + jnp.dot(lhs_ref[q], kk_ref[qq - q], preferred_element_type=F32)
        for r in range(2):
            j = 2 * qq + r
            y = acc[:, r * LANES:(r + 1) * LANES] + d_ref[...] * u_ref[pl.ds(j, nc, stride=L), :]
            y_ref[pl.ds(j, nc, stride=L), :] = jax.nn.gelu(y)


def _s5_core(u, tables, d_skip):
    s, width = u.shape
    L, P, M, gb = SSM_CHUNK, SSM_STATE, SSM_GROUP, SSM_GB
    nc = s // L
    nb = width // (gb * M)
    kk_src, bb_src, cc_src, ap = tables
    return pl.pallas_call(
        _s5_body,
        grid=(nb,),
        in_specs=[pl.BlockSpec((s, gb * M), lambda b: (0, b)),
                  pl.BlockSpec((1, L // 2, 2, 2, M, gb * M), lambda b: (b, 0, 0, 0, 0, 0)),
                  pl.BlockSpec((1, 2, 2, M, gb * P), lambda b: (b, 0, 0, 0, 0)),
                  pl.BlockSpec((1, 2, 2, P, gb * M), lambda b: (b, 0, 0, 0, 0)),
                  pl.BlockSpec((L + 1, 2 * gb * P), lambda b: (0, b)),
                  pl.BlockSpec((1, gb * M), lambda b: (0, b))],
        out_specs=pl.BlockSpec((s, gb * M), lambda b: (0, b)),
        out_shape=jax.ShapeDtypeStruct((s, width), F32),
        scratch_shapes=[pltpu.VMEM((L // 2, 2 * gb * M, 2 * gb * M), BF16),
                        pltpu.VMEM((2 * gb * M, 2 * gb * P), BF16),
                        pltpu.VMEM((2 * gb * P, 2 * gb * M), BF16),
                        pltpu.VMEM((L // 2, nc, 2 * gb * M), BF16)]
        + [pltpu.VMEM((nc, gb * P), F32) for _ in range(4)],
        compiler_params=_cparams(("parallel",)),
        name="s5_scan",
    )(u, kk_src, bb_src, cc_src, ap, d_skip.astype(F32).reshape(1, width))


def _glu_body(y_ref, w_ref, b_ref, o_ref):
    y = y_ref[...]
    z = jnp.dot(y.astype(BF16), w_ref[...], preferred_element_type=F32) + b_ref[...]
    o_ref[...] = (y * jax.nn.sigmoid(z)).astype(o_ref.dtype)


def _glu(y, w, b, tm=512):
    s, n = y.shape
    return pl.pallas_call(
        _glu_body,
        grid=(s // tm,),
        in_specs=[pl.BlockSpec((tm, n), lambda i: (i, 0)),
                  pl.BlockSpec((n, n), lambda i: (0, 0)),
                  pl.BlockSpec((1, n), lambda i: (0, 0))],
        out_specs=pl.BlockSpec((tm, n), lambda i: (i, 0)),
        out_shape=jax.ShapeDtypeStruct((s, n), BF16),
        compiler_params=_cparams(("parallel",)),
        name="s5_glu",
    )(y, w, b.reshape(1, n).astype(F32))


def _attn_body(q_ref, k_ref, v_ref, lam_ref, sw_ref, o_ref,
               qs_ref, m_ref, acc_ref, s_ref, p_ref, al_ref, *, tq):
    qi = pl.program_id(1)
    hw = 2 * HEAD_DIM
    q = q_ref[...]
    lane = lax.broadcasted_iota(jnp.int32, q.shape, 1)
    zero = jnp.zeros_like(q)
    qs_ref[0] = jnp.where(lane < HEAD_DIM, q, zero)
    qs_ref[1] = jnp.where(lane >= HEAD_DIM, q, zero)
    m_ref[...] = jnp.full(m_ref.shape, -jnp.inf, F32)
    acc_ref[...] = jnp.zeros(acc_ref.shape, F32)

    def scores(c, slot, masked=False):
        off = pl.multiple_of(c * tq, tq)
        k = k_ref[pl.ds(off, tq), :]
        for mp in range(2):
            s = lax.dot_general(qs_ref[mp], k, (((1,), (1,)), ((), ())),
                                preferred_element_type=F32)
            if masked:
                row = lax.broadcasted_iota(jnp.int32, s.shape, 0)
                col = lax.broadcasted_iota(jnp.int32, s.shape, 1)
                s = jnp.where(col <= row, s, -jnp.inf)
            s_ref[slot, mp] = s

    def softmax(slot):
        for mp in range(2):
            s = s_ref[slot, mp]
            m_old = m_ref[mp]
            m_new = jnp.maximum(m_old, jnp.max(s, axis=-1, keepdims=True))
            al_ref[slot, mp] = jnp.exp(m_old - m_new)
            p_ref[slot, mp] = jnp.exp(s - m_new).astype(BF16)
            m_ref[mp] = m_new

    def pv(c, slot):
        off = pl.multiple_of(c * tq, tq)
        v = v_ref[pl.ds(off, tq), :]
        for mp in range(2):
            acc_ref[mp] = al_ref[slot, mp] * acc_ref[mp] + jnp.dot(
                p_ref[slot, mp], v, preferred_element_type=F32)

    @pl.when(qi == 0)
    def _one_chunk():
        scores(0, 0, masked=True)
        softmax(0)
        pv(0, 0)

    @pl.when(qi == 1)
    def _two_chunks():
        scores(0, 1)
        softmax(1)
        pv(0, 1)
        scores(1, 0, masked=True)
        softmax(0)
        pv(1, 0)

    @pl.when(qi >= 2)
    def _pipelined():
        odd = qi % 2

        @pl.when(odd == 0)
        def _fill_even():
            scores(0, 0)
            scores(1, 1)
            softmax(0)

        @pl.when(odd == 1)
        def _fill_odd():
            scores(0, 1)
            scores(1, 0)
            softmax(1)

        def odd_step(_, carry):
            scores(2, 1)
            softmax(0)
            pv(0, 1)
            return carry

        lax.fori_loop(0, odd, odd_step, 0)
        t0 = 2 + odd

        def two_steps(u, carry):
            t = t0 + 2 * u
            scores(t, 0)
            softmax(1)
            pv(t - 2, 0)
            scores(t + 1, 1)
            softmax(0)
            pv(t - 1, 1)
            return carry

        lax.fori_loop(0, (qi - t0) // 2, two_steps, 0)
        scores(qi, 0, masked=True)
        softmax(1)
        pv(qi - 2, 0)
        softmax(0)
        pv(qi - 1, 1)
        pv(qi, 0)

    lam = (jnp.exp(jnp.sum(lam_ref[0:1, :] * lam_ref[1:2, :]))
           - jnp.exp(jnp.sum(lam_ref[2:3, :] * lam_ref[3:4, :])) + LAMBDA_INIT)
    a1 = acc_ref[0]
    a2 = acc_ref[1]
    o = a1[:, :hw] / a1[:, hw:] - lam * (a2[:, :hw] / a2[:, hw:])
    o = o * lax.rsqrt(jnp.mean(o * o, axis=-1, keepdims=True) + SUBLN_EPS)
    o_ref[...] = (o * sw_ref[...] * (1.0 - LAMBDA_INIT)).astype(o_ref.dtype)


def _diff_attention(qk, v1, lam_params, subln_w, tq=512):
    s = qk.shape[0]
    nq = s // tq
    hw = 2 * HEAD_DIM
    kcol0 = ATTN_WIDTH // hw
    return pl.pallas_call(
        functools.partial(_attn_body, tq=tq),
        grid=(ATTN_HEADS, nq),
        in_specs=[pl.BlockSpec((tq, hw), lambda h, i: (i, h)),
                  pl.BlockSpec((s, hw), lambda h, i: (0, kcol0 + h)),
                  pl.BlockSpec((s, 2 * hw), lambda h, i: (0, h)),
                  pl.BlockSpec((4, HEAD_DIM), lambda h, i: (0, 0)),
                  pl.BlockSpec((1, hw), lambda h, i: (0, 0))],
        out_specs=pl.BlockSpec((tq, hw), lambda h, i: (i, h)),
        out_shape=jax.ShapeDtypeStruct((s, ATTN_WIDTH), BF16),
        scratch_shapes=[pltpu.VMEM((2, tq, hw), BF16),
                        pltpu.VMEM((2, tq, 1), F32),
                        pltpu.VMEM((2, tq, 2 * hw), F32),
                        pltpu.VMEM((2, 2, tq, tq), F32),
                        pltpu.VMEM((2, 2, tq, tq), BF16),
                        pltpu.VMEM((2, 2, tq, 1), F32)],
        compiler_params=_cparams(("parallel", "arbitrary")),
        name="diff_attn",
    )(qk, qk, v1, lam_params, subln_w.reshape(1, hw).astype(F32))


def _merge_body(x_ref, ys_ref, ya_ref, gs_ref, ga_ref, wps_ref, wpa_ref, wo_ref, n2_ref, wr_ref,
                x1_ref, h2_ref, eid_ref, gate_ref, cnt_ref):
    ps = jnp.dot(ys_ref[...], wps_ref[...], preferred_element_type=F32)
    pa = jnp.dot(ya_ref[...], wpa_ref[...], preferred_element_type=F32)
    merged = gs_ref[...].astype(F32) * ps + ga_ref[...].astype(F32) * pa
    x1 = x_ref[...] + jnp.dot(merged.astype(BF16), wo_ref[...], preferred_element_type=F32)
    x1_ref[...] = x1
    h2 = x1 * lax.rsqrt(jnp.mean(x1 * x1, axis=-1, keepdims=True) + NORM_EPS) * n2_ref[...]
    h2_ref[...] = h2
    wr = wr_ref[...]
    h_hi = h2.astype(BF16)
    h_lo = (h2 - h_hi.astype(F32)).astype(BF16)
    w_hi = wr.astype(BF16)
    w_lo = (wr - w_hi.astype(F32)).astype(BF16)
    tm = h2.shape[0]
    parts = jnp.dot(jnp.concatenate([h_hi, h_lo], axis=0), jnp.concatenate([w_hi, w_lo], axis=1),
                    preferred_element_type=F32)
    logits = (parts[:tm, :LANES] + parts[:tm, LANES:]) + (parts[tm:, :LANES] + parts[tm:, LANES:])
    lane = lax.broadcasted_iota(jnp.int32, logits.shape, 1)
    big = jnp.int32(1 << 20)
    ninf = jnp.float32(-jnp.inf)
    is_g = lane < N_GROUPS
    gl = jnp.where(is_g, logits, ninf)
    gm = jnp.max(gl, axis=-1, keepdims=True)
    g_idx = jnp.min(jnp.where(gl == gm, lane, big), axis=-1, keepdims=True)
    g_val = 1.0 / jnp.sum(jnp.where(is_g, jnp.exp(gl - gm), 0.0), axis=-1, keepdims=True)
    lo = N_GROUPS + g_idx * EXPERTS_PER_GROUP
    in_grp = (lane >= lo) & (lane < lo + EXPERTS_PER_GROUP)
    el = jnp.where(in_grp, logits, ninf)
    e1 = jnp.max(el, axis=-1, keepdims=True)
    i1 = jnp.min(jnp.where(el == e1, lane, big), axis=-1, keepdims=True)
    el2 = jnp.where(lane == i1, ninf, el)
    e2 = jnp.max(el2, axis=-1, keepdims=True)
    i2 = jnp.min(jnp.where(el2 == e2, lane, big), axis=-1, keepdims=True)
    t = jnp.exp(e2 - e1)
    p1 = 1.0 / (1.0 + t)
    p2 = t / (1.0 + t)
    gate_ref[...] = jnp.where(lane == 0, p1 * g_val, jnp.where(lane == 1, p2 * g_val, 0.0))

    @pl.when(pl.program_id(0) == 0)
    def _zero_counts():
        cnt_ref[...] = jnp.zeros(cnt_ref.shape, F32)

    ex1 = i1 - N_GROUPS
    ex2 = i2 - N_GROUPS
    oh1 = lane == ex1
    oh2 = lane == ex2
    onehot = jnp.where(oh1 | oh2, 1.0, 0.0)
    r_i =lax.broadcasted_iota(jnp.int32, (tm, tm), 0)
    c_i = lax.broadcasted_iota(jnp.int32, (tm, tm), 1)
    tri = jnp.where(c_i < r_i, 1.0, 0.0).astype(BF16)
    before = jnp.dot(tri, onehot.astype(BF16), preferred_element_type=F32) + cnt_ref[...]
    rank1 = jnp.sum(jnp.where(oh1, before, 0.0), axis=-1, keepdims=True).astype(jnp.int32)
    rank2 = jnp.sum(jnp.where(oh2, before, 0.0), axis=-1, keepdims=True).astype(jnp.int32)
    cnt_ref[...] = cnt_ref[...] + jnp.sum(onehot, axis=0, keepdims=True)
    eid_ref[...] = jnp.where(lane == 0, ex1, jnp.where(lane == 1, ex2,
                             jnp.where(lane == 2, rank1, jnp.where(lane == 3, rank2, 0))))


def _merge_route(x, y_ssm, y_attn, gates, wps, wpa, wo, n2w, wr, tm=256):
    s, d = x.shape
    nw = y_ssm.shape[1]
    const = lambda i: (0, 0)
    return pl.pallas_call(
        _merge_body,
        grid=(s // tm,),
        in_specs=[pl.BlockSpec((tm, d), lambda i: (i, 0)),
                  pl.BlockSpec((tm, nw), lambda i: (i, 0)),
                  pl.BlockSpec((tm, nw), lambda i: (i, 0)),
                  pl.BlockSpec((tm, d), lambda i: (i, 0)),
                  pl.BlockSpec((tm, d), lambda i: (i, 1)),
                  pl.BlockSpec((nw, d), const),
                  pl.BlockSpec((nw, d), const),
                  pl.BlockSpec((d, d), const),
                  pl.BlockSpec((1, d), const),
                  pl.BlockSpec((d, LANES), const)],
        out_specs=[pl.BlockSpec((tm, d), lambda i: (i, 0)),
                   pl.BlockSpec((tm, d), lambda i: (i, 0)),
                   pl.BlockSpec((tm, LANES), lambda i: (i, 0)),
                   pl.BlockSpec((tm, LANES), lambda i: (i, 0)),
                   pl.BlockSpec((1, LANES), const)],
        out_shape=[jax.ShapeDtypeStruct((s, d), F32),
                   jax.ShapeDtypeStruct((s, d), F32),
                   jax.ShapeDtypeStruct((s, LANES), jnp.int32),
                   jax.ShapeDtypeStruct((s, LANES), F32),
                   jax.ShapeDtypeStruct((1, LANES), F32)],
        compiler_params=_cparams(("arbitrary",)),
        name="merge_route",
    )(x, y_ssm, y_attn, gates, gates, wps, wpa, wo, n2w.reshape(1, d).astype(F32), wr)


def _row_copy(src_hbm, row, dst_ref, r, sem):
    return pltpu.make_async_copy(src_hbm.at[pl.ds(row, 1), :], dst_ref.at[pl.ds(r, 1), :], sem)


def _gather_rows(src_hbm, idx_ref, base, stride, dst_ref, sem, n):
    def start(r, c):
        _row_copy(src_hbm, idx_ref[base + stride * r], dst_ref, r, sem).start()
        return c

    lax.fori_loop(0, n, start, 0, unroll=8)


def _wait_rows(src_hbm, dst_ref, sem, n):
    pltpu.make_async_copy(src_hbm.at[pl.ds(0, n), :], dst_ref, sem).wait()


def _scatter_tok_body(dest_ref, tok_ref):
    def zero(i, c):
        tok_ref[i] = 0
        return c

    lax.fori_loop(0, tok_ref.shape[0], zero, 0, unroll=8)

    def put(i, c):
        tok_ref[dest_ref[i]] = lax.shift_right_logical(i, TOP_K_LOG2)
        return c

    lax.fori_loop(0, dest_ref.shape[0], put, 0, unroll=8)


def _scatter_tok(dest, buf_len):
    return pl.pallas_call(
        _scatter_tok_body,
        in_specs=[pl.BlockSpec(memory_space=pltpu.SMEM)],
        out_specs=pl.BlockSpec(memory_space=pltpu.SMEM),
        out_shape=jax.ShapeDtypeStruct((buf_len,), jnp.int32),
        name="scatter_tok",
    )(dest)


def _block_out_copy(ob_ref, slot, ys_hbm, blk, sem):
    return pltpu.make_async_copy(ob_ref.at[slot], ys_hbm.at[pl.ds(blk * MOE_BLK, MOE_BLK), :],
                                 sem.at[slot])


WEIGHT_DMA_PRIORITY = 1


def _weight_copies(e, slot, w_hbm, w_buf, wsem):
    return [pltpu.make_async_copy(h.at[e], b.at[slot], wsem.at[slot])
            for h, b in zip(w_hbm, w_buf)]


def _expert_body(fb_ref, nblk_ref, nb_ref, tok_ref, h2_hbm, wg_hbm, wu_hbm, wd_hbm, ys_hbm,
                 xb_ref, ob_ref, wg_ref, wu_ref, wd_ref, wgb_ref, wub_ref, wdb_ref,
                 gsem, osem, wsem, *, n_blocks):
    e = pl.program_id(0)
    nb = nb_ref[0]
    first = fb_ref[e]
    count = nblk_ref[e]
    w_hbm = (wg_hbm, wu_hbm, wd_hbm)
    w_buf = (wg_ref, wu_ref, wd_ref)

    @pl.when(e == 0)
    def _first_weights():
        for c in _weight_copies(0, 0, w_hbm, w_buf, wsem):
            c.start(priority=WEIGHT_DMA_PRIORITY)

    @pl.when(e + 1 < pl.num_programs(0))
    def _next_weights():
        for c in _weight_copies(e + 1, (e + 1) % 2, w_hbm, w_buf, wsem):
            c.start(priority=WEIGHT_DMA_PRIORITY)

    for c in _weight_copies(e, e % 2, w_hbm, w_buf, wsem):
        c.wait()

    @pl.when(count > 0)
    def _expert():
        wgb_ref[...] = wg_ref[e % 2].astype(BF16)
        wub_ref[...] = wu_ref[e % 2].astype(BF16)
        wdb_ref[...] = wd_ref[e % 2].astype(BF16)

        @pl.when(first == 0)
        def _first_rows():
            _gather_rows(h2_hbm, tok_ref, 0, 1, xb_ref.at[0], gsem.at[0], MOE_BLK)

        def block(t, carry):
            g = first + t
            cur = g % 2
            nxt = 1 - cur
            _wait_rows(h2_hbm, xb_ref.at[cur], gsem.at[cur], MOE_BLK)

            @pl.when(g >= 2)
            def _staging_free():
                _block_out_copy(ob_ref, cur, ys_hbm, g - 2, osem).wait()

            base = (g + 1) * MOE_BLK
            for r in range(MOE_BLK):
                _row_copy(h2_hbm, tok_ref[base + r], xb_ref.at[nxt], r, gsem.at[nxt]).start()
            xb = xb_ref[cur].astype(BF16)
            hg = jnp.dot(xb, wgb_ref[...], preferred_element_type=F32)
            hu = jnp.dot(xb, wub_ref[...], preferred_element_type=F32)
            act = (jax.nn.silu(hg) * hu).astype(BF16)
            ob_ref[cur] = jnp.dot(act, wdb_ref[...], preferred_element_type=F32)
            _block_out_copy(ob_ref, cur, ys_hbm, g, osem).start()
            return carry

        lax.fori_loop(0, count, block, 0)

    @pl.when(e == pl.num_programs(0) - 1)
    def _finish():
        _wait_rows(h2_hbm, xb_ref.at[nb % 2], gsem.at[nb % 2], MOE_BLK)
        _block_out_copy(ob_ref, (nb - 1) % 2, ys_hbm, nb - 1, osem).wait()

        @pl.when(nb >= 2)
        def _():
            _block_out_copy(ob_ref, nb % 2, ys_hbm, nb - 2, osem).wait()

        ob_ref[0] = jnp.zeros(ob_ref.shape[1:], ob_ref.dtype)

        def zero_block(g, carry):
            _block_out_copy(ob_ref, 0, ys_hbm, g, osem).start()
            return carry

        lax.fori_loop(nb, n_blocks, zero_block, 0)

        def zero_wait(g, carry):
            _block_out_copy(ob_ref, 0, ys_hbm, g, osem).wait()
            return carry

        lax.fori_loop(nb, n_blocks, zero_wait, 0)


def _experts(h2, first_blk, n_blk, n_used, buf_tok, wg, wu, wd):
    s, d = h2.shape
    buf_len = buf_tok.shape[0]
    n_blocks = buf_len // MOE_BLK
    return pl.pallas_call(
        functools.partial(_expert_body, n_blocks=n_blocks),
        grid_spec=pltpu.PrefetchScalarGridSpec(
            num_scalar_prefetch=4,
            grid=(N_EXPERTS,),
            in_specs=[pl.BlockSpec(memory_space=pl.ANY)] * 4,
            out_specs=pl.BlockSpec(memory_space=pl.ANY),
            scratch_shapes=[pltpu.VMEM((2, MOE_BLK, d), F32), pltpu.VMEM((2, MOE_BLK, d), F32),
                            pltpu.VMEM((2, d, D_FF), F32), pltpu.VMEM((2, d, D_FF), F32),
                            pltpu.VMEM((2, D_FF, d), F32),
                            pltpu.VMEM((d, D_FF), BF16), pltpu.VMEM((d, D_FF), BF16),
                            pltpu.VMEM((D_FF, d), BF16),
                            pltpu.SemaphoreType.DMA((2,)), pltpu.SemaphoreType.DMA((2,)),
                            pltpu.SemaphoreType.DMA((2,))]),
        out_shape=jax.ShapeDtypeStruct((buf_len, d), F32),
        compiler_params=_cparams(("arbitrary",)),
        name="experts",
    )(first_blk, n_blk, n_used, buf_tok, h2, wg, wu, wd)


def _combine_body(pos_ref, x1_ref, gate_ref, fw_ref, ys_hbm, o_ref, g_ref, sem, *, tm):
    i = pl.program_id(0)

    def gather(tile, slot):
        for k in range(TOP_K):
            _gather_rows(ys_hbm, pos_ref, TOP_K * tile * tm + k, TOP_K, g_ref.at[slot, k],
                         sem.at[slot], tm)

    @pl.when(i == 0)
    def _first():
        gather(0, 0)

    @pl.when(i + 1 < pl.num_programs(0))
    def _prefetch():
        gather(i + 1, (i + 1) % 2)

    cur = i % 2
    for k in range(TOP_K):
        _wait_rows(ys_hbm, g_ref.at[cur, k], sem.at[cur], tm)
    gate = gate_ref[...]
    x = x1_ref[...] + (gate[:, 0:1] * g_ref[cur, 0] + gate[:, 1:2] * g_ref[cur, 1])
    y = x * lax.rsqrt(jnp.mean(x * x, axis=-1, keepdims=True) + NORM_EPS)
    o_ref[...] = y * fw_ref[...]


def _combine(x1, ys, pos, gate, fw, tm=512):
    s, d = x1.shape
    return pl.pallas_call(
        functools.partial(_combine_body, tm=tm),
        grid_spec=pltpu.PrefetchScalarGridSpec(
            num_scalar_prefetch=1,
            grid=(s // tm,),
            in_specs=[pl.BlockSpec((tm, d), lambda i, p: (i, 0)),
                      pl.BlockSpec((tm, LANES), lambda i, p: (i, 0)),
                      pl.BlockSpec((1, d), lambda i, p: (0, 0)),
                      pl.BlockSpec(memory_space=pl.ANY)],
            out_specs=pl.BlockSpec((tm, d), lambda i, p: (i, 0)),
            scratch_shapes=[pltpu.VMEM((2, TOP_K, tm, d), F32),
                            pltpu.SemaphoreType.DMA((2,))]),
        out_shape=jax.ShapeDtypeStruct((s, d), F32),
        compiler_params=_cparams(("arbitrary",)),
        name="combine_norm",
    )(pos, x1, gate, fw.reshape(1, d).astype(F32), ys)


def _dispatch_tables(eid, counts):
    n_tok = eid.shape[0]
    n_slots = n_tok * TOP_K
    buf_len = n_slots + N_EXPERTS * MOE_BLK
    counts = counts.astype(jnp.int32)
    padded = ((counts + MOE_BLK - 1) // MOE_BLK) * MOE_BLK
    padded_ends = jnp.cumsum(padded)
    padded_starts = padded_ends - padded
    experts = eid[:, :TOP_K]
    onehot = experts[:, :, None] == jnp.arange(N_EXPERTS, dtype=jnp.int32)
    dest = jnp.sum(jnp.where(onehot, padded_starts, 0), axis=-1) + eid[:, TOP_K:2 * TOP_K]
    n_used = (padded_ends[-1] // MOE_BLK).astype(jnp.int32).reshape(1)
    first_blk = (padded_starts // MOE_BLK).astype(jnp.int32)
    n_blk = (padded // MOE_BLK).astype(jnp.int32)
    return first_blk, n_blk, n_used, dest.reshape(n_slots).astype(jnp.int32), buf_len


def kernel(x, positions, norm1_w, w_in, ssm_lambda_re, ssm_lambda_im, ssm_log_dt, ssm_b_re, ssm_b_im, ssm_c_re, ssm_c_im, ssm_d, ssm_glu_w, ssm_glu_b, attn_lambda_q1, attn_lambda_k1, attn_lambda_q2, attn_lambda_k2, attn_subln_w, w_proj_ssm, w_proj_attn, w_out, norm2_w, router_group_w, router_expert_w, expert_w_gate, expert_w_up, expert_w_down, final_norm_w):
    bsz, seq, d = x.shape
    depth = norm1_w.shape[0]
    xs = x.reshape(bsz * seq, d)
    pos = positions.reshape(bsz * seq)
    o_q = SSM_WIDTH
    o_v = o_q + 2 * ATTN_WIDTH
    o_g = o_v + ATTN_WIDTH
    for l in range(depth):
        h, rope_cos, rope_sin = _rmsnorm_rope(xs, norm1_w[l].astype(F32), pos, BF16)
        w_in_b = w_in[l].astype(F32)
        u = _mm(h, w_in_b, 0, SSM_WIDTH, "none", F32)
        qk = _mm_rope(h, w_in_b, rope_cos, rope_sin, o_q)
        v = _mm_vones(h, w_in_b, o_v)
        gates = _mm(h, w_in_b, o_g, 2 * D_MODEL, "sigmoid", F32)

        tables = _s5_tables(ssm_lambda_re[l], ssm_lambda_im[l], ssm_log_dt[l],
                            ssm_b_re[l], ssm_b_im[l], ssm_c_re[l], ssm_c_im[l])
        y = _s5_core(u, tables, ssm_d[l])
        y_ssm = _glu(y, ssm_glu_w[l].astype(BF16), ssm_glu_b[l])

        lam_params = jnp.stack([attn_lambda_q1[l], attn_lambda_k1[l],
                                attn_lambda_q2[l], attn_lambda_k2[l]]).astype(F32)
        y_attn = _diff_attention(qk, v, lam_params, attn_subln_w[l])

        wr = jnp.concatenate([router_group_w[l], router_expert_w[l]], axis=1).astype(F32)
        wr = jnp.pad(wr, ((0, 0), (0, LANES - wr.shape[1])))
        x1, h2, eid, gate, counts = _merge_route(
            xs, y_ssm, y_attn, gates, w_proj_ssm[l].astype(BF16), w_proj_attn[l].astype(BF16),
            w_out[l].astype(BF16), norm2_w[l], wr)

        first_blk, n_blk, n_used, dest, buf_len = _dispatch_tables(eid, counts[0, :N_EXPERTS])
        buf_tok = _scatter_tok(dest, buf_len)
        ys = _experts(h2, first_blk, n_blk, n_used, buf_tok, expert_w_gate[l].astype(F32),
                      expert_w_up[l].astype(F32), expert_w_down[l].astype(F32))
        last = l == depth - 1
        assert last, "DEPTH > 1 needs an un-normalised combine"
        xs = _combine(x1, ys, dest, gate, final_norm_w)
    return xs.reshape(bsz, seq, d)
```

```python
import functools
import math

import jax
import jax.numpy as jnp
from jax import lax
from jax.experimental import pallas as pl
from jax.experimental.pallas import tpu as pltpu

F32 = jnp.float32
BF16 = jnp.bfloat16

D_MODEL = 2048
SSM_WIDTH = 1024
SSM_GROUP = 16
SSM_GROUPS = 64
SSM_STATE = 64
ATTN_WIDTH = 1024
ATTN_HEADS = 8
HEAD_DIM = 64
ROPE_THETA = 10000.0
N_GROUPS = 4
EXPERTS_PER_GROUP = 8
N_EXPERTS = 32
TOP_K = 2
TOP_K_LOG2 = 1
D_FF = 512
NORM_EPS = 1e-6
SUBLN_EPS = 1e-5
LAMBDA_INIT = 0.8 - 0.6 * math.exp(-0.3 * 0)

LANES = 128
VMEM_LIMIT = 48 * 1024 * 1024

SSM_CHUNK = 16
SSM_GB = 8
MOE_BLK = 256


def _cparams(sem):
    return pltpu.CompilerParams(dimension_semantics=sem, vmem_limit_bytes=VMEM_LIMIT)


def _rmsnorm_body(x_ref, w_ref, pos_ref, invf_ref, o_ref, cos_ref, sin_ref):
    x = x_ref[...]
    y = x * lax.rsqrt(jnp.mean(x * x, axis=-1, keepdims=True) + NORM_EPS)
    o_ref[...] = (y * w_ref[...]).astype(o_ref.dtype)
    ang = pos_ref[...].astype(F32) * invf_ref[...]
    lane = lax.broadcasted_iota(jnp.int32, ang.shape, 1)
    low = (lane % HEAD_DIM) < (HEAD_DIM // 2)
    cos_ref[...] = jnp.cos(ang)
    sin_ref[...] = jnp.where(low, -jnp.sin(ang), jnp.sin(ang))


def _rmsnorm_rope(x, w, positions, out_dtype, tm=512):
    s, d = x.shape
    inv_freq = 1.0 / (ROPE_THETA ** (jnp.arange(0, HEAD_DIM, 2, dtype=F32) / HEAD_DIM))
    invf = jnp.tile(inv_freq, LANES // (HEAD_DIM // 2)).reshape(1, LANES)
    return pl.pallas_call(
        _rmsnorm_body,
        grid=(s // tm,),
        in_specs=[pl.BlockSpec((tm, d), lambda i: (i, 0)),
                  pl.BlockSpec((1, d), lambda i: (0, 0)),
                  pl.BlockSpec((tm, 1), lambda i: (i, 0)),
                  pl.BlockSpec((1, LANES), lambda i: (0, 0))],
        out_specs=[pl.BlockSpec((tm, d), lambda i: (i, 0)),
                   pl.BlockSpec((tm, LANES), lambda i: (i, 0)),
                   pl.BlockSpec((tm, LANES), lambda i: (i, 0))],
        out_shape=[jax.ShapeDtypeStruct((s, d), out_dtype),
                   jax.ShapeDtypeStruct((s, LANES), F32),
                   jax.ShapeDtypeStruct((s, LANES), F32)],
        compiler_params=_cparams(("parallel",)),
        name="rmsnorm",
    )(x, w.reshape(1, d), positions.reshape(s, 1), invf)


def _proj(h_ref, w_ref, wb_ref, row_axis):
    @pl.when(pl.program_id(row_axis) == 0)
    def _cast():
        wb_ref[...] = w_ref[...].astype(BF16)

    return jnp.dot(h_ref[...], wb_ref[...], preferred_element_type=F32)


def _mm_body(h_ref, w_ref, o_ref, wb_ref, *, act):
    acc = _proj(h_ref, w_ref, wb_ref, 1)
    if act == "sigmoid":
        acc = jax.nn.sigmoid(acc)
    o_ref[...] = acc.astype(o_ref.dtype)


def _mm(h, w, col0, ncols, act, out_dtype, tm=1024, tn=1024):
    s, k = h.shape
    jb = col0 // tn
    return pl.pallas_call(
        functools.partial(_mm_body, act=act),
        grid=(ncols // tn, s // tm),
        in_specs=[pl.BlockSpec((tm, k), lambda j, i: (i, 0)),
                  pl.BlockSpec((k, tn), lambda j, i: (0, j + jb))],
        out_specs=pl.BlockSpec((tm, tn), lambda j, i: (i, j)),
        out_shape=jax.ShapeDtypeStruct((s, ncols), out_dtype),
        scratch_shapes=[pltpu.VMEM((k, tn), BF16)],
        compiler_params=_cparams(("arbitrary", "arbitrary")),
        name="mm_" + act,
    )(h, w)


def _mm_vones_body(h_ref, w_ref, o_ref, wb_ref):
    acc = _proj(h_ref, w_ref, wb_ref, 0)
    hw = 2 * HEAD_DIM
    ones = jnp.ones((acc.shape[0], hw), o_ref.dtype)
    for hd in range(acc.shape[1] // hw):
        o_ref[:, 2 * hd * hw:(2 * hd + 1) * hw] = acc[:, hd * hw:(hd + 1) * hw].astype(o_ref.dtype)
        o_ref[:, (2 * hd + 1) * hw:(2 * hd + 2) * hw] = ones


def _mm_vones(h, w, col0, tm=1024):
    s, k = h.shape
    tn = ATTN_WIDTH
    jb = col0 // tn
    return pl.pallas_call(
        _mm_vones_body,
        grid=(s // tm,),
        in_specs=[pl.BlockSpec((tm, k), lambda i: (i, 0)),
                  pl.BlockSpec((k, tn), lambda i: (0, jb))],
        out_specs=pl.BlockSpec((tm, 2 * tn), lambda i: (i, 0)),
        out_shape=jax.ShapeDtypeStruct((s, 2 * tn), BF16),
        scratch_shapes=[pltpu.VMEM((k, tn), BF16)],
        compiler_params=_cparams(("arbitrary",)),
        name="mm_vones",
    )(h, w)


def _mm_rope_body(h_ref, w_ref, cos_ref, sin_ref, o_ref, wb_ref, *, tn):
    j = pl.program_id(0)
    acc = _proj(h_ref, w_ref, wb_ref, 1)
    scale = jnp.where(j == 0, HEAD_DIM ** -0.5, 1.0).astype(F32)
    lane = lax.broadcasted_iota(jnp.int32, cos_ref.shape, 1)
    low = (lane % HEAD_DIM) < (HEAD_DIM // 2)
    cos = cos_ref[...] * scale
    sin = sin_ref[...] * scale
    for c in range(tn // LANES):
        t = acc[:, c * LANES:(c + 1) * LANES]
        partner = jnp.where(low, pltpu.roll(t, LANES - HEAD_DIM // 2, 1),
                            pltpu.roll(t, HEAD_DIM // 2, 1))
        o_ref[:, c * LANES:(c + 1) * LANES] = (t * cos + partner * sin).astype(o_ref.dtype)


def _mm_rope(h, w, cos, sin, col0, tm=1024, tn=1024):
    s, k = h.shape
    jb = col0 // tn
    return pl.pallas_call(
        functools.partial(_mm_rope_body, tn=tn),
        grid=(2, s // tm),
        in_specs=[pl.BlockSpec((tm, k), lambda j, i: (i, 0)),
                  pl.BlockSpec((k, tn), lambda j, i: (0, j + jb)),
                  pl.BlockSpec((tm, LANES), lambda j, i: (i, 0)),
                  pl.BlockSpec((tm, LANES), lambda j, i: (i, 0))],
        out_specs=pl.BlockSpec((tm, tn), lambda j, i: (i, j)),
        out_shape=jax.ShapeDtypeStruct((s, 2 * tn), BF16),
        scratch_shapes=[pltpu.VMEM((k, tn), BF16)],
        compiler_params=_cparams(("arbitrary", "arbitrary")),
        name="mm_rope",
    )(h, w, cos, sin)


def _s5_tables(lam_re, lam_im, log_dt, b_re, b_im, c_re, c_im):
    L, G, P, M = SSM_CHUNK, SSM_GROUPS, SSM_STATE, SSM_GROUP
    hi = lax.Precision.HIGHEST
    lr, li = lam_re.astype(F32), lam_im.astype(F32)
    dt = jnp.exp(log_dt.astype(F32))[:, None]
    n = jnp.arange(L + 1, dtype=F32)[:, None, None]
    mag = jnp.exp(lr * dt * n)
    pr = mag * jnp.cos(li * dt * n)
    pi = mag * jnp.sin(li * dt * n)
    nr, ni = pr[1] - 1.0, pi[1]
    den = lr * lr + li * li
    f_re = (nr * lr + ni * li) / den
    f_im = (ni * lr - nr * li) / den
    br, bi = b_re.astype(F32), b_im.astype(F32)
    bbr = f_re[..., None] * br - f_im[..., None] * bi
    bbi = f_re[..., None] * bi + f_im[..., None] * br
    cr, ci = c_re.astype(F32), c_im.astype(F32)

    nb, gb = G // SSM_GB, SSM_GB
    pr_g = jnp.transpose(pr, (1, 0, 2))
    pi_g = jnp.transpose(pi, (1, 0, 2))
    bbr_t = jnp.transpose(bbr, (0, 2, 1))
    bbi_t = jnp.transpose(bbi, (0, 2, 1))

    abr = pr_g[:, :L, None, :] * bbr_t[:, None] - pi_g[:, :L, None, :] * bbi_t[:, None]
    abi = pr_g[:, :L, None, :] * bbi_t[:, None] + pi_g[:, :L, None, :] * bbr_t[:, None]
    kmat = (jnp.einsum('gnip,gop->gnio', abr, cr, precision=hi)
            - jnp.einsum('gnip,gop->gnio', abi, ci, precision=hi))
    kpad = jnp.pad(kmat, ((0, 0), (1, 0), (0, 0), (0, 0)))
    kq = jnp.stack([jnp.stack([kpad[:, 1 + c2 - r2::2][:, :L // 2] for c2 in range(2)], axis=3)
                    for r2 in range(2)], axis=2)
    kk_src = jnp.transpose(kq.reshape(nb, gb, L // 2, 2, M, 2, M),
                           (0, 2, 3, 5, 4, 1, 6)).reshape(nb, L // 2, 2, 2, M, gb * M)

    ab1r = pr_g[:, 1, None, :] * bbr_t - pi_g[:, 1, None, :] * bbi_t
    ab1i = pr_g[:, 1, None, :] * bbi_t + pi_g[:, 1, None, :] * bbr_t
    b4 = jnp.stack([jnp.stack([ab1r, ab1i], axis=2), jnp.stack([bbr_t, bbi_t], axis=2)],
                   axis=1)
    bb_src = jnp.transpose(b4.reshape(nb, gb, 2, M, 2, P),
                           (0, 2, 4, 3, 1, 5)).reshape(nb, 2, 2, M, gb * P)

    cr_t = jnp.transpose(cr, (0, 2, 1))
    ci_t = jnp.transpose(ci, (0, 2, 1))
    car = cr_t * pr_g[:, 1, :, None] - ci_t * pi_g[:, 1, :, None]
    cai = cr_t * pi_g[:, 1, :, None] + ci_t * pr_g[:, 1, :, None]
    c4 = jnp.stack([jnp.stack([cr_t, car], axis=2), jnp.stack([-ci_t, -cai], axis=2)],
                   axis=1)
    cc_src = jnp.transpose(c4.reshape(nb, gb, 2, P, 2, M),
                           (0, 2, 4, 3, 1, 5)).reshape(nb, 2, 2, P, gb * M)

    ap = jnp.stack([pr.reshape(L + 1, nb, gb * P), pi.reshape(L + 1, nb, gb * P)], axis=2)
    return kk_src, bb_src, cc_src, ap.reshape(L + 1, 2 * G * P)


def _cmul(ar, ai, zr, zi):
    return ar * zr - ai * zi, ar * zi + ai * zr


def _block_diag(src, rows_per_group, cols_per_group):
    t = jnp.concatenate([src] * SSM_GB, axis=0)
    row = lax.broadcasted_iota(jnp.int32, t.shape, 0)
    col = lax.broadcasted_iota(jnp.int32, t.shape, 1)
    return jnp.where(row // rows_per_group == col // cols_per_group, t, 0.0).astype(BF16)


def _s5_body(u_ref, kk_src, bb_src, cc_src, ap_ref, d_ref, y_ref,
             kk_ref, bb_ref, cc_ref, lhs_ref, er_ref, ei_ref, xr_ref, xi_ref):
    L, M, P = SSM_CHUNK, SSM_GROUP, SSM_STATE
    nc = u_ref.shape[0] // L
    hp = SSM_GB * P
    gm = SSM_GB * M
    nq = L // 2
    for a in range(2):
        for b in range(2):
            for dd in range(nq):
                kk_ref[dd, a * gm:(a + 1) * gm, b * gm:(b + 1) * gm] = _block_diag(
                    kk_src[0, dd, a, b], M, M)
            bb_ref[a * gm:(a + 1) * gm, b * hp:(b + 1) * hp] = _block_diag(bb_src[0, a, b], M, P)
            cc_ref[a * hp:(a + 1) * hp, b * gm:(b + 1) * gm] = _block_diag(cc_src[0, a, b], P, M)
    for q in range(nq):
        lhs_ref[q] = jnp.concatenate(
            [u_ref[pl.ds(2 * q, nc, stride=L), :], u_ref[pl.ds(2 * q + 1, nc, stride=L), :]],
            axis=1).astype(BF16)

    er = jnp.zeros((nc, hp), F32)
    ei = jnp.zeros((nc, hp), F32)
    for q in range(nq):
        z = jnp.dot(lhs_ref[q], bb_ref[...], preferred_element_type=F32)
        n = L - 2 - 2 * q
        dr, di = _cmul(ap_ref[n:n + 1, :hp], ap_ref[n:n + 1, hp:], z[:, :hp], z[:, hp:])
        er = er + dr
        ei = ei + di
    er_ref[...] = er
    ei_ref[...] = ei

    ar = ap_ref[L:L + 1, :hp]
    ai = ap_ref[L:L + 1, hp:]

    def step(c, carry):
        xr, xi = carry
        xr_ref[pl.ds(c, 1), :] = xr
        xi_ref[pl.ds(c, 1), :] = xi
        nr, ni = _cmul(ar, ai, xr, xi)
        return nr + er_ref[pl.ds(c, 1), :], ni + ei_ref[pl.ds(c, 1), :]

    zero = jnp.zeros((1, hp), F32)
    lax.fori_loop(0, nc, step, (zero, zero))

    for qq in range(nq):
        n = 2 * qq + 1
        wr, wi = _cmul(ap_ref[n:n + 1, :hp], ap_ref[n:n + 1, hp:], xr_ref[...], xi_ref[...])
        w = jnp.concatenate([wr, wi], axis=1).astype(BF16)
        acc = jnp.dot(w, cc_ref[...], preferred_element_type=F32)
        for q in range(qq + 1):
            acc = acc + jnp.dot(lhs_ref[q], kk_ref[qq - q], preferred_element_type=F32)
        for r in range(2):
            j = 2 * qq + r
            y = acc[:, r * LANES:(r + 1) * LANES] + d_ref[...] * u_ref[pl.ds(j, nc, stride=L), :]
            y_ref[pl.ds(j, nc, stride=L), :] = jax.nn.gelu(y)


def _s5_core(u, tables, d_skip):
    s, width = u.shape
    L, P, M, gb = SSM_CHUNK, SSM_STATE, SSM_GROUP, SSM_GB
    nc = s // L
    nb = width // (gb * M)
    kk_src, bb_src, cc_src, ap = tables
    return pl.pallas_call(
        _s5_body,
        grid=(nb,),
        in_specs=[pl.BlockSpec((s, gb * M), lambda b: (0, b)),
                  pl.BlockSpec((1, L // 2, 2, 2, M, gb * M), lambda b: (b, 0, 0, 0, 0, 0)),
                  pl.BlockSpec((1, 2, 2, M, gb * P), lambda b: (b, 0, 0, 0, 0)),
                  pl.BlockSpec((1, 2, 2, P, gb * M), lambda b: (b, 0, 0, 0, 0)),
                  pl.BlockSpec((L + 1, 2 * gb * P), lambda b: (0, b)),
                  pl.BlockSpec((1, gb * M), lambda b: (0, b))],
        out_specs=pl.BlockSpec((s, gb * M), lambda b: (0, b)),
        out_shape=jax.ShapeDtypeStruct((s, width), F32),
        scratch_shapes=[pltpu.VMEM((L // 2, 2 * gb * M, 2 * gb * M), BF16),
                        pltpu.VMEM((2 * gb * M, 2 * gb * P), BF16),
                        pltpu.VMEM((2 * gb * P, 2 * gb * M), BF16),
                        pltpu.VMEM((L // 2, nc, 2 * gb * M), BF16)]
        + [pltpu.VMEM((nc, gb * P), F32) for _ in range(4)],
        compiler_params=_cparams(("parallel",)),
        name="s5_scan",
    )(u, kk_src, bb_src, cc_src, ap, d_skip.astype(F32).reshape(1, width))


def _glu_body(y_ref, w_ref, b_ref, o_ref):
    y = y_ref[...]
    z = jnp.dot(y.astype(BF16), w_ref[...], preferred_element_type=F32) + b_ref[...]
    o_ref[...] = (y * jax.nn.sigmoid(z)).astype(o_ref.dtype)


def _glu(y, w, b, tm=512):
    s, n = y.shape
    return pl.pallas_call(
        _glu_body,
        grid=(s // tm,),
        in_specs=[pl.BlockSpec((tm, n), lambda i: (i, 0)),
                  pl.BlockSpec((n, n), lambda i: (0, 0)),
                  pl.BlockSpec((1, n), lambda i: (0, 0))],
        out_specs=pl.BlockSpec((tm, n), lambda i: (i, 0)),
        out_shape=jax.ShapeDtypeStruct((s, n), BF16),
        compiler_params=_cparams(("parallel",)),
        name="s5_glu",
    )(y, w, b.reshape(1, n).astype(F32))


def _attn_body(q_ref, k_ref, v_ref, lam_ref, sw_ref, o_ref,
               qs_ref, m_ref, acc_ref, s_ref, p_ref, al_ref, *, tq):
    qi = pl.program_id(1)
    hw = 2 * HEAD_DIM
    q = q_ref[...]
    lane = lax.broadcasted_iota(jnp.int32, q.shape, 1)
    zero = jnp.zeros_like(q)
    qs_ref[0] = jnp.where(lane < HEAD_DIM, q, zero)
    qs_ref[1] = jnp.where(lane >= HEAD_DIM, q, zero)
    m_ref[...] = jnp.full(m_ref.shape, -jnp.inf, F32)
    acc_ref[...] = jnp.zeros(acc_ref.shape, F32)

    def scores(c, slot, masked=False):
        off = pl.multiple_of(c * tq, tq)
        k = k_ref[pl.ds(off, tq), :]
        for mp in range(2):
            s = lax.dot_general(qs_ref[mp], k, (((1,), (1,)), ((), ())),
                                preferred_element_type=F32)
            if masked:
                row = lax.broadcasted_iota(jnp.int32, s.shape, 0)
                col = lax.broadcasted_iota(jnp.int32, s.shape, 1)
                s = jnp.where(col <= row, s, -jnp.inf)
            s_ref[slot, mp] = s

    def softmax(slot):
        for mp in range(2):
            s = s_ref[slot, mp]
            m_old = m_ref[mp]
            m_new = jnp.maximum(m_old, jnp.max(s, axis=-1, keepdims=True))
            al_ref[slot, mp] = jnp.exp(m_old - m_new)
            p_ref[slot, mp] = jnp.exp(s - m_new).astype(BF16)
            m_ref[mp] = m_new

    def pv(c, slot):
        off = pl.multiple_of(c * tq, tq)
        v = v_ref[pl.ds(off, tq), :]
        for mp in range(2):
            acc_ref[mp] = al_ref[slot, mp] * acc_ref[mp] + jnp.dot(
                p_ref[slot, mp], v, preferred_element_type=F32)

    @pl.when(qi == 0)
    def _one_chunk():
        scores(0, 0, masked=True)
        softmax(0)
        pv(0, 0)

    @pl.when(qi == 1)
    def _two_chunks():
        scores(0, 1)
        softmax(1)
        pv(0, 1)
        scores(1, 0, masked=True)
        softmax(0)
        pv(1, 0)

    @pl.when(qi >= 2)
    def _pipelined():
        odd = qi % 2

        @pl.when(odd == 0)
        def _fill_even():
            scores(0, 0)
            scores(1, 1)
            softmax(0)

        @pl.when(odd == 1)
        def _fill_odd():
            scores(0, 1)
            scores(1, 0)
            softmax(1)

        def odd_step(_, carry):
            scores(2, 1)
            softmax(0)
            pv(0, 1)
            return carry

        lax.fori_loop(0, odd, odd_step, 0)
        t0 = 2 + odd

        def two_steps(u, carry):
            t = t0 + 2 * u
            scores(t, 0)
            softmax(1)
            pv(t - 2, 0)
            scores(t + 1, 1)
            softmax(0)
            pv(t - 1, 1)
            return carry

        lax.fori_loop(0, (qi - t0) // 2, two_steps, 0)
        scores(qi, 0, masked=True)
        softmax(1)
        pv(qi - 2, 0)
        softmax(0)
        pv(qi - 1, 1)
        pv(qi, 0)

    lam = (jnp.exp(jnp.sum(lam_ref[0:1, :] * lam_ref[1:2, :]))
           - jnp.exp(jnp.sum(lam_ref[2:3, :] * lam_ref[3:4, :])) + LAMBDA_INIT)
    a1 = acc_ref[0]
    a2 = acc_ref[1]
    o = a1[:, :hw] / a1[:, hw:] - lam * (a2[:, :hw] / a2[:, hw:])
    o = o * lax.rsqrt(jnp.mean(o * o, axis=-1, keepdims=True) + SUBLN_EPS)
    o_ref[...] = (o * sw_ref[...] * (1.0 - LAMBDA_INIT)).astype(o_ref.dtype)


def _diff_attention(qk, v1, lam_params, subln_w, tq=512):
    s = qk.shape[0]
    nq = s // tq
    hw = 2 * HEAD_DIM
    kcol0 = ATTN_WIDTH // hw
    return pl.pallas_call(
        functools.partial(_attn_body, tq=tq),
        grid=(ATTN_HEADS, nq),
        in_specs=[pl.BlockSpec((tq, hw), lambda h, i: (i, h)),
                  pl.BlockSpec((s, hw), lambda h, i: (0, kcol0 + h)),
                  pl.BlockSpec((s, 2 * hw), lambda h, i: (0, h)),
                  pl.BlockSpec((4, HEAD_DIM), lambda h, i: (0, 0)),
                  pl.BlockSpec((1, hw), lambda h, i: (0, 0))],
        out_specs=pl.BlockSpec((tq, hw), lambda h, i: (i, h)),
        out_shape=jax.ShapeDtypeStruct((s, ATTN_WIDTH), BF16),
        scratch_shapes=[pltpu.VMEM((2, tq, hw), BF16),
                        pltpu.VMEM((2, tq, 1), F32),
                        pltpu.VMEM((2, tq, 2 * hw), F32),
                        pltpu.VMEM((2, 2, tq, tq), F32),
                        pltpu.VMEM((2, 2, tq, tq), BF16),
                        pltpu.VMEM((2, 2, tq, 1), F32)],
        compiler_params=_cparams(("parallel", "arbitrary")),
        name="diff_attn",
    )(qk, qk, v1, lam_params, subln_w.reshape(1, hw).astype(F32))


def _merge_body(x_ref, ys_ref, ya_ref, gs_ref, ga_ref, wps_ref, wpa_ref, wo_ref, n2_ref, wr_ref,
                x1_ref, h2_ref, eid_ref, gate_ref, cnt_ref):
    ps = jnp.dot(ys_ref[...], wps_ref[...], preferred_element_type=F32)
    pa = jnp.dot(ya_ref[...], wpa_ref[...], preferred_element_type=F32)
    merged = gs_ref[...].astype(F32) * ps + ga_ref[...].astype(F32) * pa
    x1 = x_ref[...] + jnp.dot(merged.astype(BF16), wo_ref[...], preferred_element_type=F32)
    x1_ref[...] = x1
    h2 = x1 * lax.rsqrt(jnp.mean(x1 * x1, axis=-1, keepdims=True) + NORM_EPS) * n2_ref[...]
    h2_ref[...] = h2
    wr = wr_ref[...]
    h_hi = h2.astype(BF16)
    h_lo = (h2 - h_hi.astype(F32)).astype(BF16)
    w_hi = wr.astype(BF16)
    w_lo = (wr - w_hi.astype(F32)).astype(BF16)
    tm = h2.shape[0]
    parts = jnp.dot(jnp.concatenate([h_hi, h_lo], axis=0), jnp.concatenate([w_hi, w_lo], axis=1),
                    preferred_element_type=F32)
    logits = (parts[:tm, :LANES] + parts[:tm, LANES:]) + (parts[tm:, :LANES] + parts[tm:, LANES:])
    lane = lax.broadcasted_iota(jnp.int32, logits.shape, 1)
    big = jnp.int32(1 << 20)
    ninf = jnp.float32(-jnp.inf)
    is_g = lane < N_GROUPS
    gl = jnp.where(is_g, logits, ninf)
    gm = jnp.max(gl, axis=-1, keepdims=True)
    g_idx = jnp.min(jnp.where(gl == gm, lane, big), axis=-1, keepdims=True)
    g_val = 1.0 / jnp.sum(jnp.where(is_g, jnp.exp(gl - gm), 0.0), axis=-1, keepdims=True)
    lo = N_GROUPS + g_idx * EXPERTS_PER_GROUP
    in_grp = (lane >= lo) & (lane < lo + EXPERTS_PER_GROUP)
    el = jnp.where(in_grp, logits, ninf)
    e1 = jnp.max(el, axis=-1, keepdims=True)
    i1 = jnp.min(jnp.where(el == e1, lane, big), axis=-1, keepdims=True)
    el2 = jnp.where(lane == i1, ninf, el)
    e2 = jnp.max(el2, axis=-1, keepdims=True)
    i2 = jnp.min(jnp.where(el2 == e2, lane, big), axis=-1, keepdims=True)
    t = jnp.exp(e2 - e1)
    p1 = 1.0 / (1.0 + t)
    p2 = t / (1.0 + t)
    gate_ref[...] = jnp.where(lane == 0, p1 * g_val, jnp.where(lane == 1, p2 * g_val, 0.0))

    @pl.when(pl.program_id(0) == 0)
    def _zero_counts():
        cnt_ref[...] = jnp.zeros(cnt_ref.shape, F32)

    ex1 = i1 - N_GROUPS
    ex2 = i2 - N_GROUPS
    oh1 = lane == ex1
    oh2 = lane == ex2
    onehot = jnp.where(oh1 | oh2, 1.0, 0.0)
    r_i =lax.broadcasted_iota(jnp.int32, (tm, tm), 0)
    c_i = lax.broadcasted_iota(jnp.int32, (tm, tm), 1)
    tri = jnp.where(c_i < r_i, 1.0, 0.0).astype(BF16)
    before = jnp.dot(tri, onehot.astype(BF16), preferred_element_type=F32) + cnt_ref[...]
    rank1 = jnp.sum(jnp.where(oh1, before, 0.0), axis=-1, keepdims=True).astype(jnp.int32)
    rank2 = jnp.sum(jnp.where(oh2, before, 0.0), axis=-1, keepdims=True).astype(jnp.int32)
    cnt_ref[...] = cnt_ref[...] + jnp.sum(onehot, axis=0, keepdims=True)
    eid_ref[...] = jnp.where(lane == 0, ex1, jnp.where(lane == 1, ex2,
                             jnp.where(lane == 2, rank1, jnp.where(lane == 3, rank2, 0))))


def _merge_route(x, y_ssm, y_attn, gates, wps, wpa, wo, n2w, wr, tm=256):
    s, d = x.shape
    nw = y_ssm.shape[1]
    const = lambda i: (0, 0)
    return pl.pallas_call(
        _merge_body,
        grid=(s // tm,),
        in_specs=[pl.BlockSpec((tm, d), lambda i: (i, 0)),
                  pl.BlockSpec((tm, nw), lambda i: (i, 0)),
                  pl.BlockSpec((tm, nw), lambda i: (i, 0)),
                  pl.BlockSpec((tm, d), lambda i: (i, 0)),
                  pl.BlockSpec((tm, d), lambda i: (i, 1)),
                  pl.BlockSpec((nw, d), const),
                  pl.BlockSpec((nw, d), const),
                  pl.BlockSpec((d, d), const),
                  pl.BlockSpec((1, d), const),
                  pl.BlockSpec((d, LANES), const)],
        out_specs=[pl.BlockSpec((tm, d), lambda i: (i, 0)),
                   pl.BlockSpec((tm, d), lambda i: (i, 0)),
                   pl.BlockSpec((tm, LANES), lambda i: (i, 0)),
                   pl.BlockSpec((tm, LANES), lambda i: (i, 0)),
                   pl.BlockSpec((1, LANES), const)],
        out_shape=[jax.ShapeDtypeStruct((s, d), F32),
                   jax.ShapeDtypeStruct((s, d), F32),
                   jax.ShapeDtypeStruct((s, LANES), jnp.int32),
                   jax.ShapeDtypeStruct((s, LANES), F32),
                   jax.ShapeDtypeStruct((1, LANES), F32)],
        compiler_params=_cparams(("arbitrary",)),
        name="merge_route",
    )(x, y_ssm, y_attn, gates, gates, wps, wpa, wo, n2w.reshape(1, d).astype(F32), wr)


def _row_copy(src_hbm, row, dst_ref, r, sem):
    return pltpu.make_async_copy(src_hbm.at[pl.ds(row, 1), :], dst_ref.at[pl.ds(r, 1), :], sem)


def _gather_rows(src_hbm, idx_ref, base, stride, dst_ref, sem, n):
    def start(r, c):
        _row_copy(src_hbm, idx_ref[base + stride * r], dst_ref, r, sem).start()
        return c

    lax.fori_loop(0, n, start, 0, unroll=8)


def _wait_rows(src_hbm, dst_ref, sem, n):
    pltpu.make_async_copy(src_hbm.at[pl.ds(0, n), :], dst_ref, sem).wait()


def _scatter_tok_body(dest_ref, tok_ref):
    def zero(i, c):
        tok_ref[i] = 0
        return c

    lax.fori_loop(0, tok_ref.shape[0], zero, 0, unroll=8)

    def put(i, c):
        tok_ref[dest_ref[i]] = lax.shift_right_logical(i, TOP_K_LOG2)
        return c

    lax.fori_loop(0, dest_ref.shape[0], put, 0, unroll=8)


def _scatter_tok(dest, buf_len):
    return pl.pallas_call(
        _scatter_tok_body,
        in_specs=[pl.BlockSpec(memory_space=pltpu.SMEM)],
        out_specs=pl.BlockSpec(memory_space=pltpu.SMEM),
        out_shape=jax.ShapeDtypeStruct((buf_len,), jnp.int32),
        name="scatter_tok",
    )(dest)


def _block_out_copy(ob_ref, slot, ys_hbm, blk, sem):
    return pltpu.make_async_copy(ob_ref.at[slot], ys_hbm.at[pl.ds(blk * MOE_BLK, MOE_BLK), :],
                                 sem.at[slot])


WEIGHT_DMA_PRIORITY = 1


def _weight_copies(e, slot, w_hbm, w_buf, wsem):
    return [pltpu.make_async_copy(h.at[e], b.at[slot], wsem.at[slot])
            for h, b in zip(w_hbm, w_buf)]


def _expert_body(fb_ref, nblk_ref, nb_ref, tok_ref, h2_hbm, wg_hbm, wu_hbm, wd_hbm, ys_hbm,
                 xb_ref, ob_ref, wg_ref, wu_ref, wd_ref, wgb_ref, wub_ref, wdb_ref,
                 gsem, osem, wsem, *, n_blocks):
    e = pl.program_id(0)
    nb = nb_ref[0]
    first = fb_ref[e]
    count = nblk_ref[e]
    w_hbm = (wg_hbm, wu_hbm, wd_hbm)
    w_buf = (wg_ref, wu_ref, wd_ref)

    @pl.when(e == 0)
    def _first_weights():
        for c in _weight_copies(0, 0, w_hbm, w_buf, wsem):
            c.start(priority=WEIGHT_DMA_PRIORITY)

    @pl.when(e + 1 < pl.num_programs(0))
    def _next_weights():
        for c in _weight_copies(e + 1, (e + 1) % 2, w_hbm, w_buf, wsem):
            c.start(priority=WEIGHT_DMA_PRIORITY)

    for c in _weight_copies(e, e % 2, w_hbm, w_buf, wsem):
        c.wait()

    @pl.when(count > 0)
    def _expert():
        wgb_ref[...] = wg_ref[e % 2].astype(BF16)
        wub_ref[...] = wu_ref[e % 2].astype(BF16)
        wdb_ref[...] = wd_ref[e % 2].astype(BF16)

        @pl.when(first == 0)
        def _first_rows():
            _gather_rows(h2_hbm, tok_ref, 0, 1, xb_ref.at[0], gsem.at[0], MOE_BLK)

        def block(t, carry):
            g = first + t
            cur = g % 2
            nxt = 1 - cur
            _wait_rows(h2_hbm, xb_ref.at[cur], gsem.at[cur], MOE_BLK)

            @pl.when(g >= 2)
            def _staging_free():
                _block_out_copy(ob_ref, cur, ys_hbm, g - 2, osem).wait()

            base = (g + 1) * MOE_BLK
            for r in range(MOE_BLK):
                _row_copy(h2_hbm, tok_ref[base + r], xb_ref.at[nxt], r, gsem.at[nxt]).start()
            xb = xb_ref[cur].astype(BF16)
            hg = jnp.dot(xb, wgb_ref[...], preferred_element_type=F32)
            hu = jnp.dot(xb, wub_ref[...], preferred_element_type=F32)
            act = (jax.nn.silu(hg) * hu).astype(BF16)
            ob_ref[cur] = jnp.dot(act, wdb_ref[...], preferred_element_type=F32)
            _block_out_copy(ob_ref, cur, ys_hbm, g, osem).start()
            return carry

        lax.fori_loop(0, count, block, 0)

    @pl.when(e == pl.num_programs(0) - 1)
    def _finish():
        _wait_rows(h2_hbm, xb_ref.at[nb % 2], gsem.at[nb % 2], MOE_BLK)
        _block_out_copy(ob_ref, (nb - 1) % 2, ys_hbm, nb - 1, osem).wait()

        @pl.when(nb >= 2)
        def _():
            _block_out_copy(ob_ref, nb % 2, ys_hbm, nb - 2, osem).wait()

        ob_ref[0] = jnp.zeros(ob_ref.shape[1:], ob_ref.dtype)

        def zero_block(g, carry):
            _block_out_copy(ob_ref, 0, ys_hbm, g, osem).start()
            return carry

        lax.fori_loop(nb, n_blocks, zero_block, 0)

        def zero_wait(g, carry):
            _block_out_copy(ob_ref, 0, ys_hbm, g, osem).wait()
            return carry

        lax.fori_loop(nb, n_blocks, zero_wait, 0)


def _experts(h2, first_blk, n_blk, n_used, buf_tok, wg, wu, wd):
    s, d = h2.shape
    buf_len = buf_tok.shape[0]
    n_blocks = buf_len // MOE_BLK
    return pl.pallas_call(
        functools.partial(_expert_body, n_blocks=n_blocks),
        grid_spec=pltpu.PrefetchScalarGridSpec(
            num_scalar_prefetch=4,
            grid=(N_EXPERTS,),
            in_specs=[pl.BlockSpec(memory_space=pl.ANY)] * 4,
            out_specs=pl.BlockSpec(memory_space=pl.ANY),
            scratch_shapes=[pltpu.VMEM((2, MOE_BLK, d), F32), pltpu.VMEM((2, MOE_BLK, d), F32),
                            pltpu.VMEM((2, d, D_FF), F32), pltpu.VMEM((2, d, D_FF), F32),
                            pltpu.VMEM((2, D_FF, d), F32),
                            pltpu.VMEM((d, D_FF), BF16), pltpu.VMEM((d, D_FF), BF16),
                            pltpu.VMEM((D_FF, d), BF16),
                            pltpu.SemaphoreType.DMA((2,)), pltpu.SemaphoreType.DMA((2,)),
                            pltpu.SemaphoreType.DMA((2,))]),
        out_shape=jax.ShapeDtypeStruct((buf_len, d), F32),
        compiler_params=_cparams(("arbitrary",)),
        name="experts",
    )(first_blk, n_blk, n_used, buf_tok, h2, wg, wu, wd)


def _combine_body(pos_ref, x1_ref, gate_ref, fw_ref, ys_hbm, o_ref, g_ref, sem, *, tm):
    i = pl.program_id(0)

    def gather(tile, slot):
        for k in range(TOP_K):
            _gather_rows(ys_hbm, pos_ref, TOP_K * tile * tm + k, TOP_K, g_ref.at[slot, k],
                         sem.at[slot], tm)

    @pl.when(i == 0)
    def _first():
        gather(0, 0)

    @pl.when(i + 1 < pl.num_programs(0))
    def _prefetch():
        gather(i + 1, (i + 1) % 2)

    cur = i % 2
    for k in range(TOP_K):
        _wait_rows(ys_hbm, g_ref.at[cur, k], sem.at[cur], tm)
    gate = gate_ref[...]
    x = x1_ref[...] + (gate[:, 0:1] * g_ref[cur, 0] + gate[:, 1:2] * g_ref[cur, 1])
    y = x * lax.rsqrt(jnp.mean(x * x, axis=-1, keepdims=True) + NORM_EPS)
    o_ref[...] = y * fw_ref[...]


def _combine(x1, ys, pos, gate, fw, tm=512):
    s, d = x1.shape
    return pl.pallas_call(
        functools.partial(_combine_body, tm=tm),
        grid_spec=pltpu.PrefetchScalarGridSpec(
            num_scalar_prefetch=1,
            grid=(s // tm,),
            in_specs=[pl.BlockSpec((tm, d), lambda i, p: (i, 0)),
                      pl.BlockSpec((tm, LANES), lambda i, p: (i, 0)),
                      pl.BlockSpec((1, d), lambda i, p: (0, 0)),
                      pl.BlockSpec(memory_space=pl.ANY)],
            out_specs=pl.BlockSpec((tm, d), lambda i, p: (i, 0)),
            scratch_shapes=[pltpu.VMEM((2, TOP_K, tm, d), F32),
                            pltpu.SemaphoreType.DMA((2,))]),
        out_shape=jax.ShapeDtypeStruct((s, d), F32),
        compiler_params=_cparams(("arbitrary",)),
        name="combine_norm",
    )(pos, x1, gate, fw.reshape(1, d).astype(F32), ys)


def _dispatch_tables(eid, counts):
    n_tok = eid.shape[0]
    n_slots = n_tok * TOP_K
    buf_len = n_slots + N_EXPERTS * MOE_BLK
    counts = counts.astype(jnp.int32)
    padded = ((counts + MOE_BLK - 1) // MOE_BLK) * MOE_BLK
    padded_ends = jnp.cumsum(padded)
    padded_starts = padded_ends - padded
    experts = eid[:, :TOP_K]
    onehot = experts[:, :, None] == jnp.arange(N_EXPERTS, dtype=jnp.int32)
    dest = jnp.sum(jnp.where(onehot, padded_starts, 0), axis=-1) + eid[:, TOP_K:2 * TOP_K]
    n_used = (padded_ends[-1] // MOE_BLK).astype(jnp.int32).reshape(1)
    first_blk = (padded_starts // MOE_BLK).astype(jnp.int32)
    n_blk = (padded // MOE_BLK).astype(jnp.int32)
    return first_blk, n_blk, n_used, dest.reshape(n_slots).astype(jnp.int32), buf_len


def kernel(x, positions, norm1_w, w_in, ssm_lambda_re, ssm_lambda_im, ssm_log_dt, ssm_b_re, ssm_b_im, ssm_c_re, ssm_c_im, ssm_d, ssm_glu_w, ssm_glu_b, attn_lambda_q1, attn_lambda_k1, attn_lambda_q2, attn_lambda_k2, attn_subln_w, w_proj_ssm, w_proj_attn, w_out, norm2_w, router_group_w, router_expert_w, expert_w_gate, expert_w_up, expert_w_down, final_norm_w):
    bsz, seq, d = x.shape
    depth = norm1_w.shape[0]
    xs = x.reshape(bsz * seq, d)
    pos = positions.reshape(bsz * seq)
    o_q = SSM_WIDTH
    o_v = o_q + 2 * ATTN_WIDTH
    o_g = o_v + ATTN_WIDTH
    for l in range(depth):
        h, rope_cos, rope_sin = _rmsnorm_rope(xs, norm1_w[l].astype(F32), pos, BF16)
        w_in_b = w_in[l].astype(F32)
        u = _mm(h, w_in_b, 0, SSM_WIDTH, "none", F32)
        qk = _mm_rope(h, w_in_b, rope_cos, rope_sin, o_q)
        v = _mm_vones(h, w_in_b, o_v)
        gates = _mm(h, w_in_b, o_g, 2 * D_MODEL, "sigmoid", F32)

        tables = _s5_tables(ssm_lambda_re[l], ssm_lambda_im[l], ssm_log_dt[l],
                            ssm_b_re[l], ssm_b_im[l], ssm_c_re[l], ssm_c_im[l])
        y = _s5_core(u, tables, ssm_d[l])
        y_ssm = _glu(y, ssm_glu_w[l].astype(BF16), ssm_glu_b[l])

        lam_params = jnp.stack([attn_lambda_q1[l], attn_lambda_k1[l],
                                attn_lambda_q2[l], attn_lambda_k2[l]]).astype(F32)
        y_attn = _diff_attention(qk, v, lam_params, attn_subln_w[l])

        wr = jnp.concatenate([router_group_w[l], router_expert_w[l]], axis=1).astype(F32)
        wr = jnp.pad(wr, ((0, 0), (0, LANES - wr.shape[1])))
        x1, h2, eid, gate, counts = _merge_route(
            xs, y_ssm, y_attn, gates, w_proj_ssm[l].astype(BF16), w_proj_attn[l].astype(BF16),
            w_out[l].astype(BF16), norm2_w[l], wr)

        first_blk, n_blk, n_used, dest, buf_len = _dispatch_tables(eid, counts[0, :N_EXPERTS])
        buf_tok = _scatter_tok(dest, buf_len)
        ys = _experts(h2, first_blk, n_blk, n_used, buf_tok, expert_w_gate[l].astype(F32),
                      expert_w_up[l].astype(F32), expert_w_down[l].astype(F32))
        last = l == depth - 1
        assert last, "DEPTH > 1 needs an un-normalised combine"
        xs = _combine(x1, ys, dest, gate, final_norm_w)
    return xs.reshape(bsz, seq, d)
```

```python
import functools
import math

import jax
import jax.numpy as jnp
from jax import lax
from jax.experimental import pallas as pl
from jax.experimental.pallas import tpu as pltpu

F32 = jnp.float32
BF16 = jnp.bfloat16

D_MODEL = 2048
SSM_WIDTH = 1024
SSM_GROUP = 16
SSM_GROUPS = 64
SSM_STATE = 64
ATTN_WIDTH = 1024
ATTN_HEADS = 8
HEAD_DIM = 64
ROPE_THETA = 10000.0
N_GROUPS = 4
EXPERTS_PER_GROUP = 8
N_EXPERTS = 32
TOP_K = 2
TOP_K_LOG2 = 1
D_FF = 512
NORM_EPS = 1e-6
SUBLN_EPS = 1e-5
LAMBDA_INIT = 0.8 - 0.6 * math.exp(-0.3 * 0)

LANES = 128
VMEM_LIMIT = 48 * 1024 * 1024

SSM_CHUNK = 16
SSM_GB = 8
MOE_BLK = 128


def _cparams(sem):
    return pltpu.CompilerParams(dimension_semantics=sem, vmem_limit_bytes=VMEM_LIMIT)


def _rmsnorm_body(x_ref, w_ref, pos_ref, invf_ref, o_ref, cos_ref, sin_ref):
    x = x_ref[...]
    y = x * lax.rsqrt(jnp.mean(x * x, axis=-1, keepdims=True) + NORM_EPS)
    o_ref[...] = (y * w_ref[...]).astype(o_ref.dtype)
    ang = pos_ref[...].astype(F32) * invf_ref[...]
    lane = lax.broadcasted_iota(jnp.int32, ang.shape, 1)
    low = (lane % HEAD_DIM) < (HEAD_DIM // 2)
    cos_ref[...] = jnp.cos(ang)
    sin_ref[...] = jnp.where(low, -jnp.sin(ang), jnp.sin(ang))


def _rmsnorm_rope(x, w, positions, out_dtype, tm=512):
    s, d = x.shape
    inv_freq = 1.0 / (ROPE_THETA ** (jnp.arange(0, HEAD_DIM, 2, dtype=F32) / HEAD_DIM))
    invf = jnp.tile(inv_freq, LANES // (HEAD_DIM // 2)).reshape(1, LANES)
    return pl.pallas_call(
        _rmsnorm_body,
        grid=(s // tm,),
        in_specs=[pl.BlockSpec((tm, d), lambda i: (i, 0)),
                  pl.BlockSpec((1, d), lambda i: (0, 0)),
                  pl.BlockSpec((tm, 1), lambda i: (i, 0)),
                  pl.BlockSpec((1, LANES), lambda i: (0, 0))],
        out_specs=[pl.BlockSpec((tm, d), lambda i: (i, 0)),
                   pl.BlockSpec((tm, LANES), lambda i: (i, 0)),
                   pl.BlockSpec((tm, LANES), lambda i: (i, 0))],
        out_shape=[jax.ShapeDtypeStruct((s, d), out_dtype),
                   jax.ShapeDtypeStruct((s, LANES), F32),
                   jax.ShapeDtypeStruct((s, LANES), F32)],
        compiler_params=_cparams(("parallel",)),
        name="rmsnorm",
    )(x, w.reshape(1, d), positions.reshape(s, 1), invf)


def _proj(h_ref, w_ref, wb_ref, row_axis):
    @pl.when(pl.program_id(row_axis) == 0)
    def _cast():
        wb_ref[...] = w_ref[...].astype(BF16)

    return jnp.dot(h_ref[...], wb_ref[...], preferred_element_type=F32)


def _mm_body(h_ref, w_ref, o_ref, wb_ref, *, act):
    acc = _proj(h_ref, w_ref, wb_ref, 1)
    if act == "sigmoid":
        acc = jax.nn.sigmoid(acc)
    o_ref[...] = acc.astype(o_ref.dtype)


def _mm(h, w, col0, ncols, act, out_dtype, tm=1024, tn=1024):
    s, k = h.shape
    jb = col0 // tn
    return pl.pallas_call(
        functools.partial(_mm_body, act=act),
        grid=(ncols // tn, s // tm),
        in_specs=[pl.BlockSpec((tm, k), lambda j, i: (i, 0)),
                  pl.BlockSpec((k, tn), lambda j, i: (0, j + jb))],
        out_specs=pl.BlockSpec((tm, tn), lambda j, i: (i, j)),
        out_shape=jax.ShapeDtypeStruct((s, ncols), out_dtype),
        scratch_shapes=[pltpu.VMEM((k, tn), BF16)],
        compiler_params=_cparams(("arbitrary", "arbitrary")),
        name="mm_" + act,
    )(h, w)


def _mm_vones_body(h_ref, w_ref, o_ref, wb_ref):
    acc = _proj(h_ref, w_ref, wb_ref, 0)
    hw = 2 * HEAD_DIM
    ones = jnp.ones((acc.shape[0], hw), o_ref.dtype)
    for hd in range(acc.shape[1] // hw):
        o_ref[:, 2 * hd * hw:(2 * hd + 1) * hw] = acc[:, hd * hw:(hd + 1) * hw].astype(o_ref.dtype)
        o_ref[:, (2 * hd + 1) * hw:(2 * hd + 2) * hw] = ones


def _mm_vones(h, w, col0, tm=1024):
    s, k = h.shape
    tn = ATTN_WIDTH
    jb = col0 // tn
    return pl.pallas_call(
        _mm_vones_body,
        grid=(s // tm,),
        in_specs=[pl.BlockSpec((tm, k), lambda i: (i, 0)),
                  pl.BlockSpec((k, tn), lambda i: (0, jb))],
        out_specs=pl.BlockSpec((tm, 2 * tn), lambda i: (i, 0)),
        out_shape=jax.ShapeDtypeStruct((s, 2 * tn), BF16),
        scratch_shapes=[pltpu.VMEM((k, tn), BF16)],
        compiler_params=_cparams(("arbitrary",)),
        name="mm_vones",
    )(h, w)


def _mm_rope_body(h_ref, w_ref, cos_ref, sin_ref, o_ref, wb_ref, *, tn):
    j = pl.program_id(0)
    acc = _proj(h_ref, w_ref, wb_ref, 1)
    scale = jnp.where(j == 0, HEAD_DIM ** -0.5, 1.0).astype(F32)
    lane = lax.broadcasted_iota(jnp.int32, cos_ref.shape, 1)
    low = (lane % HEAD_DIM) < (HEAD_DIM // 2)
    cos = cos_ref[...] * scale
    sin = sin_ref[...] * scale
    for c in range(tn // LANES):
        t = acc[:, c * LANES:(c + 1) * LANES]
        partner = jnp.where(low, pltpu.roll(t, LANES - HEAD_DIM // 2, 1),
                            pltpu.roll(t, HEAD_DIM // 2, 1))
        o_ref[:, c * LANES:(c + 1) * LANES] = (t * cos + partner * sin).astype(o_ref.dtype)


def _mm_rope(h, w, cos, sin, col0, tm=1024, tn=1024):
    s, k = h.shape
    jb = col0 // tn
    return pl.pallas_call(
        functools.partial(_mm_rope_body, tn=tn),
        grid=(2, s // tm),
        in_specs=[pl.BlockSpec((tm, k), lambda j, i: (i, 0)),
                  pl.BlockSpec((k, tn), lambda j, i: (0, j + jb)),
                  pl.BlockSpec((tm, LANES), lambda j, i: (i, 0)),
                  pl.BlockSpec((tm, LANES), lambda j, i: (i, 0))],
        out_specs=pl.BlockSpec((tm, tn), lambda j, i: (i, j)),
        out_shape=jax.ShapeDtypeStruct((s, 2 * tn), BF16),
        scratch_shapes=[pltpu.VMEM((k, tn), BF16)],
        compiler_params=_cparams(("arbitrary", "arbitrary")),
        name="mm_rope",
    )(h, w, cos, sin)


def _s5_tables(lam_re, lam_im, log_dt, b_re, b_im, c_re, c_im):
    L, G, P, M = SSM_CHUNK, SSM_GROUPS, SSM_STATE, SSM_GROUP
    hi = lax.Precision.HIGHEST
    lr, li = lam_re.astype(F32), lam_im.astype(F32)
    dt = jnp.exp(log_dt.astype(F32))[:, None]
    n = jnp.arange(L + 1, dtype=F32)[:, None, None]
    mag = jnp.exp(lr * dt * n)
    pr = mag * jnp.cos(li * dt * n)
    pi = mag * jnp.sin(li * dt * n)
    nr, ni = pr[1] - 1.0, pi[1]
    den = lr * lr + li * li
    f_re = (nr * lr + ni * li) / den
    f_im = (ni * lr - nr * li) / den
    br, bi = b_re.astype(F32), b_im.astype(F32)
    bbr = f_re[..., None] * br - f_im[..., None] * bi
    bbi = f_re[..., None] * bi + f_im[..., None] * br
    cr, ci = c_re.astype(F32), c_im.astype(F32)

    nb, gb = G // SSM_GB, SSM_GB
    pr_g = jnp.transpose(pr, (1, 0, 2))
    pi_g = jnp.transpose(pi, (1, 0, 2))
    bbr_t = jnp.transpose(bbr, (0, 2, 1))
    bbi_t = jnp.transpose(bbi, (0, 2, 1))

    abr = pr_g[:, :L, None, :] * bbr_t[:, None] - pi_g[:, :L, None, :] * bbi_t[:, None]
    abi = pr_g[:, :L, None, :] * bbi_t[:, None] + pi_g[:, :L, None, :] * bbr_t[:, None]
    kmat = (jnp.einsum('gnip,gop->gnio', abr, cr, precision=hi)
            - jnp.einsum('gnip,gop->gnio', abi, ci, precision=hi))
    kpad = jnp.pad(kmat, ((0, 0), (1, 0), (0, 0), (0, 0)))
    kq = jnp.stack([jnp.stack([kpad[:, 1 + c2 - r2::2][:, :L // 2] for c2 in range(2)], axis=3)
                    for r2 in range(2)], axis=2)
    kk_src = jnp.transpose(kq.reshape(nb, gb, L // 2, 2, M, 2, M),
                           (0, 2, 3, 5, 4, 1, 6)).reshape(nb, L // 2, 2, 2, M, gb * M)

    ab1r = pr_g[:, 1, None, :] * bbr_t - pi_g[:, 1, None, :] * bbi_t
    ab1i = pr_g[:, 1, None, :] * bbi_t + pi_g[:, 1, None, :] * bbr_t
    b4 = jnp.stack([jnp.stack([ab1r, ab1i], axis=2), jnp.stack([bbr_t, bbi_t], axis=2)],
                   axis=1)
    bb_src = jnp.transpose(b4.reshape(nb, gb, 2, M, 2, P),
                           (0, 2, 4, 3, 1, 5)).reshape(nb, 2, 2, M, gb * P)

    cr_t = jnp.transpose(cr, (0, 2, 1))
    ci_t = jnp.transpose(ci, (0, 2, 1))
    car = cr_t * pr_g[:, 1, :, None] - ci_t * pi_g[:, 1, :, None]
    cai = cr_t * pi_g[:, 1, :, None] + ci_t * pr_g[:, 1, :, None]
    c4 = jnp.stack([jnp.stack([cr_t, car], axis=2), jnp.stack([-ci_t, -cai], axis=2)],
                   axis=1)
    cc_src = jnp.transpose(c4.reshape(nb, gb, 2, P, 2, M),
                           (0, 2, 4, 3, 1, 5)).reshape(nb, 2, 2, P, gb * M)

    ap = jnp.stack([pr.reshape(L + 1, nb, gb * P), pi.reshape(L + 1, nb, gb * P)], axis=2)
    return kk_src, bb_src, cc_src, ap.reshape(L + 1, 2 * G * P)


def _cmul(ar, ai, zr, zi):
    return ar * zr - ai * zi, ar * zi + ai * zr


def _block_diag(src, rows_per_group, cols_per_group):
    t = jnp.concatenate([src] * SSM_GB, axis=0)
    row = lax.broadcasted_iota(jnp.int32, t.shape, 0)
    col = lax.broadcasted_iota(jnp.int32, t.shape, 1)
    return jnp.where(row // rows_per_group == col // cols_per_group, t, 0.0).astype(BF16)


def _s5_body(u_ref, kk_src, bb_src, cc_src, ap_ref, d_ref, y_ref,
             kk_ref, bb_ref, cc_ref, lhs_ref, er_ref, ei_ref, xr_ref, xi_ref):
    L, M, P = SSM_CHUNK, SSM_GROUP, SSM_STATE
    nc = u_ref.shape[0] // L
    hp = SSM_GB * P
    gm = SSM_GB * M
    nq = L // 2
    for a in range(2):
        for b in range(2):
            for dd in range(nq):
                kk_ref[dd, a * gm:(a + 1) * gm, b * gm:(b + 1) * gm] = _block_diag(
                    kk_src[0, dd, a, b], M, M)
            bb_ref[a * gm:(a + 1) * gm, b * hp:(b + 1) * hp] = _block_diag(bb_src[0, a, b], M, P)
            cc_ref[a * hp:(a + 1) * hp, b * gm:(b + 1) * gm] = _block_diag(cc_src[0, a, b], P, M)
    for q in range(nq):
        lhs_ref[q] = jnp.concatenate(
            [u_ref[pl.ds(2 * q, nc, stride=L), :], u_ref[pl.ds(2 * q + 1, nc, stride=L), :]],
            axis=1).astype(BF16)

    er = jnp.zeros((nc, hp), F32)
    ei = jnp.zeros((nc, hp), F32)
    for q in range(nq):
        z = jnp.dot(lhs_ref[q], bb_ref[...], preferred_element_type=F32)
        n = L - 2 - 2 * q
        dr, di = _cmul(ap_ref[n:n + 1, :hp], ap_ref[n:n + 1, hp:], z[:, :hp], z[:, hp:])
        er = er + dr
        ei = ei + di
    er_ref[...] = er
    ei_ref[...] = ei

    ar = ap_ref[L:L + 1, :hp]
    ai = ap_ref[L:L + 1, hp:]

    def step(c, carry):
        xr, xi = carry
        xr_ref[pl.ds(c, 1), :] = xr
        xi_ref[pl.ds(c, 1), :] = xi
        nr, ni = _cmul(ar, ai, xr, xi)
        return nr + er_ref[pl.ds(c, 1), :], ni + ei_ref[pl.ds(c, 1), :]

    zero = jnp.zeros((1, hp), F32)
    lax.fori_loop(0, nc, step, (zero, zero))

    for qq in range(nq):
        n = 2 * qq + 1
        wr, wi = _cmul(ap_ref[n:n + 1, :hp], ap_ref[n:n + 1, hp:], xr_ref[...], xi_ref[...])
        w = jnp.concatenate([wr, wi], axis=1).astype(BF16)
        acc = jnp.dot(w, cc_ref[...], preferred_element_type=F32)
        for q in range(qq + 1):
            acc = acc + jnp.dot(lhs_ref[q], kk_ref[qq - q], preferred_element_type=F32)
        for r in range(2):
            j = 2 * qq + r
            y = acc[:, r * LANES:(r + 1) * LANES] + d_ref[...] * u_ref[pl.ds(j, nc, stride=L), :]
            y_ref[pl.ds(j, nc, stride=L), :] = jax.nn.gelu(y)


def _s5_core(u, tables, d_skip):
    s, width = u.shape
    L, P, M, gb = SSM_CHUNK, SSM_STATE, SSM_GROUP, SSM_GB
    nc = s // L
    nb = width // (gb * M)
    kk_src, bb_src, cc_src, ap = tables
    return pl.pallas_call(
        _s5_body,
        grid=(nb,),
        in_specs=[pl.BlockSpec((s, gb * M), lambda b: (0, b)),
                  pl.BlockSpec((1, L // 2, 2, 2, M, gb * M), lambda b: (b, 0, 0, 0, 0, 0)),
                  pl.BlockSpec((1, 2, 2, M, gb * P), lambda b: (b, 0, 0, 0, 0)),
                  pl.BlockSpec((1, 2, 2, P, gb * M), lambda b: (b, 0, 0, 0, 0)),
                  pl.BlockSpec((L + 1, 2 * gb * P), lambda b: (0, b)),
                  pl.BlockSpec((1, gb * M), lambda b: (0, b))],
        out_specs=pl.BlockSpec((s, gb * M), lambda b: (0, b)),
        out_shape=jax.ShapeDtypeStruct((s, width), F32),
        scratch_shapes=[pltpu.VMEM((L // 2, 2 * gb * M, 2 * gb * M), BF16),
                        pltpu.VMEM((2 * gb * M, 2 * gb * P), BF16),
                        pltpu.VMEM((2 * gb * P, 2 * gb * M), BF16),
                        pltpu.VMEM((L // 2, nc, 2 * gb * M), BF16)]
        + [pltpu.VMEM((nc, gb * P), F32) for _ in range(4)],
        compiler_params=_cparams(("parallel",)),
        name="s5_scan",
    )(u, kk_src, bb_src, cc_src, ap, d_skip.astype(F32).reshape(1, width))


def _attn_body(q_ref, k_ref, v_ref, lam_ref, sw_ref, o_ref,
               qs_ref, m_ref, acc_ref, s_ref, p_ref, al_ref, *, tq):
    qi = pl.program_id(1)
    hw = 2 * HEAD_DIM
    q = q_ref[...]
    lane = lax.broadcasted_iota(jnp.int32, q.shape, 1)
    zero = jnp.zeros_like(q)
    qs_ref[0] = jnp.where(lane < HEAD_DIM, q, zero)
    qs_ref[1] = jnp.where(lane >= HEAD_DIM, q, zero)
    m_ref[...] = jnp.full(m_ref.shape, -jnp.inf, F32)
    acc_ref[...] = jnp.zeros(acc_ref.shape, F32)

    def scores(c, slot, masked=False):
        off = pl.multiple_of(c * tq, tq)
        k = k_ref[pl.ds(off, tq), :]
        for mp in range(2):
            s = lax.dot_general(qs_ref[mp], k, (((1,), (1,)), ((), ())),
                                preferred_element_type=F32)
            if masked:
                row = lax.broadcasted_iota(jnp.int32, s.shape, 0)
                col = lax.broadcasted_iota(jnp.int32, s.shape, 1)
                s = jnp.where(col <= row, s, -jnp.inf)
            s_ref[slot, mp] = s

    def softmax(slot):
        for mp in range(2):
            s = s_ref[slot, mp]
            m_old = m_ref[mp]
            m_new = jnp.maximum(m_old, jnp.max(s, axis=-1, keepdims=True))
            al_ref[slot, mp] = jnp.exp(m_old - m_new)
            p_ref[slot, mp] = jnp.exp(s - m_new).astype(BF16)
            m_ref[mp] = m_new

    def pv(c, slot):
        off = pl.multiple_of(c * tq, tq)
        v = v_ref[pl.ds(off, tq), :]
        for mp in range(2):
            acc_ref[mp] = al_ref[slot, mp] * acc_ref[mp] + jnp.dot(
                p_ref[slot, mp], v, preferred_element_type=F32)

    @pl.when(qi == 0)
    def _one_chunk():
        scores(0, 0, masked=True)
        softmax(0)
        pv(0, 0)

    @pl.when(qi == 1)
    def _two_chunks():
        scores(0, 1)
        softmax(1)
        pv(0, 1)
        scores(1, 0, masked=True)
        softmax(0)
        pv(1, 0)

    @pl.when(qi >= 2)
    def _pipelined():
        odd = qi % 2

        @pl.when(odd == 0)
        def _fill_even():
            scores(0, 0)
            scores(1, 1)
            softmax(0)

        @pl.when(odd == 1)
        def _fill_odd():
            scores(0, 1)
            scores(1, 0)
            softmax(1)

        def odd_step(_, carry):
            scores(2, 1)
            softmax(0)
            pv(0, 1)
            return carry

        lax.fori_loop(0, odd, odd_step, 0)
        t0 = 2 + odd

        def two_steps(u, carry):
            t = t0 + 2 * u
            scores(t, 0)
            softmax(1)
            pv(t - 2, 0)
            scores(t + 1, 1)
            softmax(0)
            pv(t - 1, 1)
            return carry

        lax.fori_loop(0, (qi - t0) // 2, two_steps, 0)
        scores(qi, 0, masked=True)
        softmax(1)
        pv(qi - 2, 0)
        softmax(0)
        pv(qi - 1, 1)
        pv(qi, 0)

    lam = (jnp.exp(jnp.sum(lam_ref[0:1, :] * lam_ref[1:2, :]))
           - jnp.exp(jnp.sum(lam_ref[2:3, :] * lam_ref[3:4, :])) + LAMBDA_INIT)
    a1 = acc_ref[0]
    a2 = acc_ref[1]
    o = a1[:, :hw] / a1[:, hw:] - lam * (a2[:, :hw] / a2[:, hw:])
    o = o * lax.rsqrt(jnp.mean(o * o, axis=-1, keepdims=True) + SUBLN_EPS)
    o_ref[...] = (o * sw_ref[...] * (1.0 - LAMBDA_INIT)).astype(o_ref.dtype)


def _diff_attention(qk, v1, lam_params, subln_w, tq=512):
    s = qk.shape[0]
    nq = s // tq
    hw = 2 * HEAD_DIM
    kcol0 = ATTN_WIDTH // hw
    return pl.pallas_call(
        functools.partial(_attn_body, tq=tq),
        grid=(ATTN_HEADS, nq),
        in_specs=[pl.BlockSpec((tq, hw), lambda h, i: (i, h)),
                  pl.BlockSpec((s, hw), lambda h, i: (0, kcol0 + h)),
                  pl.BlockSpec((s, 2 * hw), lambda h, i: (0, h)),
                  pl.BlockSpec((4, HEAD_DIM), lambda h, i: (0, 0)),
                  pl.BlockSpec((1, hw), lambda h, i: (0, 0))],
        out_specs=pl.BlockSpec((tq, hw), lambda h, i: (i, h)),
        out_shape=jax.ShapeDtypeStruct((s, ATTN_WIDTH), BF16),
        scratch_shapes=[pltpu.VMEM((2, tq, hw), BF16),
                        pltpu.VMEM((2, tq, 1), F32),
                        pltpu.VMEM((2, tq, 2 * hw), F32),
                        pltpu.VMEM((2, 2, tq, tq), F32),
                        pltpu.VMEM((2, 2, tq, tq), BF16),
                        pltpu.VMEM((2, 2, tq, 1), F32)],
        compiler_params=_cparams(("parallel", "arbitrary")),
        name="diff_attn",
    )(qk, qk, v1, lam_params, subln_w.reshape(1, hw).astype(F32))


def _merge_body(x_ref, y_ref, glw_ref, glb_ref, ya_ref, gs_ref, ga_ref, wps_ref, wpa_ref, wo_ref,
                n2_ref, wr_ref, x1_ref, h2_ref, eid_ref, gate_ref, cnt_ref):
    y = y_ref[...]
    glu = jnp.dot(y.astype(BF16), glw_ref[...], preferred_element_type=F32) + glb_ref[...]
    y_ssm = (y * jax.nn.sigmoid(glu)).astype(BF16)
    ps = jnp.dot(y_ssm, wps_ref[...], preferred_element_type=F32)
    pa = jnp.dot(ya_ref[...], wpa_ref[...], preferred_element_type=F32)
    merged = gs_ref[...].astype(F32) * ps + ga_ref[...].astype(F32) * pa
    x1 = x_ref[...] + jnp.dot(merged.astype(BF16), wo_ref[...], preferred_element_type=F32)
    x1_ref[...] = x1
    h2 = x1 * lax.rsqrt(jnp.mean(x1 * x1, axis=-1, keepdims=True) + NORM_EPS) * n2_ref[...]
    h2_ref[...] = h2
    wr = wr_ref[...]
    h_hi = h2.astype(BF16)
    h_lo = (h2 - h_hi.astype(F32)).astype(BF16)
    w_hi = wr.astype(BF16)
    w_lo = (wr - w_hi.astype(F32)).astype(BF16)
    tm = h2.shape[0]
    parts = jnp.dot(jnp.concatenate([h_hi, h_lo], axis=0), jnp.concatenate([w_hi, w_lo], axis=1),
                    preferred_element_type=F32)
    logits = (parts[:tm, :LANES] + parts[:tm, LANES:]) + (parts[tm:, :LANES] + parts[tm:, LANES:])
    lane = lax.broadcasted_iota(jnp.int32, logits.shape, 1)
    big = jnp.int32(1 << 20)
    ninf = jnp.float32(-jnp.inf)
    is_g = lane < N_GROUPS
    gl = jnp.where(is_g, logits, ninf)
    gm = jnp.max(gl, axis=-1, keepdims=True)
    g_idx = jnp.min(jnp.where(gl == gm, lane, big), axis=-1, keepdims=True)
    g_val = 1.0 / jnp.sum(jnp.where(is_g, jnp.exp(gl - gm), 0.0), axis=-1, keepdims=True)
    lo = N_GROUPS + g_idx * EXPERTS_PER_GROUP
    in_grp = (lane >= lo) & (lane < lo + EXPERTS_PER_GROUP)
    el = jnp.where(in_grp, logits, ninf)
    e1 = jnp.max(el, axis=-1, keepdims=True)
    i1 = jnp.min(jnp.where(el == e1, lane, big), axis=-1, keepdims=True)
    el2 = jnp.where(lane == i1, ninf, el)
    e2 = jnp.max(el2, axis=-1, keepdims=True)
    i2 = jnp.min(jnp.where(el2 == e2, lane, big), axis=-1, keepdims=True)
    t = jnp.exp(e2 - e1)
    p1 = 1.0 / (1.0 + t)
    p2 = t / (1.0 + t)
    gate_ref[...] = jnp.where(lane == 0, p1 * g_val, jnp.where(lane == 1, p2 * g_val, 0.0))

    @pl.when(pl.program_id(0) == 0)
    def _zero_counts():
        cnt_ref[...] = jnp.zeros(cnt_ref.shape, F32)

    ex1 = i1 - N_GROUPS
    ex2 = i2 - N_GROUPS
    oh1 = lane == ex1
    oh2 = lane == ex2
    onehot = jnp.where(oh1 | oh2, 1.0, 0.0)
    r_i =lax.broadcasted_iota(jnp.int32, (tm, tm), 0)
    c_i = lax.broadcasted_iota(jnp.int32, (tm, tm), 1)
    tri = jnp.where(c_i < r_i, 1.0, 0.0).astype(BF16)
    before = jnp.dot(tri, onehot.astype(BF16), preferred_element_type=F32) + cnt_ref[...]
    rank1 = jnp.sum(jnp.where(oh1, before, 0.0), axis=-1, keepdims=True).astype(jnp.int32)
    rank2 = jnp.sum(jnp.where(oh2, before, 0.0), axis=-1, keepdims=True).astype(jnp.int32)
    cnt_ref[...] = cnt_ref[...] + jnp.sum(onehot, axis=0, keepdims=True)
    eid_ref[...] = jnp.where(lane == 0, ex1, jnp.where(lane == 1, ex2,
                             jnp.where(lane == 2, rank1, jnp.where(lane == 3, rank2, 0))))


def _merge_route(x, y_s5, glu_w, glu_b, y_attn, gates, wps, wpa, wo, n2w, wr, tm=256):
    s, d = x.shape
    nw = y_s5.shape[1]
    const = lambda i: (0, 0)
    return pl.pallas_call(
        _merge_body,
        grid=(s // tm,),
        in_specs=[pl.BlockSpec((tm, d), lambda i: (i, 0)),
                  pl.BlockSpec((tm, nw), lambda i: (i, 0)),
                  pl.BlockSpec((nw, nw), const),
                  pl.BlockSpec((1, nw), const),
                  pl.BlockSpec((tm, nw), lambda i: (i, 0)),
                  pl.BlockSpec((tm, d), lambda i: (i, 0)),
                  pl.BlockSpec((tm, d), lambda i: (i, 1)),
                  pl.BlockSpec((nw, d), const),
                  pl.BlockSpec((nw, d), const),
                  pl.BlockSpec((d, d), const),
                  pl.BlockSpec((1, d), const),
                  pl.BlockSpec((d, LANES), const)],
        out_specs=[pl.BlockSpec((tm, d), lambda i: (i, 0)),
                   pl.BlockSpec((tm, d), lambda i: (i, 0)),
                   pl.BlockSpec((tm, LANES), lambda i: (i, 0)),
                   pl.BlockSpec((tm, LANES), lambda i: (i, 0)),
                   pl.BlockSpec((1, LANES), const)],
        out_shape=[jax.ShapeDtypeStruct((s, d), F32),
                   jax.ShapeDtypeStruct((s, d), F32),
                   jax.ShapeDtypeStruct((s, LANES), jnp.int32),
                   jax.ShapeDtypeStruct((s, LANES), F32),
                   jax.ShapeDtypeStruct((1, LANES), F32)],
        compiler_params=_cparams(("arbitrary",)),
        name="merge_route",
    )(x, y_s5, glu_w, glu_b.reshape(1, nw).astype(F32), y_attn, gates, gates, wps, wpa, wo,
      n2w.reshape(1, d).astype(F32), wr)


def _row_copy(src_hbm, row, dst_ref, r, sem):
    return pltpu.make_async_copy(src_hbm.at[pl.ds(row, 1), :], dst_ref.at[pl.ds(r, 1), :], sem)


def _gather_rows(src_hbm, idx_ref, base, stride, dst_ref, sem, n):
    def start(r, c):
        _row_copy(src_hbm, idx_ref[base + stride * r], dst_ref, r, sem).start()
        return c

    lax.fori_loop(0, n, start, 0, unroll=8)


def _wait_rows(src_hbm, dst_ref, sem, n):
    pltpu.make_async_copy(src_hbm.at[pl.ds(0, n), :], dst_ref, sem).wait()


def _scatter_tok_body(dest_ref, tok_ref):
    def zero(i, c):
        tok_ref[i] = 0
        return c

    lax.fori_loop(0, tok_ref.shape[0], zero, 0, unroll=8)

    def put(i, c):
        tok_ref[dest_ref[i]] = lax.shift_right_logical(i, TOP_K_LOG2)
        return c

    lax.fori_loop(0, dest_ref.shape[0], put, 0, unroll=8)


def _scatter_tok(dest, buf_len):
    return pl.pallas_call(
        _scatter_tok_body,
        in_specs=[pl.BlockSpec(memory_space=pltpu.SMEM)],
        out_specs=pl.BlockSpec(memory_space=pltpu.SMEM),
        out_shape=jax.ShapeDtypeStruct((buf_len,), jnp.int32),
        name="scatter_tok",
    )(dest)


def _block_out_copy(ob_ref, slot, ys_hbm, blk, sem):
    return pltpu.make_async_copy(ob_ref.at[slot], ys_hbm.at[pl.ds(blk * MOE_BLK, MOE_BLK), :],
                                 sem.at[slot])


WEIGHT_DMA_PRIORITY = 1


def _weight_copies(e, slot, w_hbm, w_buf, wsem):
    return [pltpu.make_async_copy(h.at[e], b.at[slot], wsem.at[slot])
            for h, b in zip(w_hbm, w_buf)]


def _expert_body(fb_ref, nblk_ref, nb_ref, tok_ref, h2_hbm, wg_hbm, wu_hbm, wd_hbm, ys_hbm,
                 xb_ref, ob_ref, wg_ref, wu_ref, wd_ref, wgb_ref, wub_ref, wdb_ref,
                 gsem, osem, wsem, *, n_blocks):
    e = pl.program_id(0)
    nb = nb_ref[0]
    first = fb_ref[e]
    count = nblk_ref[e]
    w_hbm = (wg_hbm, wu_hbm, wd_hbm)
    w_buf = (wg_ref, wu_ref, wd_ref)

    @pl.when(e == 0)
    def _first_weights():
        for c in _weight_copies(0, 0, w_hbm, w_buf, wsem):
            c.start(priority=WEIGHT_DMA_PRIORITY)

    @pl.when(e + 1 < pl.num_programs(0))
    def _next_weights():
        for c in _weight_copies(e + 1, (e + 1) % 2, w_hbm, w_buf, wsem):
            c.start(priority=WEIGHT_DMA_PRIORITY)

    for c in _weight_copies(e, e % 2, w_hbm, w_buf, wsem):
        c.wait()

    @pl.when(count > 0)
    def _expert():
        wgb_ref[...] = wg_ref[e % 2].astype(BF16)
        wub_ref[...] = wu_ref[e % 2].astype(BF16)
        wdb_ref[...] = wd_ref[e % 2].astype(BF16)

        @pl.when(first == 0)
        def _first_rows():
            _gather_rows(h2_hbm, tok_ref, 0, 1, xb_ref.at[0], gsem.at[0], MOE_BLK)

        def block(t, carry):
            g = first + t
            cur = g % 2
            nxt = 1 - cur
            _wait_rows(h2_hbm, xb_ref.at[cur], gsem.at[cur], MOE_BLK)

            @pl.when(g >= 2)
            def _staging_free():
                _block_out_copy(ob_ref, cur, ys_hbm, g - 2, osem).wait()

            base = (g + 1) * MOE_BLK
            for r in range(MOE_BLK):
                _row_copy(h2_hbm, tok_ref[base + r], xb_ref.at[nxt], r, gsem.at[nxt]).start()
            xb = xb_ref[cur].astype(BF16)
            hg = jnp.dot(xb, wgb_ref[...], preferred_element_type=F32)
            hu = jnp.dot(xb, wub_ref[...], preferred_element_type=F32)
            act = (jax.nn.silu(hg) * hu).astype(BF16)
            ob_ref[cur] = jnp.dot(act, wdb_ref[...], preferred_element_type=F32)
            _block_out_copy(ob_ref, cur, ys_hbm, g, osem).start()
            return carry

        lax.fori_loop(0, count, block, 0)

    @pl.when(e == pl.num_programs(0) - 1)
    def _finish():
        _wait_rows(h2_hbm, xb_ref.at[nb % 2], gsem.at[nb % 2], MOE_BLK)
        _block_out_copy(ob_ref, (nb - 1) % 2, ys_hbm, nb - 1, osem).wait()

        @pl.when(nb >= 2)
        def _():
            _block_out_copy(ob_ref, nb % 2, ys_hbm, nb - 2, osem).wait()

        ob_ref[0] = jnp.zeros(ob_ref.shape[1:], ob_ref.dtype)

        def zero_block(g, carry):
            _block_out_copy(ob_ref, 0, ys_hbm, g, osem).start()
            return carry

        lax.fori_loop(nb, n_blocks, zero_block, 0)

        def zero_wait(g, carry):
            _block_out_copy(ob_ref, 0, ys_hbm, g, osem).wait()
            return carry

        lax.fori_loop(nb, n_blocks, zero_wait, 0)


def _experts(h2, first_blk, n_blk, n_used, buf_tok, wg, wu, wd):
    s, d = h2.shape
    buf_len = buf_tok.shape[0]
    n_blocks = buf_len // MOE_BLK
    return pl.pallas_call(
        functools.partial(_expert_body, n_blocks=n_blocks),
        grid_spec=pltpu.PrefetchScalarGridSpec(
            num_scalar_prefetch=4,
            grid=(N_EXPERTS,),
            in_specs=[pl.BlockSpec(memory_space=pl.ANY)] * 4,
            out_specs=pl.BlockSpec(memory_space=pl.ANY),
            scratch_shapes=[pltpu.VMEM((2, MOE_BLK, d), F32), pltpu.VMEM((2, MOE_BLK, d), F32),
                            pltpu.VMEM((2, d, D_FF), F32), pltpu.VMEM((2, d, D_FF), F32),
                            pltpu.VMEM((2, D_FF, d), F32),
                            pltpu.VMEM((d, D_FF), BF16), pltpu.VMEM((d, D_FF), BF16),
                            pltpu.VMEM((D_FF, d), BF16),
                            pltpu.SemaphoreType.DMA((2,)), pltpu.SemaphoreType.DMA((2,)),
                            pltpu.SemaphoreType.DMA((2,))]),
        out_shape=jax.ShapeDtypeStruct((buf_len, d), F32),
        compiler_params=_cparams(("arbitrary",)),
        name="experts",
    )(first_blk, n_blk, n_used, buf_tok, h2, wg, wu, wd)


def _combine_body(pos_ref, x1_ref, gate_ref, fw_ref, ys_hbm, o_ref, g_ref, sem, *, tm):
    i = pl.program_id(0)

    def gather(tile, slot):
        for k in range(TOP_K):
            _gather_rows(ys_hbm, pos_ref, TOP_K * tile * tm + k, TOP_K, g_ref.at[slot, k],
                         sem.at[slot], tm)

    @pl.when(i == 0)
    def _first():
        gather(0, 0)

    @pl.when(i + 1 < pl.num_programs(0))
    def _prefetch():
        gather(i + 1, (i + 1) % 2)

    cur = i % 2
    for k in range(TOP_K):
        _wait_rows(ys_hbm, g_ref.at[cur, k], sem.at[cur], tm)
    gate = gate_ref[...]
    x = x1_ref[...] + (gate[:, 0:1] * g_ref[cur, 0] + gate[:, 1:2] * g_ref[cur, 1])
    y = x * lax.rsqrt(jnp.mean(x * x, axis=-1, keepdims=True) + NORM_EPS)
    o_ref[...] = y * fw_ref[...]


def _combine(x1, ys, pos, gate, fw, tm=512):
    s, d = x1.shape
    return pl.pallas_call(
        functools.partial(_combine_body, tm=tm),
        grid_spec=pltpu.PrefetchScalarGridSpec(
            num_scalar_prefetch=1,
            grid=(s // tm,),
            in_specs=[pl.BlockSpec((tm, d), lambda i, p: (i, 0)),
                      pl.BlockSpec((tm, LANES), lambda i, p: (i, 0)),
                      pl.BlockSpec((1, d), lambda i, p: (0, 0)),
                      pl.BlockSpec(memory_space=pl.ANY)],
            out_specs=pl.BlockSpec((tm, d), lambda i, p: (i, 0)),
            scratch_shapes=[pltpu.VMEM((2, TOP_K, tm, d), F32),
                            pltpu.SemaphoreType.DMA((2,))]),
        out_shape=jax.ShapeDtypeStruct((s, d), F32),
        compiler_params=_cparams(("arbitrary",)),
        name="combine_norm",
    )(pos, x1, gate, fw.reshape(1, d).astype(F32), ys)


def _dispatch_tables(eid, counts):
    n_tok = eid.shape[0]
    n_slots = n_tok * TOP_K
    buf_len = n_slots + N_EXPERTS * MOE_BLK
    counts = counts.astype(jnp.int32)
    padded = ((counts + MOE_BLK - 1) // MOE_BLK) * MOE_BLK
    padded_ends = jnp.cumsum(padded)
    padded_starts = padded_ends - padded
    experts = eid[:, :TOP_K]
    onehot = experts[:, :, None] == jnp.arange(N_EXPERTS, dtype=jnp.int32)
    dest = jnp.sum(jnp.where(onehot, padded_starts, 0), axis=-1) + eid[:, TOP_K:2 * TOP_K]
    n_used = (padded_ends[-1] // MOE_BLK).astype(jnp.int32).reshape(1)
    first_blk = (padded_starts // MOE_BLK).astype(jnp.int32)
    n_blk = (padded // MOE_BLK).astype(jnp.int32)
    return first_blk, n_blk, n_used, dest.reshape(n_slots).astype(jnp.int32), buf_len


def kernel(x, positions, norm1_w, w_in, ssm_lambda_re, ssm_lambda_im, ssm_log_dt, ssm_b_re, ssm_b_im, ssm_c_re, ssm_c_im, ssm_d, ssm_glu_w, ssm_glu_b, attn_lambda_q1, attn_lambda_k1, attn_lambda_q2, attn_lambda_k2, attn_subln_w, w_proj_ssm, w_proj_attn, w_out, norm2_w, router_group_w, router_expert_w, expert_w_gate, expert_w_up, expert_w_down, final_norm_w):
    bsz, seq, d = x.shape
    depth = norm1_w.shape[0]
    xs = x.reshape(bsz * seq, d)
    pos = positions.reshape(bsz * seq)
    o_q = SSM_WIDTH
    o_v = o_q + 2 * ATTN_WIDTH
    o_g = o_v + ATTN_WIDTH
    for l in range(depth):
        h, rope_cos, rope_sin = _rmsnorm_rope(xs, norm1_w[l].astype(F32), pos, BF16)
        w_in_b = w_in[l].astype(F32)
        u = _mm(h, w_in_b, 0, SSM_WIDTH, "none", F32)
        qk = _mm_rope(h, w_in_b, rope_cos, rope_sin, o_q)
        v = _mm_vones(h, w_in_b, o_v)
        gates = _mm(h, w_in_b, o_g, 2 * D_MODEL, "sigmoid", F32)

        tables = _s5_tables(ssm_lambda_re[l], ssm_lambda_im[l], ssm_log_dt[l],
                            ssm_b_re[l], ssm_b_im[l], ssm_c_re[l], ssm_c_im[l])
        y_s5 = _s5_core(u, tables, ssm_d[l])

        lam_params = jnp.stack([attn_lambda_q1[l], attn_lambda_k1[l],
                                attn_lambda_q2[l], attn_lambda_k2[l]]).astype(F32)
        y_attn = _diff_attention(qk, v, lam_params, attn_subln_w[l])

        wr = jnp.concatenate([router_group_w[l], router_expert_w[l]], axis=1).astype(F32)
        wr = jnp.pad(wr, ((0, 0), (0, LANES - wr.shape[1])))
        x1, h2, eid, gate, counts = _merge_route(
            xs, y_s5, ssm_glu_w[l].astype(BF16), ssm_glu_b[l], y_attn, gates,
            w_proj_ssm[l].astype(BF16), w_proj_attn[l].astype(BF16),
            w_out[l].astype(BF16), norm2_w[l], wr)

        first_blk, n_blk, n_used, dest, buf_len = _dispatch_tables(eid, counts[0, :N_EXPERTS])
        buf_tok = _scatter_tok(dest, buf_len)
        ys = _experts(h2, first_blk, n_blk, n_used, buf_tok, expert_w_gate[l].astype(F32),
                      expert_w_up[l].astype(F32), expert_w_down[l].astype(F32))
        last = l == depth - 1
        assert last, "DEPTH > 1 needs an un-normalised combine"
        xs = _combine(x1, ys, dest, gate, final_norm_w)
    return xs.reshape(bsz, seq, d)
```

```python
import functools
import math

import jax
import jax.numpy as jnp
from jax import lax
from jax.experimental import pallas as pl
from jax.experimental.pallas import tpu as pltpu

F32 = jnp.float32
BF16 = jnp.bfloat16

D_MODEL = 2048
SSM_WIDTH = 1024
SSM_GROUP = 16
SSM_GROUPS = 64
SSM_STATE = 64
ATTN_WIDTH = 1024
ATTN_HEADS = 8
HEAD_DIM = 64
ROPE_THETA = 10000.0
N_GROUPS = 4
EXPERTS_PER_GROUP = 8
N_EXPERTS = 32
TOP_K = 2
TOP_K_LOG2 = 1
D_FF = 512
NORM_EPS = 1e-6
SUBLN_EPS = 1e-5
LAMBDA_INIT = 0.8 - 0.6 * math.exp(-0.3 * 0)

LANES = 128
VMEM_LIMIT = 48 * 1024 * 1024

SSM_CHUNK = 16
SSM_GB = 8
MOE_BLK = 128


def _cparams(sem):
    return pltpu.CompilerParams(dimension_semantics=sem, vmem_limit_bytes=VMEM_LIMIT)


def _rmsnorm_body(x_ref, w_ref, pos_ref, invf_ref, o_ref, cos_ref, sin_ref):
    x = x_ref[...]
    y = x * lax.rsqrt(jnp.mean(x * x, axis=-1, keepdims=True) + NORM_EPS)
    o_ref[...] = (y * w_ref[...]).astype(o_ref.dtype)
    ang = pos_ref[...].astype(F32) * invf_ref[...]
    lane = lax.broadcasted_iota(jnp.int32, ang.shape, 1)
    low = (lane % HEAD_DIM) < (HEAD_DIM // 2)
    cos_ref[...] = jnp.cos(ang)
    sin_ref[...] = jnp.where(low, -jnp.sin(ang), jnp.sin(ang))


def _rmsnorm_rope(x, w, positions, out_dtype, tm=512):
    s, d = x.shape
    inv_freq = 1.0 / (ROPE_THETA ** (jnp.arange(0, HEAD_DIM, 2, dtype=F32) / HEAD_DIM))
    invf = jnp.tile(inv_freq, LANES // (HEAD_DIM // 2)).reshape(1, LANES)
    return pl.pallas_call(
        _rmsnorm_body,
        grid=(s // tm,),
        in_specs=[pl.BlockSpec((tm, d), lambda i: (i, 0)),
                  pl.BlockSpec((1, d), lambda i: (0, 0)),
                  pl.BlockSpec((tm, 1), lambda i: (i, 0)),
                  pl.BlockSpec((1, LANES), lambda i: (0, 0))],
        out_specs=[pl.BlockSpec((tm, d), lambda i: (i, 0)),
                   pl.BlockSpec((tm, LANES), lambda i: (i, 0)),
                   pl.BlockSpec((tm, LANES), lambda i: (i, 0))],
        out_shape=[jax.ShapeDtypeStruct((s, d), out_dtype),
                   jax.ShapeDtypeStruct((s, LANES), F32),
                   jax.ShapeDtypeStruct((s, LANES), F32)],
        compiler_params=_cparams(("parallel",)),
        name="rmsnorm",
    )(x, w.reshape(1, d), positions.reshape(s, 1), invf)


def _proj(h_ref, w_ref, wb_ref, row_axis):
    @pl.when(pl.program_id(row_axis) == 0)
    def _cast():
        wb_ref[...] = w_ref[...].astype(BF16)

    return jnp.dot(h_ref[...], wb_ref[...], preferred_element_type=F32)


def _mm_body(h_ref, w_ref, o_ref, wb_ref, *, act):
    acc = _proj(h_ref, w_ref, wb_ref, 1)
    if act == "sigmoid":
        acc = jax.nn.sigmoid(acc)
    o_ref[...] = acc.astype(o_ref.dtype)


def _mm(h, w, col0, ncols, act, out_dtype, tm=1024, tn=1024):
    s, k = h.shape
    jb = col0 // tn
    return pl.pallas_call(
        functools.partial(_mm_body, act=act),
        grid=(ncols // tn, s // tm),
        in_specs=[pl.BlockSpec((tm, k), lambda j, i: (i, 0)),
                  pl.BlockSpec((k, tn), lambda j, i: (0, j + jb))],
        out_specs=pl.BlockSpec((tm, tn), lambda j, i: (i, j)),
        out_shape=jax.ShapeDtypeStruct((s, ncols), out_dtype),
        scratch_shapes=[pltpu.VMEM((k, tn), BF16)],
        compiler_params=_cparams(("arbitrary", "arbitrary")),
        name="mm_" + act,
    )(h, w)


def _mm_vones_body(h_ref, w_ref, o_ref, wb_ref):
    acc = _proj(h_ref, w_ref, wb_ref, 0)
    hw = 2 * HEAD_DIM
    ones = jnp.ones((acc.shape[0], hw), o_ref.dtype)
    for hd in range(acc.shape[1] // hw):
        o_ref[:, 2 * hd * hw:(2 * hd + 1) * hw] = acc[:, hd * hw:(hd + 1) * hw].astype(o_ref.dtype)
        o_ref[:, (2 * hd + 1) * hw:(2 * hd + 2) * hw] = ones


def _mm_vones(h, w, col0, tm=1024):
    s, k = h.shape
    tn = ATTN_WIDTH
    jb = col0 // tn
    return pl.pallas_call(
        _mm_vones_body,
        grid=(s // tm,),
        in_specs=[pl.BlockSpec((tm, k), lambda i: (i, 0)),
                  pl.BlockSpec((k, tn), lambda i: (0, jb))],
        out_specs=pl.BlockSpec((tm, 2 * tn), lambda i: (i, 0)),
        out_shape=jax.ShapeDtypeStruct((s, 2 * tn), BF16),
        scratch_shapes=[pltpu.VMEM((k, tn), BF16)],
        compiler_params=_cparams(("arbitrary",)),
        name="mm_vones",
    )(h, w)


def _mm_rope_body(h_ref, w_ref, cos_ref, sin_ref, o_ref, wb_ref, *, tn):
    j = pl.program_id(0)
    acc = _proj(h_ref, w_ref, wb_ref, 1)
    scale = jnp.where(j == 0, HEAD_DIM ** -0.5, 1.0).astype(F32)
    lane = lax.broadcasted_iota(jnp.int32, cos_ref.shape, 1)
    low = (lane % HEAD_DIM) < (HEAD_DIM // 2)
    cos = cos_ref[...] * scale
    sin = sin_ref[...] * scale
    for c in range(tn // LANES):
        t = acc[:, c * LANES:(c + 1) * LANES]
        partner = jnp.where(low, pltpu.roll(t, LANES - HEAD_DIM // 2, 1),
                            pltpu.roll(t, HEAD_DIM // 2, 1))
        o_ref[:, c * LANES:(c + 1) * LANES] = (t * cos + partner * sin).astype(o_ref.dtype)


def _mm_rope(h, w, cos, sin, col0, tm=1024, tn=1024):
    s, k = h.shape
    jb = col0 // tn
    return pl.pallas_call(
        functools.partial(_mm_rope_body, tn=tn),
        grid=(2, s // tm),
        in_specs=[pl.BlockSpec((tm, k), lambda j, i: (i, 0)),
                  pl.BlockSpec((k, tn), lambda j, i: (0, j + jb)),
                  pl.BlockSpec((tm, LANES), lambda j, i: (i, 0)),
                  pl.BlockSpec((tm, LANES), lambda j, i: (i, 0))],
        out_specs=pl.BlockSpec((tm, tn), lambda j, i: (i, j)),
        out_shape=jax.ShapeDtypeStruct((s, 2 * tn), BF16),
        scratch_shapes=[pltpu.VMEM((k, tn), BF16)],
        compiler_params=_cparams(("arbitrary", "arbitrary")),
        name="mm_rope",
    )(h, w, cos, sin)


def _s5_tables(lam_re, lam_im, log_dt, b_re, b_im, c_re, c_im):
    L, G, P, M = SSM_CHUNK, SSM_GROUPS, SSM_STATE, SSM_GROUP
    hi = lax.Precision.HIGHEST
    lr, li = lam_re.astype(F32), lam_im.astype(F32)
    dt = jnp.exp(log_dt.astype(F32))[:, None]
    n = jnp.arange(L + 1, dtype=F32)[:, None, None]
    mag = jnp.exp(lr * dt * n)
    pr = mag * jnp.cos(li * dt * n)
    pi = mag * jnp.sin(li * dt * n)
    nr, ni = pr[1] - 1.0, pi[1]
    den = lr * lr + li * li
    f_re = (nr * lr + ni * li) / den
    f_im = (ni * lr - nr * li) / den
    br, bi = b_re.astype(F32), b_im.astype(F32)
    bbr = f_re[..., None] * br - f_im[..., None] * bi
    bbi = f_re[..., None] * bi + f_im[..., None] * br
    cr, ci = c_re.astype(F32), c_im.astype(F32)

    nb, gb = G // SSM_GB, SSM_GB
    pr_g = jnp.transpose(pr, (1, 0, 2))
    pi_g = jnp.transpose(pi, (1, 0, 2))
    bbr_t = jnp.transpose(bbr, (0, 2, 1))
    bbi_t = jnp.transpose(bbi, (0, 2, 1))

    abr = pr_g[:, :L, None, :] * bbr_t[:, None] - pi_g[:, :L, None, :] * bbi_t[:, None]
    abi = pr_g[:, :L, None, :] * bbi_t[:, None] + pi_g[:, :L, None, :] * bbr_t[:, None]
    kmat = (jnp.einsum('gnip,gop->gnio', abr, cr, precision=hi)
            - jnp.einsum('gnip,gop->gnio', abi, ci, precision=hi))
    kpad = jnp.pad(kmat, ((0, 0), (1, 0), (0, 0), (0, 0)))
    kq = jnp.stack([jnp.stack([kpad[:, 1 + c2 - r2::2][:, :L // 2] for c2 in range(2)], axis=3)
                    for r2 in range(2)], axis=2)
    kk_src = jnp.transpose(kq.reshape(nb, gb, L // 2, 2, M, 2, M),
                           (0, 2, 3, 5, 4, 1, 6)).reshape(nb, L // 2, 2, 2, M, gb * M)

    ab1r = pr_g[:, 1, None, :] * bbr_t - pi_g[:, 1, None, :] * bbi_t
    ab1i = pr_g[:, 1, None, :] * bbi_t + pi_g[:, 1, None, :] * bbr_t
    b4 = jnp.stack([jnp.stack([ab1r, ab1i], axis=2), jnp.stack([bbr_t, bbi_t], axis=2)],
                   axis=1)
    bb_src = jnp.transpose(b4.reshape(nb, gb, 2, M, 2, P),
                           (0, 2, 4, 3, 1, 5)).reshape(nb, 2, 2, M, gb * P)

    cr_t = jnp.transpose(cr, (0, 2, 1))
    ci_t = jnp.transpose(ci, (0, 2, 1))
    car = cr_t * pr_g[:, 1, :, None] - ci_t * pi_g[:, 1, :, None]
    cai = cr_t * pi_g[:, 1, :, None] + ci_t * pr_g[:, 1, :, None]
    c4 = jnp.stack([jnp.stack([cr_t, car], axis=2), jnp.stack([-ci_t, -cai], axis=2)],
                   axis=1)
    cc_src = jnp.transpose(c4.reshape(nb, gb, 2, P, 2, M),
                           (0, 2, 4, 3, 1, 5)).reshape(nb, 2, 2, P, gb * M)

    ap = jnp.stack([pr.reshape(L + 1, nb, gb * P), pi.reshape(L + 1, nb, gb * P)], axis=2)
    return kk_src, bb_src, cc_src, ap.reshape(L + 1, 2 * G * P)


def _cmul(ar, ai, zr, zi):
    return ar * zr - ai * zi, ar * zi + ai * zr


def _block_diag(src, rows_per_group, cols_per_group):
    t = jnp.concatenate([src] * SSM_GB, axis=0)
    row = lax.broadcasted_iota(jnp.int32, t.shape, 0)
    col = lax.broadcasted_iota(jnp.int32, t.shape, 1)
    return jnp.where(row // rows_per_group == col // cols_per_group, t, 0.0).astype(BF16)


def _s5_body(u_ref, kk_src, bb_src, cc_src, ap_ref, d_ref, y_ref,
             kk_ref, bb_ref, cc_ref, lhs_ref, er_ref, ei_ref, xr_ref, xi_ref):
    L, M, P = SSM_CHUNK, SSM_GROUP, SSM_STATE
    nc = u_ref.shape[0] // L
    hp = SSM_GB * P
    gm = SSM_GB * M
    nq = L // 2
    for a in range(2):
        for b in range(2):
            for dd in range(nq):
                kk_ref[dd, a * gm:(a + 1) * gm, b * gm:(b + 1) * gm] = _block_diag(
                    kk_src[0, dd, a, b], M, M)
            bb_ref[a * gm:(a + 1) * gm, b * hp:(b + 1) * hp] = _block_diag(bb_src[0, a, b], M, P)
            cc_ref[a * hp:(a + 1) * hp, b * gm:(b + 1) * gm] = _block_diag(cc_src[0, a, b], P, M)
    for q in range(nq):
        lhs_ref[q] = jnp.concatenate(
            [u_ref[pl.ds(2 * q, nc, stride=L), :], u_ref[pl.ds(2 * q + 1, nc, stride=L), :]],
            axis=1).astype(BF16)

    er = jnp.zeros((nc, hp), F32)
    ei = jnp.zeros((nc, hp), F32)
    for q in range(nq):
        z = jnp.dot(lhs_ref[q], bb_ref[...], preferred_element_type=F32)
        n = L - 2 - 2 * q
        dr, di = _cmul(ap_ref[n:n + 1, :hp], ap_ref[n:n + 1, hp:], z[:, :hp], z[:, hp:])
        er = er + dr
        ei = ei + di
    er_ref[...] = er
    ei_ref[...] = ei

    ar = ap_ref[L:L + 1, :hp]
    ai = ap_ref[L:L + 1, hp:]

    def step(c, carry):
        xr, xi = carry
        xr_ref[pl.ds(c, 1), :] = xr
        xi_ref[pl.ds(c, 1), :] = xi
        nr, ni = _cmul(ar, ai, xr, xi)
        return nr + er_ref[pl.ds(c, 1), :], ni + ei_ref[pl.ds(c, 1), :]

    zero = jnp.zeros((1, hp), F32)
    lax.fori_loop(0, nc, step, (zero, zero))

    for qq in range(nq):
        n = 2 * qq + 1
        wr, wi = _cmul(ap_ref[n:n + 1, :hp], ap_ref[n:n + 1, hp:], xr_ref[...], xi_ref[...])
        w = jnp.concatenate([wr, wi], axis=1).astype(BF16)
        acc = jnp.dot(w, cc_ref[...], preferred_element_type=F32)
        for q in range(qq + 1):
            acc = acc + jnp.dot(lhs_ref[q], kk_ref[qq - q], preferred_element_type=F32)
        for r in range(2):
            j = 2 * qq + r
            y = acc[:, r * LANES:(r + 1) * LANES] + d_ref[...] * u_ref[pl.ds(j, nc, stride=L), :]
            y_ref[pl.ds(j, nc, stride=L), :] = jax.nn.gelu(y)


def _s5_core(u, tables, d_skip):
    s, width = u.shape
    L, P, M, gb = SSM_CHUNK, SSM_STATE, SSM_GROUP, SSM_GB
    nc = s // L
    nb = width // (gb * M)
    kk_src, bb_src, cc_src, ap = tables
    return pl.pallas_call(
        _s5_body,
        grid=(nb,),
        in_specs=[pl.BlockSpec((s, gb * M), lambda b: (0, b)),
                  pl.BlockSpec((1, L // 2, 2, 2, M, gb * M), lambda b: (b, 0, 0, 0, 0, 0)),
                  pl.BlockSpec((1, 2, 2, M, gb * P), lambda b: (b, 0, 0, 0, 0)),
                  pl.BlockSpec((1, 2, 2, P, gb * M), lambda b: (b, 0, 0, 0, 0)),
                  pl.BlockSpec((L + 1, 2 * gb * P), lambda b: (0, b)),
                  pl.BlockSpec((1, gb * M), lambda b: (0, b))],
        out_specs=pl.BlockSpec((s, gb * M), lambda b: (0, b)),
        out_shape=jax.ShapeDtypeStruct((s, width), F32),
        scratch_shapes=[pltpu.VMEM((L // 2, 2 * gb * M, 2 * gb * M), BF16),
                        pltpu.VMEM((2 * gb * M, 2 * gb * P), BF16),
                        pltpu.VMEM((2 * gb * P, 2 * gb * M), BF16),
                        pltpu.VMEM((L // 2, nc, 2 * gb * M), BF16)]
        + [pltpu.VMEM((nc, gb * P), F32) for _ in range(4)],
        compiler_params=_cparams(("parallel",)),
        name="s5_scan",
    )(u, kk_src, bb_src, cc_src, ap, d_skip.astype(F32).reshape(1, width))


def _attn_body(q_ref, k_ref, v_ref, lam_ref, sw_ref, o_ref,
               qs_ref, m_ref, acc_ref, s_ref, p_ref, al_ref, *, tq):
    qi = pl.program_id(1)
    hw = 2 * HEAD_DIM
    q = q_ref[...]
    lane = lax.broadcasted_iota(jnp.int32, q.shape, 1)
    zero = jnp.zeros_like(q)
    qs_ref[0] = jnp.where(lane < HEAD_DIM, q, zero)
    qs_ref[1] = jnp.where(lane >= HEAD_DIM, q, zero)
    m_ref[...] = jnp.full(m_ref.shape, -jnp.inf, F32)
    acc_ref[...] = jnp.zeros(acc_ref.shape, F32)

    def scores(c, slot, masked=False):
        off = pl.multiple_of(c * tq, tq)
        k = k_ref[pl.ds(off, tq), :]
        for mp in range(2):
            s = lax.dot_general(qs_ref[mp], k, (((1,), (1,)), ((), ())),
                                preferred_element_type=F32)
            if masked:
                row = lax.broadcasted_iota(jnp.int32, s.shape, 0)
                col = lax.broadcasted_iota(jnp.int32, s.shape, 1)
                s = jnp.where(col <= row, s, -jnp.inf)
            s_ref[slot, mp] = s

    def softmax(slot):
        for mp in range(2):
            s = s_ref[slot, mp]
            m_old = m_ref[mp]
            m_new = jnp.maximum(m_old, jnp.max(s, axis=-1, keepdims=True))
            al_ref[slot, mp] = jnp.exp(m_old - m_new)
            p_ref[slot, mp] = jnp.exp(s - m_new).astype(BF16)
            m_ref[mp] = m_new

    def pv(c, slot):
        off = pl.multiple_of(c * tq, tq)
        v = v_ref[pl.ds(off, tq), :]
        for mp in range(2):
            acc_ref[mp] = al_ref[slot, mp] * acc_ref[mp] + jnp.dot(
                p_ref[slot, mp], v, preferred_element_type=F32)

    @pl.when(qi == 0)
    def _one_chunk():
        scores(0, 0, masked=True)
        softmax(0)
        pv(0, 0)

    @pl.when(qi == 1)
    def _two_chunks():
        scores(0, 1)
        softmax(1)
        pv(0, 1)
        scores(1, 0, masked=True)
        softmax(0)
        pv(1, 0)

    @pl.when(qi >= 2)
    def _pipelined():
        odd = qi % 2

        @pl.when(odd == 0)
        def _fill_even():
            scores(0, 0)
            scores(1, 1)
            softmax(0)

        @pl.when(odd == 1)
        def _fill_odd():
            scores(0, 1)
            scores(1, 0)
            softmax(1)

        def odd_step(_, carry):
            scores(2, 1)
            softmax(0)
            pv(0, 1)
            return carry

        lax.fori_loop(0, odd, odd_step, 0)
        t0 = 2 + odd

        def two_steps(u, carry):
            t = t0 + 2 * u
            scores(t, 0)
            softmax(1)
            pv(t - 2, 0)
            scores(t + 1, 1)
            softmax(0)
            pv(t - 1, 1)
            return carry

        lax.fori_loop(0, (qi - t0) // 2, two_steps, 0)
        scores(qi, 0, masked=True)
        softmax(1)
        pv(qi - 2, 0)
        softmax(0)
        pv(qi - 1, 1)
        pv(qi, 0)

    lam = (jnp.exp(jnp.sum(lam_ref[0:1, :] * lam_ref[1:2, :]))
           - jnp.exp(jnp.sum(lam_ref[2:3, :] * lam_ref[3:4, :])) + LAMBDA_INIT)
    a1 = acc_ref[0]
    a2 = acc_ref[1]
    o = a1[:, :hw] / a1[:, hw:] - lam * (a2[:, :hw] / a2[:, hw:])
    o = o * lax.rsqrt(jnp.mean(o * o, axis=-1, keepdims=True) + SUBLN_EPS)
    o_ref[...] = (o * sw_ref[...] * (1.0 - LAMBDA_INIT)).astype(o_ref.dtype)


def _diff_attention(qk, v1, lam_params, subln_w, tq=512):
    s = qk.shape[0]
    nq = s // tq
    hw = 2 * HEAD_DIM
    kcol0 = ATTN_WIDTH // hw
    return pl.pallas_call(
        functools.partial(_attn_body, tq=tq),
        grid=(ATTN_HEADS, nq),
        in_specs=[pl.BlockSpec((tq, hw), lambda h, i: (i, h)),
                  pl.BlockSpec((s, hw), lambda h, i: (0, kcol0 + h)),
                  pl.BlockSpec((s, 2 * hw), lambda h, i: (0, h)),
                  pl.BlockSpec((4, HEAD_DIM), lambda h, i: (0, 0)),
                  pl.BlockSpec((1, hw), lambda h, i: (0, 0))],
        out_specs=pl.BlockSpec((tq, hw), lambda h, i: (i, h)),
        out_shape=jax.ShapeDtypeStruct((s, ATTN_WIDTH), BF16),
        scratch_shapes=[pltpu.VMEM((2, tq, hw), BF16),
                        pltpu.VMEM((2, tq, 1), F32),
                        pltpu.VMEM((2, tq, 2 * hw), F32),
                        pltpu.VMEM((2, 2, tq, tq), F32),
                        pltpu.VMEM((2, 2, tq, tq), BF16),
                        pltpu.VMEM((2, 2, tq, 1), F32)],
        compiler_params=_cparams(("parallel", "arbitrary")),
        name="diff_attn",
    )(qk, qk, v1, lam_params, subln_w.reshape(1, hw).astype(F32))


def _merge_body(x_ref, y_ref, glw_ref, glb_ref, ya_ref, gs_ref, ga_ref, wps_ref, wpa_ref, wo_ref,
                n2_ref, wr_ref, x1_ref, h2_ref, eid_ref, gate_ref, cnt_ref):
    y = y_ref[...]
    glu = jnp.dot(y.astype(BF16), glw_ref[...], preferred_element_type=F32) + glb_ref[...]
    y_ssm = (y * jax.nn.sigmoid(glu)).astype(BF16)
    ps = jnp.dot(y_ssm, wps_ref[...], preferred_element_type=F32)
    pa = jnp.dot(ya_ref[...], wpa_ref[...], preferred_element_type=F32)
    merged = gs_ref[...].astype(F32) * ps + ga_ref[...].astype(F32) * pa
    x1 = x_ref[...] + jnp.dot(merged.astype(BF16), wo_ref[...], preferred_element_type=F32)
    x1_ref[...] = x1
    h2 = x1 * lax.rsqrt(jnp.mean(x1 * x1, axis=-1, keepdims=True) + NORM_EPS) * n2_ref[...]
    h2_ref[...] = h2
    wr = wr_ref[...]
    h_hi = h2.astype(BF16)
    h_lo = (h2 - h_hi.astype(F32)).astype(BF16)
    w_hi = wr.astype(BF16)
    w_lo = (wr - w_hi.astype(F32)).astype(BF16)
    tm = h2.shape[0]
    parts = jnp.dot(jnp.concatenate([h_hi, h_lo], axis=0), jnp.concatenate([w_hi, w_lo], axis=1),
                    preferred_element_type=F32)
    logits = (parts[:tm, :LANES] + parts[:tm, LANES:]) + (parts[tm:, :LANES] + parts[tm:, LANES:])
    lane = lax.broadcasted_iota(jnp.int32, logits.shape, 1)
    big = jnp.int32(1 << 20)
    ninf = jnp.float32(-jnp.inf)
    is_g = lane < N_GROUPS
    gl = jnp.where(is_g, logits, ninf)
    gm = jnp.max(gl, axis=-1, keepdims=True)
    g_idx = jnp.min(jnp.where(gl == gm, lane, big), axis=-1, keepdims=True)
    g_val = 1.0 / jnp.sum(jnp.where(is_g, jnp.exp(gl - gm), 0.0), axis=-1, keepdims=True)
    lo = N_GROUPS + g_idx * EXPERTS_PER_GROUP
    in_grp = (lane >= lo) & (lane < lo + EXPERTS_PER_GROUP)
    el = jnp.where(in_grp, logits, ninf)
    e1 = jnp.max(el, axis=-1, keepdims=True)
    i1 = jnp.min(jnp.where(el == e1, lane, big), axis=-1, keepdims=True)
    el2 = jnp.where(lane == i1, ninf, el)
    e2 = jnp.max(el2, axis=-1, keepdims=True)
    i2 = jnp.min(jnp.where(el2 == e2, lane, big), axis=-1, keepdims=True)
    t = jnp.exp(e2 - e1)
    p1 = 1.0 / (1.0 + t)
    p2 = t / (1.0 + t)
    gate_ref[...] = jnp.where(lane == 0, p1 * g_val, jnp.where(lane == 1, p2 * g_val, 0.0))

    @pl.when(pl.program_id(0) == 0)
    def _zero_counts():
        cnt_ref[...] = jnp.zeros(cnt_ref.shape, F32)

    ex1 = i1 - N_GROUPS
    ex2 = i2 - N_GROUPS
    oh1 = lane == ex1
    oh2 = lane == ex2
    onehot = jnp.where(oh1 | oh2, 1.0, 0.0)
    r_i =lax.broadcasted_iota(jnp.int32, (tm, tm), 0)
    c_i = lax.broadcasted_iota(jnp.int32, (tm, tm), 1)
    tri = jnp.where(c_i < r_i, 1.0, 0.0).astype(BF16)
    before = jnp.dot(tri, onehot.astype(BF16), preferred_element_type=F32) + cnt_ref[...]
    rank1 = jnp.sum(jnp.where(oh1, before, 0.0), axis=-1, keepdims=True).astype(jnp.int32)
    rank2 = jnp.sum(jnp.where(oh2, before, 0.0), axis=-1, keepdims=True).astype(jnp.int32)
    cnt_ref[...] = cnt_ref[...] + jnp.sum(onehot, axis=0, keepdims=True)
    eid_ref[...] = jnp.where(lane == 0, ex1, jnp.where(lane == 1, ex2,
                             jnp.where(lane == 2, rank1, jnp.where(lane == 3, rank2, 0))))


def _merge_route(x, y_s5, glu_w, glu_b, y_attn, gates, wps, wpa, wo, n2w, wr, tm=256):
    s, d = x.shape
    nw = y_s5.shape[1]
    const = lambda i: (0, 0)
    return pl.pallas_call(
        _merge_body,
        grid=(s // tm,),
        in_specs=[pl.BlockSpec((tm, d), lambda i: (i, 0)),
                  pl.BlockSpec((tm, nw), lambda i: (i, 0)),
                  pl.BlockSpec((nw, nw), const),
                  pl.BlockSpec((1, nw), const),
                  pl.BlockSpec((tm, nw), lambda i: (i, 0)),
                  pl.BlockSpec((tm, d), lambda i: (i, 0)),
                  pl.BlockSpec((tm, d), lambda i: (i, 1)),
                  pl.BlockSpec((nw, d), const),
                  pl.BlockSpec((nw, d), const),
                  pl.BlockSpec((d, d), const),
                  pl.BlockSpec((1, d), const),
                  pl.BlockSpec((d, LANES), const)],
        out_specs=[pl.BlockSpec((tm, d), lambda i: (i, 0)),
                   pl.BlockSpec((tm, d), lambda i: (i, 0)),
                   pl.BlockSpec((tm, LANES), lambda i: (i, 0)),
                   pl.BlockSpec((tm, LANES), lambda i: (i, 0)),
                   pl.BlockSpec((1, LANES), const)],
        out_shape=[jax.ShapeDtypeStruct((s, d), F32),
                   jax.ShapeDtypeStruct((s, d), F32),
                   jax.ShapeDtypeStruct((s, LANES), jnp.int32),
                   jax.ShapeDtypeStruct((s, LANES), F32),
                   jax.ShapeDtypeStruct((1, LANES), F32)],
        compiler_params=_cparams(("arbitrary",)),
        name="merge_route",
    )(x, y_s5, glu_w, glu_b.reshape(1, nw).astype(F32), y_attn, gates, gates, wps, wpa, wo,
      n2w.reshape(1, d).astype(F32), wr)


def _row_copy(src_hbm, row, dst_ref, r, sem):
    return pltpu.make_async_copy(src_hbm.at[pl.ds(row, 1), :], dst_ref.at[pl.ds(r, 1), :], sem)


def _gather_rows(src_hbm, idx_ref, base, stride, dst_ref, sem, n):
    def start(r, c):
        _row_copy(src_hbm, idx_ref[base + stride * r], dst_ref, r, sem).start()
        return c

    lax.fori_loop(0, n, start, 0, unroll=8)


def _wait_rows(src_hbm, dst_ref, sem, n):
    pltpu.make_async_copy(src_hbm.at[pl.ds(0, n), :], dst_ref, sem).wait()


def _scatter_tok_body(dest_ref, tok_ref):
    def zero(i, c):
        tok_ref[i] = 0
        return c

    lax.fori_loop(0, tok_ref.shape[0], zero, 0, unroll=8)

    def put(i, c):
        tok_ref[dest_ref[i]] = lax.shift_right_logical(i, TOP_K_LOG2)
        return c

    lax.fori_loop(0, dest_ref.shape[0], put, 0, unroll=8)


def _scatter_tok(dest, buf_len):
    return pl.pallas_call(
        _scatter_tok_body,
        in_specs=[pl.BlockSpec(memory_space=pltpu.SMEM)],
        out_specs=pl.BlockSpec(memory_space=pltpu.SMEM),
        out_shape=jax.ShapeDtypeStruct((buf_len,), jnp.int32),
        name="scatter_tok",
    )(dest)


def _block_out_copy(ob_ref, slot, ys_hbm, blk, sem):
    return pltpu.make_async_copy(ob_ref.at[slot], ys_hbm.at[pl.ds(blk * MOE_BLK, MOE_BLK), :],
                                 sem.at[slot])


WEIGHT_DMA_PRIORITY = 1


def _weight_copies(e, slot, w_hbm, w_buf, wsem):
    return [pltpu.make_async_copy(h.at[e], b.at[slot], wsem.at[slot])
            for h, b in zip(w_hbm, w_buf)]


def _expert_body(fb_ref, nblk_ref, nb_ref, tok_ref, h2_hbm, wg_hbm, wu_hbm, wd_hbm, ys_hbm,
                 xb_ref, ob_ref, wg_ref, wu_ref, wd_ref, wgb_ref, wub_ref, wdb_ref,
                 gsem, osem, wsem, *, n_blocks):
    e = pl.program_id(0)
    nb = nb_ref[0]
    first = fb_ref[e]
    count = nblk_ref[e]
    w_hbm = (wg_hbm, wu_hbm, wd_hbm)
    w_buf = (wg_ref, wu_ref, wd_ref)

    @pl.when(e == 0)
    def _first_weights():
        for c in _weight_copies(0, 0, w_hbm, w_buf, wsem):
            c.start(priority=WEIGHT_DMA_PRIORITY)

    @pl.when(e + 1 < pl.num_programs(0))
    def _next_weights():
        for c in _weight_copies(e + 1, (e + 1) % 2, w_hbm, w_buf, wsem):
            c.start(priority=WEIGHT_DMA_PRIORITY)

    for c in _weight_copies(e, e % 2, w_hbm, w_buf, wsem):
        c.wait()

    @pl.when(count > 0)
    def _expert():
        wgb_ref[...] = wg_ref[e % 2].astype(BF16)
        wub_ref[...] = wu_ref[e % 2].astype(BF16)
        wdb_ref[...] = wd_ref[e % 2].astype(BF16)

        @pl.when(first == 0)
        def _first_rows():
            _gather_rows(h2_hbm, tok_ref, 0, 1, xb_ref.at[0], gsem.at[0], MOE_BLK)

        def block(t, carry):
            g = first + t
            cur = g % 2
            nxt = 1 - cur
            _wait_rows(h2_hbm, xb_ref.at[cur], gsem.at[cur], MOE_BLK)

            @pl.when(g >= 2)
            def _staging_free():
                _block_out_copy(ob_ref, cur, ys_hbm, g - 2, osem).wait()

            base = (g + 1) * MOE_BLK
            xb = xb_ref[cur].astype(BF16)
            hg = jnp.dot(xb, wgb_ref[...], preferred_element_type=F32)
            hu = jnp.dot(xb, wub_ref[...], preferred_element_type=F32)
            for r in range(MOE_BLK):
                _row_copy(h2_hbm, tok_ref[base + r], xb_ref.at[nxt], r, gsem.at[nxt]).start()
            act = (jax.nn.silu(hg) * hu).astype(BF16)
            ob_ref[cur] = jnp.dot(act, wdb_ref[...], preferred_element_type=F32)
            _block_out_copy(ob_ref, cur, ys_hbm, g, osem).start()
            return carry

        lax.fori_loop(0, count, block, 0)

    @pl.when(e == pl.num_programs(0) - 1)
    def _finish():
        _wait_rows(h2_hbm, xb_ref.at[nb % 2], gsem.at[nb % 2], MOE_BLK)
        _block_out_copy(ob_ref, (nb - 1) % 2, ys_hbm, nb - 1, osem).wait()

        @pl.when(nb >= 2)
        def _():
            _block_out_copy(ob_ref, nb % 2, ys_hbm, nb - 2, osem).wait()

        ob_ref[0] = jnp.zeros(ob_ref.shape[1:], ob_ref.dtype)

        def zero_block(g, carry):
            _block_out_copy(ob_ref, 0, ys_hbm, g, osem).start()
            return carry

        lax.fori_loop(nb, n_blocks, zero_block, 0)

        def zero_wait(g, carry):
            _block_out_copy(ob_ref, 0, ys_hbm, g, osem).wait()
            return carry

        lax.fori_loop(nb, n_blocks, zero_wait, 0)


def _experts(h2, first_blk, n_blk, n_used, buf_tok, wg, wu, wd):
    s, d = h2.shape
    buf_len = buf_tok.shape[0]
    n_blocks = buf_len // MOE_BLK
    return pl.pallas_call(
        functools.partial(_expert_body, n_blocks=n_blocks),
        grid_spec=pltpu.PrefetchScalarGridSpec(
            num_scalar_prefetch=4,
            grid=(N_EXPERTS,),
            in_specs=[pl.BlockSpec(memory_space=pl.ANY)] * 4,
            out_specs=pl.BlockSpec(memory_space=pl.ANY),
            scratch_shapes=[pltpu.VMEM((2, MOE_BLK, d), F32), pltpu.VMEM((2, MOE_BLK, d), F32),
                            pltpu.VMEM((2, d, D_FF), F32), pltpu.VMEM((2, d, D_FF), F32),
                            pltpu.VMEM((2, D_FF, d), F32),
                            pltpu.VMEM((d, D_FF), BF16), pltpu.VMEM((d, D_FF), BF16),
                            pltpu.VMEM((D_FF, d), BF16),
                            pltpu.SemaphoreType.DMA((2,)), pltpu.SemaphoreType.DMA((2,)),
                            pltpu.SemaphoreType.DMA((2,))]),
        out_shape=jax.ShapeDtypeStruct((buf_len, d), F32),
        compiler_params=_cparams(("arbitrary",)),
        name="experts",
    )(first_blk, n_blk, n_used, buf_tok, h2, wg, wu, wd)


def _combine_body(pos_ref, x1_ref, gate_ref, fw_ref, ys_hbm, o_ref, g_ref, sem, *, tm):
    i = pl.program_id(0)

    def gather(tile, slot):
        for k in range(TOP_K):
            _gather_rows(ys_hbm, pos_ref, TOP_K * tile * tm + k, TOP_K, g_ref.at[slot, k],
                         sem.at[slot], tm)

    @pl.when(i == 0)
    def _first():
        gather(0, 0)

    @pl.when(i + 1 < pl.num_programs(0))
    def _prefetch():
        gather(i + 1, (i + 1) % 2)

    cur = i % 2
    for k in range(TOP_K):
        _wait_rows(ys_hbm, g_ref.at[cur, k], sem.at[cur], tm)
    gate = gate_ref[...]
    x = x1_ref[...] + (gate[:, 0:1] * g_ref[cur, 0] + gate[:, 1:2] * g_ref[cur, 1])
    y = x * lax.rsqrt(jnp.mean(x * x, axis=-1, keepdims=True) + NORM_EPS)
    o_ref[...] = y * fw_ref[...]


def _combine(x1, ys, pos, gate, fw, tm=512):
    s, d = x1.shape
    return pl.pallas_call(
        functools.partial(_combine_body, tm=tm),
        grid_spec=pltpu.PrefetchScalarGridSpec(
            num_scalar_prefetch=1,
            grid=(s // tm,),
            in_specs=[pl.BlockSpec((tm, d), lambda i, p: (i, 0)),
                      pl.BlockSpec((tm, LANES), lambda i, p: (i, 0)),
                      pl.BlockSpec((1, d), lambda i, p: (0, 0)),
                      pl.BlockSpec(memory_space=pl.ANY)],
            out_specs=pl.BlockSpec((tm, d), lambda i, p: (i, 0)),
            scratch_shapes=[pltpu.VMEM((2, TOP_K, tm, d), F32),
                            pltpu.SemaphoreType.DMA((2,))]),
        out_shape=jax.ShapeDtypeStruct((s, d), F32),
        compiler_params=_cparams(("arbitrary",)),
        name="combine_norm",
    )(pos, x1, gate, fw.reshape(1, d).astype(F32), ys)


def _dispatch_tables(eid, counts):
    n_tok = eid.shape[0]
    n_slots = n_tok * TOP_K
    buf_len = n_slots + N_EXPERTS * MOE_BLK
    counts = counts.astype(jnp.int32)
    padded = ((counts + MOE_BLK - 1) // MOE_BLK) * MOE_BLK
    padded_ends = jnp.cumsum(padded)
    padded_starts = padded_ends - padded
    experts = eid[:, :TOP_K]
    onehot = experts[:, :, None] == jnp.arange(N_EXPERTS, dtype=jnp.int32)
    dest = jnp.sum(jnp.where(onehot, padded_starts, 0), axis=-1) + eid[:, TOP_K:2 * TOP_K]
    n_used = (padded_ends[-1] // MOE_BLK).astype(jnp.int32).reshape(1)
    first_blk = (padded_starts // MOE_BLK).astype(jnp.int32)
    n_blk = (padded // MOE_BLK).astype(jnp.int32)
    return first_blk, n_blk, n_used, dest.reshape(n_slots).astype(jnp.int32), buf_len


def kernel(x, positions, norm1_w, w_in, ssm_lambda_re, ssm_lambda_im, ssm_log_dt, ssm_b_re, ssm_b_im, ssm_c_re, ssm_c_im, ssm_d, ssm_glu_w, ssm_glu_b, attn_lambda_q1, attn_lambda_k1, attn_lambda_q2, attn_lambda_k2, attn_subln_w, w_proj_ssm, w_proj_attn, w_out, norm2_w, router_group_w, router_expert_w, expert_w_gate, expert_w_up, expert_w_down, final_norm_w):
    bsz, seq, d = x.shape
    depth = norm1_w.shape[0]
    xs = x.reshape(bsz * seq, d)
    pos = positions.reshape(bsz * seq)
    o_q = SSM_WIDTH
    o_v = o_q + 2 * ATTN_WIDTH
    o_g = o_v + ATTN_WIDTH
    for l in range(depth):
        h, rope_cos, rope_sin = _rmsnorm_rope(xs, norm1_w[l].astype(F32), pos, BF16)
        w_in_b = w_in[l].astype(F32)
        u = _mm(h, w_in_b, 0, SSM_WIDTH, "none", F32)
        qk = _mm_rope(h, w_in_b, rope_cos, rope_sin, o_q)
        v = _mm_vones(h, w_in_b, o_v)
        gates = _mm(h, w_in_b, o_g, 2 * D_MODEL, "sigmoid", F32)

        tables = _s5_tables(ssm_lambda_re[l], ssm_lambda_im[l], ssm_log_dt[l],
                            ssm_b_re[l], ssm_b_im[l], ssm_c_re[l], ssm_c_im[l])
        y_s5 = _s5_core(u, tables, ssm_d[l])

        lam_params = jnp.stack([attn_lambda_q1[l], attn_lambda_k1[l],
                                attn_lambda_q2[l], attn_lambda_k2[l]]).astype(F32)
        y_attn = _diff_attention(qk, v, lam_params, attn_subln_w[l])

        wr = jnp.concatenate([router_group_w[l], router_expert_w[l]], axis=1).astype(F32)
        wr = jnp.pad(wr, ((0, 0), (0, LANES - wr.shape[1])))
        x1, h2, eid, gate, counts = _merge_route(
            xs, y_s5, ssm_glu_w[l].astype(BF16), ssm_glu_b[l], y_attn, gates,
            w_proj_ssm[l].astype(BF16), w_proj_attn[l].astype(BF16),
            w_out[l].astype(BF16), norm2_w[l], wr)

        first_blk, n_blk, n_used, dest, buf_len = _dispatch_tables(eid, counts[0, :N_EXPERTS])
        buf_tok = _scatter_tok(dest, buf_len)
        ys = _experts(h2, first_blk, n_blk, n_used, buf_tok, expert_w_gate[l].astype(F32),
                      expert_w_up[l].astype(F32), expert_w_down[l].astype(F32))
        last = l == depth - 1
        assert last, "DEPTH > 1 needs an un-normalised combine"
        xs = _combine(x1, ys, dest, gate, final_norm_w)
    return xs.reshape(bsz, seq, d)
```
